```python
import math
import jax
import jax.numpy as jnp
from jax import lax
import numpy as np

D_MODEL = 1024
BATCH = 2
SEQ = 8192
DEPTH = 2
DEC_BATCH = 32
DEC_SEQ = 32
PAST_LEN = 2048

CHUNK = 64
N_META = 16
LEAD = CHUNK - N_META
N_EVEN = (DEPTH + 1) // 2
N_ODD = DEPTH // 2
CONV_CH = D_MODEL // 2
CONV_W = 3
S5_CH = D_MODEL // 2
S5_GROUP = 16
S5_GROUPS = S5_CH // S5_GROUP
S5_P = 64
D_MIX = CONV_CH + S5_CH
IN0_COLS = 3 * CONV_CH + S5_CH
D_FF = 2816
N_HEADS = 16
N_KV = 2
GQ = N_HEADS // N_KV
HEAD_DIM = 64
WINDOW = 128
WIN_CHUNKS = WINDOW // CHUNK
N_BUCKETS = 32
MAX_DIST = 128
N_EXPERTS = 8
TOP_K = 2
D_FF_E = 1024
EPS = 1e-6
NEG = -1e30

kernel_name = 'hybrid_stream_conv_s5_swa_moe_step'


def _rmsnorm(x, g):
    xf = x.astype(jnp.float32)
    y = xf * lax.rsqrt(jnp.mean(xf * xf, axis=-1, keepdims=True) + EPS) * g.astype(jnp.float32)
    return y.astype(x.dtype)


def _swiglu(h, w1, w3, w2):
    return (jax.nn.silu(h @ w1) * (h @ w3)) @ w2


def _short_conv(x, buf, w, b):
    L = x.shape[1]
    xp = jnp.concatenate([buf.astype(x.dtype), x], axis=1)
    y = lax.conv_general_dilated(xp, w[:, None, :].astype(xp.dtype), window_strides=(1,), padding='VALID',
                                 dimension_numbers=('NWC', 'WIO', 'NWC'), feature_group_count=CONV_CH)
    return y + b, xp[:, L:]


def _cmul(ar, ai, br, bi):
    return ar * br - ai * bi, ar * bi + ai * br


def _s5_combine(e, l):
    ear, eai, ebr, ebi = e
    lar, lai, lbr, lbi = l
    ar, ai = _cmul(lar, lai, ear, eai)
    br, bi = _cmul(lar, lai, ebr, ebi)
    return (ar, ai, br + lbr, bi + lbi)


def _s5(u, h0_re, h0_im, a_re, a_im, log_dt, b_re, b_im, c_re, c_im, d):
    f32 = jnp.float32
    a_re, a_im = a_re.astype(f32), a_im.astype(f32)
    dt = jnp.exp(log_dt.astype(f32))[:, None]
    mag = jnp.exp(a_re * dt)
    ab_re, ab_im = mag * jnp.cos(a_im * dt), mag * jnp.sin(a_im * dt)
    den = a_re * a_re + a_im * a_im
    num_re = ab_re - 1.0
    cf_re = (num_re * a_re + ab_im * a_im) / den
    cf_im = (ab_im * a_re - num_re * a_im) / den
    b_re, b_im = b_re.astype(f32), b_im.astype(f32)
    bb_re = cf_re[..., None] * b_re - cf_im[..., None] * b_im
    bb_im = cf_re[..., None] * b_im + cf_im[..., None] * b_re
    nbatch, L = u.shape[0], u.shape[1]
    ug = u.reshape(nbatch, L, S5_GROUPS, S5_GROUP)
    bu_re = jnp.einsum('blgj,gpj->blgp', ug, bb_re)
    bu_im = jnp.einsum('blgj,gpj->blgp', ug, bb_im)
    i0_re, i0_im = _cmul(ab_re, ab_im, h0_re.astype(f32), h0_im.astype(f32))
    bu_re = bu_re.at[:, 0].add(i0_re)
    bu_im = bu_im.at[:, 0].add(i0_im)
    elems = (jnp.broadcast_to(ab_re, bu_re.shape), jnp.broadcast_to(ab_im, bu_im.shape), bu_re, bu_im)
    _, _, s_re, s_im = lax.associative_scan(_s5_combine, elems, axis=1)
    y = (jnp.einsum('blgp,gjp->blgj', s_re, c_re.astype(f32))
         - jnp.einsum('blgp,gjp->blgj', s_im, c_im.astype(f32)))
    y = y.reshape(nbatch, L, S5_CH) + d.astype(f32) * u
    return y, s_re[:, -1], s_im[:, -1]


def _even_mixer(h, conv_buf, s_re, s_im, w_in, cw, cb, a_re, a_im, log_dt, b_re, b_im, c_re, c_im, d,
                glu_w, glu_b, w_out):
    proj = h @ w_in
    g_b, g_c, val, u = jnp.split(proj, [CONV_CH, 2 * CONV_CH, 3 * CONV_CH], axis=-1)
    conv_out, new_buf = _short_conv(g_c * val, conv_buf, cw, cb)
    out_a = g_b * conv_out
    y, f_re, f_im = _s5(u.astype(jnp.float32), s_re, s_im, a_re, a_im, log_dt, b_re, b_im, c_re, c_im, d)
    z = jax.nn.gelu(y).astype(h.dtype)
    out_b = z * jax.nn.sigmoid(z @ glu_w + glu_b)
    out = jnp.concatenate([out_a, out_b], axis=-1) @ w_out
    return out.astype(h.dtype), new_buf, f_re, f_im


def _t5_bucket(rel):
    half = N_BUCKETS // 2
    max_exact = half // 2
    ret = jnp.where(rel > 0, half, 0)
    n = jnp.abs(rel)
    nf = jnp.maximum(n, 1).astype(jnp.float32)
    large = max_exact + (jnp.log(nf / max_exact) / math.log(MAX_DIST / max_exact) * (half - max_exact)).astype(jnp.int32)
    large = jnp.minimum(large, half - 1)
    return ret + jnp.where(n < max_exact, n, large)


def _rel_bias(table, nq, nk):
    rel = jnp.arange(nk, dtype=jnp.int32)[None, :] - WINDOW - jnp.arange(nq, dtype=jnp.int32)[:, None]
    bias = table.astype(jnp.float32)[_t5_bucket(rel)]
    return jnp.transpose(bias, (2, 0, 1)).reshape(N_KV, GQ, nq, nk)


def _attend(q, k, v, bias, valid, sink):
    s = jnp.einsum('...qkgd,...mkd->...kgqm', q, k, preferred_element_type=jnp.float32) * (HEAD_DIM ** -0.5) + bias
    if valid is not None:
        s = jnp.where(valid, s, NEG)
    sk = sink.astype(jnp.float32).reshape(N_KV, GQ, 1, 1)
    m = jnp.maximum(jnp.max(s, axis=-1, keepdims=True), sk)
    p = jnp.exp(s - m)
    p = p / (jnp.sum(p, axis=-1, keepdims=True) + jnp.exp(sk - m))
    return jnp.einsum('...kgqm,...mkd->...qkgd', p.astype(v.dtype), v)


def _qkv(h, w_qkv, b_qkv):
    qkv = h @ w_qkv + b_qkv
    q, k, v = jnp.split(qkv, [N_HEADS * HEAD_DIM, (N_HEADS + N_KV) * HEAD_DIM], axis=-1)
    lead = h.shape[:-1]
    return (q.reshape(*lead, N_KV, GQ, HEAD_DIM), k.reshape(*lead, N_KV, HEAD_DIM),
            v.reshape(*lead, N_KV, HEAD_DIM))


def _swa_prompt(h, w_qkv, b_qkv, sink, w_o, b_o, rel_table):
    nbatch, L = h.shape[0], h.shape[1]
    q, k, v = _qkv(h, w_qkv, b_qkv)
    nblk = (L + LEAD) // CHUNK
    qb = jnp.pad(q, ((0, 0), (LEAD, 0), (0, 0), (0, 0), (0, 0))).reshape(nbatch, nblk, CHUNK, N_KV, GQ, HEAD_DIM)

    def band(t):
        tb = jnp.pad(t, ((0, 0), (LEAD + WINDOW, 0), (0, 0), (0, 0)))
        tb = tb.reshape(nbatch, nblk + WIN_CHUNKS, CHUNK, N_KV, HEAD_DIM)
        return jnp.concatenate([tb[:, s:s + nblk] for s in range(WIN_CHUNKS + 1)], axis=2)

    kband, vband = band(k), band(v)
    key_pos = jnp.arange(nblk)[:, None] * CHUNK + jnp.arange(WINDOW + CHUNK)[None, :]
    valid = (key_pos >= LEAD + WINDOW)[:, None, None, None, :]
    o = _attend(qb, kband, vband, _rel_bias(rel_table, CHUNK, WINDOW + CHUNK), valid, sink)
    o = o.reshape(nbatch, nblk * CHUNK, N_HEADS * HEAD_DIM)[:, LEAD:]
    y = o @ w_o + b_o
    return y.astype(h.dtype), k[:, -WINDOW:], v[:, -WINDOW:]


def _swa_sample(h, cache_k, cache_v, w_qkv, b_qkv, sink, w_o, b_o, rel_table):
    nbatch, S = h.shape[0], h.shape[1]
    q, k, v = _qkv(h, w_qkv, b_qkv)
    kk = jnp.concatenate([cache_k.astype(k.dtype), k], axis=1)
    vv = jnp.concatenate([cache_v.astype(v.dtype), v], axis=1)
    o = _attend(q, kk, vv, _rel_bias(rel_table, S, WINDOW + S), None, sink)
    y = o.reshape(nbatch, S, N_HEADS * HEAD_DIM) @ w_o + b_o
    return y.astype(h.dtype), k, v


def _moe(h, rw, rb, w1, w3, w2):
    logits = (h @ rw + rb).astype(jnp.float32)
    top_v, top_i = lax.top_k(logits, TOP_K)
    top_g = jax.nn.softmax(top_v, axis=-1)
    gate = jnp.sum(jax.nn.one_hot(top_i, N_EXPERTS, dtype=jnp.float32) * top_g[..., None], axis=-2).astype(h.dtype)
    out = jnp.zeros_like(h)
    for e in range(N_EXPERTS):
        out = out + gate[..., e:e + 1] * _swiglu(h, w1[e], w3[e], w2[e])
    return out


def setup_inputs(seed: int = 0) -> dict:
    key = jax.random.key(seed)
    ks = iter(jax.random.split(key, 64))
    f32 = jnp.float32

    def nrm(shape, scale):
        return scale * jax.random.normal(next(ks), shape, f32)

    def gain(shape):
        return 1.0 + 0.02 * jax.random.normal(next(ks), shape, f32)

    n_idx = jnp.arange(S5_P, dtype=f32)
    return {
        'x_prompt': nrm((BATCH, SEQ, D_MODEL), 1.0),
        'x_sample': nrm((DEC_BATCH, DEC_SEQ, D_MODEL), 1.0),
        'cache_conv': nrm((N_EVEN, DEC_BATCH, CONV_W - 1, CONV_CH), 1.0),
        'state_s5_re': nrm((N_EVEN, DEC_BATCH, S5_GROUPS, S5_P), 0.3),
        'state_s5_im': nrm((N_EVEN, DEC_BATCH, S5_GROUPS, S5_P), 0.3),
        'cache_swa_k': nrm((N_ODD, DEC_BATCH, WINDOW, N_KV, HEAD_DIM), 1.0),
        'cache_swa_v': nrm((N_ODD, DEC_BATCH, WINDOW, N_KV, HEAD_DIM), 1.0),
        'meta_tokens': nrm((N_META, D_MODEL), 1.0),
        'rel_bias_table': nrm((N_BUCKETS, N_HEADS), 0.3),
        'norm_mix': gain((DEPTH, D_MODEL)),
        'norm_ffn': gain((DEPTH, D_MODEL)),
        'norm_final': gain((D_MODEL,)),
        'w_in0': nrm((N_EVEN, D_MODEL, IN0_COLS), D_MODEL ** -0.5),
        'conv_w': nrm((N_EVEN, CONV_W, CONV_CH), CONV_W ** -0.5),
        'conv_b': nrm((N_EVEN, CONV_CH), 0.01),
        's5_a_re': -0.5 + nrm((N_EVEN, S5_GROUPS, S5_P), 0.01),
        's5_a_im': math.pi * n_idx + nrm((N_EVEN, S5_GROUPS, S5_P), 0.01),
        's5_log_dt': jax.random.uniform(next(ks), (N_EVEN, S5_GROUPS), f32, math.log(1e-3), math.log(1e-1)),
        's5_b_re': nrm((N_EVEN, S5_GROUPS, S5_P, S5_GROUP), (2 * S5_GROUP) ** -0.5),
        's5_b_im': nrm((N_EVEN, S5_GROUPS, S5_P, S5_GROUP), (2 * S5_GROUP) ** -0.5),
        's5_c_re': nrm((N_EVEN, S5_GROUPS, S5_GROUP, S5_P), (2 * S5_P) ** -0.5),
        's5_c_im': nrm((N_EVEN, S5_GROUPS, S5_GROUP, S5_P), (2 * S5_P) ** -0.5),
        's5_d': nrm((N_EVEN, S5_CH), 1.0),
        's5_glu_w': nrm((N_EVEN, S5_CH, S5_CH), S5_CH ** -0.5),
        's5_glu_b': nrm((N_EVEN, S5_CH), 0.01),
        'w_out0': nrm((N_EVEN, D_MIX, D_MODEL), D_MIX ** -0.5),
        'ffn_w1': nrm((N_EVEN, D_MODEL, D_FF), D_MODEL ** -0.5),
        'ffn_w3': nrm((N_EVEN, D_MODEL, D_FF), D_MODEL ** -0.5),
        'ffn_w2': nrm((N_EVEN, D_FF, D_MODEL), D_FF ** -0.5),
        'w_qkv': nrm((N_ODD, D_MODEL, (N_HEADS + 2 * N_KV) * HEAD_DIM), D_MODEL ** -0.5),
        'b_qkv': nrm((N_ODD, (N_HEADS + 2 * N_KV) * HEAD_DIM), 0.02),
        'attn_sink': nrm((N_ODD, N_HEADS), 0.5),
        'w_o': nrm((N_ODD, N_HEADS * HEAD_DIM, D_MODEL), (N_HEADS * HEAD_DIM) ** -0.5),
        'b_o': nrm((N_ODD, D_MODEL), 0.01),
        'router_w': nrm((N_ODD, D_MODEL, N_EXPERTS), D_MODEL ** -0.5),
        'router_b': nrm((N_ODD, N_EXPERTS), 0.01),
        'moe_w1': nrm((N_ODD, N_EXPERTS, D_MODEL, D_FF_E), D_MODEL ** -0.5),
        'moe_w3': nrm((N_ODD, N_EXPERTS, D_MODEL, D_FF_E), D_MODEL ** -0.5),
        'moe_w2': nrm((N_ODD, N_EXPERTS, D_FF_E, D_MODEL), D_FF_E ** -0.5),
    }


def reference(x_prompt, x_sample, cache_conv, state_s5_re, state_s5_im, cache_swa_k, cache_swa_v,
              meta_tokens, rel_bias_table, norm_mix, norm_ffn, norm_final,
              w_in0, conv_w, conv_b, s5_a_re, s5_a_im, s5_log_dt, s5_b_re, s5_b_im, s5_c_re, s5_c_im,
              s5_d, s5_glu_w, s5_glu_b, w_out0, ffn_w1, ffn_w3, ffn_w2,
              w_qkv, b_qkv, attn_sink, w_o, b_o, router_w, router_b, moe_w1, moe_w3, moe_w2):
    f32 = jnp.float32
    bp = x_prompt.shape[0]
    meta = jnp.broadcast_to(meta_tokens.astype(x_prompt.dtype)[None], (bp, N_META, D_MODEL))
    hp = jnp.concatenate([meta, x_prompt], axis=1)
    hs = x_sample
    conv_p, conv_s, s5rp, s5ip, s5rs, s5is = [], [], [], [], [], []
    kp_l, vp_l, ks_l, vs_l = [], [], [], []
    for layer in range(DEPTH):
        i = layer // 2
        gm, gf = norm_mix[layer], norm_ffn[layer]
        if layer % 2 == 0:
            prm = (w_in0[i], conv_w[i], conv_b[i], s5_a_re[i], s5_a_im[i], s5_log_dt[i], s5_b_re[i], s5_b_im[i],
                   s5_c_re[i], s5_c_im[i], s5_d[i], s5_glu_w[i], s5_glu_b[i], w_out0[i])
            zbuf = jnp.zeros((bp, CONV_W - 1, CONV_CH), hp.dtype)
            zst = jnp.zeros((bp, S5_GROUPS, S5_P), f32)
            mp, cbp, srp, sip = _even_mixer(_rmsnorm(hp, gm), zbuf, zst, zst, *prm)
            ms, cbs, srs, sis = _even_mixer(_rmsnorm(hs, gm), cache_conv[i], state_s5_re[i], state_s5_im[i], *prm)
            hp = hp + mp
            hs = hs + ms
            conv_p.append(cbp)
            conv_s.append(cbs)
            s5rp.append(srp)
            s5ip.append(sip)
            s5rs.append(srs)
            s5is.append(sis)
            hp = hp + _swiglu(_rmsnorm(hp, gf), ffn_w1[i], ffn_w3[i], ffn_w2[i])
            hs = hs + _swiglu(_rmsnorm(hs, gf), ffn_w1[i], ffn_w3[i], ffn_w2[i])
        else:
            prm = (w_qkv[i], b_qkv[i], attn_sink[i], w_o[i], b_o[i], rel_bias_table)
            mp, kpn, vpn = _swa_prompt(_rmsnorm(hp, gm), *prm)
            ms, ksn, vsn = _swa_sample(_rmsnorm(hs, gm), cache_swa_k[i], cache_swa_v[i], *prm)
            hp = hp + mp
            hs = hs + ms
            kp_l.append(kpn)
            vp_l.append(vpn)
            ks_l.append(ksn)
            vs_l.append(vsn)
            ep = (router_w[i], router_b[i], moe_w1[i], moe_w3[i], moe_w2[i])
            hp = hp + _moe(_rmsnorm(hp, gf), *ep)
            hs = hs + _moe(_rmsnorm(hs, gf), *ep)
    y_prompt = _rmsnorm(hp, norm_final)[:, N_META:]
    y_sample = _rmsnorm(hs, norm_final)
    new_cache_conv_prompt = jnp.stack(conv_p, 0)
    new_cache_conv_sample = jnp.stack(conv_s, 0)
    new_state_s5_re_prompt = jnp.stack(s5rp, 0)
    new_state_s5_im_prompt = jnp.stack(s5ip, 0)
    new_state_s5_re_sample = jnp.stack(s5rs, 0)
    new_state_s5_im_sample = jnp.stack(s5is, 0)
    new_cache_swa_k_prompt = jnp.stack(kp_l, 0)
    new_cache_swa_v_prompt = jnp.stack(vp_l, 0)
    new_cache_swa_k_sample = jnp.stack(ks_l, 0)
    new_cache_swa_v_sample = jnp.stack(vs_l, 0)
    return (y_prompt, y_sample, new_cache_conv_prompt, new_cache_conv_sample,
            new_state_s5_re_prompt, new_state_s5_im_prompt, new_state_s5_re_sample, new_state_s5_im_sample,
            new_cache_swa_k_prompt, new_cache_swa_v_prompt, new_cache_swa_k_sample, new_cache_swa_v_sample)
```

```python
import functools
import math

import numpy as np
import jax
import jax.numpy as jnp
from jax import lax
from jax.experimental import pallas as pl
from jax.experimental.pallas import tpu as pltpu

F32 = jnp.float32
BF16 = jnp.bfloat16

D_MODEL = 1024
CONV_CH = 512
S5_CH = 512
S5_GROUP = 16
S5_GROUPS = 32
S5_P = 64
S5_STATE = S5_GROUPS * S5_P
D_FF = 2816
N_HEADS = 16
N_KV = 2
GQ = N_HEADS // N_KV
HEAD_DIM = 64
KV_DIM = N_KV * HEAD_DIM
WINDOW = 128
CHUNK = 64
N_META = 16
N_BUCKETS = 32
MAX_DIST = 128
N_EXPERTS = 8
D_FF_E = 1024
EPS = 1e-6
NEG = -1e30

SEG = 32
N_SAMPLE_SEG = 32
META_SEG = N_SAMPLE_SEG
N_SMALL_SEG = 40
TM_MIX = 256
SCAN_ROWS = 8
SCAN_LANES = 256
TQ = 128
VMEM_LIMIT = 56 * 1024 * 1024


def _const_spec(shape):
    nd = len(shape)
    return pl.BlockSpec(shape, lambda *_: (0,) * nd, pipeline_mode=pl.Buffered(1))


def _params(*sem):
    return pltpu.CompilerParams(dimension_semantics=sem, vmem_limit_bytes=VMEM_LIMIT)


def _rms(x, g):
    return x * lax.rsqrt(jnp.mean(x * x, axis=-1, keepdims=True) + EPS) * g


def _dot(a, b):
    return jnp.dot(a, b, preferred_element_type=F32)


def _even_mixer_kernel(x_ref, gm_ref, win_ref, cw_ref, cb_ref, cinit_ref, sinit_ref, sc_ref, bb_ref, cc_ref,
                       d_ref, gluw_ref, glub_ref, wout_ref,
                       h_ref, cout_ref, sout_ref,
                       bu_ref, mix_ref, ccarry_ref, scarry_ref, *, seg, nseg, tiles_per_seq):
    t = pl.program_id(0)
    x = x_ref[...]
    hn = _rms(x, gm_ref[...]).astype(BF16)
    proj = _dot(hn, win_ref[...])
    g_b = proj[:, :CONV_CH]
    cin = proj[:, CONV_CH:2 * CONV_CH] * proj[:, 2 * CONV_CH:3 * CONV_CH]
    u = proj[:, 3 * CONV_CH:]
    bu_ref[...] = _dot(u.astype(BF16), bb_ref[...])
    if tiles_per_seq > 1:
        @pl.when(t % tiles_per_seq == 0)
        def _load_initial_state():
            ccarry_ref[...] = cinit_ref[0]
            scarry_ref[...] = sinit_ref[0]

    cw = cw_ref[...]
    row = lax.broadcasted_iota(jnp.int32, (seg, CONV_CH), 0)
    for j in range(nseg):
        rows = slice(j * seg, (j + 1) * seg)
        c_seg = cin[rows]
        init = ccarry_ref[...] if tiles_per_seq > 1 else cinit_ref[j]
        older, newer = init[0:1], init[1:2]
        p1 = jnp.where(row == 0, newer, pltpu.roll(c_seg, 1, 0))
        p2 = jnp.where(row == 0, older, jnp.where(row == 1, newer, pltpu.roll(c_seg, 2, 0)))
        conv = cw[0:1] * p2 + cw[1:2] * p1 + cw[2:3] * c_seg + cb_ref[...]
        mix_ref[rows, :CONV_CH] = g_b[rows] * conv
        tail = c_seg[seg - 2:seg]
        cout_ref[j] = tail
        if tiles_per_seq > 1:
            ccarry_ref[...] = tail

    groups_per_seg = seg // SCAN_ROWS
    n_groups = nseg * groups_per_seg
    for c in range(S5_STATE // SCAN_LANES):
        re_cols = slice(c * SCAN_LANES, (c + 1) * SCAN_LANES)
        im_cols = slice(S5_STATE + c * SCAN_LANES, S5_STATE + (c + 1) * SCAN_LANES)
        a1r, a1i, a2r, a2i, a4r, a4i, pr, pi = [sc_ref[k, :, re_cols] for k in range(8)]

        def seg_init(s, cols):
            return jnp.broadcast_to(sinit_ref[s, :, cols], (SCAN_ROWS, SCAN_LANES))

        def body(r, carry, re_cols=re_cols, im_cols=im_cols, a1r=a1r, a1i=a1i, a2r=a2r, a2i=a2i, a4r=a4r,
                 a4i=a4i, pr=pr, pi=pi, seg_init=seg_init):
            cr, ci = carry
            if nseg > 1:
                s = r // groups_per_seg
                starts = (r % groups_per_seg) == 0
                cr = jnp.where(starts, seg_init(s, re_cols), cr)
                ci = jnp.where(starts, seg_init(s, im_cols), ci)
            rows = pl.ds(pl.multiple_of(r * SCAN_ROWS, SCAN_ROWS), SCAN_ROWS)
            vr = bu_ref[rows, re_cols]
            vi = bu_ref[rows, im_cols]
            for ar, ai, sh in ((a1r, a1i, 1), (a2r, a2i, 2), (a4r, a4i, 4)):
                sr = pltpu.roll(vr, sh, 0)
                si = pltpu.roll(vi, sh, 0)
                vr, vi = vr + ar * sr - ai * si, vi + ar * si + ai * sr
            vr, vi = vr + pr * cr - pi * ci, vi + pr * ci + pi * cr
            bu_ref[rows, re_cols] = vr
            bu_ref[rows, im_cols] = vi
            if nseg > 1:
                s = r // groups_per_seg
                sout_ref[s, :, re_cols] = vr[SCAN_ROWS - 1:SCAN_ROWS]
                sout_ref[s, :, im_cols] = vi[SCAN_ROWS - 1:SCAN_ROWS]
            return (jnp.broadcast_to(vr[SCAN_ROWS - 1:SCAN_ROWS], (SCAN_ROWS, SCAN_LANES)),
                    jnp.broadcast_to(vi[SCAN_ROWS - 1:SCAN_ROWS], (SCAN_ROWS, SCAN_LANES)))

        if nseg > 1:
            zero = jnp.zeros((SCAN_ROWS, SCAN_LANES), F32)
            lax.fori_loop(0, n_groups, body, (zero, zero))
        else:
            state = scarry_ref if tiles_per_seq > 1 else sinit_ref.at[0]
            cr, ci = lax.fori_loop(0, n_groups, body,
                                   (jnp.broadcast_to(state[:, re_cols], (SCAN_ROWS, SCAN_LANES)),
                                    jnp.broadcast_to(state[:, im_cols], (SCAN_ROWS, SCAN_LANES))))
            scarry_ref[:, re_cols] = cr[0:1]
            scarry_ref[:, im_cols] = ci[0:1]
            sout_ref[0, :, re_cols] = cr[0:1]
            sout_ref[0, :, im_cols] = ci[0:1]

    y = _dot(bu_ref[...].astype(BF16), cc_ref[...]) + d_ref[...] * u
    z = jax.nn.gelu(y)
    mix_ref[:, CONV_CH:] = z * jax.nn.sigmoid(_dot(z.astype(BF16), gluw_ref[...]) + glub_ref[...])
    h_ref[...] = x + _dot(mix_ref[...].astype(BF16), wout_ref[...])


def _even_mixer(x, gm, w_in, cw, cb, cinit, sinit, scan_consts, bb, cc, d, glu_w, glu_b, w_out, *, seg, nseg,
                tiles_per_seq):
    rows = x.shape[0]
    tm = seg * nseg
    nseq = cinit.shape[0]
    kern = functools.partial(_even_mixer_kernel, seg=seg, nseg=nseg, tiles_per_seq=tiles_per_seq)
    return pl.pallas_call(
        kern,
        grid=(rows // tm,),
        in_specs=[
            pl.BlockSpec((tm, D_MODEL), lambda t: (t, 0)),
            _const_spec((1, D_MODEL)),
            _const_spec((D_MODEL, 4 * CONV_CH)),
            _const_spec((3, CONV_CH)),
            _const_spec((1, CONV_CH)),
            pl.BlockSpec((nseg, 2, CONV_CH), lambda t: (t // tiles_per_seq, 0, 0)),
            pl.BlockSpec((nseg, 1, 2 * S5_STATE), lambda t: (t // tiles_per_seq, 0, 0)),
            _const_spec((8, SCAN_ROWS, S5_STATE)),
            _const_spec((S5_CH, 2 * S5_STATE)),
            _const_spec((2 * S5_STATE, S5_CH)),
            _const_spec((1, S5_CH)),
            _const_spec((S5_CH, S5_CH)),
            _const_spec((1, S5_CH)),
            _const_spec((D_MODEL, D_MODEL)),
        ],
        out_specs=[
            pl.BlockSpec((tm, D_MODEL), lambda t: (t, 0)),
            pl.BlockSpec((nseg, 2, CONV_CH), lambda t: (t // tiles_per_seq, 0, 0)),
            pl.BlockSpec((nseg, 1, 2 * S5_STATE), lambda t: (t // tiles_per_seq, 0, 0)),
        ],
        out_shape=[
            jax.ShapeDtypeStruct((rows, D_MODEL), F32),
            jax.ShapeDtypeStruct((nseq, 2, CONV_CH), F32),
            jax.ShapeDtypeStruct((nseq, 1, 2 * S5_STATE), F32),
        ],
        scratch_shapes=[
            pltpu.VMEM((tm, 2 * S5_STATE), F32),
            pltpu.VMEM((tm, D_MODEL), F32),
            pltpu.VMEM((2, CONV_CH), F32),
            pltpu.VMEM((1, 2 * S5_STATE), F32),
        ],
        compiler_params=_params("arbitrary"),
        name="even_mixer",
    )(x, gm, w_in, cw, cb, cinit, sinit, scan_consts, bb, cc, d, glu_w, glu_b, w_out)


def _ffn_kernel(x_ref, g_ref, w1_ref, w3_ref, w2_ref, o_ref):
    x = x_ref[...]
    hn = _rms(x, g_ref[...]).astype(BF16)
    a = _dot(hn, w1_ref[...])
    b = _dot(hn, w3_ref[...])
    o_ref[...] = x + _dot((a * jax.nn.sigmoid(a) * b).astype(BF16), w2_ref[...])


def _ffn(x, g, w1, w3, w2, *, tm):
    rows = x.shape[0]
    return pl.pallas_call(
        _ffn_kernel,
        grid=(rows // tm,),
        in_specs=[
            pl.BlockSpec((tm, D_MODEL), lambda t: (t, 0)),
            _const_spec((1, D_MODEL)),
            _const_spec((D_MODEL, D_FF)),
            _const_spec((D_MODEL, D_FF)),
            _const_spec((D_FF, D_MODEL)),
        ],
        out_specs=pl.BlockSpec((tm, D_MODEL), lambda t: (t, 0)),
        out_shape=jax.ShapeDtypeStruct((rows, D_MODEL), F32),
        compiler_params=_params("parallel"),
        name="ffn",
    )(x, g, w1, w3, w2)


def _qkv_kernel(x_ref, g_ref, w_ref, b_ref, q_ref, k_ref, v_ref):
    hn = _rms(x_ref[...], g_ref[...]).astype(BF16)
    qkv = _dot(hn, w_ref[...]) + b_ref[...]
    nq = N_HEADS * HEAD_DIM
    q_ref[...] = (qkv[:, :nq] * (HEAD_DIM ** -0.5)).astype(BF16)
    k_ref[...] = qkv[:, nq:nq + KV_DIM]
    v_ref[...] = qkv[:, nq + KV_DIM:]


def _qkv(x, g, w, b, *, tm):
    rows = x.shape[0]
    ncol = (N_HEADS + 2 * N_KV) * HEAD_DIM
    return pl.pallas_call(
        _qkv_kernel,
        grid=(rows // tm,),
        in_specs=[
            pl.BlockSpec((tm, D_MODEL), lambda t: (t, 0)),
            _const_spec((1, D_MODEL)),
            _const_spec((D_MODEL, ncol)),
            _const_spec((1, ncol)),
        ],
        out_specs=[
            pl.BlockSpec((tm, N_HEADS * HEAD_DIM), lambda t: (t, 0)),
            pl.BlockSpec((tm, KV_DIM), lambda t: (t, 0)),
            pl.BlockSpec((tm, KV_DIM), lambda t: (t, 0)),
        ],
        out_shape=[
            jax.ShapeDtypeStruct((rows, N_HEADS * HEAD_DIM), BF16),
            jax.ShapeDtypeStruct((rows, KV_DIM), F32),
            jax.ShapeDtypeStruct((rows, KV_DIM), F32),
        ],
        compiler_params=_params("parallel"),
        name="qkv",
    )(x, g, w, b)


def _attend_heads(q, keys, vals, bias_of_head, sink_ref, o_scr):
    for h in range(N_HEADS):
        kv = h // GQ
        cols = slice(h * HEAD_DIM, (h + 1) * HEAD_DIM)
        kv_cols = slice(kv * HEAD_DIM, (kv + 1) * HEAD_DIM)
        s = lax.dot_general(q[:, cols], keys[:, kv_cols], (((1,), (1,)), ((), ())),
                            preferred_element_type=F32) + bias_of_head(h)
        sk = sink_ref[h]
        m = jnp.maximum(jnp.max(s, axis=-1, keepdims=True), sk)
        p = jnp.exp(s - m)
        den = jnp.sum(p, axis=-1, keepdims=True) + jnp.exp(sk - m)
        p = p * (1.0 / den)
        o_scr[:, cols] = _dot(p.astype(BF16), vals[:, kv_cols])


def _attn_prompt_kernel(q_ref, k_ref, v_ref, bias_ref, sink_ref, x_ref, wo_ref, bo_ref, o_ref, o_scr):
    t = pl.program_id(1)
    start = pl.multiple_of(t * TQ, TQ)
    keys = k_ref[0, pl.ds(start, TQ + WINDOW), :]
    vals = v_ref[0, pl.ds(start, TQ + WINDOW), :]
    _attend_heads(q_ref[...], keys, vals, lambda h: bias_ref[0, h], sink_ref, o_scr)
    o_ref[...] = x_ref[...] + _dot(o_scr[...].astype(BF16), wo_ref[...]) + bo_ref[...]


def _attn_prompt(q, kpad, vpad, bias, sink, x, w_o, b_o):
    nb, lpad = kpad.shape[0], kpad.shape[1]
    tiles = (lpad - WINDOW) // TQ
    return pl.pallas_call(
        _attn_prompt_kernel,
        grid=(nb, tiles),
        in_specs=[
            pl.BlockSpec((TQ, D_MODEL), lambda b, t: (b * tiles + t, 0)),
            pl.BlockSpec((1, lpad, KV_DIM), lambda b, t: (b, 0, 0)),
            pl.BlockSpec((1, lpad, KV_DIM), lambda b, t: (b, 0, 0)),
            pl.BlockSpec((1, N_HEADS, TQ, TQ + WINDOW), lambda b, t: (jnp.minimum(t, 1), 0, 0, 0)),
            pl.BlockSpec(memory_space=pltpu.SMEM),
            pl.BlockSpec((TQ, D_MODEL), lambda b, t: (b * tiles + t, 0)),
            _const_spec((D_MODEL, D_MODEL)),
            _const_spec((1, D_MODEL)),
        ],
        out_specs=pl.BlockSpec((TQ, D_MODEL), lambda b, t: (b * tiles + t, 0)),
        out_shape=jax.ShapeDtypeStruct(x.shape, F32),
        scratch_shapes=[pltpu.VMEM((TQ, D_MODEL), F32)],
        compiler_params=_params("parallel", "arbitrary"),
        name="attn_prompt",
    )(q, kpad, vpad, bias, sink, x, w_o, b_o)


def _attn_small_kernel(q_ref, kc_ref, vc_ref, kn_ref, vn_ref, bias_ref, mask_ref, sink_ref, x_ref, wo_ref, bo_ref,
                       o_ref, o_scr):
    keys = jnp.concatenate([kc_ref[0], kn_ref[...]], axis=0).astype(BF16)
    vals = jnp.concatenate([vc_ref[0], vn_ref[...]], axis=0).astype(BF16)
    mask = mask_ref[0]
    _attend_heads(q_ref[...], keys, vals, lambda h: bias_ref[h] + mask, sink_ref, o_scr)
    o_ref[...] = x_ref[...] + _dot(o_scr[...].astype(BF16), wo_ref[...]) + bo_ref[...]


def _attn_small(q, kc, vc, kn, vn, bias, mask, sink, x, w_o, b_o):
    nseg = x.shape[0] // SEG
    ncache = kc.shape[0]
    cache_spec = pl.BlockSpec((1, WINDOW, KV_DIM), lambda s: (jnp.minimum(s, ncache - 1), 0, 0))
    return pl.pallas_call(
        _attn_small_kernel,
        grid=(nseg,),
        in_specs=[
            pl.BlockSpec((SEG, D_MODEL), lambda s: (s, 0)),
            cache_spec,
            cache_spec,
            pl.BlockSpec((SEG, KV_DIM), lambda s: (s, 0)),
            pl.BlockSpec((SEG, KV_DIM), lambda s: (s, 0)),
            _const_spec((N_HEADS, SEG, WINDOW + SEG)),
            pl.BlockSpec((1, 1, WINDOW + SEG), lambda s: (s, 0, 0)),
            pl.BlockSpec(memory_space=pltpu.SMEM),
            pl.BlockSpec((SEG, D_MODEL), lambda s: (s, 0)),
            _const_spec((D_MODEL, D_MODEL)),
            _const_spec((1, D_MODEL)),
        ],
        out_specs=pl.BlockSpec((SEG, D_MODEL), lambda s: (s, 0)),
        out_shape=jax.ShapeDtypeStruct(x.shape, F32),
        scratch_shapes=[pltpu.VMEM((SEG, D_MODEL), F32)],
        compiler_params=_params("parallel"),
        name="attn_small",
    )(q, kc, vc, kn, vn, bias, mask, sink, x, w_o, b_o)


def _moe_kernel(x_ref, g_ref, rw_ref, rb_ref, w1_ref, w3_ref, w2_ref, gf_ref, o_ref, hn_scr, gate_scr, acc_scr):
    e = pl.program_id(1)
    lane = lax.broadcasted_iota(jnp.int32, gate_scr.shape, 1).astype(F32)

    @pl.when(e == 0)
    def _route():
        hn = _rms(x_ref[...], g_ref[...]).astype(BF16)
        hn_scr[...] = hn
        logits = _dot(hn, rw_ref[...]) + rb_ref[...]
        v1 = jnp.max(logits, axis=-1, keepdims=True)
        i1 = jnp.min(jnp.where(logits == v1, lane, float(N_EXPERTS)), axis=-1, keepdims=True)
        rest = jnp.where(lane == i1, -jnp.inf, logits)
        v2 = jnp.max(rest, axis=-1, keepdims=True)
        i2 = jnp.min(jnp.where(rest == v2, lane, float(N_EXPERTS)), axis=-1, keepdims=True)
        e2 = jnp.exp(v2 - v1)
        den = 1.0 + e2
        gate_scr[...] = jnp.where(lane == i1, 1.0 / den, 0.0) + jnp.where(lane == i2, e2 / den, 0.0)
        acc_scr[...] = jnp.zeros_like(acc_scr)

    hn = hn_scr[...]
    a = _dot(hn, w1_ref[0])
    b = _dot(hn, w3_ref[0])
    y = _dot((a * jax.nn.sigmoid(a) * b).astype(BF16), w2_ref[0])
    gate = jnp.sum(jnp.where(lane == e.astype(F32), gate_scr[...], 0.0), axis=-1, keepdims=True)
    acc_scr[...] += gate * y

    @pl.when(e == N_EXPERTS - 1)
    def _finish():
        o_ref[...] = _rms(x_ref[...] + acc_scr[...], gf_ref[...])


def _moe_final(x, g, rw, rb, w1, w3, w2, gf, *, tm):
    rows = x.shape[0]
    return pl.pallas_call(
        _moe_kernel,
        grid=(rows // tm, N_EXPERTS),
        in_specs=[
            pl.BlockSpec((tm, D_MODEL), lambda t, e: (t, 0)),
            _const_spec((1, D_MODEL)),
            _const_spec((D_MODEL, N_EXPERTS)),
            _const_spec((1, N_EXPERTS)),
            pl.BlockSpec((1, D_MODEL, D_FF_E), lambda t, e: (e, 0, 0)),
            pl.BlockSpec((1, D_MODEL, D_FF_E), lambda t, e: (e, 0, 0)),
            pl.BlockSpec((1, D_FF_E, D_MODEL), lambda t, e: (e, 0, 0)),
            _const_spec((1, D_MODEL)),
        ],
        out_specs=pl.BlockSpec((tm, D_MODEL), lambda t, e: (t, 0)),
        out_shape=jax.ShapeDtypeStruct((rows, D_MODEL), F32),
        scratch_shapes=[
            pltpu.VMEM((tm, D_MODEL), BF16),
            pltpu.VMEM((tm, N_EXPERTS), F32),
            pltpu.VMEM((tm, D_MODEL), F32),
        ],
        compiler_params=_params("parallel", "arbitrary"),
        name="moe_final",
    )(x, g, rw, rb, w1, w3, w2, gf)


def _t5_bucket_np(rel):
    half = N_BUCKETS // 2
    max_exact = half // 2
    ret = np.where(rel > 0, half, 0)
    n = np.abs(rel)
    nf = np.maximum(n, 1).astype(np.float32)
    large = max_exact + (np.log(nf / np.float32(max_exact)) / np.float32(math.log(MAX_DIST / max_exact))
                         * np.float32(half - max_exact)).astype(np.int32)
    large = np.minimum(large, half - 1)
    return ret + np.where(n < max_exact, n, large)


def _prompt_bias(table):
    r = np.arange(TQ)[:, None]
    j = np.arange(TQ + WINDOW)[None, :]
    bucket = _t5_bucket_np(j - WINDOW - r)
    band = (j - CHUNK * (r // CHUNK) >= 0) & (j - CHUNK * (r // CHUNK) < WINDOW + CHUNK)
    first = band & (j >= WINDOW - N_META)
    bias = jnp.transpose(table[bucket], (2, 0, 1))
    return jnp.stack([jnp.where(first[None], bias, NEG), jnp.where(band[None], bias, NEG)], axis=0)


def _small_bias(table):
    i = np.arange(SEG)[:, None]
    m = np.arange(WINDOW + SEG)[None, :]
    return jnp.transpose(table[_t5_bucket_np(m - WINDOW - i)], (2, 0, 1))


def _small_mask():
    mask = np.zeros((N_SMALL_SEG, 1, WINDOW + SEG), np.float32)
    mask[META_SEG, 0, :WINDOW + SEG - N_META] = NEG
    return jnp.asarray(mask)


def _cmul(ar, ai, br, bi):
    return ar * br - ai * bi, ar * bi + ai * br


def _s5_prepare(a_re, a_im, log_dt, b_re, b_im, c_re, c_im):
    dt = jnp.exp(log_dt)[:, None]
    mag = jnp.exp(a_re * dt)
    ab_re, ab_im = mag * jnp.cos(a_im * dt), mag * jnp.sin(a_im * dt)
    den = a_re * a_re + a_im * a_im
    num_re = ab_re - 1.0
    cf_re = (num_re * a_re + ab_im * a_im) / den
    cf_im = (ab_im * a_re - num_re * a_im) / den
    bb_re = cf_re[..., None] * b_re - cf_im[..., None] * b_im
    bb_im = cf_re[..., None] * b_im + cf_im[..., None] * b_re
    eye = jnp.eye(S5_GROUPS, dtype=F32)

    def in_map(bb):
        return jnp.einsum('gpj,gh->gjhp', bb, eye).reshape(S5_CH, S5_STATE)

    def out_map(c):
        return jnp.einsum('gjp,gh->gphj', c, eye).reshape(S5_STATE, S5_CH)

    bb = jnp.concatenate([in_map(bb_re), in_map(bb_im)], axis=1).astype(BF16)
    cc = jnp.concatenate([out_map(c_re), -out_map(c_im)], axis=0).astype(BF16)

    ar, ai = ab_re.reshape(1, S5_STATE), ab_im.reshape(1, S5_STATE)
    powers = [(ar, ai)]
    for _ in range(SCAN_ROWS - 1):
        powers.append(_cmul(*powers[-1], ar, ai))
    row = np.arange(SCAN_ROWS)[:, None]

    def shifted(k):
        return [jnp.where(row >= k, p, 0.0) for p in powers[k - 1]]

    carry = [jnp.concatenate([p[i] for p in powers], axis=0) for i in range(2)]
    consts = jnp.stack(shifted(1) + shifted(2) + shifted(4) + carry, axis=0)
    return bb, cc, consts


def kernel(x_prompt, x_sample, cache_conv, state_s5_re, state_s5_im, cache_swa_k, cache_swa_v, meta_tokens, rel_bias_table, norm_mix, norm_ffn, norm_final, w_in0, conv_w, conv_b, s5_a_re, s5_a_im, s5_log_dt, s5_b_re, s5_b_im, s5_c_re, s5_c_im, s5_d, s5_glu_w, s5_glu_b, w_out0, ffn_w1, ffn_w3, ffn_w2, w_qkv, b_qkv, attn_sink, w_o, b_o, router_w, router_b, moe_w1, moe_w3, moe_w2):
    nb, seq = x_prompt.shape[0], x_prompt.shape[1]
    nsb = x_sample.shape[0]
    assert x_sample.shape[1] == SEG and nsb == N_SAMPLE_SEG and seq % TM_MIX == 0

    def row(v):
        return v.reshape(1, -1)

    xp = x_prompt.reshape(nb * seq, D_MODEL)
    meta_seg = jnp.concatenate([jnp.zeros((SEG - N_META, D_MODEL), F32), meta_tokens], axis=0)
    filler = jnp.zeros(((N_SMALL_SEG - N_SAMPLE_SEG - 1) * SEG, D_MODEL), F32)
    xs = jnp.concatenate([x_sample.reshape(nsb * SEG, D_MODEL), meta_seg, filler], axis=0)
    n_extra = N_SMALL_SEG - N_SAMPLE_SEG

    bb, cc, consts = _s5_prepare(s5_a_re[0], s5_a_im[0], s5_log_dt[0], s5_b_re[0], s5_b_im[0], s5_c_re[0],
                                 s5_c_im[0])
    mixer_w = (row(norm_mix[0]), w_in0[0].astype(BF16), conv_w[0], row(conv_b[0]))
    mixer_w2 = (consts, bb, cc, row(s5_d[0]), s5_glu_w[0].astype(BF16), row(s5_glu_b[0]), w_out0[0].astype(BF16))
    cinit_s = jnp.concatenate([cache_conv[0], jnp.zeros((n_extra, 2, CONV_CH), F32)], axis=0)
    sinit_s = jnp.concatenate([state_s5_re[0].reshape(nsb, 1, S5_STATE), state_s5_im[0].reshape(nsb, 1, S5_STATE)],
                              axis=-1)
    sinit_s = jnp.concatenate([sinit_s, jnp.zeros((n_extra, 1, 2 * S5_STATE), F32)], axis=0)
    hs, conv_s, state_s = _even_mixer(xs, *mixer_w, cinit_s, sinit_s, *mixer_w2, seg=SEG, nseg=TM_MIX // SEG,
                                      tiles_per_seq=1)
    cinit_p = jnp.broadcast_to(conv_s[META_SEG][None], (nb, 2, CONV_CH))
    sinit_p = jnp.broadcast_to(state_s[META_SEG][None], (nb, 1, 2 * S5_STATE))
    hp, conv_p, state_p = _even_mixer(xp, *mixer_w, cinit_p, sinit_p, *mixer_w2, seg=TM_MIX, nseg=1,
                                      tiles_per_seq=seq // TM_MIX)
    ffn_w = (row(norm_ffn[0]), ffn_w1[0].astype(BF16), ffn_w3[0].astype(BF16), ffn_w2[0].astype(BF16))
    hs = _ffn(hs, *ffn_w, tm=256)
    hp = _ffn(hp, *ffn_w, tm=512)

    qkv_w = (row(norm_mix[1]), w_qkv[0].astype(BF16), row(b_qkv[0]))
    qs, ks, vs = _qkv(hs, *qkv_w, tm=256)
    qp, kp, vp = _qkv(hp, *qkv_w, tm=512)
    wo = (w_o[0].astype(BF16), row(b_o[0]))
    hs = _attn_small(qs, cache_swa_k[0].reshape(nsb, WINDOW, KV_DIM), cache_swa_v[0].reshape(nsb, WINDOW, KV_DIM),
                     ks, vs, _small_bias(rel_bias_table), _small_mask(), attn_sink[0], hs, *wo)

    def padded(new, small):
        meta_rows = small[META_SEG * SEG:(META_SEG + 1) * SEG]
        front = jnp.concatenate([jnp.zeros((WINDOW - SEG, KV_DIM), F32), meta_rows], axis=0)
        front = jnp.broadcast_to(front[None], (nb, WINDOW, KV_DIM))
        return jnp.concatenate([front, new.reshape(nb, seq, KV_DIM)], axis=1).astype(BF16)

    hp = _attn_prompt(qp, padded(kp, ks), padded(vp, vs), _prompt_bias(rel_bias_table), attn_sink[0], hp, *wo)
    moe_w = (row(norm_ffn[1]), router_w[0].astype(BF16), row(router_b[0]), moe_w1[0].astype(BF16),
             moe_w3[0].astype(BF16), moe_w2[0].astype(BF16), row(norm_final))
    ys = _moe_final(hs, *moe_w, tm=256)
    yp = _moe_final(hp, *moe_w, tm=512)

    n_real = nsb * SEG
    y_prompt = yp.reshape(nb, seq, D_MODEL)
    y_sample = ys[:n_real].reshape(nsb, SEG, D_MODEL)

    def split_state(st, n):
        st = st[:n, 0]
        return (st[:, :S5_STATE].reshape(1, n, S5_GROUPS, S5_P), st[:, S5_STATE:].reshape(1, n, S5_GROUPS, S5_P))

    s5rp, s5ip = split_state(state_p, nb)
    s5rs, s5is = split_state(state_s, nsb)
    kp4 = kp.reshape(nb, seq, N_KV, HEAD_DIM)[:, -WINDOW:][None]
    vp4 = vp.reshape(nb, seq, N_KV, HEAD_DIM)[:, -WINDOW:][None]
    ks4 = ks[:n_real].reshape(1, nsb, SEG, N_KV, HEAD_DIM)
    vs4 = vs[:n_real].reshape(1, nsb, SEG, N_KV, HEAD_DIM)
    return (y_prompt, y_sample, conv_p[None], conv_s[:nsb][None], s5rp, s5ip, s5rs, s5is, kp4, vp4, ks4, vs4)
```

```python
import functools
import math

import numpy as np
import jax
import jax.numpy as jnp
from jax import lax
from jax.experimental import pallas as pl
from jax.experimental.pallas import tpu as pltpu

F32 = jnp.float32
BF16 = jnp.bfloat16

D_MODEL = 1024
CONV_CH = 512
S5_CH = 512
S5_GROUP = 16
S5_GROUPS = 32
S5_P = 64
S5_STATE = S5_GROUPS * S5_P
D_FF = 2816
N_HEADS = 16
N_KV = 2
GQ = N_HEADS // N_KV
HEAD_DIM = 64
KV_DIM = N_KV * HEAD_DIM
WINDOW = 128
CHUNK = 64
N_META = 16
N_BUCKETS = 32
MAX_DIST = 128
N_EXPERTS = 8
D_FF_E = 1024
EPS = 1e-6
NEG = -1e30

SEG = 32
N_SAMPLE_SEG = 32
META_SEG = N_SAMPLE_SEG
N_SMALL_SEG = 40
TM_MIX = 256
SCAN_ROWS = 8
SCAN_LANES = 256
TQ = 128
TQ_PER_STEP = 2
SEG_PER_STEP = 8
VMEM_LIMIT = 56 * 1024 * 1024


def _const_spec(shape):
    nd = len(shape)
    return pl.BlockSpec(shape, lambda *_: (0,) * nd, pipeline_mode=pl.Buffered(1))


def _params(*sem):
    return pltpu.CompilerParams(dimension_semantics=sem, vmem_limit_bytes=VMEM_LIMIT)


def _rms(x, g):
    return x * lax.rsqrt(jnp.mean(x * x, axis=-1, keepdims=True) + EPS) * g


def _dot(a, b):
    return jnp.dot(a, b, preferred_element_type=F32)


def _even_mixer_kernel(x_ref, gm_ref, win_ref, cw_ref, cb_ref, cinit_ref, sinit_ref, sc_ref, bb_ref, cc_ref,
                       d_ref, gluw_ref, glub_ref, wout_ref,
                       h_ref, cout_ref, sout_ref,
                       bu_ref, mix_ref, ccarry_ref, scarry_ref, *, seg, nseg, tiles_per_seq):
    t = pl.program_id(0)
    x = x_ref[...]
    hn = _rms(x, gm_ref[...]).astype(BF16)
    proj = _dot(hn, win_ref[...])
    g_b = proj[:, :CONV_CH]
    cin = proj[:, CONV_CH:2 * CONV_CH] * proj[:, 2 * CONV_CH:3 * CONV_CH]
    u = proj[:, 3 * CONV_CH:]
    bu_ref[...] = _dot(u.astype(BF16), bb_ref[...])
    if tiles_per_seq > 1:
        @pl.when(t % tiles_per_seq == 0)
        def _load_initial_state():
            ccarry_ref[...] = cinit_ref[0]
            scarry_ref[...] = sinit_ref[0]

    cw = cw_ref[...]
    row = lax.broadcasted_iota(jnp.int32, (seg, CONV_CH), 0)
    for j in range(nseg):
        rows = slice(j * seg, (j + 1) * seg)
        c_seg = cin[rows]
        init = ccarry_ref[...] if tiles_per_seq > 1 else cinit_ref[j]
        older, newer = init[0:1], init[1:2]
        p1 = jnp.where(row == 0, newer, pltpu.roll(c_seg, 1, 0))
        p2 = jnp.where(row == 0, older, jnp.where(row == 1, newer, pltpu.roll(c_seg, 2, 0)))
        conv = cw[0:1] * p2 + cw[1:2] * p1 + cw[2:3] * c_seg + cb_ref[...]
        mix_ref[rows, :CONV_CH] = g_b[rows] * conv
        tail = c_seg[seg - 2:seg]
        cout_ref[j] = tail
        if tiles_per_seq > 1:
            ccarry_ref[...] = tail

    groups_per_seg = seg // SCAN_ROWS
    n_groups = nseg * groups_per_seg
    for c in range(S5_STATE // SCAN_LANES):
        re_cols = slice(c * SCAN_LANES, (c + 1) * SCAN_LANES)
        im_cols = slice(S5_STATE + c * SCAN_LANES, S5_STATE + (c + 1) * SCAN_LANES)
        a1r, a1i, a2r, a2i, a4r, a4i, pr, pi = [sc_ref[k, :, re_cols] for k in range(8)]

        def seg_init(s, cols):
            return jnp.broadcast_to(sinit_ref[s, :, cols], (SCAN_ROWS, SCAN_LANES))

        def body(r, carry, re_cols=re_cols, im_cols=im_cols, a1r=a1r, a1i=a1i, a2r=a2r, a2i=a2i, a4r=a4r,
                 a4i=a4i, pr=pr, pi=pi, seg_init=seg_init):
            cr, ci = carry
            if nseg > 1:
                s = r // groups_per_seg
                starts = (r % groups_per_seg) == 0
                cr = jnp.where(starts, seg_init(s, re_cols), cr)
                ci = jnp.where(starts, seg_init(s, im_cols), ci)
            rows = pl.ds(pl.multiple_of(r * SCAN_ROWS, SCAN_ROWS), SCAN_ROWS)
            vr = bu_ref[rows, re_cols]
            vi = bu_ref[rows, im_cols]
            for ar, ai, sh in ((a1r, a1i, 1), (a2r, a2i, 2), (a4r, a4i, 4)):
                sr = pltpu.roll(vr, sh, 0)
                si = pltpu.roll(vi, sh, 0)
                vr, vi = vr + ar * sr - ai * si, vi + ar * si + ai * sr
            vr, vi = vr + pr * cr - pi * ci, vi + pr * ci + pi * cr
            bu_ref[rows, re_cols] = vr
            bu_ref[rows, im_cols] = vi
            if nseg > 1:
                s = r // groups_per_seg
                sout_ref[s, :, re_cols] = vr[SCAN_ROWS - 1:SCAN_ROWS]
                sout_ref[s, :, im_cols] = vi[SCAN_ROWS - 1:SCAN_ROWS]
            return (jnp.broadcast_to(vr[SCAN_ROWS - 1:SCAN_ROWS], (SCAN_ROWS, SCAN_LANES)),
                    jnp.broadcast_to(vi[SCAN_ROWS - 1:SCAN_ROWS], (SCAN_ROWS, SCAN_LANES)))

        if nseg > 1:
            zero = jnp.zeros((SCAN_ROWS, SCAN_LANES), F32)
            lax.fori_loop(0, n_groups, body, (zero, zero))
        else:
            state = scarry_ref if tiles_per_seq > 1 else sinit_ref.at[0]
            cr, ci = lax.fori_loop(0, n_groups, body,
                                   (jnp.broadcast_to(state[:, re_cols], (SCAN_ROWS, SCAN_LANES)),
                                    jnp.broadcast_to(state[:, im_cols], (SCAN_ROWS, SCAN_LANES))))
            scarry_ref[:, re_cols] = cr[0:1]
            scarry_ref[:, im_cols] = ci[0:1]
            sout_ref[0, :, re_cols] = cr[0:1]
            sout_ref[0, :, im_cols] = ci[0:1]

    y = _dot(bu_ref[...].astype(BF16), cc_ref[...]) + d_ref[...] * u
    z = jax.nn.gelu(y)
    mix_ref[:, CONV_CH:] = z * jax.nn.sigmoid(_dot(z.astype(BF16), gluw_ref[...]) + glub_ref[...])
    h_ref[...] = x + _dot(mix_ref[...].astype(BF16), wout_ref[...])


def _even_mixer(x, gm, w_in, cw, cb, cinit, sinit, scan_consts, bb, cc, d, glu_w, glu_b, w_out, *, seg, nseg,
                tiles_per_seq):
    rows = x.shape[0]
    tm = seg * nseg
    nseq = cinit.shape[0]
    kern = functools.partial(_even_mixer_kernel, seg=seg, nseg=nseg, tiles_per_seq=tiles_per_seq)
    return pl.pallas_call(
        kern,
        grid=(rows // tm,),
        in_specs=[
            pl.BlockSpec((tm, D_MODEL), lambda t: (t, 0)),
            _const_spec((1, D_MODEL)),
            _const_spec((D_MODEL, 4 * CONV_CH)),
            _const_spec((3, CONV_CH)),
            _const_spec((1, CONV_CH)),
            pl.BlockSpec((nseg, 2, CONV_CH), lambda t: (t // tiles_per_seq, 0, 0)),
            pl.BlockSpec((nseg, 1, 2 * S5_STATE), lambda t: (t // tiles_per_seq, 0, 0)),
            _const_spec((8, SCAN_ROWS, S5_STATE)),
            _const_spec((S5_CH, 2 * S5_STATE)),
            _const_spec((2 * S5_STATE, S5_CH)),
            _const_spec((1, S5_CH)),
            _const_spec((S5_CH, S5_CH)),
            _const_spec((1, S5_CH)),
            _const_spec((D_MODEL, D_MODEL)),
        ],
        out_specs=[
            pl.BlockSpec((tm, D_MODEL), lambda t: (t, 0)),
            pl.BlockSpec((nseg, 2, CONV_CH), lambda t: (t // tiles_per_seq, 0, 0)),
            pl.BlockSpec((nseg, 1, 2 * S5_STATE), lambda t: (t // tiles_per_seq, 0, 0)),
        ],
        out_shape=[
            jax.ShapeDtypeStruct((rows, D_MODEL), F32),
            jax.ShapeDtypeStruct((nseq, 2, CONV_CH), F32),
            jax.ShapeDtypeStruct((nseq, 1, 2 * S5_STATE), F32),
        ],
        scratch_shapes=[
            pltpu.VMEM((tm, 2 * S5_STATE), F32),
            pltpu.VMEM((tm, D_MODEL), F32),
            pltpu.VMEM((2, CONV_CH), F32),
            pltpu.VMEM((1, 2 * S5_STATE), F32),
        ],
        compiler_params=_params("arbitrary"),
        name="even_mixer",
    )(x, gm, w_in, cw, cb, cinit, sinit, scan_consts, bb, cc, d, glu_w, glu_b, w_out)


def _ffn_kernel(x_ref, g_ref, w1_ref, w3_ref, w2_ref, o_ref):
    x = x_ref[...]
    hn = _rms(x, g_ref[...]).astype(BF16)
    a = _dot(hn, w1_ref[...])
    b = _dot(hn, w3_ref[...])
    o_ref[...] = x + _dot((a * jax.nn.sigmoid(a) * b).astype(BF16), w2_ref[...])


def _ffn(x, g, w1, w3, w2, *, tm):
    rows = x.shape[0]
    return pl.pallas_call(
        _ffn_kernel,
        grid=(rows // tm,),
        in_specs=[
            pl.BlockSpec((tm, D_MODEL), lambda t: (t, 0)),
            _const_spec((1, D_MODEL)),
            _const_spec((D_MODEL, D_FF)),
            _const_spec((D_MODEL, D_FF)),
            _const_spec((D_FF, D_MODEL)),
        ],
        out_specs=pl.BlockSpec((tm, D_MODEL), lambda t: (t, 0)),
        out_shape=jax.ShapeDtypeStruct((rows, D_MODEL), F32),
        compiler_params=_params("parallel"),
        name="ffn",
    )(x, g, w1, w3, w2)


def _qkv_kernel(x_ref, g_ref, w_ref, b_ref, q_ref, k_ref, v_ref):
    hn = _rms(x_ref[...], g_ref[...]).astype(BF16)
    qkv = _dot(hn, w_ref[...]) + b_ref[...]
    nq = N_HEADS * HEAD_DIM
    q_ref[...] = (qkv[:, :nq] * (HEAD_DIM ** -0.5)).astype(BF16)
    k_ref[...] = qkv[:, nq:nq + KV_DIM]
    v_ref[...] = qkv[:, nq + KV_DIM:]


def _qkv(x, g, w, b, *, tm):
    rows = x.shape[0]
    ncol = (N_HEADS + 2 * N_KV) * HEAD_DIM
    return pl.pallas_call(
        _qkv_kernel,
        grid=(rows // tm,),
        in_specs=[
            pl.BlockSpec((tm, D_MODEL), lambda t: (t, 0)),
            _const_spec((1, D_MODEL)),
            _const_spec((D_MODEL, ncol)),
            _const_spec((1, ncol)),
        ],
        out_specs=[
            pl.BlockSpec((tm, N_HEADS * HEAD_DIM), lambda t: (t, 0)),
            pl.BlockSpec((tm, KV_DIM), lambda t: (t, 0)),
            pl.BlockSpec((tm, KV_DIM), lambda t: (t, 0)),
        ],
        out_shape=[
            jax.ShapeDtypeStruct((rows, N_HEADS * HEAD_DIM), BF16),
            jax.ShapeDtypeStruct((rows, KV_DIM), F32),
            jax.ShapeDtypeStruct((rows, KV_DIM), F32),
        ],
        compiler_params=_params("parallel"),
        name="qkv",
    )(x, g, w, b)


def _attend_groups(q, keys, vals, bias_of_group, sink_of_group, o_scr, row0):
    nq = q.shape[0]
    for kv in range(N_KV):
        kv_cols = slice(kv * HEAD_DIM, (kv + 1) * HEAD_DIM)
        heads = [slice((kv * GQ + g) * HEAD_DIM, (kv * GQ + g + 1) * HEAD_DIM) for g in range(GQ)]
        qs = jnp.concatenate([q[:, cols] for cols in heads], axis=0)
        st = lax.dot_general(keys[:, kv_cols], qs, (((1,), (1,)), ((), ())),
                             preferred_element_type=F32) + bias_of_group(kv)
        sk = sink_of_group(kv)
        m = jnp.maximum(jnp.max(st, axis=0, keepdims=True), sk)
        p = jnp.exp(st - m)
        den = jnp.sum(p, axis=0, keepdims=True) + jnp.exp(sk - m)
        o = lax.dot_general((p * (1.0 / den)).astype(BF16), vals[:, kv_cols], (((0,), (0,)), ((), ())),
                            preferred_element_type=F32)
        for g, cols in enumerate(heads):
            o_scr[row0:row0 + nq, cols] = o[g * nq:(g + 1) * nq]


def _attn_prompt_kernel(q_ref, k_ref, v_ref, bias_ref, sink_ref, x_ref, wo_ref, bo_ref, o_ref, o_scr):
    t = pl.program_id(1)
    for u in range(TQ_PER_STEP):
        tile = t * TQ_PER_STEP + u
        start = pl.multiple_of(tile * TQ, TQ)
        keys = k_ref[0, pl.ds(start, TQ + WINDOW), :]
        vals = v_ref[0, pl.ds(start, TQ + WINDOW), :]
        variant = jnp.minimum(tile, 1)
        _attend_groups(q_ref[u * TQ:(u + 1) * TQ], keys, vals, lambda kv: bias_ref[variant, kv],
                       lambda kv: sink_ref[kv], o_scr, u * TQ)
    o_ref[...] = x_ref[...] + _dot(o_scr[...].astype(BF16), wo_ref[...]) + bo_ref[...]


def _attn_prompt(q, kpad, vpad, bias, sink, x, w_o, b_o):
    nb, lpad = kpad.shape[0], kpad.shape[1]
    tm = TQ_PER_STEP * TQ
    steps = (lpad - WINDOW) // tm
    return pl.pallas_call(
        _attn_prompt_kernel,
        grid=(nb, steps),
        in_specs=[
            pl.BlockSpec((tm, D_MODEL), lambda b, t: (b * steps + t, 0)),
            pl.BlockSpec((1, lpad, KV_DIM), lambda b, t: (b, 0, 0)),
            pl.BlockSpec((1, lpad, KV_DIM), lambda b, t: (b, 0, 0)),
            _const_spec((2, N_KV, TQ + WINDOW, GQ * TQ)),
            _const_spec((N_KV, 1, GQ * TQ)),
            pl.BlockSpec((tm, D_MODEL), lambda b, t: (b * steps + t, 0)),
            _const_spec((D_MODEL, D_MODEL)),
            _const_spec((1, D_MODEL)),
        ],
        out_specs=pl.BlockSpec((tm, D_MODEL), lambda b, t: (b * steps + t, 0)),
        out_shape=jax.ShapeDtypeStruct(x.shape, F32),
        scratch_shapes=[pltpu.VMEM((tm, D_MODEL), F32)],
        compiler_params=_params("parallel", "arbitrary"),
        name="attn_prompt",
    )(q, kpad, vpad, bias, sink, x, w_o, b_o)


def _attn_small_kernel(q_ref, kc_ref, vc_ref, kn_ref, vn_ref, bias_ref, sink_ref, x_ref, wo_ref, bo_ref,
                       o_ref, o_scr):
    step = pl.program_id(0)
    for j in range(SEG_PER_STEP):
        rows = slice(j * SEG, (j + 1) * SEG)
        keys = jnp.concatenate([kc_ref[j], kn_ref[rows]], axis=0).astype(BF16)
        vals = jnp.concatenate([vc_ref[j], vn_ref[rows]], axis=0).astype(BF16)
        variant = jnp.where(step * SEG_PER_STEP + j == META_SEG, 1, 0)
        _attend_groups(q_ref[rows], keys, vals, lambda kv: bias_ref[variant, kv], lambda kv: sink_ref[kv], o_scr,
                       j * SEG)
    o_ref[...] = x_ref[...] + _dot(o_scr[...].astype(BF16), wo_ref[...]) + bo_ref[...]


def _attn_small(q, kc, vc, kn, vn, bias, sink, x, w_o, b_o):
    tm = SEG_PER_STEP * SEG
    steps = x.shape[0] // tm
    cache_blocks = kc.shape[0] // SEG_PER_STEP
    cache_spec = pl.BlockSpec((SEG_PER_STEP, WINDOW, KV_DIM), lambda s: (jnp.minimum(s, cache_blocks - 1), 0, 0))
    return pl.pallas_call(
        _attn_small_kernel,
        grid=(steps,),
        in_specs=[
            pl.BlockSpec((tm, D_MODEL), lambda s: (s, 0)),
            cache_spec,
            cache_spec,
            pl.BlockSpec((tm, KV_DIM), lambda s: (s, 0)),
            pl.BlockSpec((tm, KV_DIM), lambda s: (s, 0)),
            _const_spec((2, N_KV, WINDOW + SEG, GQ * SEG)),
            _const_spec((N_KV, 1, GQ * SEG)),
            pl.BlockSpec((tm, D_MODEL), lambda s: (s, 0)),
            _const_spec((D_MODEL, D_MODEL)),
            _const_spec((1, D_MODEL)),
        ],
        out_specs=pl.BlockSpec((tm, D_MODEL), lambda s: (s, 0)),
        out_shape=jax.ShapeDtypeStruct(x.shape, F32),
        scratch_shapes=[pltpu.VMEM((tm, D_MODEL), F32)],
        compiler_params=_params("parallel"),
        name="attn_small",
    )(q, kc, vc, kn, vn, bias, sink, x, w_o, b_o)


def _moe_kernel(x_ref, g_ref, rw_ref, rb_ref, w1_ref, w3_ref, w2_ref, gf_ref, o_ref, hn_scr, gate_scr, acc_scr):
    e = pl.program_id(1)
    lane = lax.broadcasted_iota(jnp.int32, gate_scr.shape, 1).astype(F32)

    @pl.when(e == 0)
    def _route():
        hn = _rms(x_ref[...], g_ref[...]).astype(BF16)
        hn_scr[...] = hn
        logits = _dot(hn, rw_ref[...]) + rb_ref[...]
        v1 = jnp.max(logits, axis=-1, keepdims=True)
        i1 = jnp.min(jnp.where(logits == v1, lane, float(N_EXPERTS)), axis=-1, keepdims=True)
        rest = jnp.where(lane == i1, -jnp.inf, logits)
        v2 = jnp.max(rest, axis=-1, keepdims=True)
        i2 = jnp.min(jnp.where(rest == v2, lane, float(N_EXPERTS)), axis=-1, keepdims=True)
        e2 = jnp.exp(v2 - v1)
        den = 1.0 + e2
        gate_scr[...] = jnp.where(lane == i1, 1.0 / den, 0.0) + jnp.where(lane == i2, e2 / den, 0.0)
        acc_scr[...] = jnp.zeros_like(acc_scr)

    hn = hn_scr[...]
    a = _dot(hn, w1_ref[0])
    b = _dot(hn, w3_ref[0])
    y = _dot((a * jax.nn.sigmoid(a) * b).astype(BF16), w2_ref[0])
    gate = jnp.sum(jnp.where(lane == e.astype(F32), gate_scr[...], 0.0), axis=-1, keepdims=True)
    acc_scr[...] += gate * y

    @pl.when(e == N_EXPERTS - 1)
    def _finish():
        o_ref[...] = _rms(x_ref[...] + acc_scr[...], gf_ref[...])


def _moe_final(x, g, rw, rb, w1, w3, w2, gf, *, tm):
    rows = x.shape[0]
    return pl.pallas_call(
        _moe_kernel,
        grid=(rows // tm, N_EXPERTS),
        in_specs=[
            pl.BlockSpec((tm, D_MODEL), lambda t, e: (t, 0)),
            _const_spec((1, D_MODEL)),
            _const_spec((D_MODEL, N_EXPERTS)),
            _const_spec((1, N_EXPERTS)),
            pl.BlockSpec((1, D_MODEL, D_FF_E), lambda t, e: (e, 0, 0)),
            pl.BlockSpec((1, D_MODEL, D_FF_E), lambda t, e: (e, 0, 0)),
            pl.BlockSpec((1, D_FF_E, D_MODEL), lambda t, e: (e, 0, 0)),
            _const_spec((1, D_MODEL)),
        ],
        out_specs=pl.BlockSpec((tm, D_MODEL), lambda t, e: (t, 0)),
        out_shape=jax.ShapeDtypeStruct((rows, D_MODEL), F32),
        scratch_shapes=[
            pltpu.VMEM((tm, D_MODEL), BF16),
            pltpu.VMEM((tm, N_EXPERTS), F32),
            pltpu.VMEM((tm, D_MODEL), F32),
        ],
        compiler_params=_params("parallel", "arbitrary"),
        name="moe_final",
    )(x, g, rw, rb, w1, w3, w2, gf)


def _t5_bucket_np(rel):
    half = N_BUCKETS // 2
    max_exact = half // 2
    ret = np.where(rel > 0, half, 0)
    n = np.abs(rel)
    nf = np.maximum(n, 1).astype(np.float32)
    large = max_exact + (np.log(nf / np.float32(max_exact)) / np.float32(math.log(MAX_DIST / max_exact))
                         * np.float32(half - max_exact)).astype(np.int32)
    large = np.minimum(large, half - 1)
    return ret + np.where(n < max_exact, n, large)


def _bias_variants(table, bucket, keeps):
    nq, nk = bucket.shape
    onehot = jnp.asarray(np.eye(N_BUCKETS, dtype=np.float32)[bucket])
    bias = jnp.einsum('rjb,bh->hjr', onehot, table, precision=lax.Precision.HIGHEST)
    bias = jnp.stack([jnp.where(keep.T[None], bias, NEG) for keep in keeps], axis=0)
    bias = bias.reshape(len(keeps), N_KV, GQ, nk, nq)
    return jnp.transpose(bias, (0, 1, 3, 2, 4)).reshape(len(keeps), N_KV, nk, GQ * nq)


def _lane_sink(sink, nq):
    return jnp.broadcast_to(sink.reshape(N_KV, 1, GQ, 1), (N_KV, 1, GQ, nq)).reshape(N_KV, 1, GQ * nq)


def _prompt_bias(table):
    r = np.arange(TQ)[:, None]
    j = np.arange(TQ + WINDOW)[None, :]
    band = (j - CHUNK * (r // CHUNK) >= 0) & (j - CHUNK * (r // CHUNK) < WINDOW + CHUNK)
    first = band & (j >= WINDOW - N_META)
    return _bias_variants(table, _t5_bucket_np(j - WINDOW - r), [first, band])


def _small_bias(table):
    i = np.arange(SEG)[:, None]
    m = np.arange(WINDOW + SEG)[None, :]
    everything = np.ones((SEG, WINDOW + SEG), bool)
    meta_only = everything & (m >= WINDOW + SEG - N_META)
    return _bias_variants(table, _t5_bucket_np(m - WINDOW - i), [everything, meta_only])


def _cmul(ar, ai, br, bi):
    return ar * br - ai * bi, ar * bi + ai * br


def _s5_prepare(a_re, a_im, log_dt, b_re, b_im, c_re, c_im):
    dt = jnp.exp(log_dt)[:, None]
    mag = jnp.exp(a_re * dt)
    ab_re, ab_im = mag * jnp.cos(a_im * dt), mag * jnp.sin(a_im * dt)
    den = a_re * a_re + a_im * a_im
    num_re = ab_re - 1.0
    cf_re = (num_re * a_re + ab_im * a_im) / den
    cf_im = (ab_im * a_re - num_re * a_im) / den
    bb_re = cf_re[..., None] * b_re - cf_im[..., None] * b_im
    bb_im = cf_re[..., None] * b_im + cf_im[..., None] * b_re
    eye = jnp.eye(S5_GROUPS, dtype=F32)

    def in_map(bb):
        return jnp.einsum('gpj,gh->gjhp', bb, eye).reshape(S5_CH, S5_STATE)

    def out_map(c):
        return jnp.einsum('gjp,gh->gphj', c, eye).reshape(S5_STATE, S5_CH)

    bb = jnp.concatenate([in_map(bb_re), in_map(bb_im)], axis=1).astype(BF16)
    cc = jnp.concatenate([out_map(c_re), -out_map(c_im)], axis=0).astype(BF16)

    ar, ai = ab_re.reshape(1, S5_STATE), ab_im.reshape(1, S5_STATE)
    powers = [(ar, ai)]
    for _ in range(SCAN_ROWS - 1):
        powers.append(_cmul(*powers[-1], ar, ai))
    row = np.arange(SCAN_ROWS)[:, None]

    def shifted(k):
        return [jnp.where(row >= k, p, 0.0) for p in powers[k - 1]]

    carry = [jnp.concatenate([p[i] for p in powers], axis=0) for i in range(2)]
    consts = jnp.stack(shifted(1) + shifted(2) + shifted(4) + carry, axis=0)
    return bb, cc, consts


def kernel(x_prompt, x_sample, cache_conv, state_s5_re, state_s5_im, cache_swa_k, cache_swa_v, meta_tokens, rel_bias_table, norm_mix, norm_ffn, norm_final, w_in0, conv_w, conv_b, s5_a_re, s5_a_im, s5_log_dt, s5_b_re, s5_b_im, s5_c_re, s5_c_im, s5_d, s5_glu_w, s5_glu_b, w_out0, ffn_w1, ffn_w3, ffn_w2, w_qkv, b_qkv, attn_sink, w_o, b_o, router_w, router_b, moe_w1, moe_w3, moe_w2):
    nb, seq = x_prompt.shape[0], x_prompt.shape[1]
    nsb = x_sample.shape[0]
    assert x_sample.shape[1] == SEG and nsb == N_SAMPLE_SEG and seq % TM_MIX == 0

    def row(v):
        return v.reshape(1, -1)

    xp = x_prompt.reshape(nb * seq, D_MODEL)
    meta_seg = jnp.concatenate([jnp.zeros((SEG - N_META, D_MODEL), F32), meta_tokens], axis=0)
    filler = jnp.zeros(((N_SMALL_SEG - N_SAMPLE_SEG - 1) * SEG, D_MODEL), F32)
    xs = jnp.concatenate([x_sample.reshape(nsb * SEG, D_MODEL), meta_seg, filler], axis=0)
    n_extra = N_SMALL_SEG - N_SAMPLE_SEG

    bb, cc, consts = _s5_prepare(s5_a_re[0], s5_a_im[0], s5_log_dt[0], s5_b_re[0], s5_b_im[0], s5_c_re[0],
                                 s5_c_im[0])
    mixer_w = (row(norm_mix[0]), w_in0[0].astype(BF16), conv_w[0], row(conv_b[0]))
    mixer_w2 = (consts, bb, cc, row(s5_d[0]), s5_glu_w[0].astype(BF16), row(s5_glu_b[0]), w_out0[0].astype(BF16))
    cinit_s = jnp.concatenate([cache_conv[0], jnp.zeros((n_extra, 2, CONV_CH), F32)], axis=0)
    sinit_s = jnp.concatenate([state_s5_re[0].reshape(nsb, 1, S5_STATE), state_s5_im[0].reshape(nsb, 1, S5_STATE)],
                              axis=-1)
    sinit_s = jnp.concatenate([sinit_s, jnp.zeros((n_extra, 1, 2 * S5_STATE), F32)], axis=0)
    hs, conv_s, state_s = _even_mixer(xs, *mixer_w, cinit_s, sinit_s, *mixer_w2, seg=SEG, nseg=TM_MIX // SEG,
                                      tiles_per_seq=1)
    cinit_p = jnp.broadcast_to(conv_s[META_SEG][None], (nb, 2, CONV_CH))
    sinit_p = jnp.broadcast_to(state_s[META_SEG][None], (nb, 1, 2 * S5_STATE))
    hp, conv_p, state_p = _even_mixer(xp, *mixer_w, cinit_p, sinit_p, *mixer_w2, seg=TM_MIX, nseg=1,
                                      tiles_per_seq=seq // TM_MIX)
    ffn_w = (row(norm_ffn[0]), ffn_w1[0].astype(BF16), ffn_w3[0].astype(BF16), ffn_w2[0].astype(BF16))
    hs = _ffn(hs, *ffn_w, tm=256)
    hp = _ffn(hp, *ffn_w, tm=512)

    qkv_w = (row(norm_mix[1]), w_qkv[0].astype(BF16), row(b_qkv[0]))
    qs, ks, vs = _qkv(hs, *qkv_w, tm=256)
    qp, kp, vp = _qkv(hp, *qkv_w, tm=512)
    wo = (w_o[0].astype(BF16), row(b_o[0]))
    hs = _attn_small(qs, cache_swa_k[0].reshape(nsb, WINDOW, KV_DIM), cache_swa_v[0].reshape(nsb, WINDOW, KV_DIM),
                     ks, vs, _small_bias(rel_bias_table), _lane_sink(attn_sink[0], SEG), hs, *wo)

    def padded(new, small):
        meta_rows = small[META_SEG * SEG:(META_SEG + 1) * SEG]
        front = jnp.concatenate([jnp.zeros((WINDOW - SEG, KV_DIM), F32), meta_rows], axis=0)
        front = jnp.broadcast_to(front[None], (nb, WINDOW, KV_DIM))
        return jnp.concatenate([front, new.reshape(nb, seq, KV_DIM)], axis=1).astype(BF16)

    hp = _attn_prompt(qp, padded(kp, ks), padded(vp, vs), _prompt_bias(rel_bias_table),
                      _lane_sink(attn_sink[0], TQ), hp, *wo)
    moe_w = (row(norm_ffn[1]), router_w[0].astype(BF16), row(router_b[0]), moe_w1[0].astype(BF16),
             moe_w3[0].astype(BF16), moe_w2[0].astype(BF16), row(norm_final))
    ys = _moe_final(hs, *moe_w, tm=256)
    yp = _moe_final(hp, *moe_w, tm=512)

    n_real = nsb * SEG
    y_prompt = yp.reshape(nb, seq, D_MODEL)
    y_sample = ys[:n_real].reshape(nsb, SEG, D_MODEL)

    def split_state(st, n):
        st = st[:n, 0]
        return (st[:, :S5_STATE].reshape(1, n, S5_GROUPS, S5_P), st[:, S5_STATE:].reshape(1, n, S5_GROUPS, S5_P))

    s5rp, s5ip = split_state(state_p, nb)
    s5rs, s5is = split_state(state_s, nsb)
    kp4 = kp.reshape(nb, seq, N_KV, HEAD_DIM)[:, -WINDOW:][None]
    vp4 = vp.reshape(nb, seq, N_KV, HEAD_DIM)[:, -WINDOW:][None]
    ks4 = ks[:n_real].reshape(1, nsb, SEG, N_KV, HEAD_DIM)
    vs4 = vs[:n_real].reshape(1, nsb, SEG, N_KV, HEAD_DIM)
    return (y_prompt, y_sample, conv_p[None], conv_s[:nsb][None], s5rp, s5ip, s5rs, s5is, kp4, vp4, ks4, vs4)
```

```python
import functools
import math

import numpy as np
import jax
import jax.numpy as jnp
from jax import lax
from jax.experimental import pallas as pl
from jax.experimental.pallas import tpu as pltpu

F32 = jnp.float32
BF16 = jnp.bfloat16

D_MODEL = 1024
CONV_CH = 512
S5_CH = 512
S5_GROUP = 16
S5_GROUPS = 32
S5_P = 64
S5_STATE = S5_GROUPS * S5_P
D_FF = 2816
N_HEADS = 16
N_KV = 2
GQ = N_HEADS // N_KV
HEAD_DIM = 64
KV_DIM = N_KV * HEAD_DIM
WINDOW = 128
CHUNK = 64
N_META = 16
N_BUCKETS = 32
MAX_DIST = 128
N_EXPERTS = 8
D_FF_E = 1024
EPS = 1e-6
NEG = -1e30

SEG = 32
N_SAMPLE_SEG = 32
META_SEG = N_SAMPLE_SEG
N_SMALL_SEG = 40
TM_MIX = 256
SCAN_ROWS = 8
SCAN_LANES = 256
TQ = 128
TQ_PER_STEP = 2
SEG_PER_STEP = 8
VMEM_LIMIT = 56 * 1024 * 1024


def _const_spec(shape):
    nd = len(shape)
    return pl.BlockSpec(shape, lambda *_: (0,) * nd, pipeline_mode=pl.Buffered(1))


def _params(*sem):
    return pltpu.CompilerParams(dimension_semantics=sem, vmem_limit_bytes=VMEM_LIMIT)


def _rms(x, g):
    return x * lax.rsqrt(jnp.mean(x * x, axis=-1, keepdims=True) + EPS) * g


def _dot(a, b):
    return jnp.dot(a, b, preferred_element_type=F32)


def _even_mixer_kernel(x_ref, gm_ref, win_ref, cw_ref, cb_ref, cinit_ref, sinit_ref, sc_ref, bb_ref, cc_ref,
                       d_ref, gluw_ref, glub_ref, wout_ref,
                       h_ref, cout_ref, sout_ref,
                       bu_ref, mix_ref, ccarry_ref, scarry_ref, *, seg, nseg, tiles_per_seq):
    t = pl.program_id(0)
    x = x_ref[...]
    hn = _rms(x, gm_ref[...]).astype(BF16)
    proj = _dot(hn, win_ref[...])
    g_b = proj[:, :CONV_CH]
    cin = proj[:, CONV_CH:2 * CONV_CH] * proj[:, 2 * CONV_CH:3 * CONV_CH]
    u = proj[:, 3 * CONV_CH:]
    bu_ref[...] = _dot(u.astype(BF16), bb_ref[...])
    if tiles_per_seq > 1:
        @pl.when(t % tiles_per_seq == 0)
        def _load_initial_state():
            ccarry_ref[...] = cinit_ref[0]
            scarry_ref[...] = sinit_ref[0]

    cw = cw_ref[...]
    row = lax.broadcasted_iota(jnp.int32, (seg, CONV_CH), 0)
    for j in range(nseg):
        rows = slice(j * seg, (j + 1) * seg)
        c_seg = cin[rows]
        init = ccarry_ref[...] if tiles_per_seq > 1 else cinit_ref[j]
        older, newer = init[0:1], init[1:2]
        p1 = jnp.where(row == 0, newer, pltpu.roll(c_seg, 1, 0))
        p2 = jnp.where(row == 0, older, jnp.where(row == 1, newer, pltpu.roll(c_seg, 2, 0)))
        conv = cw[0:1] * p2 + cw[1:2] * p1 + cw[2:3] * c_seg + cb_ref[...]
        mix_ref[rows, :CONV_CH] = g_b[rows] * conv
        tail = c_seg[seg - 2:seg]
        cout_ref[j] = tail
        if tiles_per_seq > 1:
            ccarry_ref[...] = tail

    groups_per_seg = seg // SCAN_ROWS
    n_groups = nseg * groups_per_seg
    for c in range(S5_STATE // SCAN_LANES):
        re_cols = slice(c * SCAN_LANES, (c + 1) * SCAN_LANES)
        im_cols = slice(S5_STATE + c * SCAN_LANES, S5_STATE + (c + 1) * SCAN_LANES)
        a1r, a1i, a2r, a2i, a4r, a4i, pr, pi = [sc_ref[k, :, re_cols] for k in range(8)]

        def seg_init(s, cols):
            return jnp.broadcast_to(sinit_ref[s, :, cols], (SCAN_ROWS, SCAN_LANES))

        def body(r, carry, re_cols=re_cols, im_cols=im_cols, a1r=a1r, a1i=a1i, a2r=a2r, a2i=a2i, a4r=a4r,
                 a4i=a4i, pr=pr, pi=pi, seg_init=seg_init):
            cr, ci = carry
            if nseg > 1:
                s = r // groups_per_seg
                starts = (r % groups_per_seg) == 0
                cr = jnp.where(starts, seg_init(s, re_cols), cr)
                ci = jnp.where(starts, seg_init(s, im_cols), ci)
            rows = pl.ds(pl.multiple_of(r * SCAN_ROWS, SCAN_ROWS), SCAN_ROWS)
            vr = bu_ref[rows, re_cols]
            vi = bu_ref[rows, im_cols]
            for ar, ai, sh in ((a1r, a1i, 1), (a2r, a2i, 2), (a4r, a4i, 4)):
                sr = pltpu.roll(vr, sh, 0)
                si = pltpu.roll(vi, sh, 0)
                vr, vi = vr + ar * sr - ai * si, vi + ar * si + ai * sr
            vr, vi = vr + pr * cr - pi * ci, vi + pr * ci + pi * cr
            bu_ref[rows, re_cols] = vr
            bu_ref[rows, im_cols] = vi
            if nseg > 1:
                s = r // groups_per_seg
                sout_ref[s, :, re_cols] = vr[SCAN_ROWS - 1:SCAN_ROWS]
                sout_ref[s, :, im_cols] = vi[SCAN_ROWS - 1:SCAN_ROWS]
            return (jnp.broadcast_to(vr[SCAN_ROWS - 1:SCAN_ROWS], (SCAN_ROWS, SCAN_LANES)),
                    jnp.broadcast_to(vi[SCAN_ROWS - 1:SCAN_ROWS], (SCAN_ROWS, SCAN_LANES)))

        if nseg > 1:
            zero = jnp.zeros((SCAN_ROWS, SCAN_LANES), F32)
            lax.fori_loop(0, n_groups, body, (zero, zero))
        else:
            state = scarry_ref if tiles_per_seq > 1 else sinit_ref.at[0]
            cr, ci = lax.fori_loop(0, n_groups, body,
                                   (jnp.broadcast_to(state[:, re_cols], (SCAN_ROWS, SCAN_LANES)),
                                    jnp.broadcast_to(state[:, im_cols], (SCAN_ROWS, SCAN_LANES))))
            scarry_ref[:, re_cols] = cr[0:1]
            scarry_ref[:, im_cols] = ci[0:1]
            sout_ref[0, :, re_cols] = cr[0:1]
            sout_ref[0, :, im_cols] = ci[0:1]

    y = _dot(bu_ref[...].astype(BF16), cc_ref[...]) + d_ref[...] * u
    z = jax.nn.gelu(y)
    mix_ref[:, CONV_CH:] = z * jax.nn.sigmoid(_dot(z.astype(BF16), gluw_ref[...]) + glub_ref[...])
    h_ref[...] = x + _dot(mix_ref[...].astype(BF16), wout_ref[...])


def _even_mixer(x, gm, w_in, cw, cb, cinit, sinit, scan_consts, bb, cc, d, glu_w, glu_b, w_out, *, seg, nseg,
                tiles_per_seq):
    rows = x.shape[0]
    tm = seg * nseg
    nseq = cinit.shape[0]
    kern = functools.partial(_even_mixer_kernel, seg=seg, nseg=nseg, tiles_per_seq=tiles_per_seq)
    return pl.pallas_call(
        kern,
        grid=(rows // tm,),
        in_specs=[
            pl.BlockSpec((tm, D_MODEL), lambda t: (t, 0)),
            _const_spec((1, D_MODEL)),
            _const_spec((D_MODEL, 4 * CONV_CH)),
            _const_spec((3, CONV_CH)),
            _const_spec((1, CONV_CH)),
            pl.BlockSpec((nseg, 2, CONV_CH), lambda t: (t // tiles_per_seq, 0, 0)),
            pl.BlockSpec((nseg, 1, 2 * S5_STATE), lambda t: (t // tiles_per_seq, 0, 0)),
            _const_spec((8, SCAN_ROWS, S5_STATE)),
            _const_spec((S5_CH, 2 * S5_STATE)),
            _const_spec((2 * S5_STATE, S5_CH)),
            _const_spec((1, S5_CH)),
            _const_spec((S5_CH, S5_CH)),
            _const_spec((1, S5_CH)),
            _const_spec((D_MODEL, D_MODEL)),
        ],
        out_specs=[
            pl.BlockSpec((tm, D_MODEL), lambda t: (t, 0)),
            pl.BlockSpec((nseg, 2, CONV_CH), lambda t: (t // tiles_per_seq, 0, 0)),
            pl.BlockSpec((nseg, 1, 2 * S5_STATE), lambda t: (t // tiles_per_seq, 0, 0)),
        ],
        out_shape=[
            jax.ShapeDtypeStruct((rows, D_MODEL), F32),
            jax.ShapeDtypeStruct((nseq, 2, CONV_CH), F32),
            jax.ShapeDtypeStruct((nseq, 1, 2 * S5_STATE), F32),
        ],
        scratch_shapes=[
            pltpu.VMEM((tm, 2 * S5_STATE), F32),
            pltpu.VMEM((tm, D_MODEL), F32),
            pltpu.VMEM((2, CONV_CH), F32),
            pltpu.VMEM((1, 2 * S5_STATE), F32),
        ],
        compiler_params=_params("arbitrary"),
        name="even_mixer",
    )(x, gm, w_in, cw, cb, cinit, sinit, scan_consts, bb, cc, d, glu_w, glu_b, w_out)


def _ffn_kernel(x_ref, g_ref, w1_ref, w3_ref, w2_ref, o_ref):
    x = x_ref[...]
    hn = _rms(x, g_ref[...]).astype(BF16)
    a = _dot(hn, w1_ref[...])
    b = _dot(hn, w3_ref[...])
    o_ref[...] = x + _dot((a * jax.nn.sigmoid(a) * b).astype(BF16), w2_ref[...])


def _ffn(x, g, w1, w3, w2, *, tm):
    rows = x.shape[0]
    return pl.pallas_call(
        _ffn_kernel,
        grid=(rows // tm,),
        in_specs=[
            pl.BlockSpec((tm, D_MODEL), lambda t: (t, 0)),
            _const_spec((1, D_MODEL)),
            _const_spec((D_MODEL, D_FF)),
            _const_spec((D_MODEL, D_FF)),
            _const_spec((D_FF, D_MODEL)),
        ],
        out_specs=pl.BlockSpec((tm, D_MODEL), lambda t: (t, 0)),
        out_shape=jax.ShapeDtypeStruct((rows, D_MODEL), F32),
        compiler_params=_params("parallel"),
        name="ffn",
    )(x, g, w1, w3, w2)


def _qkv_kernel(x_ref, g_ref, w_ref, b_ref, q_ref, k_ref, v_ref):
    hn = _rms(x_ref[...], g_ref[...]).astype(BF16)
    qkv = _dot(hn, w_ref[...]) + b_ref[...]
    nq = N_HEADS * HEAD_DIM
    q_ref[...] = (qkv[:, :nq] * (HEAD_DIM ** -0.5)).astype(BF16)
    k_ref[...] = qkv[:, nq:nq + KV_DIM]
    v_ref[...] = qkv[:, nq + KV_DIM:]


def _qkv(x, g, w, b, *, tm):
    rows = x.shape[0]
    ncol = (N_HEADS + 2 * N_KV) * HEAD_DIM
    return pl.pallas_call(
        _qkv_kernel,
        grid=(rows // tm,),
        in_specs=[
            pl.BlockSpec((tm, D_MODEL), lambda t: (t, 0)),
            _const_spec((1, D_MODEL)),
            _const_spec((D_MODEL, ncol)),
            _const_spec((1, ncol)),
        ],
        out_specs=[
            pl.BlockSpec((tm, N_HEADS * HEAD_DIM), lambda t: (t, 0)),
            pl.BlockSpec((tm, KV_DIM), lambda t: (t, 0)),
            pl.BlockSpec((tm, KV_DIM), lambda t: (t, 0)),
        ],
        out_shape=[
            jax.ShapeDtypeStruct((rows, N_HEADS * HEAD_DIM), BF16),
            jax.ShapeDtypeStruct((rows, KV_DIM), F32),
            jax.ShapeDtypeStruct((rows, KV_DIM), F32),
        ],
        compiler_params=_params("parallel"),
        name="qkv",
    )(x, g, w, b)


def _attend_groups(q, keys, vals, bias_of_group, sink_of_group, o_scr, row0):
    nq = q.shape[0]
    for kv in range(N_KV):
        kv_cols = slice(kv * HEAD_DIM, (kv + 1) * HEAD_DIM)
        heads = [slice((kv * GQ + g) * HEAD_DIM, (kv * GQ + g + 1) * HEAD_DIM) for g in range(GQ)]
        qs = jnp.concatenate([q[:, cols] for cols in heads], axis=0)
        st = lax.dot_general(keys[:, kv_cols], qs, (((1,), (1,)), ((), ())),
                             preferred_element_type=F32) + bias_of_group(kv)
        sk = sink_of_group(kv)
        m = jnp.maximum(jnp.max(st, axis=0, keepdims=True), sk)
        p = jnp.exp(st - m)
        den = jnp.sum(p, axis=0, keepdims=True) + jnp.exp(sk - m)
        o = lax.dot_general((p * (1.0 / den)).astype(BF16), vals[:, kv_cols], (((0,), (0,)), ((), ())),
                            preferred_element_type=F32)
        for g, cols in enumerate(heads):
            o_scr[row0:row0 + nq, cols] = o[g * nq:(g + 1) * nq]


def _attn_prompt_kernel(q_ref, k_ref, v_ref, bias_ref, sink_ref, x_ref, wo_ref, bo_ref, o_ref, o_scr):
    t = pl.program_id(1)
    for u in range(TQ_PER_STEP):
        tile = t * TQ_PER_STEP + u
        start = pl.multiple_of(tile * TQ, TQ)
        keys = k_ref[0, pl.ds(start, TQ + WINDOW), :]
        vals = v_ref[0, pl.ds(start, TQ + WINDOW), :]
        variant = jnp.minimum(tile, 1)
        _attend_groups(q_ref[u * TQ:(u + 1) * TQ], keys, vals, lambda kv: bias_ref[variant, kv],
                       lambda kv: sink_ref[kv], o_scr, u * TQ)
    o_ref[...] = x_ref[...] + _dot(o_scr[...].astype(BF16), wo_ref[...]) + bo_ref[...]


def _attn_prompt(q, kpad, vpad, bias, sink, x, w_o, b_o):
    nb, lpad = kpad.shape[0], kpad.shape[1]
    tm = TQ_PER_STEP * TQ
    steps = (lpad - WINDOW) // tm
    return pl.pallas_call(
        _attn_prompt_kernel,
        grid=(nb, steps),
        in_specs=[
            pl.BlockSpec((tm, D_MODEL), lambda b, t: (b * steps + t, 0)),
            pl.BlockSpec((1, lpad, KV_DIM), lambda b, t: (b, 0, 0)),
            pl.BlockSpec((1, lpad, KV_DIM), lambda b, t: (b, 0, 0)),
            _const_spec((2, N_KV, TQ + WINDOW, GQ * TQ)),
            _const_spec((N_KV, 1, GQ * TQ)),
            pl.BlockSpec((tm, D_MODEL), lambda b, t: (b * steps + t, 0)),
            _const_spec((D_MODEL, D_MODEL)),
            _const_spec((1, D_MODEL)),
        ],
        out_specs=pl.BlockSpec((tm, D_MODEL), lambda b, t: (b * steps + t, 0)),
        out_shape=jax.ShapeDtypeStruct(x.shape, F32),
        scratch_shapes=[pltpu.VMEM((tm, D_MODEL), F32)],
        compiler_params=_params("parallel", "arbitrary"),
        name="attn_prompt",
    )(q, kpad, vpad, bias, sink, x, w_o, b_o)


def _attn_small_kernel(q_ref, kc_ref, vc_ref, kn_ref, vn_ref, bias_ref, sink_ref, x_ref, wo_ref, bo_ref,
                       o_ref, o_scr):
    step = pl.program_id(0)
    for j in range(SEG_PER_STEP):
        rows = slice(j * SEG, (j + 1) * SEG)
        keys = jnp.concatenate([kc_ref[j], kn_ref[rows]], axis=0).astype(BF16)
        vals = jnp.concatenate([vc_ref[j], vn_ref[rows]], axis=0).astype(BF16)
        variant = jnp.where(step * SEG_PER_STEP + j == META_SEG, 1, 0)
        _attend_groups(q_ref[rows], keys, vals, lambda kv: bias_ref[variant, kv], lambda kv: sink_ref[kv], o_scr,
                       j * SEG)
    o_ref[...] = x_ref[...] + _dot(o_scr[...].astype(BF16), wo_ref[...]) + bo_ref[...]


def _attn_small(q, kc, vc, kn, vn, bias, sink, x, w_o, b_o):
    tm = SEG_PER_STEP * SEG
    steps = x.shape[0] // tm
    cache_blocks = kc.shape[0] // SEG_PER_STEP
    cache_spec = pl.BlockSpec((SEG_PER_STEP, WINDOW, KV_DIM), lambda s: (jnp.minimum(s, cache_blocks - 1), 0, 0))
    return pl.pallas_call(
        _attn_small_kernel,
        grid=(steps,),
        in_specs=[
            pl.BlockSpec((tm, D_MODEL), lambda s: (s, 0)),
            cache_spec,
            cache_spec,
            pl.BlockSpec((tm, KV_DIM), lambda s: (s, 0)),
            pl.BlockSpec((tm, KV_DIM), lambda s: (s, 0)),
            _const_spec((2, N_KV, WINDOW + SEG, GQ * SEG)),
            _const_spec((N_KV, 1, GQ * SEG)),
            pl.BlockSpec((tm, D_MODEL), lambda s: (s, 0)),
            _const_spec((D_MODEL, D_MODEL)),
            _const_spec((1, D_MODEL)),
        ],
        out_specs=pl.BlockSpec((tm, D_MODEL), lambda s: (s, 0)),
        out_shape=jax.ShapeDtypeStruct(x.shape, F32),
        scratch_shapes=[pltpu.VMEM((tm, D_MODEL), F32)],
        compiler_params=_params("parallel"),
        name="attn_small",
    )(q, kc, vc, kn, vn, bias, sink, x, w_o, b_o)


def _moe_kernel(x_ref, g_ref, rwt_ref, rb_ref, w1_ref, w3_ref, w2_ref, gf_ref, o_ref,
                hn_scr, slot_scr, gate_scr, cnt_smem, acc_scr, *, rt):
    e = pl.program_id(1)
    tb = x_ref.shape[0]

    @pl.when(e == 0)
    def _route():
        hn = _rms(x_ref[...], g_ref[...]).astype(BF16)
        hn_scr[...] = hn
        logits = lax.dot_general(rwt_ref[...], hn, (((1,), (1,)), ((), ())),
                                 preferred_element_type=F32) + rb_ref[...]
        row = lax.broadcasted_iota(jnp.int32, logits.shape, 0).astype(F32)
        v1 = jnp.max(logits, axis=0, keepdims=True)
        i1 = jnp.min(jnp.where(logits == v1, row, float(N_EXPERTS)), axis=0, keepdims=True)
        rest = jnp.where(row == i1, -jnp.inf, logits)
        v2 = jnp.max(rest, axis=0, keepdims=True)
        i2 = jnp.min(jnp.where(rest == v2, row, float(N_EXPERTS)), axis=0, keepdims=True)
        e2 = jnp.exp(v2 - v1)
        den = 1.0 + e2
        gate_scr[...] = jnp.where(row == i1, 1.0 / den, 0.0) + jnp.where(row == i2, e2 / den, 0.0)
        chosen = (row == i1) | (row == i2)
        ind = jnp.where(chosen, 1.0, 0.0)
        earlier = (lax.broadcasted_iota(jnp.int32, (tb, tb), 0) < lax.broadcasted_iota(jnp.int32, (tb, tb), 1))
        rank = _dot(ind.astype(BF16), jnp.where(earlier, 1.0, 0.0).astype(BF16))
        slot_scr[...] = jnp.where(chosen, rank, -1.0).astype(jnp.int32)
        for k in range(N_EXPERTS):
            cnt_smem[k] = jnp.sum(ind[k:k + 1, :]).astype(jnp.int32)
        acc_scr[...] = jnp.zeros_like(acc_scr)

    slot = slot_scr[pl.ds(e, 1), :]
    gate = gate_scr[pl.ds(e, 1), :]

    def _tile(i, carry):
        packed_row = lax.broadcasted_iota(jnp.int32, (rt, tb), 0) + i * rt
        hit = packed_row == slot
        onehot = jnp.where(hit, 1.0, 0.0).astype(BF16)
        xe = _dot(onehot, hn_scr[...]).astype(BF16)
        a = _dot(xe, w1_ref[0])
        b = _dot(xe, w3_ref[0])
        y = _dot((a * jax.nn.sigmoid(a) * b).astype(BF16), w2_ref[0])
        g_row = jnp.sum(jnp.where(hit, gate, 0.0), axis=1, keepdims=True)
        acc_scr[...] += lax.dot_general(onehot, (y * g_row).astype(BF16), (((0,), (0,)), ((), ())),
                                        preferred_element_type=F32)
        return carry

    lax.fori_loop(0, (cnt_smem[e] + rt - 1) // rt, _tile, 0)

    @pl.when(e == N_EXPERTS - 1)
    def _finish():
        o_ref[...] = _rms(x_ref[...] + acc_scr[...], gf_ref[...])


def _moe_final(x, g, rwt, rb, w1, w3, w2, gf, *, tb, rt):
    rows = x.shape[0]
    return pl.pallas_call(
        functools.partial(_moe_kernel, rt=rt),
        grid=(rows // tb, N_EXPERTS),
        in_specs=[
            pl.BlockSpec((tb, D_MODEL), lambda t, e: (t, 0)),
            _const_spec((1, D_MODEL)),
            _const_spec((N_EXPERTS, D_MODEL)),
            _const_spec((N_EXPERTS, 1)),
            pl.BlockSpec((1, D_MODEL, D_FF_E), lambda t, e: (e, 0, 0)),
            pl.BlockSpec((1, D_MODEL, D_FF_E), lambda t, e: (e, 0, 0)),
            pl.BlockSpec((1, D_FF_E, D_MODEL), lambda t, e: (e, 0, 0)),
            _const_spec((1, D_MODEL)),
        ],
        out_specs=pl.BlockSpec((tb, D_MODEL), lambda t, e: (t, 0)),
        out_shape=jax.ShapeDtypeStruct((rows, D_MODEL), F32),
        scratch_shapes=[
            pltpu.VMEM((tb, D_MODEL), BF16),
            pltpu.VMEM((N_EXPERTS, tb), jnp.int32),
            pltpu.VMEM((N_EXPERTS, tb), F32),
            pltpu.SMEM((N_EXPERTS,), jnp.int32),
            pltpu.VMEM((tb, D_MODEL), F32),
        ],
        compiler_params=_params("parallel", "arbitrary"),
        name="moe_final",
    )(x, g, rwt, rb, w1, w3, w2, gf)


def _t5_bucket_np(rel):
    half = N_BUCKETS // 2
    max_exact = half // 2
    ret = np.where(rel > 0, half, 0)
    n = np.abs(rel)
    nf = np.maximum(n, 1).astype(np.float32)
    large = max_exact + (np.log(nf / np.float32(max_exact)) / np.float32(math.log(MAX_DIST / max_exact))
                         * np.float32(half - max_exact)).astype(np.int32)
    large = np.minimum(large, half - 1)
    return ret + np.where(n < max_exact, n, large)


def _bias_variants(table, bucket, keeps):
    nq, nk = bucket.shape
    onehot = jnp.asarray(np.eye(N_BUCKETS, dtype=np.float32)[bucket])
    bias = jnp.einsum('rjb,bh->hjr', onehot, table, precision=lax.Precision.HIGHEST)
    bias = jnp.stack([jnp.where(keep.T[None], bias, NEG) for keep in keeps], axis=0)
    bias = bias.reshape(len(keeps), N_KV, GQ, nk, nq)
    return jnp.transpose(bias, (0, 1, 3, 2, 4)).reshape(len(keeps), N_KV, nk, GQ * nq)


def _lane_sink(sink, nq):
    return jnp.broadcast_to(sink.reshape(N_KV, 1, GQ, 1), (N_KV, 1, GQ, nq)).reshape(N_KV, 1, GQ * nq)


def _prompt_bias(table):
    r = np.arange(TQ)[:, None]
    j = np.arange(TQ + WINDOW)[None, :]
    band = (j - CHUNK * (r // CHUNK) >= 0) & (j - CHUNK * (r // CHUNK) < WINDOW + CHUNK)
    first = band & (j >= WINDOW - N_META)
    return _bias_variants(table, _t5_bucket_np(j - WINDOW - r), [first, band])


def _small_bias(table):
    i = np.arange(SEG)[:, None]
    m = np.arange(WINDOW + SEG)[None, :]
    everything = np.ones((SEG, WINDOW + SEG), bool)
    meta_only = everything & (m >= WINDOW + SEG - N_META)
    return _bias_variants(table, _t5_bucket_np(m - WINDOW - i), [everything, meta_only])


def _cmul(ar, ai, br, bi):
    return ar * br - ai * bi, ar * bi + ai * br


def _s5_prepare(a_re, a_im, log_dt, b_re, b_im, c_re, c_im):
    dt = jnp.exp(log_dt)[:, None]
    mag = jnp.exp(a_re * dt)
    ab_re, ab_im = mag * jnp.cos(a_im * dt), mag * jnp.sin(a_im * dt)
    den = a_re * a_re + a_im * a_im
    num_re = ab_re - 1.0
    cf_re = (num_re * a_re + ab_im * a_im) / den
    cf_im = (ab_im * a_re - num_re * a_im) / den
    bb_re = cf_re[..., None] * b_re - cf_im[..., None] * b_im
    bb_im = cf_re[..., None] * b_im + cf_im[..., None] * b_re
    eye = jnp.eye(S5_GROUPS, dtype=F32)

    def in_map(bb):
        return jnp.einsum('gpj,gh->gjhp', bb, eye).reshape(S5_CH, S5_STATE)

    def out_map(c):
        return jnp.einsum('gjp,gh->gphj', c, eye).reshape(S5_STATE, S5_CH)

    bb = jnp.concatenate([in_map(bb_re), in_map(bb_im)], axis=1).astype(BF16)
    cc = jnp.concatenate([out_map(c_re), -out_map(c_im)], axis=0).astype(BF16)

    ar, ai = ab_re.reshape(1, S5_STATE), ab_im.reshape(1, S5_STATE)
    powers = [(ar, ai)]
    for _ in range(SCAN_ROWS - 1):
        powers.append(_cmul(*powers[-1], ar, ai))
    row = np.arange(SCAN_ROWS)[:, None]

    def shifted(k):
        return [jnp.where(row >= k, p, 0.0) for p in powers[k - 1]]

    carry = [jnp.concatenate([p[i] for p in powers], axis=0) for i in range(2)]
    consts = jnp.stack(shifted(1) + shifted(2) + shifted(4) + carry, axis=0)
    return bb, cc, consts


def kernel(x_prompt, x_sample, cache_conv, state_s5_re, state_s5_im, cache_swa_k, cache_swa_v, meta_tokens, rel_bias_table, norm_mix, norm_ffn, norm_final, w_in0, conv_w, conv_b, s5_a_re, s5_a_im, s5_log_dt, s5_b_re, s5_b_im, s5_c_re, s5_c_im, s5_d, s5_glu_w, s5_glu_b, w_out0, ffn_w1, ffn_w3, ffn_w2, w_qkv, b_qkv, attn_sink, w_o, b_o, router_w, router_b, moe_w1, moe_w3, moe_w2):
    nb, seq = x_prompt.shape[0], x_prompt.shape[1]
    nsb = x_sample.shape[0]
    assert x_sample.shape[1] == SEG and nsb == N_SAMPLE_SEG and seq % TM_MIX == 0

    def row(v):
        return v.reshape(1, -1)

    xp = x_prompt.reshape(nb * seq, D_MODEL)
    meta_seg = jnp.concatenate([jnp.zeros((SEG - N_META, D_MODEL), F32), meta_tokens], axis=0)
    filler = jnp.zeros(((N_SMALL_SEG - N_SAMPLE_SEG - 1) * SEG, D_MODEL), F32)
    xs = jnp.concatenate([x_sample.reshape(nsb * SEG, D_MODEL), meta_seg, filler], axis=0)
    n_extra = N_SMALL_SEG - N_SAMPLE_SEG

    bb, cc, consts = _s5_prepare(s5_a_re[0], s5_a_im[0], s5_log_dt[0], s5_b_re[0], s5_b_im[0], s5_c_re[0],
                                 s5_c_im[0])
    mixer_w = (row(norm_mix[0]), w_in0[0].astype(BF16), conv_w[0], row(conv_b[0]))
    mixer_w2 = (consts, bb, cc, row(s5_d[0]), s5_glu_w[0].astype(BF16), row(s5_glu_b[0]), w_out0[0].astype(BF16))
    cinit_s = jnp.concatenate([cache_conv[0], jnp.zeros((n_extra, 2, CONV_CH), F32)], axis=0)
    sinit_s = jnp.concatenate([state_s5_re[0].reshape(nsb, 1, S5_STATE), state_s5_im[0].reshape(nsb, 1, S5_STATE)],
                              axis=-1)
    sinit_s = jnp.concatenate([sinit_s, jnp.zeros((n_extra, 1, 2 * S5_STATE), F32)], axis=0)
    hs, conv_s, state_s = _even_mixer(xs, *mixer_w, cinit_s, sinit_s, *mixer_w2, seg=SEG, nseg=TM_MIX // SEG,
                                      tiles_per_seq=1)
    cinit_p = jnp.broadcast_to(conv_s[META_SEG][None], (nb, 2, CONV_CH))
    sinit_p = jnp.broadcast_to(state_s[META_SEG][None], (nb, 1, 2 * S5_STATE))
    hp, conv_p, state_p = _even_mixer(xp, *mixer_w, cinit_p, sinit_p, *mixer_w2, seg=TM_MIX, nseg=1,
                                      tiles_per_seq=seq // TM_MIX)
    ffn_w = (row(norm_ffn[0]), ffn_w1[0].astype(BF16), ffn_w3[0].astype(BF16), ffn_w2[0].astype(BF16))
    hs = _ffn(hs, *ffn_w, tm=256)
    hp = _ffn(hp, *ffn_w, tm=512)

    qkv_w = (row(norm_mix[1]), w_qkv[0].astype(BF16), row(b_qkv[0]))
    qs, ks, vs = _qkv(hs, *qkv_w, tm=256)
    qp, kp, vp = _qkv(hp, *qkv_w, tm=512)
    wo = (w_o[0].astype(BF16), row(b_o[0]))
    hs = _attn_small(qs, cache_swa_k[0].reshape(nsb, WINDOW, KV_DIM), cache_swa_v[0].reshape(nsb, WINDOW, KV_DIM),
                     ks, vs, _small_bias(rel_bias_table), _lane_sink(attn_sink[0], SEG), hs, *wo)

    def padded(new, small):
        meta_rows = small[META_SEG * SEG:(META_SEG + 1) * SEG]
        front = jnp.concatenate([jnp.zeros((WINDOW - SEG, KV_DIM), F32), meta_rows], axis=0)
        front = jnp.broadcast_to(front[None], (nb, WINDOW, KV_DIM))
        return jnp.concatenate([front, new.reshape(nb, seq, KV_DIM)], axis=1).astype(BF16)

    hp = _attn_prompt(qp, padded(kp, ks), padded(vp, vs), _prompt_bias(rel_bias_table),
                      _lane_sink(attn_sink[0], TQ), hp, *wo)
    moe_w = (row(norm_ffn[1]), router_w[0].T.astype(BF16), router_b[0].reshape(N_EXPERTS, 1), moe_w1[0].astype(BF16),
             moe_w3[0].astype(BF16), moe_w2[0].astype(BF16), row(norm_final))
    ys = _moe_final(hs, *moe_w, tb=hs.shape[0], rt=352)
    yp = _moe_final(hp, *moe_w, tb=1024, rt=288)

    n_real = nsb * SEG
    y_prompt = yp.reshape(nb, seq, D_MODEL)
    y_sample = ys[:n_real].reshape(nsb, SEG, D_MODEL)

    def split_state(st, n):
        st = st[:n, 0]
        return (st[:, :S5_STATE].reshape(1, n, S5_GROUPS, S5_P), st[:, S5_STATE:].reshape(1, n, S5_GROUPS, S5_P))

    s5rp, s5ip = split_state(state_p, nb)
    s5rs, s5is = split_state(state_s, nsb)
    kp4 = kp.reshape(nb, seq, N_KV, HEAD_DIM)[:, -WINDOW:][None]
    vp4 = vp.reshape(nb, seq, N_KV, HEAD_DIM)[:, -WINDOW:][None]
    ks4 = ks[:n_real].reshape(1, nsb, SEG, N_KV, HEAD_DIM)
    vs4 = vs[:n_real].reshape(1, nsb, SEG, N_KV, HEAD_DIM)
    return (y_prompt, y_sample, conv_p[None], conv_s[:nsb][None], s5rp, s5ip, s5rs, s5is, kp4, vp4, ks4, vs4)
```

```python
import functools
import math

import numpy as np
import jax
import jax.numpy as jnp
from jax import lax
from jax.experimental import pallas as pl
from jax.experimental.pallas import tpu as pltpu

F32 = jnp.float32
BF16 = jnp.bfloat16

D_MODEL = 1024
CONV_CH = 512
S5_CH = 512
S5_GROUP = 16
S5_GROUPS = 32
S5_P = 64
S5_STATE = S5_GROUPS * S5_P
HALF_CH = S5_CH // 2
HALF_STATE = S5_STATE // 2
D_FF = 2816
N_HEADS = 16
N_KV = 2
GQ = N_HEADS // N_KV
HEAD_DIM = 64
KV_DIM = N_KV * HEAD_DIM
WINDOW = 128
CHUNK = 64
N_META = 16
N_BUCKETS = 32
MAX_DIST = 128
N_EXPERTS = 8
D_FF_E = 1024
EPS = 1e-6
NEG = -1e30

SEG = 32
N_SAMPLE_SEG = 32
META_SEG = N_SAMPLE_SEG
N_SMALL_SEG = 40
TM_MIX = 256
SCAN_ROWS = 8
SCAN_LANES = 512
TQ = 128
TQ_PER_STEP = 2
SEG_PER_STEP = 8
VMEM_LIMIT = 56 * 1024 * 1024


def _const_spec(shape):
    nd = len(shape)
    return pl.BlockSpec(shape, lambda *_: (0,) * nd, pipeline_mode=pl.Buffered(1))


def _params(*sem):
    return pltpu.CompilerParams(dimension_semantics=sem, vmem_limit_bytes=VMEM_LIMIT)


def _rms(x, g):
    return x * lax.rsqrt(jnp.mean(x * x, axis=-1, keepdims=True) + EPS) * g


def _dot(a, b):
    return jnp.dot(a, b, preferred_element_type=F32)


def _even_mixer_kernel(x_ref, gm_ref, win_ref, cw_ref, cb_ref, cinit_ref, sinit_ref, sc_ref, bb_ref, cc_ref,
                       d_ref, gluw_ref, glub_ref, wout_ref,
                       h_ref, cout_ref, sout_ref,
                       bu_ref, mix_ref, ccarry_ref, scarry_ref, *, seg, nseg, tiles_per_seq):
    t = pl.program_id(0)
    x = x_ref[...]
    hn = _rms(x, gm_ref[...]).astype(BF16)
    proj = _dot(hn, win_ref[...])
    g_b = proj[:, :CONV_CH]
    cin = proj[:, CONV_CH:2 * CONV_CH] * proj[:, 2 * CONV_CH:3 * CONV_CH]
    u = proj[:, 3 * CONV_CH:]
    ub = u.astype(BF16)
    for half in range(2):
        bu_ref[:, half * S5_STATE:(half + 1) * S5_STATE] = _dot(ub[:, half * HALF_CH:(half + 1) * HALF_CH],
                                                               bb_ref[half])
    if tiles_per_seq > 1:
        @pl.when(t % tiles_per_seq == 0)
        def _load_initial_state():
            ccarry_ref[...] = cinit_ref[0]
            scarry_ref[...] = sinit_ref[0]

    cw = cw_ref[...]
    row = lax.broadcasted_iota(jnp.int32, (seg, CONV_CH), 0)
    for j in range(nseg):
        rows = slice(j * seg, (j + 1) * seg)
        c_seg = cin[rows]
        init = ccarry_ref[...] if tiles_per_seq > 1 else cinit_ref[j]
        older, newer = init[0:1], init[1:2]
        p1 = jnp.where(row == 0, newer, pltpu.roll(c_seg, 1, 0))
        p2 = jnp.where(row == 0, older, jnp.where(row == 1, newer, pltpu.roll(c_seg, 2, 0)))
        conv = cw[0:1] * p2 + cw[1:2] * p1 + cw[2:3] * c_seg + cb_ref[...]
        mix_ref[rows, :CONV_CH] = g_b[rows] * conv
        tail = c_seg[seg - 2:seg]
        cout_ref[j] = tail
        if tiles_per_seq > 1:
            ccarry_ref[...] = tail

    groups_per_seg = seg // SCAN_ROWS
    n_groups = nseg * groups_per_seg
    for c in range(S5_STATE // SCAN_LANES):
        half, within = divmod(c * SCAN_LANES, HALF_STATE)
        re_cols = slice(half * S5_STATE + within, half * S5_STATE + within + SCAN_LANES)
        im_cols = slice(re_cols.start + HALF_STATE, re_cols.stop + HALF_STATE)
        a1r, a1i, a2r, a2i, a4r, a4i, pr, pi = [sc_ref[k, :, c * SCAN_LANES:(c + 1) * SCAN_LANES] for k in range(8)]

        def seg_init(s, cols):
            return jnp.broadcast_to(sinit_ref[s, :, cols], (SCAN_ROWS, SCAN_LANES))

        def body(r, carry, re_cols=re_cols, im_cols=im_cols, a1r=a1r, a1i=a1i, a2r=a2r, a2i=a2i, a4r=a4r,
                 a4i=a4i, pr=pr, pi=pi, seg_init=seg_init):
            cr, ci = carry
            if nseg > 1:
                s = r // groups_per_seg
                starts = (r % groups_per_seg) == 0
                cr = jnp.where(starts, seg_init(s, re_cols), cr)
                ci = jnp.where(starts, seg_init(s, im_cols), ci)
            rows = pl.ds(pl.multiple_of(r * SCAN_ROWS, SCAN_ROWS), SCAN_ROWS)
            vr = bu_ref[rows, re_cols]
            vi = bu_ref[rows, im_cols]
            for ar, ai, sh in ((a1r, a1i, 1), (a2r, a2i, 2), (a4r, a4i, 4)):
                sr = pltpu.roll(vr, sh, 0)
                si = pltpu.roll(vi, sh, 0)
                vr, vi = vr + ar * sr - ai * si, vi + ar * si + ai * sr
            vr, vi = vr + pr * cr - pi * ci, vi + pr * ci + pi * cr
            bu_ref[rows, re_cols] = vr
            bu_ref[rows, im_cols] = vi
            if nseg > 1:
                s = r // groups_per_seg
                sout_ref[s, :, re_cols] = vr[SCAN_ROWS - 1:SCAN_ROWS]
                sout_ref[s, :, im_cols] = vi[SCAN_ROWS - 1:SCAN_ROWS]
            return (jnp.broadcast_to(vr[SCAN_ROWS - 1:SCAN_ROWS], (SCAN_ROWS, SCAN_LANES)),
                    jnp.broadcast_to(vi[SCAN_ROWS - 1:SCAN_ROWS], (SCAN_ROWS, SCAN_LANES)))

        if nseg > 1:
            zero = jnp.zeros((SCAN_ROWS, SCAN_LANES), F32)
            lax.fori_loop(0, n_groups, body, (zero, zero), unroll=2)
        else:
            state = scarry_ref if tiles_per_seq > 1 else sinit_ref.at[0]
            cr, ci = lax.fori_loop(0, n_groups, body,
                                   (jnp.broadcast_to(state[:, re_cols], (SCAN_ROWS, SCAN_LANES)),
                                    jnp.broadcast_to(state[:, im_cols], (SCAN_ROWS, SCAN_LANES))), unroll=2)
            scarry_ref[:, re_cols] = cr[0:1]
            scarry_ref[:, im_cols] = ci[0:1]
            sout_ref[0, :, re_cols] = cr[0:1]
            sout_ref[0, :, im_cols] = ci[0:1]

    y = jnp.concatenate([_dot(bu_ref[:, half * S5_STATE:(half + 1) * S5_STATE].astype(BF16), cc_ref[half])
                         for half in range(2)], axis=-1) + d_ref[...] * u
    z = jax.nn.gelu(y)
    mix_ref[:, CONV_CH:] = z * jax.nn.sigmoid(_dot(z.astype(BF16), gluw_ref[...]) + glub_ref[...])
    h_ref[...] = x + _dot(mix_ref[...].astype(BF16), wout_ref[...])


def _even_mixer(x, gm, w_in, cw, cb, cinit, sinit, scan_consts, bb, cc, d, glu_w, glu_b, w_out, *, seg, nseg,
                tiles_per_seq):
    rows = x.shape[0]
    tm = seg * nseg
    nseq = cinit.shape[0]
    kern = functools.partial(_even_mixer_kernel, seg=seg, nseg=nseg, tiles_per_seq=tiles_per_seq)
    return pl.pallas_call(
        kern,
        grid=(rows // tm,),
        in_specs=[
            pl.BlockSpec((tm, D_MODEL), lambda t: (t, 0)),
            _const_spec((1, D_MODEL)),
            _const_spec((D_MODEL, 4 * CONV_CH)),
            _const_spec((3, CONV_CH)),
            _const_spec((1, CONV_CH)),
            pl.BlockSpec((nseg, 2, CONV_CH), lambda t: (t // tiles_per_seq, 0, 0)),
            pl.BlockSpec((nseg, 1, 2 * S5_STATE), lambda t: (t // tiles_per_seq, 0, 0)),
            _const_spec((8, SCAN_ROWS, S5_STATE)),
            _const_spec((2, HALF_CH, S5_STATE)),
            _const_spec((2, S5_STATE, HALF_CH)),
            _const_spec((1, S5_CH)),
            _const_spec((S5_CH, S5_CH)),
            _const_spec((1, S5_CH)),
            _const_spec((D_MODEL, D_MODEL)),
        ],
        out_specs=[
            pl.BlockSpec((tm, D_MODEL), lambda t: (t, 0)),
            pl.BlockSpec((nseg, 2, CONV_CH), lambda t: (t // tiles_per_seq, 0, 0)),
            pl.BlockSpec((nseg, 1, 2 * S5_STATE), lambda t: (t // tiles_per_seq, 0, 0)),
        ],
        out_shape=[
            jax.ShapeDtypeStruct((rows, D_MODEL), F32),
            jax.ShapeDtypeStruct((nseq, 2, CONV_CH), F32),
            jax.ShapeDtypeStruct((nseq, 1, 2 * S5_STATE), F32),
        ],
        scratch_shapes=[
            pltpu.VMEM((tm, 2 * S5_STATE), F32),
            pltpu.VMEM((tm, D_MODEL), F32),
            pltpu.VMEM((2, CONV_CH), F32),
            pltpu.VMEM((1, 2 * S5_STATE), F32),
        ],
        compiler_params=_params("arbitrary"),
        name="even_mixer",
    )(x, gm, w_in, cw, cb, cinit, sinit, scan_consts, bb, cc, d, glu_w, glu_b, w_out)


def _ffn_kernel(x_ref, g_ref, w1_ref, w3_ref, w2_ref, o_ref):
    x = x_ref[...]
    hn = _rms(x, g_ref[...]).astype(BF16)
    a = _dot(hn, w1_ref[...])
    b = _dot(hn, w3_ref[...])
    o_ref[...] = x + _dot((a * jax.nn.sigmoid(a) * b).astype(BF16), w2_ref[...])


def _ffn(x, g, w1, w3, w2, *, tm):
    rows = x.shape[0]
    return pl.pallas_call(
        _ffn_kernel,
        grid=(rows // tm,),
        in_specs=[
            pl.BlockSpec((tm, D_MODEL), lambda t: (t, 0)),
            _const_spec((1, D_MODEL)),
            _const_spec((D_MODEL, D_FF)),
            _const_spec((D_MODEL, D_FF)),
            _const_spec((D_FF, D_MODEL)),
        ],
        out_specs=pl.BlockSpec((tm, D_MODEL), lambda t: (t, 0)),
        out_shape=jax.ShapeDtypeStruct((rows, D_MODEL), F32),
        compiler_params=_params("parallel"),
        name="ffn",
    )(x, g, w1, w3, w2)


def _qkv_kernel(x_ref, g_ref, w_ref, b_ref, q_ref, k_ref, v_ref):
    hn = _rms(x_ref[...], g_ref[...]).astype(BF16)
    qkv = _dot(hn, w_ref[...]) + b_ref[...]
    nq = N_HEADS * HEAD_DIM
    q_ref[...] = (qkv[:, :nq] * (HEAD_DIM ** -0.5)).astype(BF16)
    k_ref[...] = qkv[:, nq:nq + KV_DIM]
    v_ref[...] = qkv[:, nq + KV_DIM:]


def _qkv(x, g, w, b, *, tm):
    rows = x.shape[0]
    ncol = (N_HEADS + 2 * N_KV) * HEAD_DIM
    return pl.pallas_call(
        _qkv_kernel,
        grid=(rows // tm,),
        in_specs=[
            pl.BlockSpec((tm, D_MODEL), lambda t: (t, 0)),
            _const_spec((1, D_MODEL)),
            _const_spec((D_MODEL, ncol)),
            _const_spec((1, ncol)),
        ],
        out_specs=[
            pl.BlockSpec((tm, N_HEADS * HEAD_DIM), lambda t: (t, 0)),
            pl.BlockSpec((tm, KV_DIM), lambda t: (t, 0)),
            pl.BlockSpec((tm, KV_DIM), lambda t: (t, 0)),
        ],
        out_shape=[
            jax.ShapeDtypeStruct((rows, N_HEADS * HEAD_DIM), BF16),
            jax.ShapeDtypeStruct((rows, KV_DIM), F32),
            jax.ShapeDtypeStruct((rows, KV_DIM), F32),
        ],
        compiler_params=_params("parallel"),
        name="qkv",
    )(x, g, w, b)


def _attend_groups(q, keys, vals, bias_of_group, sink_of_group, o_scr, row0):
    nq = q.shape[0]
    for kv in range(N_KV):
        kv_cols = slice(kv * HEAD_DIM, (kv + 1) * HEAD_DIM)
        heads = [slice((kv * GQ + g) * HEAD_DIM, (kv * GQ + g + 1) * HEAD_DIM) for g in range(GQ)]
        qs = jnp.concatenate([q[:, cols] for cols in heads], axis=0)
        st = lax.dot_general(keys[:, kv_cols], qs, (((1,), (1,)), ((), ())),
                             preferred_element_type=F32) + bias_of_group(kv)
        sk = sink_of_group(kv)
        m = jnp.maximum(jnp.max(st, axis=0, keepdims=True), sk)
        p = jnp.exp(st - m)
        den = jnp.sum(p, axis=0, keepdims=True) + jnp.exp(sk - m)
        o = lax.dot_general((p * (1.0 / den)).astype(BF16), vals[:, kv_cols], (((0,), (0,)), ((), ())),
                            preferred_element_type=F32)
        for g, cols in enumerate(heads):
            o_scr[row0:row0 + nq, cols] = o[g * nq:(g + 1) * nq]


def _attn_prompt_kernel(q_ref, k_ref, v_ref, bias_ref, sink_ref, x_ref, wo_ref, bo_ref, o_ref, o_scr):
    t = pl.program_id(1)
    for u in range(TQ_PER_STEP):
        tile = t * TQ_PER_STEP + u
        start = pl.multiple_of(tile * TQ, TQ)
        keys = k_ref[0, pl.ds(start, TQ + WINDOW), :]
        vals = v_ref[0, pl.ds(start, TQ + WINDOW), :]
        variant = jnp.minimum(tile, 1)
        _attend_groups(q_ref[u * TQ:(u + 1) * TQ], keys, vals, lambda kv: bias_ref[variant, kv],
                       lambda kv: sink_ref[kv], o_scr, u * TQ)
    o_ref[...] = x_ref[...] + _dot(o_scr[...].astype(BF16), wo_ref[...]) + bo_ref[...]


def _attn_prompt(q, kpad, vpad, bias, sink, x, w_o, b_o):
    nb, lpad = kpad.shape[0], kpad.shape[1]
    tm = TQ_PER_STEP * TQ
    steps = (lpad - WINDOW) // tm
    return pl.pallas_call(
        _attn_prompt_kernel,
        grid=(nb, steps),
        in_specs=[
            pl.BlockSpec((tm, D_MODEL), lambda b, t: (b * steps + t, 0)),
            pl.BlockSpec((1, lpad, KV_DIM), lambda b, t: (b, 0, 0)),
            pl.BlockSpec((1, lpad, KV_DIM), lambda b, t: (b, 0, 0)),
            _const_spec((2, N_KV, TQ + WINDOW, GQ * TQ)),
            _const_spec((N_KV, 1, GQ * TQ)),
            pl.BlockSpec((tm, D_MODEL), lambda b, t: (b * steps + t, 0)),
            _const_spec((D_MODEL, D_MODEL)),
            _const_spec((1, D_MODEL)),
        ],
        out_specs=pl.BlockSpec((tm, D_MODEL), lambda b, t: (b * steps + t, 0)),
        out_shape=jax.ShapeDtypeStruct(x.shape, F32),
        scratch_shapes=[pltpu.VMEM((tm, D_MODEL), F32)],
        compiler_params=_params("parallel", "arbitrary"),
        name="attn_prompt",
    )(q, kpad, vpad, bias, sink, x, w_o, b_o)


def _attn_small_kernel(q_ref, kc_ref, vc_ref, kn_ref, vn_ref, bias_ref, sink_ref, x_ref, wo_ref, bo_ref,
                       o_ref, o_scr):
    step = pl.program_id(0)
    for j in range(SEG_PER_STEP):
        rows = slice(j * SEG, (j + 1) * SEG)
        keys = jnp.concatenate([kc_ref[j], kn_ref[rows]], axis=0).astype(BF16)
        vals = jnp.concatenate([vc_ref[j], vn_ref[rows]], axis=0).astype(BF16)
        variant = jnp.where(step * SEG_PER_STEP + j == META_SEG, 1, 0)
        _attend_groups(q_ref[rows], keys, vals, lambda kv: bias_ref[variant, kv], lambda kv: sink_ref[kv], o_scr,
                       j * SEG)
    o_ref[...] = x_ref[...] + _dot(o_scr[...].astype(BF16), wo_ref[...]) + bo_ref[...]


def _attn_small(q, kc, vc, kn, vn, bias, sink, x, w_o, b_o):
    tm = SEG_PER_STEP * SEG
    steps = x.shape[0] // tm
    cache_blocks = kc.shape[0] // SEG_PER_STEP
    cache_spec = pl.BlockSpec((SEG_PER_STEP, WINDOW, KV_DIM), lambda s: (jnp.minimum(s, cache_blocks - 1), 0, 0))
    return pl.pallas_call(
        _attn_small_kernel,
        grid=(steps,),
        in_specs=[
            pl.BlockSpec((tm, D_MODEL), lambda s: (s, 0)),
            cache_spec,
            cache_spec,
            pl.BlockSpec((tm, KV_DIM), lambda s: (s, 0)),
            pl.BlockSpec((tm, KV_DIM), lambda s: (s, 0)),
            _const_spec((2, N_KV, WINDOW + SEG, GQ * SEG)),
            _const_spec((N_KV, 1, GQ * SEG)),
            pl.BlockSpec((tm, D_MODEL), lambda s: (s, 0)),
            _const_spec((D_MODEL, D_MODEL)),
            _const_spec((1, D_MODEL)),
        ],
        out_specs=pl.BlockSpec((tm, D_MODEL), lambda s: (s, 0)),
        out_shape=jax.ShapeDtypeStruct(x.shape, F32),
        scratch_shapes=[pltpu.VMEM((tm, D_MODEL), F32)],
        compiler_params=_params("parallel"),
        name="attn_small",
    )(q, kc, vc, kn, vn, bias, sink, x, w_o, b_o)


def _moe_kernel(x_ref, g_ref, rwt_ref, rb_ref, w1_ref, w3_ref, w2_ref, gf_ref, o_ref,
                hn_scr, slot_scr, gate_scr, cnt_smem, acc_scr, *, rt):
    e = pl.program_id(1)
    tb = x_ref.shape[0]

    @pl.when(e == 0)
    def _route():
        hn = _rms(x_ref[...], g_ref[...]).astype(BF16)
        hn_scr[...] = hn
        logits = lax.dot_general(rwt_ref[...], hn, (((1,), (1,)), ((), ())),
                                 preferred_element_type=F32) + rb_ref[...]
        row = lax.broadcasted_iota(jnp.int32, logits.shape, 0).astype(F32)
        v1 = jnp.max(logits, axis=0, keepdims=True)
        i1 = jnp.min(jnp.where(logits == v1, row, float(N_EXPERTS)), axis=0, keepdims=True)
        rest = jnp.where(row == i1, -jnp.inf, logits)
        v2 = jnp.max(rest, axis=0, keepdims=True)
        i2 = jnp.min(jnp.where(rest == v2, row, float(N_EXPERTS)), axis=0, keepdims=True)
        e2 = jnp.exp(v2 - v1)
        den = 1.0 + e2
        gate_scr[...] = jnp.where(row == i1, 1.0 / den, 0.0) + jnp.where(row == i2, e2 / den, 0.0)
        chosen = (row == i1) | (row == i2)
        ind = jnp.where(chosen, 1.0, 0.0)
        earlier = (lax.broadcasted_iota(jnp.int32, (tb, tb), 0) < lax.broadcasted_iota(jnp.int32, (tb, tb), 1))
        rank = _dot(ind.astype(BF16), jnp.where(earlier, 1.0, 0.0).astype(BF16))
        slot_scr[...] = jnp.where(chosen, rank, -1.0).astype(jnp.int32)
        for k in range(N_EXPERTS):
            cnt_smem[k] = jnp.sum(ind[k:k + 1, :]).astype(jnp.int32)
        acc_scr[...] = jnp.zeros_like(acc_scr)

    slot = slot_scr[pl.ds(e, 1), :]
    gate = gate_scr[pl.ds(e, 1), :]

    def _tile(i, carry):
        packed_row = lax.broadcasted_iota(jnp.int32, (rt, tb), 0) + i * rt
        hit = packed_row == slot
        onehot = jnp.where(hit, 1.0, 0.0).astype(BF16)
        xe = _dot(onehot, hn_scr[...]).astype(BF16)
        a = _dot(xe, w1_ref[0])
        b = _dot(xe, w3_ref[0])
        y = _dot((a * jax.nn.sigmoid(a) * b).astype(BF16), w2_ref[0])
        g_row = jnp.sum(jnp.where(hit, gate, 0.0), axis=1, keepdims=True)
        acc_scr[...] += lax.dot_general(onehot, (y * g_row).astype(BF16), (((0,), (0,)), ((), ())),
                                        preferred_element_type=F32)
        return carry

    lax.fori_loop(0, (cnt_smem[e] + rt - 1) // rt, _tile, 0)

    @pl.when(e == N_EXPERTS - 1)
    def _finish():
        o_ref[...] = _rms(x_ref[...] + acc_scr[...], gf_ref[...])


def _moe_final(x, g, rwt, rb, w1, w3, w2, gf, *, tb, rt):
    rows = x.shape[0]
    return pl.pallas_call(
        functools.partial(_moe_kernel, rt=rt),
        grid=(rows // tb, N_EXPERTS),
        in_specs=[
            pl.BlockSpec((tb, D_MODEL), lambda t, e: (t, 0)),
            _const_spec((1, D_MODEL)),
            _const_spec((N_EXPERTS, D_MODEL)),
            _const_spec((N_EXPERTS, 1)),
            pl.BlockSpec((1, D_MODEL, D_FF_E), lambda t, e: (e, 0, 0)),
            pl.BlockSpec((1, D_MODEL, D_FF_E), lambda t, e: (e, 0, 0)),
            pl.BlockSpec((1, D_FF_E, D_MODEL), lambda t, e: (e, 0, 0)),
            _const_spec((1, D_MODEL)),
        ],
        out_specs=pl.BlockSpec((tb, D_MODEL), lambda t, e: (t, 0)),
        out_shape=jax.ShapeDtypeStruct((rows, D_MODEL), F32),
        scratch_shapes=[
            pltpu.VMEM((tb, D_MODEL), BF16),
            pltpu.VMEM((N_EXPERTS, tb), jnp.int32),
            pltpu.VMEM((N_EXPERTS, tb), F32),
            pltpu.SMEM((N_EXPERTS,), jnp.int32),
            pltpu.VMEM((tb, D_MODEL), F32),
        ],
        compiler_params=_params("parallel", "arbitrary"),
        name="moe_final",
    )(x, g, rwt, rb, w1, w3, w2, gf)


def _t5_bucket_np(rel):
    half = N_BUCKETS // 2
    max_exact = half // 2
    ret = np.where(rel > 0, half, 0)
    n = np.abs(rel)
    nf = np.maximum(n, 1).astype(np.float32)
    large = max_exact + (np.log(nf / np.float32(max_exact)) / np.float32(math.log(MAX_DIST / max_exact))
                         * np.float32(half - max_exact)).astype(np.int32)
    large = np.minimum(large, half - 1)
    return ret + np.where(n < max_exact, n, large)


def _bias_variants(table, bucket, keeps):
    nq, nk = bucket.shape
    onehot = jnp.asarray(np.eye(N_BUCKETS, dtype=np.float32)[bucket])
    bias = jnp.einsum('rjb,bh->hjr', onehot, table, precision=lax.Precision.HIGHEST)
    bias = jnp.stack([jnp.where(keep.T[None], bias, NEG) for keep in keeps], axis=0)
    bias = bias.reshape(len(keeps), N_KV, GQ, nk, nq)
    return jnp.transpose(bias, (0, 1, 3, 2, 4)).reshape(len(keeps), N_KV, nk, GQ * nq)


def _lane_sink(sink, nq):
    return jnp.broadcast_to(sink.reshape(N_KV, 1, GQ, 1), (N_KV, 1, GQ, nq)).reshape(N_KV, 1, GQ * nq)


def _prompt_bias(table):
    r = np.arange(TQ)[:, None]
    j = np.arange(TQ + WINDOW)[None, :]
    band = (j - CHUNK * (r // CHUNK) >= 0) & (j - CHUNK * (r // CHUNK) < WINDOW + CHUNK)
    first = band & (j >= WINDOW - N_META)
    return _bias_variants(table, _t5_bucket_np(j - WINDOW - r), [first, band])


def _small_bias(table):
    i = np.arange(SEG)[:, None]
    m = np.arange(WINDOW + SEG)[None, :]
    everything = np.ones((SEG, WINDOW + SEG), bool)
    meta_only = everything & (m >= WINDOW + SEG - N_META)
    return _bias_variants(table, _t5_bucket_np(m - WINDOW - i), [everything, meta_only])


def _pack_state(re, im):
    return jnp.concatenate([re[..., :HALF_STATE], im[..., :HALF_STATE], re[..., HALF_STATE:], im[..., HALF_STATE:]],
                           axis=-1)


def _cmul(ar, ai, br, bi):
    return ar * br - ai * bi, ar * bi + ai * br


def _s5_prepare(a_re, a_im, log_dt, b_re, b_im, c_re, c_im):
    dt = jnp.exp(log_dt)[:, None]
    mag = jnp.exp(a_re * dt)
    ab_re, ab_im = mag * jnp.cos(a_im * dt), mag * jnp.sin(a_im * dt)
    den = a_re * a_re + a_im * a_im
    num_re = ab_re - 1.0
    cf_re = (num_re * a_re + ab_im * a_im) / den
    cf_im = (ab_im * a_re - num_re * a_im) / den
    bb_re = cf_re[..., None] * b_re - cf_im[..., None] * b_im
    bb_im = cf_re[..., None] * b_im + cf_im[..., None] * b_re
    eye = jnp.eye(S5_GROUPS, dtype=F32)

    def in_map(bb):
        return jnp.einsum('gpj,gh->gjhp', bb, eye).reshape(S5_CH, S5_STATE)

    def out_map(c):
        return jnp.einsum('gjp,gh->gphj', c, eye).reshape(S5_STATE, S5_CH)

    def halves(m, rows, cols):
        return [m[h * rows:(h + 1) * rows, h * cols:(h + 1) * cols] for h in range(2)]

    bb = jnp.stack([jnp.concatenate([r, i], axis=1) for r, i in
                    zip(halves(in_map(bb_re), HALF_CH, HALF_STATE), halves(in_map(bb_im), HALF_CH, HALF_STATE))])
    cc = jnp.stack([jnp.concatenate([r, -i], axis=0) for r, i in
                    zip(halves(out_map(c_re), HALF_STATE, HALF_CH), halves(out_map(c_im), HALF_STATE, HALF_CH))])
    bb, cc = bb.astype(BF16), cc.astype(BF16)

    ar, ai = ab_re.reshape(1, S5_STATE), ab_im.reshape(1, S5_STATE)
    powers = [(ar, ai)]
    for _ in range(SCAN_ROWS - 1):
        powers.append(_cmul(*powers[-1], ar, ai))
    row = np.arange(SCAN_ROWS)[:, None]

    def shifted(k):
        return [jnp.where(row >= k, p, 0.0) for p in powers[k - 1]]

    carry = [jnp.concatenate([p[i] for p in powers], axis=0) for i in range(2)]
    consts = jnp.stack(shifted(1) + shifted(2) + shifted(4) + carry, axis=0)
    return bb, cc, consts


def kernel(x_prompt, x_sample, cache_conv, state_s5_re, state_s5_im, cache_swa_k, cache_swa_v, meta_tokens, rel_bias_table, norm_mix, norm_ffn, norm_final, w_in0, conv_w, conv_b, s5_a_re, s5_a_im, s5_log_dt, s5_b_re, s5_b_im, s5_c_re, s5_c_im, s5_d, s5_glu_w, s5_glu_b, w_out0, ffn_w1, ffn_w3, ffn_w2, w_qkv, b_qkv, attn_sink, w_o, b_o, router_w, router_b, moe_w1, moe_w3, moe_w2):
    nb, seq = x_prompt.shape[0], x_prompt.shape[1]
    nsb = x_sample.shape[0]
    assert x_sample.shape[1] == SEG and nsb == N_SAMPLE_SEG and seq % TM_MIX == 0

    def row(v):
        return v.reshape(1, -1)

    xp = x_prompt.reshape(nb * seq, D_MODEL)
    meta_seg = jnp.concatenate([jnp.zeros((SEG - N_META, D_MODEL), F32), meta_tokens], axis=0)
    filler = jnp.zeros(((N_SMALL_SEG - N_SAMPLE_SEG - 1) * SEG, D_MODEL), F32)
    xs = jnp.concatenate([x_sample.reshape(nsb * SEG, D_MODEL), meta_seg, filler], axis=0)
    n_extra = N_SMALL_SEG - N_SAMPLE_SEG

    bb, cc, consts = _s5_prepare(s5_a_re[0], s5_a_im[0], s5_log_dt[0], s5_b_re[0], s5_b_im[0], s5_c_re[0],
                                 s5_c_im[0])
    mixer_w = (row(norm_mix[0]), w_in0[0].astype(BF16), conv_w[0], row(conv_b[0]))
    mixer_w2 = (consts, bb, cc, row(s5_d[0]), s5_glu_w[0].astype(BF16), row(s5_glu_b[0]), w_out0[0].astype(BF16))
    cinit_s = jnp.concatenate([cache_conv[0], jnp.zeros((n_extra, 2, CONV_CH), F32)], axis=0)
    sinit_s = _pack_state(state_s5_re[0].reshape(nsb, 1, S5_STATE), state_s5_im[0].reshape(nsb, 1, S5_STATE))
    sinit_s = jnp.concatenate([sinit_s, jnp.zeros((n_extra, 1, 2 * S5_STATE), F32)], axis=0)
    hs, conv_s, state_s = _even_mixer(xs, *mixer_w, cinit_s, sinit_s, *mixer_w2, seg=SEG, nseg=TM_MIX // SEG,
                                      tiles_per_seq=1)
    cinit_p = jnp.broadcast_to(conv_s[META_SEG][None], (nb, 2, CONV_CH))
    sinit_p = jnp.broadcast_to(state_s[META_SEG][None], (nb, 1, 2 * S5_STATE))
    hp, conv_p, state_p = _even_mixer(xp, *mixer_w, cinit_p, sinit_p, *mixer_w2, seg=TM_MIX, nseg=1,
                                      tiles_per_seq=seq // TM_MIX)
    ffn_w = (row(norm_ffn[0]), ffn_w1[0].astype(BF16), ffn_w3[0].astype(BF16), ffn_w2[0].astype(BF16))
    hs = _ffn(hs, *ffn_w, tm=256)
    hp = _ffn(hp, *ffn_w, tm=512)

    qkv_w = (row(norm_mix[1]), w_qkv[0].astype(BF16), row(b_qkv[0]))
    qs, ks, vs = _qkv(hs, *qkv_w, tm=256)
    qp, kp, vp = _qkv(hp, *qkv_w, tm=512)
    wo = (w_o[0].astype(BF16), row(b_o[0]))
    hs = _attn_small(qs, cache_swa_k[0].reshape(nsb, WINDOW, KV_DIM), cache_swa_v[0].reshape(nsb, WINDOW, KV_DIM),
                     ks, vs, _small_bias(rel_bias_table), _lane_sink(attn_sink[0], SEG), hs, *wo)

    def padded(new, small):
        meta_rows = small[META_SEG * SEG:(META_SEG + 1) * SEG]
        front = jnp.concatenate([jnp.zeros((WINDOW - SEG, KV_DIM), F32), meta_rows], axis=0)
        front = jnp.broadcast_to(front[None], (nb, WINDOW, KV_DIM))
        return jnp.concatenate([front, new.reshape(nb, seq, KV_DIM)], axis=1).astype(BF16)

    hp = _attn_prompt(qp, padded(kp, ks), padded(vp, vs), _prompt_bias(rel_bias_table),
                      _lane_sink(attn_sink[0], TQ), hp, *wo)
    moe_w = (row(norm_ffn[1]), router_w[0].T.astype(BF16), router_b[0].reshape(N_EXPERTS, 1), moe_w1[0].astype(BF16),
             moe_w3[0].astype(BF16), moe_w2[0].astype(BF16), row(norm_final))
    ys = _moe_final(hs, *moe_w, tb=hs.shape[0], rt=352)
    yp = _moe_final(hp, *moe_w, tb=1024, rt=288)

    n_real = nsb * SEG
    y_prompt = yp.reshape(nb, seq, D_MODEL)
    y_sample = ys[:n_real].reshape(nsb, SEG, D_MODEL)

    def split_state(st, n):
        st = st[:n, 0].reshape(n, 2, 2, HALF_STATE)
        return tuple(st[:, :, ri].reshape(1, n, S5_GROUPS, S5_P) for ri in range(2))

    s5rp, s5ip = split_state(state_p, nb)
    s5rs, s5is = split_state(state_s, nsb)
    kp4 = kp.reshape(nb, seq, KV_DIM)[:, -WINDOW:].reshape(1, nb, WINDOW, N_KV, HEAD_DIM)
    vp4 = vp.reshape(nb, seq, KV_DIM)[:, -WINDOW:].reshape(1, nb, WINDOW, N_KV, HEAD_DIM)
    ks4 = ks[:n_real].reshape(1, nsb, SEG, N_KV, HEAD_DIM)
    vs4 = vs[:n_real].reshape(1, nsb, SEG, N_KV, HEAD_DIM)
    return (y_prompt, y_sample, conv_p[None], conv_s[:nsb][None], s5rp, s5ip, s5rs, s5is, kp4, vp4, ks4, vs4)
```

```python
import functools
import math

import numpy as np
import jax
import jax.numpy as jnp
from jax import lax
from jax.experimental import pallas as pl
from jax.experimental.pallas import tpu as pltpu

F32 = jnp.float32
BF16 = jnp.bfloat16

D_MODEL = 1024
CONV_CH = 512
S5_CH = 512
S5_GROUP = 16
S5_GROUPS = 32
S5_P = 64
S5_STATE = S5_GROUPS * S5_P
HALF_CH = S5_CH // 2
HALF_STATE = S5_STATE // 2
D_FF = 2816
N_HEADS = 16
N_KV = 2
GQ = N_HEADS // N_KV
HEAD_DIM = 64
KV_DIM = N_KV * HEAD_DIM
WINDOW = 128
CHUNK = 64
N_META = 16
N_BUCKETS = 32
MAX_DIST = 128
N_EXPERTS = 8
D_FF_E = 1024
EPS = 1e-6
NEG = -1e30

SEG = 32
N_SAMPLE_SEG = 32
META_SEG = N_SAMPLE_SEG
N_SMALL_SEG = 40
TM_MIX = 256
SCAN_ROWS = 8
SCAN_LANES = 512
TQ = 128
TQ_PER_STEP = 2
SEG_PER_STEP = 8
VMEM_LIMIT = 56 * 1024 * 1024


def _const_spec(shape):
    nd = len(shape)
    return pl.BlockSpec(shape, lambda *_: (0,) * nd, pipeline_mode=pl.Buffered(1))


def _params(*sem):
    return pltpu.CompilerParams(dimension_semantics=sem, vmem_limit_bytes=VMEM_LIMIT)


def _rms(x, g):
    return x * lax.rsqrt(jnp.mean(x * x, axis=-1, keepdims=True) + EPS) * g


def _dot(a, b):
    return jnp.dot(a, b, preferred_element_type=F32)


def _even_mixer_kernel(x_ref, gm_ref, win_ref, cw_ref, cb_ref, cinit_ref, sinit_ref, sc_ref, bb_ref, cc_ref,
                       d_ref, gluw_ref, glub_ref, wout_ref,
                       h_ref, cout_ref, sout_ref,
                       bu_ref, mix_ref, ccarry_ref, scarry_ref, *, seg, nseg, tiles_per_seq):
    t = pl.program_id(0)
    x = x_ref[...]
    hn = _rms(x, gm_ref[...]).astype(BF16)
    proj = _dot(hn, win_ref[...])
    g_b = proj[:, :CONV_CH]
    cin = proj[:, CONV_CH:2 * CONV_CH] * proj[:, 2 * CONV_CH:3 * CONV_CH]
    u = proj[:, 3 * CONV_CH:]
    ub = u.astype(BF16)
    for half in range(2):
        bu_ref[:, half * S5_STATE:(half + 1) * S5_STATE] = _dot(ub[:, half * HALF_CH:(half + 1) * HALF_CH],
                                                               bb_ref[half])
    if tiles_per_seq > 1:
        @pl.when(t % tiles_per_seq == 0)
        def _load_initial_state():
            ccarry_ref[...] = cinit_ref[0]
            scarry_ref[...] = sinit_ref[0]

    cw = cw_ref[...]
    row = lax.broadcasted_iota(jnp.int32, (seg, CONV_CH), 0)
    for j in range(nseg):
        rows = slice(j * seg, (j + 1) * seg)
        c_seg = cin[rows]
        init = ccarry_ref[...] if tiles_per_seq > 1 else cinit_ref[j]
        older, newer = init[0:1], init[1:2]
        p1 = jnp.where(row == 0, newer, pltpu.roll(c_seg, 1, 0))
        p2 = jnp.where(row == 0, older, jnp.where(row == 1, newer, pltpu.roll(c_seg, 2, 0)))
        conv = cw[0:1] * p2 + cw[1:2] * p1 + cw[2:3] * c_seg + cb_ref[...]
        mix_ref[rows, :CONV_CH] = g_b[rows] * conv
        tail = c_seg[seg - 2:seg]
        cout_ref[j] = tail
        if tiles_per_seq > 1:
            ccarry_ref[...] = tail

    groups_per_seg = seg // SCAN_ROWS
    n_groups = nseg * groups_per_seg
    for c in range(S5_STATE // SCAN_LANES):
        half, within = divmod(c * SCAN_LANES, HALF_STATE)
        re_cols = slice(half * S5_STATE + within, half * S5_STATE + within + SCAN_LANES)
        im_cols = slice(re_cols.start + HALF_STATE, re_cols.stop + HALF_STATE)
        a1r, a1i, a2r, a2i, a4r, a4i, pr, pi = [sc_ref[k, :, c * SCAN_LANES:(c + 1) * SCAN_LANES] for k in range(8)]

        def seg_init(s, cols):
            return jnp.broadcast_to(sinit_ref[s, :, cols], (SCAN_ROWS, SCAN_LANES))

        def body(r, carry, re_cols=re_cols, im_cols=im_cols, a1r=a1r, a1i=a1i, a2r=a2r, a2i=a2i, a4r=a4r,
                 a4i=a4i, pr=pr, pi=pi, seg_init=seg_init):
            cr, ci = carry
            if nseg > 1:
                s = r // groups_per_seg
                starts = (r % groups_per_seg) == 0
                cr = jnp.where(starts, seg_init(s, re_cols), cr)
                ci = jnp.where(starts, seg_init(s, im_cols), ci)
            rows = pl.ds(pl.multiple_of(r * SCAN_ROWS, SCAN_ROWS), SCAN_ROWS)
            vr = bu_ref[rows, re_cols]
            vi = bu_ref[rows, im_cols]
            for ar, ai, sh in ((a1r, a1i, 1), (a2r, a2i, 2), (a4r, a4i, 4)):
                sr = pltpu.roll(vr, sh, 0)
                si = pltpu.roll(vi, sh, 0)
                vr, vi = vr + ar * sr - ai * si, vi + ar * si + ai * sr
            vr, vi = vr + pr * cr - pi * ci, vi + pr * ci + pi * cr
            bu_ref[rows, re_cols] = vr
            bu_ref[rows, im_cols] = vi
            if nseg > 1:
                s = r // groups_per_seg
                sout_ref[s, :, re_cols] = vr[SCAN_ROWS - 1:SCAN_ROWS]
                sout_ref[s, :, im_cols] = vi[SCAN_ROWS - 1:SCAN_ROWS]
            return (jnp.broadcast_to(vr[SCAN_ROWS - 1:SCAN_ROWS], (SCAN_ROWS, SCAN_LANES)),
                    jnp.broadcast_to(vi[SCAN_ROWS - 1:SCAN_ROWS], (SCAN_ROWS, SCAN_LANES)))

        if nseg > 1:
            zero = jnp.zeros((SCAN_ROWS, SCAN_LANES), F32)
            lax.fori_loop(0, n_groups, body, (zero, zero), unroll=2)
        else:
            state = scarry_ref if tiles_per_seq > 1 else sinit_ref.at[0]
            cr, ci = lax.fori_loop(0, n_groups, body,
                                   (jnp.broadcast_to(state[:, re_cols], (SCAN_ROWS, SCAN_LANES)),
                                    jnp.broadcast_to(state[:, im_cols], (SCAN_ROWS, SCAN_LANES))), unroll=True)
            scarry_ref[:, re_cols] = cr[0:1]
            scarry_ref[:, im_cols] = ci[0:1]
            sout_ref[0, :, re_cols] = cr[0:1]
            sout_ref[0, :, im_cols] = ci[0:1]

    y = jnp.concatenate([_dot(bu_ref[:, half * S5_STATE:(half + 1) * S5_STATE].astype(BF16), cc_ref[half])
                         for half in range(2)], axis=-1) + d_ref[...] * u
    z = jax.nn.gelu(y)
    mix_ref[:, CONV_CH:] = z * jax.nn.sigmoid(_dot(z.astype(BF16), gluw_ref[...]) + glub_ref[...])
    h_ref[...] = x + _dot(mix_ref[...].astype(BF16), wout_ref[...])


def _even_mixer(x, gm, w_in, cw, cb, cinit, sinit, scan_consts, bb, cc, d, glu_w, glu_b, w_out, *, seg, nseg,
                tiles_per_seq):
    rows = x.shape[0]
    tm = seg * nseg
    nseq = cinit.shape[0]
    kern = functools.partial(_even_mixer_kernel, seg=seg, nseg=nseg, tiles_per_seq=tiles_per_seq)
    return pl.pallas_call(
        kern,
        grid=(rows // tm,),
        in_specs=[
            pl.BlockSpec((tm, D_MODEL), lambda t: (t, 0)),
            _const_spec((1, D_MODEL)),
            _const_spec((D_MODEL, 4 * CONV_CH)),
            _const_spec((3, CONV_CH)),
            _const_spec((1, CONV_CH)),
            pl.BlockSpec((nseg, 2, CONV_CH), lambda t: (t // tiles_per_seq, 0, 0)),
            pl.BlockSpec((nseg, 1, 2 * S5_STATE), lambda t: (t // tiles_per_seq, 0, 0)),
            _const_spec((8, SCAN_ROWS, S5_STATE)),
            _const_spec((2, HALF_CH, S5_STATE)),
            _const_spec((2, S5_STATE, HALF_CH)),
            _const_spec((1, S5_CH)),
            _const_spec((S5_CH, S5_CH)),
            _const_spec((1, S5_CH)),
            _const_spec((D_MODEL, D_MODEL)),
        ],
        out_specs=[
            pl.BlockSpec((tm, D_MODEL), lambda t: (t, 0)),
            pl.BlockSpec((nseg, 2, CONV_CH), lambda t: (t // tiles_per_seq, 0, 0)),
            pl.BlockSpec((nseg, 1, 2 * S5_STATE), lambda t: (t // tiles_per_seq, 0, 0)),
        ],
        out_shape=[
            jax.ShapeDtypeStruct((rows, D_MODEL), F32),
            jax.ShapeDtypeStruct((nseq, 2, CONV_CH), F32),
            jax.ShapeDtypeStruct((nseq, 1, 2 * S5_STATE), F32),
        ],
        scratch_shapes=[
            pltpu.VMEM((tm, 2 * S5_STATE), F32),
            pltpu.VMEM((tm, D_MODEL), F32),
            pltpu.VMEM((2, CONV_CH), F32),
            pltpu.VMEM((1, 2 * S5_STATE), F32),
        ],
        compiler_params=_params("arbitrary"),
        name="even_mixer",
    )(x, gm, w_in, cw, cb, cinit, sinit, scan_consts, bb, cc, d, glu_w, glu_b, w_out)


def _ffn_kernel(x_ref, g_ref, w1_ref, w3_ref, w2_ref, o_ref):
    x = x_ref[...]
    hn = _rms(x, g_ref[...]).astype(BF16)
    a = _dot(hn, w1_ref[...])
    b = _dot(hn, w3_ref[...])
    o_ref[...] = x + _dot((a * jax.nn.sigmoid(a) * b).astype(BF16), w2_ref[...])


def _ffn(x, g, w1, w3, w2, *, tm):
    rows = x.shape[0]
    return pl.pallas_call(
        _ffn_kernel,
        grid=(rows // tm,),
        in_specs=[
            pl.BlockSpec((tm, D_MODEL), lambda t: (t, 0)),
            _const_spec((1, D_MODEL)),
            _const_spec((D_MODEL, D_FF)),
            _const_spec((D_MODEL, D_FF)),
            _const_spec((D_FF, D_MODEL)),
        ],
        out_specs=pl.BlockSpec((tm, D_MODEL), lambda t: (t, 0)),
        out_shape=jax.ShapeDtypeStruct((rows, D_MODEL), F32),
        compiler_params=_params("parallel"),
        name="ffn",
    )(x, g, w1, w3, w2)


def _qkv_kernel(x_ref, g_ref, w_ref, b_ref, q_ref, k_ref, v_ref):
    hn = _rms(x_ref[...], g_ref[...]).astype(BF16)
    qkv = _dot(hn, w_ref[...]) + b_ref[...]
    nq = N_HEADS * HEAD_DIM
    q_ref[...] = (qkv[:, :nq] * (HEAD_DIM ** -0.5)).astype(BF16)
    k_ref[...] = qkv[:, nq:nq + KV_DIM]
    v_ref[...] = qkv[:, nq + KV_DIM:]


def _qkv(x, g, w, b, *, tm):
    rows = x.shape[0]
    ncol = (N_HEADS + 2 * N_KV) * HEAD_DIM
    return pl.pallas_call(
        _qkv_kernel,
        grid=(rows // tm,),
        in_specs=[
            pl.BlockSpec((tm, D_MODEL), lambda t: (t, 0)),
            _const_spec((1, D_MODEL)),
            _const_spec((D_MODEL, ncol)),
            _const_spec((1, ncol)),
        ],
        out_specs=[
            pl.BlockSpec((tm, N_HEADS * HEAD_DIM), lambda t: (t, 0)),
            pl.BlockSpec((tm, KV_DIM), lambda t: (t, 0)),
            pl.BlockSpec((tm, KV_DIM), lambda t: (t, 0)),
        ],
        out_shape=[
            jax.ShapeDtypeStruct((rows, N_HEADS * HEAD_DIM), BF16),
            jax.ShapeDtypeStruct((rows, KV_DIM), F32),
            jax.ShapeDtypeStruct((rows, KV_DIM), F32),
        ],
        compiler_params=_params("parallel"),
        name="qkv",
    )(x, g, w, b)


def _attend_groups(q, keys, vals, bias_of_group, sink_of_group, o_scr, row0):
    nq = q.shape[0]
    for kv in range(N_KV):
        kv_cols = slice(kv * HEAD_DIM, (kv + 1) * HEAD_DIM)
        heads = [slice((kv * GQ + g) * HEAD_DIM, (kv * GQ + g + 1) * HEAD_DIM) for g in range(GQ)]
        qs = jnp.concatenate([q[:, cols] for cols in heads], axis=0)
        st = lax.dot_general(keys[:, kv_cols], qs, (((1,), (1,)), ((), ())),
                             preferred_element_type=F32) + bias_of_group(kv)
        sk = sink_of_group(kv)
        m = jnp.maximum(jnp.max(st, axis=0, keepdims=True), sk)
        p = jnp.exp(st - m)
        den = jnp.sum(p, axis=0, keepdims=True) + jnp.exp(sk - m)
        o = lax.dot_general((p * (1.0 / den)).astype(BF16), vals[:, kv_cols], (((0,), (0,)), ((), ())),
                            preferred_element_type=F32)
        for g, cols in enumerate(heads):
            o_scr[row0:row0 + nq, cols] = o[g * nq:(g + 1) * nq]


def _attn_prompt_kernel(q_ref, k_ref, v_ref, bias_ref, sink_ref, x_ref, wo_ref, bo_ref, o_ref, o_scr):
    t = pl.program_id(1)
    for u in range(TQ_PER_STEP):
        tile = t * TQ_PER_STEP + u
        start = pl.multiple_of(tile * TQ, TQ)
        keys = k_ref[0, pl.ds(start, TQ + WINDOW), :]
        vals = v_ref[0, pl.ds(start, TQ + WINDOW), :]
        variant = jnp.minimum(tile, 1)
        _attend_groups(q_ref[u * TQ:(u + 1) * TQ], keys, vals, lambda kv: bias_ref[variant, kv],
                       lambda kv: sink_ref[kv], o_scr, u * TQ)
    o_ref[...] = x_ref[...] + _dot(o_scr[...].astype(BF16), wo_ref[...]) + bo_ref[...]


def _attn_prompt(q, kpad, vpad, bias, sink, x, w_o, b_o):
    nb, lpad = kpad.shape[0], kpad.shape[1]
    tm = TQ_PER_STEP * TQ
    steps = (lpad - WINDOW) // tm
    return pl.pallas_call(
        _attn_prompt_kernel,
        grid=(nb, steps),
        in_specs=[
            pl.BlockSpec((tm, D_MODEL), lambda b, t: (b * steps + t, 0)),
            pl.BlockSpec((1, lpad, KV_DIM), lambda b, t: (b, 0, 0)),
            pl.BlockSpec((1, lpad, KV_DIM), lambda b, t: (b, 0, 0)),
            _const_spec((2, N_KV, TQ + WINDOW, GQ * TQ)),
            _const_spec((N_KV, 1, GQ * TQ)),
            pl.BlockSpec((tm, D_MODEL), lambda b, t: (b * steps + t, 0)),
            _const_spec((D_MODEL, D_MODEL)),
            _const_spec((1, D_MODEL)),
        ],
        out_specs=pl.BlockSpec((tm, D_MODEL), lambda b, t: (b * steps + t, 0)),
        out_shape=jax.ShapeDtypeStruct(x.shape, F32),
        scratch_shapes=[pltpu.VMEM((tm, D_MODEL), F32)],
        compiler_params=_params("parallel", "arbitrary"),
        name="attn_prompt",
    )(q, kpad, vpad, bias, sink, x, w_o, b_o)


def _attn_small_kernel(q_ref, kc_ref, vc_ref, kn_ref, vn_ref, bias_ref, sink_ref, x_ref, wo_ref, bo_ref,
                       o_ref, o_scr):
    step = pl.program_id(0)
    for j in range(SEG_PER_STEP):
        rows = slice(j * SEG, (j + 1) * SEG)
        keys = jnp.concatenate([kc_ref[j], kn_ref[rows]], axis=0).astype(BF16)
        vals = jnp.concatenate([vc_ref[j], vn_ref[rows]], axis=0).astype(BF16)
        variant = jnp.where(step * SEG_PER_STEP + j == META_SEG, 1, 0)
        _attend_groups(q_ref[rows], keys, vals, lambda kv: bias_ref[variant, kv], lambda kv: sink_ref[kv], o_scr,
                       j * SEG)
    o_ref[...] = x_ref[...] + _dot(o_scr[...].astype(BF16), wo_ref[...]) + bo_ref[...]


def _attn_small(q, kc, vc, kn, vn, bias, sink, x, w_o, b_o):
    tm = SEG_PER_STEP * SEG
    steps = x.shape[0] // tm
    cache_blocks = kc.shape[0] // SEG_PER_STEP
    cache_spec = pl.BlockSpec((SEG_PER_STEP, WINDOW, KV_DIM), lambda s: (jnp.minimum(s, cache_blocks - 1), 0, 0))
    return pl.pallas_call(
        _attn_small_kernel,
        grid=(steps,),
        in_specs=[
            pl.BlockSpec((tm, D_MODEL), lambda s: (s, 0)),
            cache_spec,
            cache_spec,
            pl.BlockSpec((tm, KV_DIM), lambda s: (s, 0)),
            pl.BlockSpec((tm, KV_DIM), lambda s: (s, 0)),
            _const_spec((2, N_KV, WINDOW + SEG, GQ * SEG)),
            _const_spec((N_KV, 1, GQ * SEG)),
            pl.BlockSpec((tm, D_MODEL), lambda s: (s, 0)),
            _const_spec((D_MODEL, D_MODEL)),
            _const_spec((1, D_MODEL)),
        ],
        out_specs=pl.BlockSpec((tm, D_MODEL), lambda s: (s, 0)),
        out_shape=jax.ShapeDtypeStruct(x.shape, F32),
        scratch_shapes=[pltpu.VMEM((tm, D_MODEL), F32)],
        compiler_params=_params("parallel"),
        name="attn_small",
    )(q, kc, vc, kn, vn, bias, sink, x, w_o, b_o)


def _moe_kernel(x_ref, g_ref, rwt_ref, rb_ref, w1_ref, w3_ref, w2_ref, gf_ref, o_ref,
                hn_scr, slot_scr, gate_scr, cnt_smem, acc_scr, yall_scr, *, rt):
    e = pl.program_id(1)
    tb = x_ref.shape[0]

    @pl.when(e == 0)
    def _route():
        hn = _rms(x_ref[...], g_ref[...]).astype(BF16)
        hn_scr[...] = hn
        logits = lax.dot_general(rwt_ref[...], hn, (((1,), (1,)), ((), ())),
                                 preferred_element_type=F32) + rb_ref[...]
        row = lax.broadcasted_iota(jnp.int32, logits.shape, 0).astype(F32)
        v1 = jnp.max(logits, axis=0, keepdims=True)
        i1 = jnp.min(jnp.where(logits == v1, row, float(N_EXPERTS)), axis=0, keepdims=True)
        rest = jnp.where(row == i1, -jnp.inf, logits)
        v2 = jnp.max(rest, axis=0, keepdims=True)
        i2 = jnp.min(jnp.where(rest == v2, row, float(N_EXPERTS)), axis=0, keepdims=True)
        e2 = jnp.exp(v2 - v1)
        den = 1.0 + e2
        gate_scr[...] = jnp.where(row == i1, 1.0 / den, 0.0) + jnp.where(row == i2, e2 / den, 0.0)
        chosen = (row == i1) | (row == i2)
        ind = jnp.where(chosen, 1.0, 0.0)
        earlier = (lax.broadcasted_iota(jnp.int32, (tb, tb), 0) < lax.broadcasted_iota(jnp.int32, (tb, tb), 1))
        rank = _dot(ind.astype(BF16), jnp.where(earlier, 1.0, 0.0).astype(BF16))
        slot_scr[...] = jnp.where(chosen, rank, -1.0).astype(jnp.int32)
        for k in range(N_EXPERTS):
            cnt_smem[k] = jnp.sum(ind[k:k + 1, :]).astype(jnp.int32)
        acc_scr[...] = jnp.zeros_like(acc_scr)

    slot = slot_scr[pl.ds(e, 1), :]
    gate = gate_scr[pl.ds(e, 1), :]

    def _scatter(onehot, rows):
        return lax.dot_general(onehot, rows, (((0,), (0,)), ((), ())), preferred_element_type=F32)

    def _expert_tile(i):
        packed_row = lax.broadcasted_iota(jnp.int32, (rt, tb), 0) + i * rt
        hit = packed_row == slot
        onehot = jnp.where(hit, 1.0, 0.0).astype(BF16)
        xe = _dot(onehot, hn_scr[...]).astype(BF16)
        a = _dot(xe, w1_ref[0])
        b = _dot(xe, w3_ref[0])
        y = _dot((a * jax.nn.sigmoid(a) * b).astype(BF16), w2_ref[0])
        g_row = jnp.sum(jnp.where(hit, gate, 0.0), axis=1, keepdims=True)
        return onehot, (y * g_row).astype(BF16)

    yall_scr[pl.ds(pl.multiple_of(e * rt, 16), rt), :] = _expert_tile(0)[1]

    def _overflow_tile(i, carry):
        acc_scr[...] += _scatter(*_expert_tile(i))
        return carry

    lax.fori_loop(1, (cnt_smem[e] + rt - 1) // rt, _overflow_tile, 0)

    @pl.when(e == N_EXPERTS - 1)
    def _finish():
        packed_row = lax.broadcasted_iota(jnp.int32, (rt, tb), 0)
        onehot = jnp.concatenate([jnp.where(packed_row == slot_scr[k:k + 1, :], 1.0, 0.0).astype(BF16)
                                  for k in range(N_EXPERTS)], axis=0)
        o_ref[...] = _rms(x_ref[...] + acc_scr[...] + _scatter(onehot, yall_scr[...]), gf_ref[...])


def _moe_final(x, g, rwt, rb, w1, w3, w2, gf, *, tb, rt):
    rows = x.shape[0]
    return pl.pallas_call(
        functools.partial(_moe_kernel, rt=rt),
        grid=(rows // tb, N_EXPERTS),
        in_specs=[
            pl.BlockSpec((tb, D_MODEL), lambda t, e: (t, 0)),
            _const_spec((1, D_MODEL)),
            _const_spec((N_EXPERTS, D_MODEL)),
            _const_spec((N_EXPERTS, 1)),
            pl.BlockSpec((1, D_MODEL, D_FF_E), lambda t, e: (e, 0, 0)),
            pl.BlockSpec((1, D_MODEL, D_FF_E), lambda t, e: (e, 0, 0)),
            pl.BlockSpec((1, D_FF_E, D_MODEL), lambda t, e: (e, 0, 0)),
            _const_spec((1, D_MODEL)),
        ],
        out_specs=pl.BlockSpec((tb, D_MODEL), lambda t, e: (t, 0)),
        out_shape=jax.ShapeDtypeStruct((rows, D_MODEL), F32),
        scratch_shapes=[
            pltpu.VMEM((tb, D_MODEL), BF16),
            pltpu.VMEM((N_EXPERTS, tb), jnp.int32),
            pltpu.VMEM((N_EXPERTS, tb), F32),
            pltpu.SMEM((N_EXPERTS,), jnp.int32),
            pltpu.VMEM((tb, D_MODEL), F32),
            pltpu.VMEM((N_EXPERTS * rt, D_MODEL), BF16),
        ],
        compiler_params=_params("parallel", "arbitrary"),
        name="moe_final",
    )(x, g, rwt, rb, w1, w3, w2, gf)


def _t5_bucket_np(rel):
    half = N_BUCKETS // 2
    max_exact = half // 2
    ret = np.where(rel > 0, half, 0)
    n = np.abs(rel)
    nf = np.maximum(n, 1).astype(np.float32)
    large = max_exact + (np.log(nf / np.float32(max_exact)) / np.float32(math.log(MAX_DIST / max_exact))
                         * np.float32(half - max_exact)).astype(np.int32)
    large = np.minimum(large, half - 1)
    return ret + np.where(n < max_exact, n, large)


def _bias_variants(table, bucket, keeps):
    nq, nk = bucket.shape
    onehot = jnp.asarray(np.eye(N_BUCKETS, dtype=np.float32)[bucket])
    bias = jnp.einsum('rjb,bh->hjr', onehot, table, precision=lax.Precision.HIGHEST)
    bias = jnp.stack([jnp.where(keep.T[None], bias, NEG) for keep in keeps], axis=0)
    bias = bias.reshape(len(keeps), N_KV, GQ, nk, nq)
    return jnp.transpose(bias, (0, 1, 3, 2, 4)).reshape(len(keeps), N_KV, nk, GQ * nq)


def _lane_sink(sink, nq):
    return jnp.broadcast_to(sink.reshape(N_KV, 1, GQ, 1), (N_KV, 1, GQ, nq)).reshape(N_KV, 1, GQ * nq)


def _prompt_bias(table):
    r = np.arange(TQ)[:, None]
    j = np.arange(TQ + WINDOW)[None, :]
    band = (j - CHUNK * (r // CHUNK) >= 0) & (j - CHUNK * (r // CHUNK) < WINDOW + CHUNK)
    first = band & (j >= WINDOW - N_META)
    return _bias_variants(table, _t5_bucket_np(j - WINDOW - r), [first, band])


def _small_bias(table):
    i = np.arange(SEG)[:, None]
    m = np.arange(WINDOW + SEG)[None, :]
    everything = np.ones((SEG, WINDOW + SEG), bool)
    meta_only = everything & (m >= WINDOW + SEG - N_META)
    return _bias_variants(table, _t5_bucket_np(m - WINDOW - i), [everything, meta_only])


def _pack_state(re, im):
    return jnp.concatenate([re[..., :HALF_STATE], im[..., :HALF_STATE], re[..., HALF_STATE:], im[..., HALF_STATE:]],
                           axis=-1)


def _cmul(ar, ai, br, bi):
    return ar * br - ai * bi, ar * bi + ai * br


def _s5_prepare(a_re, a_im, log_dt, b_re, b_im, c_re, c_im):
    dt = jnp.exp(log_dt)[:, None]
    mag = jnp.exp(a_re * dt)
    ab_re, ab_im = mag * jnp.cos(a_im * dt), mag * jnp.sin(a_im * dt)
    den = a_re * a_re + a_im * a_im
    num_re = ab_re - 1.0
    cf_re = (num_re * a_re + ab_im * a_im) / den
    cf_im = (ab_im * a_re - num_re * a_im) / den
    bb_re = cf_re[..., None] * b_re - cf_im[..., None] * b_im
    bb_im = cf_re[..., None] * b_im + cf_im[..., None] * b_re
    eye = jnp.eye(S5_GROUPS, dtype=F32)

    def in_map(bb):
        return jnp.einsum('gpj,gh->gjhp', bb, eye).reshape(S5_CH, S5_STATE)

    def out_map(c):
        return jnp.einsum('gjp,gh->gphj', c, eye).reshape(S5_STATE, S5_CH)

    def halves(m, rows, cols):
        return [m[h * rows:(h + 1) * rows, h * cols:(h + 1) * cols] for h in range(2)]

    bb = jnp.stack([jnp.concatenate([r, i], axis=1) for r, i in
                    zip(halves(in_map(bb_re), HALF_CH, HALF_STATE), halves(in_map(bb_im), HALF_CH, HALF_STATE))])
    cc = jnp.stack([jnp.concatenate([r, -i], axis=0) for r, i in
                    zip(halves(out_map(c_re), HALF_STATE, HALF_CH), halves(out_map(c_im), HALF_STATE, HALF_CH))])
    bb, cc = bb.astype(BF16), cc.astype(BF16)

    ar, ai = ab_re.reshape(1, S5_STATE), ab_im.reshape(1, S5_STATE)
    powers = [(ar, ai)]
    for _ in range(SCAN_ROWS - 1):
        powers.append(_cmul(*powers[-1], ar, ai))
    row = np.arange(SCAN_ROWS)[:, None]

    def shifted(k):
        return [jnp.where(row >= k, p, 0.0) for p in powers[k - 1]]

    carry = [jnp.concatenate([p[i] for p in powers], axis=0) for i in range(2)]
    consts = jnp.stack(shifted(1) + shifted(2) + shifted(4) + carry, axis=0)
    return bb, cc, consts


def kernel(x_prompt, x_sample, cache_conv, state_s5_re, state_s5_im, cache_swa_k, cache_swa_v, meta_tokens, rel_bias_table, norm_mix, norm_ffn, norm_final, w_in0, conv_w, conv_b, s5_a_re, s5_a_im, s5_log_dt, s5_b_re, s5_b_im, s5_c_re, s5_c_im, s5_d, s5_glu_w, s5_glu_b, w_out0, ffn_w1, ffn_w3, ffn_w2, w_qkv, b_qkv, attn_sink, w_o, b_o, router_w, router_b, moe_w1, moe_w3, moe_w2):
    nb, seq = x_prompt.shape[0], x_prompt.shape[1]
    nsb = x_sample.shape[0]
    assert x_sample.shape[1] == SEG and nsb == N_SAMPLE_SEG and seq % TM_MIX == 0

    def row(v):
        return v.reshape(1, -1)

    xp = x_prompt.reshape(nb * seq, D_MODEL)
    meta_seg = jnp.concatenate([jnp.zeros((SEG - N_META, D_MODEL), F32), meta_tokens], axis=0)
    filler = jnp.zeros(((N_SMALL_SEG - N_SAMPLE_SEG - 1) * SEG, D_MODEL), F32)
    xs = jnp.concatenate([x_sample.reshape(nsb * SEG, D_MODEL), meta_seg, filler], axis=0)
    n_extra = N_SMALL_SEG - N_SAMPLE_SEG

    bb, cc, consts = _s5_prepare(s5_a_re[0], s5_a_im[0], s5_log_dt[0], s5_b_re[0], s5_b_im[0], s5_c_re[0],
                                 s5_c_im[0])
    mixer_w = (row(norm_mix[0]), w_in0[0].astype(BF16), conv_w[0], row(conv_b[0]))
    mixer_w2 = (consts, bb, cc, row(s5_d[0]), s5_glu_w[0].astype(BF16), row(s5_glu_b[0]), w_out0[0].astype(BF16))
    cinit_s = jnp.concatenate([cache_conv[0], jnp.zeros((n_extra, 2, CONV_CH), F32)], axis=0)
    sinit_s = _pack_state(state_s5_re[0].reshape(nsb, 1, S5_STATE), state_s5_im[0].reshape(nsb, 1, S5_STATE))
    sinit_s = jnp.concatenate([sinit_s, jnp.zeros((n_extra, 1, 2 * S5_STATE), F32)], axis=0)
    hs, conv_s, state_s = _even_mixer(xs, *mixer_w, cinit_s, sinit_s, *mixer_w2, seg=SEG, nseg=TM_MIX // SEG,
                                      tiles_per_seq=1)
    cinit_p = jnp.broadcast_to(conv_s[META_SEG][None], (nb, 2, CONV_CH))
    sinit_p = jnp.broadcast_to(state_s[META_SEG][None], (nb, 1, 2 * S5_STATE))
    hp, conv_p, state_p = _even_mixer(xp, *mixer_w, cinit_p, sinit_p, *mixer_w2, seg=TM_MIX, nseg=1,
                                      tiles_per_seq=seq // TM_MIX)
    ffn_w = (row(norm_ffn[0]), ffn_w1[0].astype(BF16), ffn_w3[0].astype(BF16), ffn_w2[0].astype(BF16))
    hs = _ffn(hs, *ffn_w, tm=256)
    hp = _ffn(hp, *ffn_w, tm=512)

    qkv_w = (row(norm_mix[1]), w_qkv[0].astype(BF16), row(b_qkv[0]))
    qs, ks, vs = _qkv(hs, *qkv_w, tm=256)
    qp, kp, vp = _qkv(hp, *qkv_w, tm=512)
    wo = (w_o[0].astype(BF16), row(b_o[0]))
    hs = _attn_small(qs, cache_swa_k[0].reshape(nsb, WINDOW, KV_DIM), cache_swa_v[0].reshape(nsb, WINDOW, KV_DIM),
                     ks, vs, _small_bias(rel_bias_table), _lane_sink(attn_sink[0], SEG), hs, *wo)

    def padded(new, small):
        meta_rows = small[META_SEG * SEG:(META_SEG + 1) * SEG]
        front = jnp.concatenate([jnp.zeros((WINDOW - SEG, KV_DIM), F32), meta_rows], axis=0)
        front = jnp.broadcast_to(front[None], (nb, WINDOW, KV_DIM))
        return jnp.concatenate([front, new.reshape(nb, seq, KV_DIM)], axis=1).astype(BF16)

    hp = _attn_prompt(qp, padded(kp, ks), padded(vp, vs), _prompt_bias(rel_bias_table),
                      _lane_sink(attn_sink[0], TQ), hp, *wo)
    moe_w = (row(norm_ffn[1]), router_w[0].T.astype(BF16), router_b[0].reshape(N_EXPERTS, 1), moe_w1[0].astype(BF16),
             moe_w3[0].astype(BF16), moe_w2[0].astype(BF16), row(norm_final))
    ys = _moe_final(hs, *moe_w, tb=hs.shape[0], rt=352)
    yp = _moe_final(hp, *moe_w, tb=1024, rt=288)

    n_real = nsb * SEG
    y_prompt = yp.reshape(nb, seq, D_MODEL)
    y_sample = ys[:n_real].reshape(nsb, SEG, D_MODEL)

    def split_state(st, n):
        st = st[:n, 0].reshape(n, 2, 2, HALF_STATE)
        return tuple(st[:, :, ri].reshape(1, n, S5_GROUPS, S5_P) for ri in range(2))

    s5rp, s5ip = split_state(state_p, nb)
    s5rs, s5is = split_state(state_s, nsb)
    kp4 = kp.reshape(nb, seq, KV_DIM)[:, -WINDOW:].reshape(1, nb, WINDOW, N_KV, HEAD_DIM)
    vp4 = vp.reshape(nb, seq, KV_DIM)[:, -WINDOW:].reshape(1, nb, WINDOW, N_KV, HEAD_DIM)
    ks4 = ks[:n_real].reshape(1, nsb, SEG, N_KV, HEAD_DIM)
    vs4 = vs[:n_real].reshape(1, nsb, SEG, N_KV, HEAD_DIM)
    return (y_prompt, y_sample, conv_p[None], conv_s[:nsb][None], s5rp, s5ip, s5rs, s5is, kp4, vp4, ks4, vs4)
```

```python
import functools
import math

import numpy as np
import jax
import jax.numpy as jnp
from jax import lax
from jax.experimental import pallas as pl
from jax.experimental.pallas import tpu as pltpu

F32 = jnp.float32
BF16 = jnp.bfloat16

D_MODEL = 1024
CONV_CH = 512
S5_CH = 512
S5_GROUP = 16
S5_GROUPS = 32
S5_P = 64
S5_STATE = S5_GROUPS * S5_P
HALF_CH = S5_CH // 2
HALF_STATE = S5_STATE // 2
D_FF = 2816
N_HEADS = 16
N_KV = 2
GQ = N_HEADS // N_KV
HEAD_DIM = 64
KV_DIM = N_KV * HEAD_DIM
WINDOW = 128
CHUNK = 64
N_META = 16
N_BUCKETS = 32
MAX_DIST = 128
N_EXPERTS = 8
D_FF_E = 1024
EPS = 1e-6
NEG = -1e30

SEG = 32
N_SAMPLE_SEG = 32
META_SEG = N_SAMPLE_SEG
N_SMALL_SEG = 40
TM_MIX = 256
SCAN_ROWS = 8
SCAN_LANES = 512
TQ = 128
TQ_PER_STEP = 2
SEG_PER_STEP = 8
VMEM_LIMIT = 56 * 1024 * 1024


def _const_spec(shape):
    nd = len(shape)
    return pl.BlockSpec(shape, lambda *_: (0,) * nd, pipeline_mode=pl.Buffered(1))


def _params(*sem):
    return pltpu.CompilerParams(dimension_semantics=sem, vmem_limit_bytes=VMEM_LIMIT)


def _rms(x, g):
    return x * lax.rsqrt(jnp.mean(x * x, axis=-1, keepdims=True) + EPS) * g


def _dot(a, b):
    return jnp.dot(a, b, preferred_element_type=F32)


def _even_mixer_kernel(x_ref, gm_ref, win_ref, cw_ref, cb_ref, cinit_ref, sinit_ref, sc_ref, bb_ref, cc_ref,
                       d_ref, gluw_ref, glub_ref, wout_ref,
                       h_ref, cout_ref, sout_ref,
                       bu_ref, mix_ref, ccarry_ref, scarry_ref, *, seg, nseg, tiles_per_seq):
    t = pl.program_id(0)
    if tiles_per_seq > 1:
        @pl.when(t % tiles_per_seq == 0)
        def _load_initial_state():
            ccarry_ref[...] = cinit_ref[...]
            scarry_ref[...] = sinit_ref[...]

    for b in range(x_ref.shape[0]):
        _even_mixer_stream(b, x_ref, gm_ref, win_ref, cw_ref, cb_ref, cinit_ref, sinit_ref, sc_ref, bb_ref, cc_ref,
                           d_ref, gluw_ref, glub_ref, wout_ref, h_ref, cout_ref, sout_ref, bu_ref, mix_ref,
                           ccarry_ref, scarry_ref, seg=seg, nseg=nseg, carried=tiles_per_seq > 1)


def _even_mixer_stream(b, x_ref, gm_ref, win_ref, cw_ref, cb_ref, cinit_ref, sinit_ref, sc_ref, bb_ref, cc_ref,
                       d_ref, gluw_ref, glub_ref, wout_ref, h_ref, cout_ref, sout_ref, bu_ref, mix_ref,
                       ccarry_ref, scarry_ref, *, seg, nseg, carried):
    x = x_ref[b]
    hn = _rms(x, gm_ref[...]).astype(BF16)
    proj = _dot(hn, win_ref[...])
    g_b = proj[:, :CONV_CH]
    cin = proj[:, CONV_CH:2 * CONV_CH] * proj[:, 2 * CONV_CH:3 * CONV_CH]
    u = proj[:, 3 * CONV_CH:]
    ub = u.astype(BF16)
    for half in range(2):
        bu_ref[b, :, half * S5_STATE:(half + 1) * S5_STATE] = _dot(ub[:, half * HALF_CH:(half + 1) * HALF_CH],
                                                                  bb_ref[half])

    cw = cw_ref[...]
    row = lax.broadcasted_iota(jnp.int32, (seg, CONV_CH), 0)
    for j in range(nseg):
        rows = slice(j * seg, (j + 1) * seg)
        c_seg = cin[rows]
        init = ccarry_ref[b] if carried else cinit_ref[b * nseg + j]
        older, newer = init[0:1], init[1:2]
        p1 = jnp.where(row == 0, newer, pltpu.roll(c_seg, 1, 0))
        p2 = jnp.where(row == 0, older, jnp.where(row == 1, newer, pltpu.roll(c_seg, 2, 0)))
        conv = cw[0:1] * p2 + cw[1:2] * p1 + cw[2:3] * c_seg + cb_ref[...]
        mix_ref[b, rows, :CONV_CH] = g_b[rows] * conv
        tail = c_seg[seg - 2:seg]
        cout_ref[b * nseg + j] = tail
        if carried:
            ccarry_ref[b] = tail

    groups_per_seg = seg // SCAN_ROWS
    n_groups = nseg * groups_per_seg
    for c in range(S5_STATE // SCAN_LANES):
        half, within = divmod(c * SCAN_LANES, HALF_STATE)
        re_cols = slice(half * S5_STATE + within, half * S5_STATE + within + SCAN_LANES)
        im_cols = slice(re_cols.start + HALF_STATE, re_cols.stop + HALF_STATE)
        a1r, a1i, a2r, a2i, a4r, a4i, pr, pi = [sc_ref[k, :, c * SCAN_LANES:(c + 1) * SCAN_LANES] for k in range(8)]

        def seg_init(s, cols):
            return jnp.broadcast_to(sinit_ref[b * nseg + s, :, cols], (SCAN_ROWS, SCAN_LANES))

        def body(r, carry, re_cols=re_cols, im_cols=im_cols, a1r=a1r, a1i=a1i, a2r=a2r, a2i=a2i, a4r=a4r,
                 a4i=a4i, pr=pr, pi=pi, seg_init=seg_init):
            cr, ci = carry
            if nseg > 1:
                s = r // groups_per_seg
                starts = (r % groups_per_seg) == 0
                cr = jnp.where(starts, seg_init(s, re_cols), cr)
                ci = jnp.where(starts, seg_init(s, im_cols), ci)
            rows = pl.ds(pl.multiple_of(r * SCAN_ROWS, SCAN_ROWS), SCAN_ROWS)
            vr = bu_ref[b, rows, re_cols]
            vi = bu_ref[b, rows, im_cols]
            for ar, ai, sh in ((a1r, a1i, 1), (a2r, a2i, 2), (a4r, a4i, 4)):
                sr = pltpu.roll(vr, sh, 0)
                si = pltpu.roll(vi, sh, 0)
                vr, vi = vr + ar * sr - ai * si, vi + ar * si + ai * sr
            vr, vi = vr + pr * cr - pi * ci, vi + pr * ci + pi * cr
            bu_ref[b, rows, re_cols] = vr
            bu_ref[b, rows, im_cols] = vi
            if nseg > 1:
                s = r // groups_per_seg
                sout_ref[b * nseg + s, :, re_cols] = vr[SCAN_ROWS - 1:SCAN_ROWS]
                sout_ref[b * nseg + s, :, im_cols] = vi[SCAN_ROWS - 1:SCAN_ROWS]
            return (jnp.broadcast_to(vr[SCAN_ROWS - 1:SCAN_ROWS], (SCAN_ROWS, SCAN_LANES)),
                    jnp.broadcast_to(vi[SCAN_ROWS - 1:SCAN_ROWS], (SCAN_ROWS, SCAN_LANES)))

        if nseg > 1:
            zero = jnp.zeros((SCAN_ROWS, SCAN_LANES), F32)
            lax.fori_loop(0, n_groups, body, (zero, zero), unroll=2)
        else:
            state = scarry_ref.at[b] if carried else sinit_ref.at[b]
            cr, ci = lax.fori_loop(0, n_groups, body,
                                   (jnp.broadcast_to(state[:, re_cols], (SCAN_ROWS, SCAN_LANES)),
                                    jnp.broadcast_to(state[:, im_cols], (SCAN_ROWS, SCAN_LANES))), unroll=True)
            scarry_ref[b, :, re_cols] = cr[0:1]
            scarry_ref[b, :, im_cols] = ci[0:1]
            sout_ref[b, :, re_cols] = cr[0:1]
            sout_ref[b, :, im_cols] = ci[0:1]

    y = jnp.concatenate([_dot(bu_ref[b, :, half * S5_STATE:(half + 1) * S5_STATE].astype(BF16), cc_ref[half])
                         for half in range(2)], axis=-1) + d_ref[...] * u
    z = jax.nn.gelu(y)
    mix_ref[b, :, CONV_CH:] = z * jax.nn.sigmoid(_dot(z.astype(BF16), gluw_ref[...]) + glub_ref[...])
    h_ref[b] = x + _dot(mix_ref[b].astype(BF16), wout_ref[...])


def _even_mixer(x, gm, w_in, cw, cb, cinit, sinit, scan_consts, bb, cc, d, glu_w, glu_b, w_out, *, seg, nseg,
                tiles_per_seq):
    nb, rows = x.shape[0], x.shape[1]
    tm = seg * nseg
    nseq = cinit.shape[0]
    per_step = nb * nseg
    kern = functools.partial(_even_mixer_kernel, seg=seg, nseg=nseg, tiles_per_seq=tiles_per_seq)
    return pl.pallas_call(
        kern,
        grid=(rows // tm,),
        in_specs=[
            pl.BlockSpec((nb, tm, D_MODEL), lambda t: (0, t, 0)),
            _const_spec((1, D_MODEL)),
            _const_spec((D_MODEL, 4 * CONV_CH)),
            _const_spec((3, CONV_CH)),
            _const_spec((1, CONV_CH)),
            pl.BlockSpec((per_step, 2, CONV_CH), lambda t: (t // tiles_per_seq, 0, 0)),
            pl.BlockSpec((per_step, 1, 2 * S5_STATE), lambda t: (t // tiles_per_seq, 0, 0)),
            _const_spec((8, SCAN_ROWS, S5_STATE)),
            _const_spec((2, HALF_CH, S5_STATE)),
            _const_spec((2, S5_STATE, HALF_CH)),
            _const_spec((1, S5_CH)),
            _const_spec((S5_CH, S5_CH)),
            _const_spec((1, S5_CH)),
            _const_spec((D_MODEL, D_MODEL)),
        ],
        out_specs=[
            pl.BlockSpec((nb, tm, D_MODEL), lambda t: (0, t, 0)),
            pl.BlockSpec((per_step, 2, CONV_CH), lambda t: (t // tiles_per_seq, 0, 0)),
            pl.BlockSpec((per_step, 1, 2 * S5_STATE), lambda t: (t // tiles_per_seq, 0, 0)),
        ],
        out_shape=[
            jax.ShapeDtypeStruct((nb, rows, D_MODEL), F32),
            jax.ShapeDtypeStruct((nseq, 2, CONV_CH), F32),
            jax.ShapeDtypeStruct((nseq, 1, 2 * S5_STATE), F32),
        ],
        scratch_shapes=[
            pltpu.VMEM((nb, tm, 2 * S5_STATE), F32),
            pltpu.VMEM((nb, tm, D_MODEL), F32),
            pltpu.VMEM((nb, 2, CONV_CH), F32),
            pltpu.VMEM((nb, 1, 2 * S5_STATE), F32),
        ],
        compiler_params=_params("arbitrary"),
        name="even_mixer",
    )(x, gm, w_in, cw, cb, cinit, sinit, scan_consts, bb, cc, d, glu_w, glu_b, w_out)


def _ffn_kernel(x_ref, g_ref, w1_ref, w3_ref, w2_ref, o_ref):
    x = x_ref[...]
    hn = _rms(x, g_ref[...]).astype(BF16)
    a = _dot(hn, w1_ref[...])
    b = _dot(hn, w3_ref[...])
    o_ref[...] = x + _dot((a * jax.nn.sigmoid(a) * b).astype(BF16), w2_ref[...])


def _ffn(x, g, w1, w3, w2, *, tm):
    rows = x.shape[0]
    return pl.pallas_call(
        _ffn_kernel,
        grid=(rows // tm,),
        in_specs=[
            pl.BlockSpec((tm, D_MODEL), lambda t: (t, 0)),
            _const_spec((1, D_MODEL)),
            _const_spec((D_MODEL, D_FF)),
            _const_spec((D_MODEL, D_FF)),
            _const_spec((D_FF, D_MODEL)),
        ],
        out_specs=pl.BlockSpec((tm, D_MODEL), lambda t: (t, 0)),
        out_shape=jax.ShapeDtypeStruct((rows, D_MODEL), F32),
        compiler_params=_params("parallel"),
        name="ffn",
    )(x, g, w1, w3, w2)


def _qkv_kernel(x_ref, g_ref, w_ref, b_ref, q_ref, k_ref, v_ref):
    hn = _rms(x_ref[...], g_ref[...]).astype(BF16)
    qkv = _dot(hn, w_ref[...]) + b_ref[...]
    nq = N_HEADS * HEAD_DIM
    q_ref[...] = (qkv[:, :nq] * (HEAD_DIM ** -0.5)).astype(BF16)
    k_ref[...] = qkv[:, nq:nq + KV_DIM]
    v_ref[...] = qkv[:, nq + KV_DIM:]


def _qkv(x, g, w, b, *, tm):
    rows = x.shape[0]
    ncol = (N_HEADS + 2 * N_KV) * HEAD_DIM
    return pl.pallas_call(
        _qkv_kernel,
        grid=(rows // tm,),
        in_specs=[
            pl.BlockSpec((tm, D_MODEL), lambda t: (t, 0)),
            _const_spec((1, D_MODEL)),
            _const_spec((D_MODEL, ncol)),
            _const_spec((1, ncol)),
        ],
        out_specs=[
            pl.BlockSpec((tm, N_HEADS * HEAD_DIM), lambda t: (t, 0)),
            pl.BlockSpec((tm, KV_DIM), lambda t: (t, 0)),
            pl.BlockSpec((tm, KV_DIM), lambda t: (t, 0)),
        ],
        out_shape=[
            jax.ShapeDtypeStruct((rows, N_HEADS * HEAD_DIM), BF16),
            jax.ShapeDtypeStruct((rows, KV_DIM), F32),
            jax.ShapeDtypeStruct((rows, KV_DIM), F32),
        ],
        compiler_params=_params("parallel"),
        name="qkv",
    )(x, g, w, b)


def _attend_groups(q, keys, vals, bias_of_group, sink_of_group, o_scr, row0):
    nq = q.shape[0]
    for kv in range(N_KV):
        kv_cols = slice(kv * HEAD_DIM, (kv + 1) * HEAD_DIM)
        heads = [slice((kv * GQ + g) * HEAD_DIM, (kv * GQ + g + 1) * HEAD_DIM) for g in range(GQ)]
        qs = jnp.concatenate([q[:, cols] for cols in heads], axis=0)
        st = lax.dot_general(keys[:, kv_cols], qs, (((1,), (1,)), ((), ())),
                             preferred_element_type=F32) + bias_of_group(kv)
        sk = sink_of_group(kv)
        m = jnp.maximum(jnp.max(st, axis=0, keepdims=True), sk)
        p = jnp.exp(st - m)
        den = jnp.sum(p, axis=0, keepdims=True) + jnp.exp(sk - m)
        o = lax.dot_general((p * (1.0 / den)).astype(BF16), vals[:, kv_cols], (((0,), (0,)), ((), ())),
                            preferred_element_type=F32)
        for g, cols in enumerate(heads):
            o_scr[row0:row0 + nq, cols] = o[g * nq:(g + 1) * nq]


def _attn_prompt_kernel(q_ref, k_ref, v_ref, bias_ref, sink_ref, x_ref, wo_ref, bo_ref, o_ref, o_scr):
    t = pl.program_id(1)
    for u in range(TQ_PER_STEP):
        tile = t * TQ_PER_STEP + u
        start = pl.multiple_of(tile * TQ, TQ)
        keys = k_ref[0, pl.ds(start, TQ + WINDOW), :]
        vals = v_ref[0, pl.ds(start, TQ + WINDOW), :]
        variant = jnp.minimum(tile, 1)
        _attend_groups(q_ref[u * TQ:(u + 1) * TQ], keys, vals, lambda kv: bias_ref[variant, kv],
                       lambda kv: sink_ref[kv], o_scr, u * TQ)
    o_ref[...] = x_ref[...] + _dot(o_scr[...].astype(BF16), wo_ref[...]) + bo_ref[...]


def _attn_prompt(q, kpad, vpad, bias, sink, x, w_o, b_o):
    nb, lpad = kpad.shape[0], kpad.shape[1]
    tm = TQ_PER_STEP * TQ
    steps = (lpad - WINDOW) // tm
    return pl.pallas_call(
        _attn_prompt_kernel,
        grid=(nb, steps),
        in_specs=[
            pl.BlockSpec((tm, D_MODEL), lambda b, t: (b * steps + t, 0)),
            pl.BlockSpec((1, lpad, KV_DIM), lambda b, t: (b, 0, 0)),
            pl.BlockSpec((1, lpad, KV_DIM), lambda b, t: (b, 0, 0)),
            _const_spec((2, N_KV, TQ + WINDOW, GQ * TQ)),
            _const_spec((N_KV, 1, GQ * TQ)),
            pl.BlockSpec((tm, D_MODEL), lambda b, t: (b * steps + t, 0)),
            _const_spec((D_MODEL, D_MODEL)),
            _const_spec((1, D_MODEL)),
        ],
        out_specs=pl.BlockSpec((tm, D_MODEL), lambda b, t: (b * steps + t, 0)),
        out_shape=jax.ShapeDtypeStruct(x.shape, F32),
        scratch_shapes=[pltpu.VMEM((tm, D_MODEL), F32)],
        compiler_params=_params("parallel", "arbitrary"),
        name="attn_prompt",
    )(q, kpad, vpad, bias, sink, x, w_o, b_o)


def _attn_small_kernel(q_ref, kc_ref, vc_ref, kn_ref, vn_ref, bias_ref, sink_ref, x_ref, wo_ref, bo_ref,
                       o_ref, o_scr):
    step = pl.program_id(0)
    for j in range(SEG_PER_STEP):
        rows = slice(j * SEG, (j + 1) * SEG)
        keys = jnp.concatenate([kc_ref[j], kn_ref[rows]], axis=0).astype(BF16)
        vals = jnp.concatenate([vc_ref[j], vn_ref[rows]], axis=0).astype(BF16)
        variant = jnp.where(step * SEG_PER_STEP + j == META_SEG, 1, 0)
        _attend_groups(q_ref[rows], keys, vals, lambda kv: bias_ref[variant, kv], lambda kv: sink_ref[kv], o_scr,
                       j * SEG)
    o_ref[...] = x_ref[...] + _dot(o_scr[...].astype(BF16), wo_ref[...]) + bo_ref[...]


def _attn_small(q, kc, vc, kn, vn, bias, sink, x, w_o, b_o):
    tm = SEG_PER_STEP * SEG
    steps = x.shape[0] // tm
    cache_blocks = kc.shape[0] // SEG_PER_STEP
    cache_spec = pl.BlockSpec((SEG_PER_STEP, WINDOW, KV_DIM), lambda s: (jnp.minimum(s, cache_blocks - 1), 0, 0))
    return pl.pallas_call(
        _attn_small_kernel,
        grid=(steps,),
        in_specs=[
            pl.BlockSpec((tm, D_MODEL), lambda s: (s, 0)),
            cache_spec,
            cache_spec,
            pl.BlockSpec((tm, KV_DIM), lambda s: (s, 0)),
            pl.BlockSpec((tm, KV_DIM), lambda s: (s, 0)),
            _const_spec((2, N_KV, WINDOW + SEG, GQ * SEG)),
            _const_spec((N_KV, 1, GQ * SEG)),
            pl.BlockSpec((tm, D_MODEL), lambda s: (s, 0)),
            _const_spec((D_MODEL, D_MODEL)),
            _const_spec((1, D_MODEL)),
        ],
        out_specs=pl.BlockSpec((tm, D_MODEL), lambda s: (s, 0)),
        out_shape=jax.ShapeDtypeStruct(x.shape, F32),
        scratch_shapes=[pltpu.VMEM((tm, D_MODEL), F32)],
        compiler_params=_params("parallel"),
        name="attn_small",
    )(q, kc, vc, kn, vn, bias, sink, x, w_o, b_o)


def _moe_kernel(x_ref, g_ref, rwt_ref, rb_ref, w1_ref, w3_ref, w2_ref, gf_ref, o_ref,
                hn_scr, slot_scr, gate_scr, cnt_smem, acc_scr, yall_scr, *, rt):
    e = pl.program_id(1)
    tb = x_ref.shape[0]

    @pl.when(e == 0)
    def _route():
        hn = _rms(x_ref[...], g_ref[...]).astype(BF16)
        hn_scr[...] = hn
        logits = lax.dot_general(rwt_ref[...], hn, (((1,), (1,)), ((), ())),
                                 preferred_element_type=F32) + rb_ref[...]
        row = lax.broadcasted_iota(jnp.int32, logits.shape, 0).astype(F32)
        v1 = jnp.max(logits, axis=0, keepdims=True)
        i1 = jnp.min(jnp.where(logits == v1, row, float(N_EXPERTS)), axis=0, keepdims=True)
        rest = jnp.where(row == i1, -jnp.inf, logits)
        v2 = jnp.max(rest, axis=0, keepdims=True)
        i2 = jnp.min(jnp.where(rest == v2, row, float(N_EXPERTS)), axis=0, keepdims=True)
        e2 = jnp.exp(v2 - v1)
        den = 1.0 + e2
        gate_scr[...] = jnp.where(row == i1, 1.0 / den, 0.0) + jnp.where(row == i2, e2 / den, 0.0)
        chosen = (row == i1) | (row == i2)
        ind = jnp.where(chosen, 1.0, 0.0)
        earlier = (lax.broadcasted_iota(jnp.int32, (tb, tb), 0) < lax.broadcasted_iota(jnp.int32, (tb, tb), 1))
        rank = _dot(ind.astype(BF16), jnp.where(earlier, 1.0, 0.0).astype(BF16))
        slot_scr[...] = jnp.where(chosen, rank, -1.0).astype(jnp.int32)
        for k in range(N_EXPERTS):
            cnt_smem[k] = jnp.sum(ind[k:k + 1, :]).astype(jnp.int32)
        acc_scr[...] = jnp.zeros_like(acc_scr)

    slot = slot_scr[pl.ds(e, 1), :]
    gate = gate_scr[pl.ds(e, 1), :]

    def _scatter(onehot, rows):
        return lax.dot_general(onehot, rows, (((0,), (0,)), ((), ())), preferred_element_type=F32)

    def _expert_tile(i):
        packed_row = lax.broadcasted_iota(jnp.int32, (rt, tb), 0) + i * rt
        hit = packed_row == slot
        onehot = jnp.where(hit, 1.0, 0.0).astype(BF16)
        xe = _dot(onehot, hn_scr[...]).astype(BF16)
        a = _dot(xe, w1_ref[0])
        b = _dot(xe, w3_ref[0])
        y = _dot((a * jax.nn.sigmoid(a) * b).astype(BF16), w2_ref[0])
        g_row = jnp.sum(jnp.where(hit, gate, 0.0), axis=1, keepdims=True)
        return onehot, (y * g_row).astype(BF16)

    yall_scr[pl.ds(pl.multiple_of(e * rt, 16), rt), :] = _expert_tile(0)[1]

    def _overflow_tile(i, carry):
        acc_scr[...] += _scatter(*_expert_tile(i))
        return carry

    lax.fori_loop(1, (cnt_smem[e] + rt - 1) // rt, _overflow_tile, 0)

    @pl.when(e == N_EXPERTS - 1)
    def _finish():
        packed_row = lax.broadcasted_iota(jnp.int32, (rt, tb), 0)
        onehot = jnp.concatenate([jnp.where(packed_row == slot_scr[k:k + 1, :], 1.0, 0.0).astype(BF16)
                                  for k in range(N_EXPERTS)], axis=0)
        o_ref[...] = _rms(x_ref[...] + acc_scr[...] + _scatter(onehot, yall_scr[...]), gf_ref[...])


def _moe_final(x, g, rwt, rb, w1, w3, w2, gf, *, tb, rt):
    rows = x.shape[0]
    return pl.pallas_call(
        functools.partial(_moe_kernel, rt=rt),
        grid=(rows // tb, N_EXPERTS),
        in_specs=[
            pl.BlockSpec((tb, D_MODEL), lambda t, e: (t, 0)),
            _const_spec((1, D_MODEL)),
            _const_spec((N_EXPERTS, D_MODEL)),
            _const_spec((N_EXPERTS, 1)),
            pl.BlockSpec((1, D_MODEL, D_FF_E), lambda t, e: (e, 0, 0)),
            pl.BlockSpec((1, D_MODEL, D_FF_E), lambda t, e: (e, 0, 0)),
            pl.BlockSpec((1, D_FF_E, D_MODEL), lambda t, e: (e, 0, 0)),
            _const_spec((1, D_MODEL)),
        ],
        out_specs=pl.BlockSpec((tb, D_MODEL), lambda t, e: (t, 0)),
        out_shape=jax.ShapeDtypeStruct((rows, D_MODEL), F32),
        scratch_shapes=[
            pltpu.VMEM((tb, D_MODEL), BF16),
            pltpu.VMEM((N_EXPERTS, tb), jnp.int32),
            pltpu.VMEM((N_EXPERTS, tb), F32),
            pltpu.SMEM((N_EXPERTS,), jnp.int32),
            pltpu.VMEM((tb, D_MODEL), F32),
            pltpu.VMEM((N_EXPERTS * rt, D_MODEL), BF16),
        ],
        compiler_params=_params("parallel", "arbitrary"),
        name="moe_final",
    )(x, g, rwt, rb, w1, w3, w2, gf)


def _t5_bucket_np(rel):
    half = N_BUCKETS // 2
    max_exact = half // 2
    ret = np.where(rel > 0, half, 0)
    n = np.abs(rel)
    nf = np.maximum(n, 1).astype(np.float32)
    large = max_exact + (np.log(nf / np.float32(max_exact)) / np.float32(math.log(MAX_DIST / max_exact))
                         * np.float32(half - max_exact)).astype(np.int32)
    large = np.minimum(large, half - 1)
    return ret + np.where(n < max_exact, n, large)


def _bias_variants(table, bucket, keeps):
    nq, nk = bucket.shape
    onehot = jnp.asarray(np.eye(N_BUCKETS, dtype=np.float32)[bucket])
    bias = jnp.einsum('rjb,bh->hjr', onehot, table, precision=lax.Precision.HIGHEST)
    bias = jnp.stack([jnp.where(keep.T[None], bias, NEG) for keep in keeps], axis=0)
    bias = bias.reshape(len(keeps), N_KV, GQ, nk, nq)
    return jnp.transpose(bias, (0, 1, 3, 2, 4)).reshape(len(keeps), N_KV, nk, GQ * nq)


def _lane_sink(sink, nq):
    return jnp.broadcast_to(sink.reshape(N_KV, 1, GQ, 1), (N_KV, 1, GQ, nq)).reshape(N_KV, 1, GQ * nq)


def _prompt_bias(table):
    r = np.arange(TQ)[:, None]
    j = np.arange(TQ + WINDOW)[None, :]
    band = (j - CHUNK * (r // CHUNK) >= 0) & (j - CHUNK * (r // CHUNK) < WINDOW + CHUNK)
    first = band & (j >= WINDOW - N_META)
    return _bias_variants(table, _t5_bucket_np(j - WINDOW - r), [first, band])


def _small_bias(table):
    i = np.arange(SEG)[:, None]
    m = np.arange(WINDOW + SEG)[None, :]
    everything = np.ones((SEG, WINDOW + SEG), bool)
    meta_only = everything & (m >= WINDOW + SEG - N_META)
    return _bias_variants(table, _t5_bucket_np(m - WINDOW - i), [everything, meta_only])


def _pack_state(re, im):
    return jnp.concatenate([re[..., :HALF_STATE], im[..., :HALF_STATE], re[..., HALF_STATE:], im[..., HALF_STATE:]],
                           axis=-1)


def _cmul(ar, ai, br, bi):
    return ar * br - ai * bi, ar * bi + ai * br


def _s5_prepare(a_re, a_im, log_dt, b_re, b_im, c_re, c_im):
    dt = jnp.exp(log_dt)[:, None]
    mag = jnp.exp(a_re * dt)
    ab_re, ab_im = mag * jnp.cos(a_im * dt), mag * jnp.sin(a_im * dt)
    den = a_re * a_re + a_im * a_im
    num_re = ab_re - 1.0
    cf_re = (num_re * a_re + ab_im * a_im) / den
    cf_im = (ab_im * a_re - num_re * a_im) / den
    bb_re = cf_re[..., None] * b_re - cf_im[..., None] * b_im
    bb_im = cf_re[..., None] * b_im + cf_im[..., None] * b_re
    eye = jnp.eye(S5_GROUPS, dtype=F32)

    def in_map(bb):
        return jnp.einsum('gpj,gh->gjhp', bb, eye).reshape(S5_CH, S5_STATE)

    def out_map(c):
        return jnp.einsum('gjp,gh->gphj', c, eye).reshape(S5_STATE, S5_CH)

    def halves(m, rows, cols):
        return [m[h * rows:(h + 1) * rows, h * cols:(h + 1) * cols] for h in range(2)]

    bb = jnp.stack([jnp.concatenate([r, i], axis=1) for r, i in
                    zip(halves(in_map(bb_re), HALF_CH, HALF_STATE), halves(in_map(bb_im), HALF_CH, HALF_STATE))])
    cc = jnp.stack([jnp.concatenate([r, -i], axis=0) for r, i in
                    zip(halves(out_map(c_re), HALF_STATE, HALF_CH), halves(out_map(c_im), HALF_STATE, HALF_CH))])
    bb, cc = bb.astype(BF16), cc.astype(BF16)

    ar, ai = ab_re.reshape(1, S5_STATE), ab_im.reshape(1, S5_STATE)
    powers = [(ar, ai)]
    for _ in range(SCAN_ROWS - 1):
        powers.append(_cmul(*powers[-1], ar, ai))
    row = np.arange(SCAN_ROWS)[:, None]

    def shifted(k):
        return [jnp.where(row >= k, p, 0.0) for p in powers[k - 1]]

    carry = [jnp.concatenate([p[i] for p in powers], axis=0) for i in range(2)]
    consts = jnp.stack(shifted(1) + shifted(2) + shifted(4) + carry, axis=0)
    return bb, cc, consts


def kernel(x_prompt, x_sample, cache_conv, state_s5_re, state_s5_im, cache_swa_k, cache_swa_v, meta_tokens, rel_bias_table, norm_mix, norm_ffn, norm_final, w_in0, conv_w, conv_b, s5_a_re, s5_a_im, s5_log_dt, s5_b_re, s5_b_im, s5_c_re, s5_c_im, s5_d, s5_glu_w, s5_glu_b, w_out0, ffn_w1, ffn_w3, ffn_w2, w_qkv, b_qkv, attn_sink, w_o, b_o, router_w, router_b, moe_w1, moe_w3, moe_w2):
    nb, seq = x_prompt.shape[0], x_prompt.shape[1]
    nsb = x_sample.shape[0]
    assert x_sample.shape[1] == SEG and nsb == N_SAMPLE_SEG and seq % TM_MIX == 0

    def row(v):
        return v.reshape(1, -1)

    meta_seg = jnp.concatenate([jnp.zeros((SEG - N_META, D_MODEL), F32), meta_tokens], axis=0)
    filler = jnp.zeros(((N_SMALL_SEG - N_SAMPLE_SEG - 1) * SEG, D_MODEL), F32)
    xs = jnp.concatenate([x_sample.reshape(nsb * SEG, D_MODEL), meta_seg, filler], axis=0)
    n_extra = N_SMALL_SEG - N_SAMPLE_SEG

    bb, cc, consts = _s5_prepare(s5_a_re[0], s5_a_im[0], s5_log_dt[0], s5_b_re[0], s5_b_im[0], s5_c_re[0],
                                 s5_c_im[0])
    mixer_w = (row(norm_mix[0]), w_in0[0].astype(BF16), conv_w[0], row(conv_b[0]))
    mixer_w2 = (consts, bb, cc, row(s5_d[0]), s5_glu_w[0].astype(BF16), row(s5_glu_b[0]), w_out0[0].astype(BF16))
    cinit_s = jnp.concatenate([cache_conv[0], jnp.zeros((n_extra, 2, CONV_CH), F32)], axis=0)
    sinit_s = _pack_state(state_s5_re[0].reshape(nsb, 1, S5_STATE), state_s5_im[0].reshape(nsb, 1, S5_STATE))
    sinit_s = jnp.concatenate([sinit_s, jnp.zeros((n_extra, 1, 2 * S5_STATE), F32)], axis=0)
    hs, conv_s, state_s = _even_mixer(xs[None], *mixer_w, cinit_s, sinit_s, *mixer_w2, seg=SEG, nseg=TM_MIX // SEG,
                                      tiles_per_seq=1)
    hs = hs[0]
    cinit_p = jnp.broadcast_to(conv_s[META_SEG][None], (nb, 2, CONV_CH))
    sinit_p = jnp.broadcast_to(state_s[META_SEG][None], (nb, 1, 2 * S5_STATE))
    hp, conv_p, state_p = _even_mixer(x_prompt, *mixer_w, cinit_p, sinit_p, *mixer_w2, seg=TM_MIX, nseg=1,
                                      tiles_per_seq=seq // TM_MIX)
    hp = hp.reshape(nb * seq, D_MODEL)
    ffn_w = (row(norm_ffn[0]), ffn_w1[0].astype(BF16), ffn_w3[0].astype(BF16), ffn_w2[0].astype(BF16))
    hs = _ffn(hs, *ffn_w, tm=256)
    hp = _ffn(hp, *ffn_w, tm=512)

    qkv_w = (row(norm_mix[1]), w_qkv[0].astype(BF16), row(b_qkv[0]))
    qs, ks, vs = _qkv(hs, *qkv_w, tm=256)
    qp, kp, vp = _qkv(hp, *qkv_w, tm=512)
    wo = (w_o[0].astype(BF16), row(b_o[0]))
    hs = _attn_small(qs, cache_swa_k[0].reshape(nsb, WINDOW, KV_DIM), cache_swa_v[0].reshape(nsb, WINDOW, KV_DIM),
                     ks, vs, _small_bias(rel_bias_table), _lane_sink(attn_sink[0], SEG), hs, *wo)

    def padded(new, small):
        meta_rows = small[META_SEG * SEG:(META_SEG + 1) * SEG]
        front = jnp.concatenate([jnp.zeros((WINDOW - SEG, KV_DIM), F32), meta_rows], axis=0)
        front = jnp.broadcast_to(front[None], (nb, WINDOW, KV_DIM))
        return jnp.concatenate([front, new.reshape(nb, seq, KV_DIM)], axis=1).astype(BF16)

    hp = _attn_prompt(qp, padded(kp, ks), padded(vp, vs), _prompt_bias(rel_bias_table),
                      _lane_sink(attn_sink[0], TQ), hp, *wo)
    moe_w = (row(norm_ffn[1]), router_w[0].T.astype(BF16), router_b[0].reshape(N_EXPERTS, 1), moe_w1[0].astype(BF16),
             moe_w3[0].astype(BF16), moe_w2[0].astype(BF16), row(norm_final))
    ys = _moe_final(hs, *moe_w, tb=hs.shape[0], rt=352)
    yp = _moe_final(hp, *moe_w, tb=1024, rt=288)

    n_real = nsb * SEG
    y_prompt = yp.reshape(nb, seq, D_MODEL)
    y_sample = ys[:n_real].reshape(nsb, SEG, D_MODEL)

    def split_state(st, n):
        st = st[:n, 0].reshape(n, 2, 2, HALF_STATE)
        return tuple(st[:, :, ri].reshape(1, n, S5_GROUPS, S5_P) for ri in range(2))

    s5rp, s5ip = split_state(state_p, nb)
    s5rs, s5is = split_state(state_s, nsb)
    kp4 = kp.reshape(nb, seq, KV_DIM)[:, -WINDOW:].reshape(1, nb, WINDOW, N_KV, HEAD_DIM)
    vp4 = vp.reshape(nb, seq, KV_DIM)[:, -WINDOW:].reshape(1, nb, WINDOW, N_KV, HEAD_DIM)
    ks4 = ks[:n_real].reshape(1, nsb, SEG, N_KV, HEAD_DIM)
    vs4 = vs[:n_real].reshape(1, nsb, SEG, N_KV, HEAD_DIM)
    return (y_prompt, y_sample, conv_p[None], conv_s[:nsb][None], s5rp, s5ip, s5rs, s5is, kp4, vp4, ks4, vs4)
```

```python
import functools
import math

import numpy as np
import jax
import jax.numpy as jnp
from jax import lax
from jax.experimental import pallas as pl
from jax.experimental.pallas import tpu as pltpu

F32 = jnp.float32
BF16 = jnp.bfloat16

D_MODEL = 1024
CONV_CH = 512
S5_CH = 512
S5_GROUP = 16
S5_GROUPS = 32
S5_P = 64
S5_STATE = S5_GROUPS * S5_P
HALF_CH = S5_CH // 2
HALF_STATE = S5_STATE // 2
D_FF = 2816
N_HEADS = 16
N_KV = 2
GQ = N_HEADS // N_KV
HEAD_DIM = 64
KV_DIM = N_KV * HEAD_DIM
WINDOW = 128
CHUNK = 64
N_META = 16
N_BUCKETS = 32
MAX_DIST = 128
N_EXPERTS = 8
D_FF_E = 1024
EPS = 1e-6
NEG = -1e30

SEG = 32
N_SAMPLE_SEG = 32
META_SEG = N_SAMPLE_SEG
N_SMALL_SEG = 40
TM_MIX = 256
CHUNK_ROWS = SEG
SCAN_LANES = 512
TQ = 128
TQ_PER_STEP = 2
SEG_PER_STEP = 8
VMEM_LIMIT = 56 * 1024 * 1024


def _const_spec(shape):
    nd = len(shape)
    return pl.BlockSpec(shape, lambda *_: (0,) * nd, pipeline_mode=pl.Buffered(1))


def _params(*sem):
    return pltpu.CompilerParams(dimension_semantics=sem, vmem_limit_bytes=VMEM_LIMIT)


def _rms(x, g):
    return x * lax.rsqrt(jnp.mean(x * x, axis=-1, keepdims=True) + EPS) * g


def _dot(a, b):
    return jnp.dot(a, b, preferred_element_type=F32)


def _even_mixer_kernel(x_ref, gm_ref, perm_ref, unperm_ref, win_ref, cw_ref, cb_ref, cinit_ref, sinit_ref,
                       ab_ref, hop_ref, pw_ref, bb_ref, cc_ref, d_ref, gluw_ref, glub_ref, wout_ref,
                       h_ref, cout_ref, sout_ref, bu_ref, ccarry_ref, scarry_ref, *, carried, tiles_per_seq):
    if carried:
        @pl.when(pl.program_id(0) % tiles_per_seq == 0)
        def _load_initial_state():
            ccarry_ref[...] = cinit_ref[...]
            scarry_ref[...] = sinit_ref[...]

    streams = range(x_ref.shape[0])
    mid = [_mixer_project(b, x_ref, gm_ref, perm_ref, win_ref, cw_ref, cb_ref, cinit_ref, bb_ref, cout_ref, bu_ref,
                          ccarry_ref, carried=carried) for b in streams]
    for b in streams:
        _mixer_recur(b, sinit_ref, ab_ref, hop_ref, pw_ref, sout_ref, bu_ref, scarry_ref, carried=carried)
        _mixer_output(b, *mid[b], unperm_ref, cc_ref, d_ref, gluw_ref, glub_ref, wout_ref, h_ref, bu_ref)


def _mixer_project(b, x_ref, gm_ref, perm_ref, win_ref, cw_ref, cb_ref, cinit_ref, bb_ref, cout_ref, bu_ref,
                   ccarry_ref, *, carried):
    tm = TM_MIX
    x = x_ref[b]
    hn = _rms(x, gm_ref[...]).astype(BF16)
    hn = _dot(perm_ref[...], hn).astype(BF16)
    proj = _dot(hn, win_ref[...])
    g_b = proj[:, :CONV_CH]
    cin = proj[:, CONV_CH:2 * CONV_CH] * proj[:, 2 * CONV_CH:3 * CONV_CH]
    u = proj[:, 3 * CONV_CH:]
    ub = u.astype(BF16)
    for half in range(2):
        bu_ref[b, :, half * S5_STATE:(half + 1) * S5_STATE] = _dot(ub[:, half * HALF_CH:(half + 1) * HALF_CH],
                                                                  bb_ref[half])

    last, before_last = cin[tm - 8:tm], cin[tm - 16:tm - 8]
    if carried:
        first_chunk = lax.broadcasted_iota(jnp.int32, (8, CONV_CH), 0) == 0
        newer = jnp.where(first_chunk, ccarry_ref[1, b:b + 1, :], pltpu.roll(last, 1, 0))
        older = jnp.where(first_chunk, ccarry_ref[0, b:b + 1, :], pltpu.roll(before_last, 1, 0))
        for i, rows in enumerate((before_last, last)):
            ccarry_ref[i, b:b + 1, :] = rows[7:8]
            cout_ref[i, b:b + 1, :] = rows[7:8]
    else:
        older, newer = cinit_ref[0], cinit_ref[1]
        cout_ref[0] = before_last
        cout_ref[1] = last
    ext = jnp.concatenate([older, newer, cin], axis=0)
    cw = cw_ref[...]
    out_a = g_b * (cw[0:1] * ext[:tm] + cw[1:2] * ext[8:tm + 8] + cw[2:3] * cin + cb_ref[...])
    return x, u, out_a


def _mixer_recur(b, sinit_ref, ab_ref, hop_ref, pw_ref, sout_ref, bu_ref, scarry_ref, *, carried):
    for c in range(S5_STATE // SCAN_LANES):
        half, within = divmod(c * SCAN_LANES, HALF_STATE)
        nat = slice(c * SCAN_LANES, (c + 1) * SCAN_LANES)
        re_cols = slice(half * S5_STATE + within, half * S5_STATE + within + SCAN_LANES)
        im_cols = slice(re_cols.start + HALF_STATE, re_cols.stop + HALF_STATE)
        ar, ai = ab_ref[0, :, nat], ab_ref[1, :, nat]
        if carried:
            sr = si = jnp.zeros((8, SCAN_LANES), F32)
        else:
            sr, si = sinit_ref[:, re_cols], sinit_ref[:, im_cols]
        for g in range(CHUNK_ROWS):
            rows = slice(8 * g, 8 * g + 8)
            sr, si = (ar * sr - ai * si + bu_ref[b, rows, re_cols], ar * si + ai * sr + bu_ref[b, rows, im_cols])
            bu_ref[b, rows, re_cols] = sr
            bu_ref[b, rows, im_cols] = si
        if not carried:
            sout_ref[:, re_cols] = sr
            sout_ref[:, im_cols] = si
            continue
        first_chunk = lax.broadcasted_iota(jnp.int32, (8, SCAN_LANES), 0) == 0
        er = jnp.where(first_chunk, scarry_ref[b:b + 1, re_cols], pltpu.roll(sr, 1, 0))
        ei = jnp.where(first_chunk, scarry_ref[b:b + 1, im_cols], pltpu.roll(si, 1, 0))
        for k, sh in enumerate((1, 2, 4)):
            wr, wi = hop_ref[2 * k, :, nat], hop_ref[2 * k + 1, :, nat]
            tr, ti = pltpu.roll(er, sh, 0), pltpu.roll(ei, sh, 0)
            er, ei = er + wr * tr - wi * ti, ei + wr * ti + wi * tr
        wr, wi = hop_ref[6, :, nat], hop_ref[7, :, nat]
        fr, fi = wr * er - wi * ei + sr, wr * ei + wi * er + si
        for ref in (scarry_ref, sout_ref):
            ref[b:b + 1, re_cols] = fr[7:8]
            ref[b:b + 1, im_cols] = fi[7:8]
        for g in range(CHUNK_ROWS):
            rows = slice(8 * g, 8 * g + 8)
            pr, pi = pw_ref[0, g, :, nat], pw_ref[1, g, :, nat]
            bu_ref[b, rows, re_cols] += pr * er - pi * ei
            bu_ref[b, rows, im_cols] += pr * ei + pi * er


def _mixer_output(b, x, u, out_a, unperm_ref, cc_ref, d_ref, gluw_ref, glub_ref, wout_ref, h_ref, bu_ref):
    y = jnp.concatenate([_dot(bu_ref[b, :, half * S5_STATE:(half + 1) * S5_STATE].astype(BF16), cc_ref[half])
                         for half in range(2)], axis=-1) + d_ref[...] * u
    z = jax.nn.gelu(y)
    out_b = z * jax.nn.sigmoid(_dot(z.astype(BF16), gluw_ref[...]) + glub_ref[...])
    mix = jnp.concatenate([out_a, out_b], axis=-1).astype(BF16)
    mix = _dot(unperm_ref[...], mix).astype(BF16)
    h_ref[b] = x + _dot(mix, wout_ref[...])


def _even_mixer(x, gm, w_in, cw, cb, cinit, sinit, ab, hop, pw, bb, cc, d, glu_w, glu_b, w_out, *, carried):
    nb, rows = x.shape[0], x.shape[1]
    tm = TM_MIX
    chunks = tm // CHUNK_ROWS
    assert chunks == 8 and rows % tm == 0 and (carried or nb == 1)
    per_step = nb if carried else chunks
    tiles_per_seq = rows // tm if carried else 1
    nseq = sinit.shape[0]
    t_of = np.arange(tm)
    perm = np.zeros((tm, tm), np.float32)
    perm[8 * (t_of % CHUNK_ROWS) + t_of // CHUNK_ROWS, t_of] = 1.0
    kern = functools.partial(_even_mixer_kernel, carried=carried, tiles_per_seq=tiles_per_seq)
    return pl.pallas_call(
        kern,
        grid=(rows // tm,),
        in_specs=[
            pl.BlockSpec((nb, tm, D_MODEL), lambda t: (0, t, 0)),
            _const_spec((1, D_MODEL)),
            _const_spec((tm, tm)),
            _const_spec((tm, tm)),
            _const_spec((D_MODEL, 4 * CONV_CH)),
            _const_spec((3, CONV_CH)),
            _const_spec((1, CONV_CH)),
            pl.BlockSpec((2, per_step, CONV_CH), lambda t: (0, t // tiles_per_seq, 0)),
            pl.BlockSpec((per_step, 2 * S5_STATE), lambda t: (t // tiles_per_seq, 0)),
            _const_spec((2, 8, S5_STATE)),
            _const_spec((8, 8, S5_STATE)),
            _const_spec((2, CHUNK_ROWS, 8, S5_STATE)),
            _const_spec((2, HALF_CH, S5_STATE)),
            _const_spec((2, S5_STATE, HALF_CH)),
            _const_spec((1, S5_CH)),
            _const_spec((S5_CH, S5_CH)),
            _const_spec((1, S5_CH)),
            _const_spec((D_MODEL, D_MODEL)),
        ],
        out_specs=[
            pl.BlockSpec((nb, tm, D_MODEL), lambda t: (0, t, 0)),
            pl.BlockSpec((2, per_step, CONV_CH), lambda t: (0, t // tiles_per_seq, 0)),
            pl.BlockSpec((per_step, 2 * S5_STATE), lambda t: (t // tiles_per_seq, 0)),
        ],
        out_shape=[
            jax.ShapeDtypeStruct((nb, rows, D_MODEL), F32),
            jax.ShapeDtypeStruct((2, nseq, CONV_CH), F32),
            jax.ShapeDtypeStruct((nseq, 2 * S5_STATE), F32),
        ],
        scratch_shapes=[
            pltpu.VMEM((nb, tm, 2 * S5_STATE), F32),
            pltpu.VMEM((2, nb, CONV_CH), F32),
            pltpu.VMEM((nb, 2 * S5_STATE), F32),
        ],
        compiler_params=_params("arbitrary"),
        name="even_mixer",
    )(x, gm, jnp.asarray(perm, BF16), jnp.asarray(perm.T, BF16), w_in, cw, cb, cinit, sinit, ab, hop, pw, bb, cc, d,
      glu_w, glu_b, w_out)


def _ffn_kernel(x_ref, g_ref, w1_ref, w3_ref, w2_ref, o_ref):
    x = x_ref[...]
    hn = _rms(x, g_ref[...]).astype(BF16)
    a = _dot(hn, w1_ref[...])
    b = _dot(hn, w3_ref[...])
    o_ref[...] = x + _dot((a * jax.nn.sigmoid(a) * b).astype(BF16), w2_ref[...])


def _ffn(x, g, w1, w3, w2, *, tm):
    rows = x.shape[0]
    return pl.pallas_call(
        _ffn_kernel,
        grid=(rows // tm,),
        in_specs=[
            pl.BlockSpec((tm, D_MODEL), lambda t: (t, 0)),
            _const_spec((1, D_MODEL)),
            _const_spec((D_MODEL, D_FF)),
            _const_spec((D_MODEL, D_FF)),
            _const_spec((D_FF, D_MODEL)),
        ],
        out_specs=pl.BlockSpec((tm, D_MODEL), lambda t: (t, 0)),
        out_shape=jax.ShapeDtypeStruct((rows, D_MODEL), F32),
        compiler_params=_params("parallel"),
        name="ffn",
    )(x, g, w1, w3, w2)


def _qkv_kernel(x_ref, g_ref, w_ref, b_ref, q_ref, k_ref, v_ref):
    hn = _rms(x_ref[...], g_ref[...]).astype(BF16)
    qkv = _dot(hn, w_ref[...]) + b_ref[...]
    nq = N_HEADS * HEAD_DIM
    q_ref[...] = (qkv[:, :nq] * (HEAD_DIM ** -0.5)).astype(BF16)
    k_ref[...] = qkv[:, nq:nq + KV_DIM]
    v_ref[...] = qkv[:, nq + KV_DIM:]


def _qkv(x, g, w, b, *, tm):
    rows = x.shape[0]
    ncol = (N_HEADS + 2 * N_KV) * HEAD_DIM
    return pl.pallas_call(
        _qkv_kernel,
        grid=(rows // tm,),
        in_specs=[
            pl.BlockSpec((tm, D_MODEL), lambda t: (t, 0)),
            _const_spec((1, D_MODEL)),
            _const_spec((D_MODEL, ncol)),
            _const_spec((1, ncol)),
        ],
        out_specs=[
            pl.BlockSpec((tm, N_HEADS * HEAD_DIM), lambda t: (t, 0)),
            pl.BlockSpec((tm, KV_DIM), lambda t: (t, 0)),
            pl.BlockSpec((tm, KV_DIM), lambda t: (t, 0)),
        ],
        out_shape=[
            jax.ShapeDtypeStruct((rows, N_HEADS * HEAD_DIM), BF16),
            jax.ShapeDtypeStruct((rows, KV_DIM), F32),
            jax.ShapeDtypeStruct((rows, KV_DIM), F32),
        ],
        compiler_params=_params("parallel"),
        name="qkv",
    )(x, g, w, b)


def _attend_groups(q, keys, vals, bias_of_group, sink_of_group, o_scr, row0):
    nq = q.shape[0]
    for kv in range(N_KV):
        kv_cols = slice(kv * HEAD_DIM, (kv + 1) * HEAD_DIM)
        heads = [slice((kv * GQ + g) * HEAD_DIM, (kv * GQ + g + 1) * HEAD_DIM) for g in range(GQ)]
        qs = jnp.concatenate([q[:, cols] for cols in heads], axis=0)
        st = lax.dot_general(keys[:, kv_cols], qs, (((1,), (1,)), ((), ())),
                             preferred_element_type=F32) + bias_of_group(kv)
        sk = sink_of_group(kv)
        m = jnp.maximum(jnp.max(st, axis=0, keepdims=True), sk)
        p = jnp.exp(st - m)
        den = jnp.sum(p, axis=0, keepdims=True) + jnp.exp(sk - m)
        o = lax.dot_general((p * (1.0 / den)).astype(BF16), vals[:, kv_cols], (((0,), (0,)), ((), ())),
                            preferred_element_type=F32)
        for g, cols in enumerate(heads):
            o_scr[row0:row0 + nq, cols] = o[g * nq:(g + 1) * nq]


def _attn_prompt_kernel(q_ref, k_ref, v_ref, bias_ref, sink_ref, x_ref, wo_ref, bo_ref, o_ref, o_scr):
    t = pl.program_id(1)
    for u in range(TQ_PER_STEP):
        tile = t * TQ_PER_STEP + u
        start = pl.multiple_of(tile * TQ, TQ)
        keys = k_ref[0, pl.ds(start, TQ + WINDOW), :]
        vals = v_ref[0, pl.ds(start, TQ + WINDOW), :]
        variant = jnp.minimum(tile, 1)
        _attend_groups(q_ref[u * TQ:(u + 1) * TQ], keys, vals, lambda kv: bias_ref[variant, kv],
                       lambda kv: sink_ref[kv], o_scr, u * TQ)
    o_ref[...] = x_ref[...] + _dot(o_scr[...].astype(BF16), wo_ref[...]) + bo_ref[...]


def _attn_prompt(q, kpad, vpad, bias, sink, x, w_o, b_o):
    nb, lpad = kpad.shape[0], kpad.shape[1]
    tm = TQ_PER_STEP * TQ
    steps = (lpad - WINDOW) // tm
    return pl.pallas_call(
        _attn_prompt_kernel,
        grid=(nb, steps),
        in_specs=[
            pl.BlockSpec((tm, D_MODEL), lambda b, t: (b * steps + t, 0)),
            pl.BlockSpec((1, lpad, KV_DIM), lambda b, t: (b, 0, 0)),
            pl.BlockSpec((1, lpad, KV_DIM), lambda b, t: (b, 0, 0)),
            _const_spec((2, N_KV, TQ + WINDOW, GQ * TQ)),
            _const_spec((N_KV, 1, GQ * TQ)),
            pl.BlockSpec((tm, D_MODEL), lambda b, t: (b * steps + t, 0)),
            _const_spec((D_MODEL, D_MODEL)),
            _const_spec((1, D_MODEL)),
        ],
        out_specs=pl.BlockSpec((tm, D_MODEL), lambda b, t: (b * steps + t, 0)),
        out_shape=jax.ShapeDtypeStruct(x.shape, F32),
        scratch_shapes=[pltpu.VMEM((tm, D_MODEL), F32)],
        compiler_params=_params("parallel", "arbitrary"),
        name="attn_prompt",
    )(q, kpad, vpad, bias, sink, x, w_o, b_o)


def _attn_small_kernel(q_ref, kc_ref, vc_ref, kn_ref, vn_ref, bias_ref, sink_ref, x_ref, wo_ref, bo_ref,
                       o_ref, o_scr):
    step = pl.program_id(0)
    for j in range(SEG_PER_STEP):
        rows = slice(j * SEG, (j + 1) * SEG)
        keys = jnp.concatenate([kc_ref[j], kn_ref[rows]], axis=0).astype(BF16)
        vals = jnp.concatenate([vc_ref[j], vn_ref[rows]], axis=0).astype(BF16)
        variant = jnp.where(step * SEG_PER_STEP + j == META_SEG, 1, 0)
        _attend_groups(q_ref[rows], keys, vals, lambda kv: bias_ref[variant, kv], lambda kv: sink_ref[kv], o_scr,
                       j * SEG)
    o_ref[...] = x_ref[...] + _dot(o_scr[...].astype(BF16), wo_ref[...]) + bo_ref[...]


def _attn_small(q, kc, vc, kn, vn, bias, sink, x, w_o, b_o):
    tm = SEG_PER_STEP * SEG
    steps = x.shape[0] // tm
    cache_blocks = kc.shape[0] // SEG_PER_STEP
    cache_spec = pl.BlockSpec((SEG_PER_STEP, WINDOW, KV_DIM), lambda s: (jnp.minimum(s, cache_blocks - 1), 0, 0))
    return pl.pallas_call(
        _attn_small_kernel,
        grid=(steps,),
        in_specs=[
            pl.BlockSpec((tm, D_MODEL), lambda s: (s, 0)),
            cache_spec,
            cache_spec,
            pl.BlockSpec((tm, KV_DIM), lambda s: (s, 0)),
            pl.BlockSpec((tm, KV_DIM), lambda s: (s, 0)),
            _const_spec((2, N_KV, WINDOW + SEG, GQ * SEG)),
            _const_spec((N_KV, 1, GQ * SEG)),
            pl.BlockSpec((tm, D_MODEL), lambda s: (s, 0)),
            _const_spec((D_MODEL, D_MODEL)),
            _const_spec((1, D_MODEL)),
        ],
        out_specs=pl.BlockSpec((tm, D_MODEL), lambda s: (s, 0)),
        out_shape=jax.ShapeDtypeStruct(x.shape, F32),
        scratch_shapes=[pltpu.VMEM((tm, D_MODEL), F32)],
        compiler_params=_params("parallel"),
        name="attn_small",
    )(q, kc, vc, kn, vn, bias, sink, x, w_o, b_o)


def _moe_kernel(x_ref, g_ref, rwt_ref, rb_ref, w1_ref, w3_ref, w2_ref, gf_ref, o_ref,
                hn_scr, slot_scr, gate_scr, cnt_smem, acc_scr, yall_scr, *, rt):
    e = pl.program_id(1)
    tb = x_ref.shape[0]

    @pl.when(e == 0)
    def _route():
        hn = _rms(x_ref[...], g_ref[...]).astype(BF16)
        hn_scr[...] = hn
        logits = lax.dot_general(rwt_ref[...], hn, (((1,), (1,)), ((), ())),
                                 preferred_element_type=F32) + rb_ref[...]
        row = lax.broadcasted_iota(jnp.int32, logits.shape, 0).astype(F32)
        v1 = jnp.max(logits, axis=0, keepdims=True)
        i1 = jnp.min(jnp.where(logits == v1, row, float(N_EXPERTS)), axis=0, keepdims=True)
        rest = jnp.where(row == i1, -jnp.inf, logits)
        v2 = jnp.max(rest, axis=0, keepdims=True)
        i2 = jnp.min(jnp.where(rest == v2, row, float(N_EXPERTS)), axis=0, keepdims=True)
        e2 = jnp.exp(v2 - v1)
        den = 1.0 + e2
        gate_scr[...] = jnp.where(row == i1, 1.0 / den, 0.0) + jnp.where(row == i2, e2 / den, 0.0)
        chosen = (row == i1) | (row == i2)
        ind = jnp.where(chosen, 1.0, 0.0)
        earlier = (lax.broadcasted_iota(jnp.int32, (tb, tb), 0) < lax.broadcasted_iota(jnp.int32, (tb, tb), 1))
        rank = _dot(ind.astype(BF16), jnp.where(earlier, 1.0, 0.0).astype(BF16))
        slot_scr[...] = jnp.where(chosen, rank, -1.0).astype(jnp.int32)
        for k in range(N_EXPERTS):
            cnt_smem[k] = jnp.sum(ind[k:k + 1, :]).astype(jnp.int32)
        acc_scr[...] = jnp.zeros_like(acc_scr)

    slot = slot_scr[pl.ds(e, 1), :]
    gate = gate_scr[pl.ds(e, 1), :]

    def _scatter(onehot, rows):
        return lax.dot_general(onehot, rows, (((0,), (0,)), ((), ())), preferred_element_type=F32)

    def _expert_tile(i):
        packed_row = lax.broadcasted_iota(jnp.int32, (rt, tb), 0) + i * rt
        hit = packed_row == slot
        onehot = jnp.where(hit, 1.0, 0.0).astype(BF16)
        xe = _dot(onehot, hn_scr[...]).astype(BF16)
        a = _dot(xe, w1_ref[0])
        b = _dot(xe, w3_ref[0])
        y = _dot((a * jax.nn.sigmoid(a) * b).astype(BF16), w2_ref[0])
        g_row = jnp.sum(jnp.where(hit, gate, 0.0), axis=1, keepdims=True)
        return onehot, (y * g_row).astype(BF16)

    yall_scr[pl.ds(pl.multiple_of(e * rt, 16), rt), :] = _expert_tile(0)[1]

    def _overflow_tile(i, carry):
        acc_scr[...] += _scatter(*_expert_tile(i))
        return carry

    lax.fori_loop(1, (cnt_smem[e] + rt - 1) // rt, _overflow_tile, 0)

    @pl.when(e == N_EXPERTS - 1)
    def _finish():
        packed_row = lax.broadcasted_iota(jnp.int32, (rt, tb), 0)
        onehot = jnp.concatenate([jnp.where(packed_row == slot_scr[k:k + 1, :], 1.0, 0.0).astype(BF16)
                                  for k in range(N_EXPERTS)], axis=0)
        o_ref[...] = _rms(x_ref[...] + acc_scr[...] + _scatter(onehot, yall_scr[...]), gf_ref[...])


def _moe_final(x, g, rwt, rb, w1, w3, w2, gf, *, tb, rt):
    rows = x.shape[0]
    return pl.pallas_call(
        functools.partial(_moe_kernel, rt=rt),
        grid=(rows // tb, N_EXPERTS),
        in_specs=[
            pl.BlockSpec((tb, D_MODEL), lambda t, e: (t, 0)),
            _const_spec((1, D_MODEL)),
            _const_spec((N_EXPERTS, D_MODEL)),
            _const_spec((N_EXPERTS, 1)),
            pl.BlockSpec((1, D_MODEL, D_FF_E), lambda t, e: (e, 0, 0)),
            pl.BlockSpec((1, D_MODEL, D_FF_E), lambda t, e: (e, 0, 0)),
            pl.BlockSpec((1, D_FF_E, D_MODEL), lambda t, e: (e, 0, 0)),
            _const_spec((1, D_MODEL)),
        ],
        out_specs=pl.BlockSpec((tb, D_MODEL), lambda t, e: (t, 0)),
        out_shape=jax.ShapeDtypeStruct((rows, D_MODEL), F32),
        scratch_shapes=[
            pltpu.VMEM((tb, D_MODEL), BF16),
            pltpu.VMEM((N_EXPERTS, tb), jnp.int32),
            pltpu.VMEM((N_EXPERTS, tb), F32),
            pltpu.SMEM((N_EXPERTS,), jnp.int32),
            pltpu.VMEM((tb, D_MODEL), F32),
            pltpu.VMEM((N_EXPERTS * rt, D_MODEL), BF16),
        ],
        compiler_params=_params("parallel", "arbitrary"),
        name="moe_final",
    )(x, g, rwt, rb, w1, w3, w2, gf)


def _t5_bucket_np(rel):
    half = N_BUCKETS // 2
    max_exact = half // 2
    ret = np.where(rel > 0, half, 0)
    n = np.abs(rel)
    nf = np.maximum(n, 1).astype(np.float32)
    large = max_exact + (np.log(nf / np.float32(max_exact)) / np.float32(math.log(MAX_DIST / max_exact))
                         * np.float32(half - max_exact)).astype(np.int32)
    large = np.minimum(large, half - 1)
    return ret + np.where(n < max_exact, n, large)


def _bias_variants(table, bucket, keeps):
    nq, nk = bucket.shape
    onehot = jnp.asarray(np.eye(N_BUCKETS, dtype=np.float32)[bucket])
    bias = jnp.einsum('rjb,bh->hjr', onehot, table, precision=lax.Precision.HIGHEST)
    bias = jnp.stack([jnp.where(keep.T[None], bias, NEG) for keep in keeps], axis=0)
    bias = bias.reshape(len(keeps), N_KV, GQ, nk, nq)
    return jnp.transpose(bias, (0, 1, 3, 2, 4)).reshape(len(keeps), N_KV, nk, GQ * nq)


def _lane_sink(sink, nq):
    return jnp.broadcast_to(sink.reshape(N_KV, 1, GQ, 1), (N_KV, 1, GQ, nq)).reshape(N_KV, 1, GQ * nq)


def _prompt_bias(table):
    r = np.arange(TQ)[:, None]
    j = np.arange(TQ + WINDOW)[None, :]
    band = (j - CHUNK * (r // CHUNK) >= 0) & (j - CHUNK * (r // CHUNK) < WINDOW + CHUNK)
    first = band & (j >= WINDOW - N_META)
    return _bias_variants(table, _t5_bucket_np(j - WINDOW - r), [first, band])


def _small_bias(table):
    i = np.arange(SEG)[:, None]
    m = np.arange(WINDOW + SEG)[None, :]
    everything = np.ones((SEG, WINDOW + SEG), bool)
    meta_only = everything & (m >= WINDOW + SEG - N_META)
    return _bias_variants(table, _t5_bucket_np(m - WINDOW - i), [everything, meta_only])


def _pack_state(re, im):
    return jnp.concatenate([re[..., :HALF_STATE], im[..., :HALF_STATE], re[..., HALF_STATE:], im[..., HALF_STATE:]],
                           axis=-1)


def _cmul(ar, ai, br, bi):
    return ar * br - ai * bi, ar * bi + ai * br


def _s5_prepare(a_re, a_im, log_dt, b_re, b_im, c_re, c_im):
    dt = jnp.exp(log_dt)[:, None]
    mag = jnp.exp(a_re * dt)
    ab_re, ab_im = mag * jnp.cos(a_im * dt), mag * jnp.sin(a_im * dt)
    den = a_re * a_re + a_im * a_im
    num_re = ab_re - 1.0
    cf_re = (num_re * a_re + ab_im * a_im) / den
    cf_im = (ab_im * a_re - num_re * a_im) / den
    bb_re = cf_re[..., None] * b_re - cf_im[..., None] * b_im
    bb_im = cf_re[..., None] * b_im + cf_im[..., None] * b_re

    def group_diag(v):
        r, c = v.shape[1], v.shape[2]
        gh = S5_GROUPS // 2
        spread = np.tile(np.eye(c, dtype=np.float32), (1, gh))
        wide = jnp.einsum('hrc,cn->hrn', v.reshape(2, gh * r, c), spread, precision=lax.Precision.HIGHEST)
        keep = (np.arange(gh * r)[:, None] // r) == (np.arange(gh * c)[None, :] // c)
        return jnp.where(keep, wide, 0.0)

    def t(v):
        return jnp.swapaxes(v, 1, 2)

    bb = jnp.concatenate([group_diag(t(bb_re)), group_diag(t(bb_im))], axis=2).astype(BF16)
    cc = jnp.concatenate([group_diag(t(c_re)), -group_diag(t(c_im))], axis=1).astype(BF16)

    pr, pi = ab_re.reshape(1, S5_STATE), ab_im.reshape(1, S5_STATE)
    n = 1
    while n < CHUNK_ROWS:
        qr, qi = _cmul(pr, pi, pr[n - 1:n], pi[n - 1:n])
        pr, pi = jnp.concatenate([pr, qr], axis=0), jnp.concatenate([pi, qi], axis=0)
        n *= 2

    def tile8(v):
        return jnp.broadcast_to(v, (8,) + v.shape[1:])

    ab = jnp.stack([tile8(pr[0:1]), tile8(pi[0:1])])
    sub = np.arange(8)[:, None]
    hop, w = [], (pr[CHUNK_ROWS - 1:], pi[CHUNK_ROWS - 1:])
    for k in (1, 2, 4):
        hop += [jnp.where(sub >= k, w[0], 0.0), jnp.where(sub >= k, w[1], 0.0)]
        if k < 4:
            w = _cmul(*w, *w)
    hop = jnp.stack(hop + [tile8(pr[CHUNK_ROWS - 1:]), tile8(pi[CHUNK_ROWS - 1:])])
    pw = jnp.stack([jnp.broadcast_to(p[:, None, :], (CHUNK_ROWS, 8, S5_STATE)) for p in (pr, pi)])
    return bb, cc, ab, hop, pw


def kernel(x_prompt, x_sample, cache_conv, state_s5_re, state_s5_im, cache_swa_k, cache_swa_v, meta_tokens, rel_bias_table, norm_mix, norm_ffn, norm_final, w_in0, conv_w, conv_b, s5_a_re, s5_a_im, s5_log_dt, s5_b_re, s5_b_im, s5_c_re, s5_c_im, s5_d, s5_glu_w, s5_glu_b, w_out0, ffn_w1, ffn_w3, ffn_w2, w_qkv, b_qkv, attn_sink, w_o, b_o, router_w, router_b, moe_w1, moe_w3, moe_w2):
    nb, seq = x_prompt.shape[0], x_prompt.shape[1]
    nsb = x_sample.shape[0]
    assert x_sample.shape[1] == SEG and nsb == N_SAMPLE_SEG and seq % TM_MIX == 0

    def row(v):
        return v.reshape(1, -1)

    meta_seg = jnp.concatenate([jnp.zeros((SEG - N_META, D_MODEL), F32), meta_tokens], axis=0)
    filler = jnp.zeros(((N_SMALL_SEG - N_SAMPLE_SEG - 1) * SEG, D_MODEL), F32)
    xs = jnp.concatenate([x_sample.reshape(nsb * SEG, D_MODEL), meta_seg, filler], axis=0)
    n_extra = N_SMALL_SEG - N_SAMPLE_SEG

    bb, cc, ab, hop, pw = _s5_prepare(s5_a_re[0], s5_a_im[0], s5_log_dt[0], s5_b_re[0], s5_b_im[0], s5_c_re[0],
                                      s5_c_im[0])
    mixer_w = (row(norm_mix[0]), w_in0[0].astype(BF16), conv_w[0], row(conv_b[0]))
    mixer_w2 = (ab, hop, pw, bb, cc, row(s5_d[0]), s5_glu_w[0].astype(BF16), row(s5_glu_b[0]), w_out0[0].astype(BF16))
    cinit_s = jnp.concatenate([jnp.swapaxes(cache_conv[0], 0, 1), jnp.zeros((2, n_extra, CONV_CH), F32)], axis=1)
    sinit_s = _pack_state(state_s5_re[0].reshape(nsb, S5_STATE), state_s5_im[0].reshape(nsb, S5_STATE))
    sinit_s = jnp.concatenate([sinit_s, jnp.zeros((n_extra, 2 * S5_STATE), F32)], axis=0)
    hs, conv_s, state_s = _even_mixer(xs[None], *mixer_w, cinit_s, sinit_s, *mixer_w2, carried=False)
    hs = hs[0]
    cinit_p = jnp.broadcast_to(conv_s[:, META_SEG:META_SEG + 1], (2, nb, CONV_CH))
    sinit_p = jnp.broadcast_to(state_s[META_SEG:META_SEG + 1], (nb, 2 * S5_STATE))
    hp, conv_p, state_p = _even_mixer(x_prompt, *mixer_w, cinit_p, sinit_p, *mixer_w2, carried=True)
    hp = hp.reshape(nb * seq, D_MODEL)
    ffn_w = (row(norm_ffn[0]), ffn_w1[0].astype(BF16), ffn_w3[0].astype(BF16), ffn_w2[0].astype(BF16))
    hs = _ffn(hs, *ffn_w, tm=256)
    hp = _ffn(hp, *ffn_w, tm=512)

    qkv_w = (row(norm_mix[1]), w_qkv[0].astype(BF16), row(b_qkv[0]))
    qs, ks, vs = _qkv(hs, *qkv_w, tm=256)
    qp, kp, vp = _qkv(hp, *qkv_w, tm=512)
    wo = (w_o[0].astype(BF16), row(b_o[0]))
    hs = _attn_small(qs, cache_swa_k[0].reshape(nsb, WINDOW, KV_DIM), cache_swa_v[0].reshape(nsb, WINDOW, KV_DIM),
                     ks, vs, _small_bias(rel_bias_table), _lane_sink(attn_sink[0], SEG), hs, *wo)

    def padded(new, small):
        meta_rows = small[META_SEG * SEG:(META_SEG + 1) * SEG]
        front = jnp.concatenate([jnp.zeros((WINDOW - SEG, KV_DIM), F32), meta_rows], axis=0)
        front = jnp.broadcast_to(front[None], (nb, WINDOW, KV_DIM))
        return jnp.concatenate([front, new.reshape(nb, seq, KV_DIM)], axis=1).astype(BF16)

    hp = _attn_prompt(qp, padded(kp, ks), padded(vp, vs), _prompt_bias(rel_bias_table),
                      _lane_sink(attn_sink[0], TQ), hp, *wo)
    moe_w = (row(norm_ffn[1]), router_w[0].T.astype(BF16), router_b[0].reshape(N_EXPERTS, 1), moe_w1[0].astype(BF16),
             moe_w3[0].astype(BF16), moe_w2[0].astype(BF16), row(norm_final))
    ys = _moe_final(hs, *moe_w, tb=hs.shape[0], rt=352)
    yp = _moe_final(hp, *moe_w, tb=1024, rt=288)

    n_real = nsb * SEG
    y_prompt = yp.reshape(nb, seq, D_MODEL)
    y_sample = ys[:n_real].reshape(nsb, SEG, D_MODEL)

    def split_state(st, n):
        st = st[:n].reshape(n, 2, 2, HALF_STATE)
        return tuple(st[:, :, ri].reshape(1, n, S5_GROUPS, S5_P) for ri in range(2))

    s5rp, s5ip = split_state(state_p, nb)
    s5rs, s5is = split_state(state_s, nsb)
    kp4 = kp.reshape(nb, seq, KV_DIM)[:, -WINDOW:].reshape(1, nb, WINDOW, N_KV, HEAD_DIM)
    vp4 = vp.reshape(nb, seq, KV_DIM)[:, -WINDOW:].reshape(1, nb, WINDOW, N_KV, HEAD_DIM)
    ks4 = ks[:n_real].reshape(1, nsb, SEG, N_KV, HEAD_DIM)
    vs4 = vs[:n_real].reshape(1, nsb, SEG, N_KV, HEAD_DIM)
    conv_p4 = jnp.swapaxes(conv_p, 0, 1)[None]
    conv_s4 = jnp.swapaxes(conv_s[:, :nsb], 0, 1)[None]
    return (y_prompt, y_sample, conv_p4, conv_s4, s5rp, s5ip, s5rs, s5is, kp4, vp4, ks4, vs4)
```

```python
import functools
import math

import numpy as np
import jax
import jax.numpy as jnp
from jax import lax
from jax.experimental import pallas as pl
from jax.experimental.pallas import tpu as pltpu

F32 = jnp.float32
BF16 = jnp.bfloat16

D_MODEL = 1024
CONV_CH = 512
S5_CH = 512
S5_GROUP = 16
S5_GROUPS = 32
S5_P = 64
S5_STATE = S5_GROUPS * S5_P
HALF_CH = S5_CH // 2
HALF_STATE = S5_STATE // 2
D_FF = 2816
N_HEADS = 16
N_KV = 2
GQ = N_HEADS // N_KV
HEAD_DIM = 64
KV_DIM = N_KV * HEAD_DIM
WINDOW = 128
CHUNK = 64
N_META = 16
N_BUCKETS = 32
MAX_DIST = 128
N_EXPERTS = 8
D_FF_E = 1024
EPS = 1e-6
NEG = -1e30

SEG = 32
N_SAMPLE_SEG = 32
META_SEG = N_SAMPLE_SEG
N_SMALL_SEG = 40
TM_MIX = 256
CHUNK_ROWS = SEG
SCAN_LANES = 512
TQ = 128
TQ_PER_STEP = 2
SEG_PER_STEP = 8
VMEM_LIMIT = 56 * 1024 * 1024


def _const_spec(shape):
    nd = len(shape)
    return pl.BlockSpec(shape, lambda *_: (0,) * nd, pipeline_mode=pl.Buffered(1))


def _params(*sem):
    return pltpu.CompilerParams(dimension_semantics=sem, vmem_limit_bytes=VMEM_LIMIT)


def _rms(x, g):
    return x * lax.rsqrt(jnp.mean(x * x, axis=-1, keepdims=True) + EPS) * g


def _dot(a, b):
    return jnp.dot(a, b, preferred_element_type=F32)


def _even_mixer_kernel(x_ref, gm_ref, perm_ref, unperm_ref, win_ref, cw_ref, cb_ref, cinit_ref, sinit_ref,
                       ab_ref, hop_ref, pw_ref, bb_ref, cc_ref, d_ref, gluw_ref, glub_ref, wout_ref,
                       h_ref, cout_ref, sout_ref, bu_ref, ccarry_ref, scarry_ref, *, carried, tiles_per_seq):
    if carried:
        @pl.when(pl.program_id(0) % tiles_per_seq == 0)
        def _load_initial_state():
            ccarry_ref[...] = cinit_ref[...]
            scarry_ref[...] = sinit_ref[...]

    streams = range(x_ref.shape[0])
    mid = [_mixer_project(b, x_ref, gm_ref, perm_ref, win_ref, cw_ref, cb_ref, cinit_ref, bb_ref, cout_ref, bu_ref,
                          ccarry_ref, carried=carried) for b in streams]
    for b in streams:
        _mixer_recur(b, sinit_ref, ab_ref, hop_ref, pw_ref, sout_ref, bu_ref, scarry_ref, carried=carried)
        _mixer_output(b, *mid[b], unperm_ref, cc_ref, d_ref, gluw_ref, glub_ref, wout_ref, h_ref, bu_ref)


def _mixer_project(b, x_ref, gm_ref, perm_ref, win_ref, cw_ref, cb_ref, cinit_ref, bb_ref, cout_ref, bu_ref,
                   ccarry_ref, *, carried):
    tm = TM_MIX
    x = x_ref[b]
    hn = _rms(x, gm_ref[...]).astype(BF16)
    hn = _dot(perm_ref[...], hn).astype(BF16)
    proj = _dot(hn, win_ref[...])
    g_b = proj[:, :CONV_CH]
    cin = proj[:, CONV_CH:2 * CONV_CH] * proj[:, 2 * CONV_CH:3 * CONV_CH]
    u = proj[:, 3 * CONV_CH:]
    ub = u.astype(BF16)
    for half in range(2):
        bu_ref[b, :, half * S5_STATE:(half + 1) * S5_STATE] = _dot(ub[:, half * HALF_CH:(half + 1) * HALF_CH],
                                                                  bb_ref[half])

    last, before_last = cin[tm - 8:tm], cin[tm - 16:tm - 8]
    if carried:
        first_chunk = lax.broadcasted_iota(jnp.int32, (8, CONV_CH), 0) == 0
        newer = jnp.where(first_chunk, ccarry_ref[1, b:b + 1, :], pltpu.roll(last, 1, 0))
        older = jnp.where(first_chunk, ccarry_ref[0, b:b + 1, :], pltpu.roll(before_last, 1, 0))
        for i, rows in enumerate((before_last, last)):
            ccarry_ref[i, b:b + 1, :] = rows[7:8]
            cout_ref[i, b:b + 1, :] = rows[7:8]
    else:
        older, newer = cinit_ref[0], cinit_ref[1]
        cout_ref[0] = before_last
        cout_ref[1] = last
    ext = jnp.concatenate([older, newer, cin], axis=0)
    cw = cw_ref[...]
    out_a = g_b * (cw[0:1] * ext[:tm] + cw[1:2] * ext[8:tm + 8] + cw[2:3] * cin + cb_ref[...])
    return x, u, out_a


def _mixer_recur(b, sinit_ref, ab_ref, hop_ref, pw_ref, sout_ref, bu_ref, scarry_ref, *, carried):
    for c in range(S5_STATE // SCAN_LANES):
        half, within = divmod(c * SCAN_LANES, HALF_STATE)
        nat = slice(c * SCAN_LANES, (c + 1) * SCAN_LANES)
        re_cols = slice(half * S5_STATE + within, half * S5_STATE + within + SCAN_LANES)
        im_cols = slice(re_cols.start + HALF_STATE, re_cols.stop + HALF_STATE)
        ar, ai = ab_ref[0, :, nat], ab_ref[1, :, nat]
        if carried:
            sr = si = jnp.zeros((8, SCAN_LANES), F32)
        else:
            sr, si = sinit_ref[:, re_cols], sinit_ref[:, im_cols]
        for g in range(CHUNK_ROWS):
            rows = slice(8 * g, 8 * g + 8)
            sr, si = (ar * sr - ai * si + bu_ref[b, rows, re_cols], ar * si + ai * sr + bu_ref[b, rows, im_cols])
            bu_ref[b, rows, re_cols] = sr
            bu_ref[b, rows, im_cols] = si
        if not carried:
            sout_ref[:, re_cols] = sr
            sout_ref[:, im_cols] = si
            continue
        first_chunk = lax.broadcasted_iota(jnp.int32, (8, SCAN_LANES), 0) == 0
        er = jnp.where(first_chunk, scarry_ref[b:b + 1, re_cols], pltpu.roll(sr, 1, 0))
        ei = jnp.where(first_chunk, scarry_ref[b:b + 1, im_cols], pltpu.roll(si, 1, 0))
        for k, sh in enumerate((1, 2, 4)):
            wr, wi = hop_ref[2 * k, :, nat], hop_ref[2 * k + 1, :, nat]
            tr, ti = pltpu.roll(er, sh, 0), pltpu.roll(ei, sh, 0)
            er, ei = er + wr * tr - wi * ti, ei + wr * ti + wi * tr
        wr, wi = hop_ref[6, :, nat], hop_ref[7, :, nat]
        fr, fi = wr * er - wi * ei + sr, wr * ei + wi * er + si
        for ref in (scarry_ref, sout_ref):
            ref[b:b + 1, re_cols] = fr[7:8]
            ref[b:b + 1, im_cols] = fi[7:8]
        for g in range(CHUNK_ROWS):
            rows = slice(8 * g, 8 * g + 8)
            pr, pi = pw_ref[0, g, :, nat], pw_ref[1, g, :, nat]
            bu_ref[b, rows, re_cols] += pr * er - pi * ei
            bu_ref[b, rows, im_cols] += pr * ei + pi * er


def _mixer_output(b, x, u, out_a, unperm_ref, cc_ref, d_ref, gluw_ref, glub_ref, wout_ref, h_ref, bu_ref):
    y = jnp.concatenate([_dot(bu_ref[b, :, half * S5_STATE:(half + 1) * S5_STATE].astype(BF16), cc_ref[half])
                         for half in range(2)], axis=-1) + d_ref[...] * u
    z = jax.nn.gelu(y)
    out_b = z * jax.nn.sigmoid(_dot(z.astype(BF16), gluw_ref[...]) + glub_ref[...])
    mix = jnp.concatenate([out_a, out_b], axis=-1).astype(BF16)
    mix = _dot(unperm_ref[...], mix).astype(BF16)
    h_ref[b] = x + _dot(mix, wout_ref[...])


def _even_mixer(x, gm, w_in, cw, cb, cinit, sinit, ab, hop, pw, bb, cc, d, glu_w, glu_b, w_out, *, carried):
    nb, rows = x.shape[0], x.shape[1]
    tm = TM_MIX
    chunks = tm // CHUNK_ROWS
    assert chunks == 8 and rows % tm == 0 and (carried or nb == 1)
    per_step = nb if carried else chunks
    tiles_per_seq = rows // tm if carried else 1
    nseq = sinit.shape[0]
    t_of = np.arange(tm)
    perm = np.zeros((tm, tm), np.float32)
    perm[8 * (t_of % CHUNK_ROWS) + t_of // CHUNK_ROWS, t_of] = 1.0
    kern = functools.partial(_even_mixer_kernel, carried=carried, tiles_per_seq=tiles_per_seq)
    return pl.pallas_call(
        kern,
        grid=(rows // tm,),
        in_specs=[
            pl.BlockSpec((nb, tm, D_MODEL), lambda t: (0, t, 0)),
            _const_spec((1, D_MODEL)),
            _const_spec((tm, tm)),
            _const_spec((tm, tm)),
            _const_spec((D_MODEL, 4 * CONV_CH)),
            _const_spec((3, CONV_CH)),
            _const_spec((1, CONV_CH)),
            pl.BlockSpec((2, per_step, CONV_CH), lambda t: (0, t // tiles_per_seq, 0)),
            pl.BlockSpec((per_step, 2 * S5_STATE), lambda t: (t // tiles_per_seq, 0)),
            _const_spec((2, 8, S5_STATE)),
            _const_spec((8, 8, S5_STATE)),
            _const_spec((2, CHUNK_ROWS, 8, S5_STATE)),
            _const_spec((2, HALF_CH, S5_STATE)),
            _const_spec((2, S5_STATE, HALF_CH)),
            _const_spec((1, S5_CH)),
            _const_spec((S5_CH, S5_CH)),
            _const_spec((1, S5_CH)),
            _const_spec((D_MODEL, D_MODEL)),
        ],
        out_specs=[
            pl.BlockSpec((nb, tm, D_MODEL), lambda t: (0, t, 0)),
            pl.BlockSpec((2, per_step, CONV_CH), lambda t: (0, t // tiles_per_seq, 0)),
            pl.BlockSpec((per_step, 2 * S5_STATE), lambda t: (t // tiles_per_seq, 0)),
        ],
        out_shape=[
            jax.ShapeDtypeStruct((nb, rows, D_MODEL), F32),
            jax.ShapeDtypeStruct((2, nseq, CONV_CH), F32),
            jax.ShapeDtypeStruct((nseq, 2 * S5_STATE), F32),
        ],
        scratch_shapes=[
            pltpu.VMEM((nb, tm, 2 * S5_STATE), F32),
            pltpu.VMEM((2, nb, CONV_CH), F32),
            pltpu.VMEM((nb, 2 * S5_STATE), F32),
        ],
        compiler_params=_params("arbitrary"),
        name="even_mixer",
    )(x, gm, jnp.asarray(perm, BF16), jnp.asarray(perm.T, BF16), w_in, cw, cb, cinit, sinit, ab, hop, pw, bb, cc, d,
      glu_w, glu_b, w_out)


def _ffn_kernel(x_ref, g_ref, w1_ref, w3_ref, w2_ref, o_ref):
    x = x_ref[...]
    hn = _rms(x, g_ref[...]).astype(BF16)
    a = _dot(hn, w1_ref[...])
    b = _dot(hn, w3_ref[...])
    o_ref[...] = x + _dot((a * jax.nn.sigmoid(a) * b).astype(BF16), w2_ref[...])


def _ffn(x, g, w1, w3, w2, *, tm):
    rows = x.shape[0]
    return pl.pallas_call(
        _ffn_kernel,
        grid=(rows // tm,),
        in_specs=[
            pl.BlockSpec((tm, D_MODEL), lambda t: (t, 0)),
            _const_spec((1, D_MODEL)),
            _const_spec((D_MODEL, D_FF)),
            _const_spec((D_MODEL, D_FF)),
            _const_spec((D_FF, D_MODEL)),
        ],
        out_specs=pl.BlockSpec((tm, D_MODEL), lambda t: (t, 0)),
        out_shape=jax.ShapeDtypeStruct((rows, D_MODEL), F32),
        compiler_params=_params("parallel"),
        name="ffn",
    )(x, g, w1, w3, w2)


def _qkv_kernel(x_ref, g_ref, w_ref, b_ref, q_ref, k_ref, v_ref):
    hn = _rms(x_ref[...], g_ref[...]).astype(BF16)
    qkv = _dot(hn, w_ref[...]) + b_ref[...]
    nq = N_HEADS * HEAD_DIM
    q_ref[...] = (qkv[:, :nq] * (HEAD_DIM ** -0.5)).astype(BF16)
    k_ref[...] = qkv[:, nq:nq + KV_DIM]
    v_ref[...] = qkv[:, nq + KV_DIM:]


def _qkv(x, g, w, b, *, tm):
    rows = x.shape[0]
    ncol = (N_HEADS + 2 * N_KV) * HEAD_DIM
    return pl.pallas_call(
        _qkv_kernel,
        grid=(rows // tm,),
        in_specs=[
            pl.BlockSpec((tm, D_MODEL), lambda t: (t, 0)),
            _const_spec((1, D_MODEL)),
            _const_spec((D_MODEL, ncol)),
            _const_spec((1, ncol)),
        ],
        out_specs=[
            pl.BlockSpec((tm, N_HEADS * HEAD_DIM), lambda t: (t, 0)),
            pl.BlockSpec((tm, KV_DIM), lambda t: (t, 0)),
            pl.BlockSpec((tm, KV_DIM), lambda t: (t, 0)),
        ],
        out_shape=[
            jax.ShapeDtypeStruct((rows, N_HEADS * HEAD_DIM), BF16),
            jax.ShapeDtypeStruct((rows, KV_DIM), F32),
            jax.ShapeDtypeStruct((rows, KV_DIM), F32),
        ],
        compiler_params=_params("parallel"),
        name="qkv",
    )(x, g, w, b)


def _attend_groups(q, keys, vals, bias_of_group, sink_of_group, o_scr, row0):
    nq = q.shape[0]
    for kv in range(N_KV):
        kv_cols = slice(kv * HEAD_DIM, (kv + 1) * HEAD_DIM)
        heads = [slice((kv * GQ + g) * HEAD_DIM, (kv * GQ + g + 1) * HEAD_DIM) for g in range(GQ)]
        qs = jnp.concatenate([q[:, cols] for cols in heads], axis=0)
        st = lax.dot_general(keys[:, kv_cols], qs, (((1,), (1,)), ((), ())),
                             preferred_element_type=F32) + bias_of_group(kv)
        sk = sink_of_group(kv)
        m = jnp.maximum(jnp.max(st, axis=0, keepdims=True), sk)
        p = jnp.exp(st - m)
        den = jnp.sum(p, axis=0, keepdims=True) + jnp.exp(sk - m)
        o = lax.dot_general((p * (1.0 / den)).astype(BF16), vals[:, kv_cols], (((0,), (0,)), ((), ())),
                            preferred_element_type=F32)
        for g, cols in enumerate(heads):
            o_scr[row0:row0 + nq, cols] = o[g * nq:(g + 1) * nq]


def _attn_prompt_kernel(q_ref, k_ref, v_ref, bias_ref, sink_ref, x_ref, wo_ref, bo_ref, o_ref, o_scr):
    t = pl.program_id(1)
    for u in range(TQ_PER_STEP):
        tile = t * TQ_PER_STEP + u
        start = pl.multiple_of(tile * TQ, TQ)
        keys = k_ref[0, pl.ds(start, TQ + WINDOW), :]
        vals = v_ref[0, pl.ds(start, TQ + WINDOW), :]
        variant = jnp.minimum(tile, 1)
        _attend_groups(q_ref[u * TQ:(u + 1) * TQ], keys, vals, lambda kv: bias_ref[variant, kv],
                       lambda kv: sink_ref[kv], o_scr, u * TQ)
    o_ref[...] = x_ref[...] + _dot(o_scr[...].astype(BF16), wo_ref[...]) + bo_ref[...]


def _attn_prompt(q, kpad, vpad, bias, sink, x, w_o, b_o):
    nb, lpad = kpad.shape[0], kpad.shape[1]
    tm = TQ_PER_STEP * TQ
    steps = (lpad - WINDOW) // tm
    return pl.pallas_call(
        _attn_prompt_kernel,
        grid=(nb, steps),
        in_specs=[
            pl.BlockSpec((tm, D_MODEL), lambda b, t: (b * steps + t, 0)),
            pl.BlockSpec((1, lpad, KV_DIM), lambda b, t: (b, 0, 0)),
            pl.BlockSpec((1, lpad, KV_DIM), lambda b, t: (b, 0, 0)),
            _const_spec((2, N_KV, TQ + WINDOW, GQ * TQ)),
            _const_spec((N_KV, 1, GQ * TQ)),
            pl.BlockSpec((tm, D_MODEL), lambda b, t: (b * steps + t, 0)),
            _const_spec((D_MODEL, D_MODEL)),
            _const_spec((1, D_MODEL)),
        ],
        out_specs=pl.BlockSpec((tm, D_MODEL), lambda b, t: (b * steps + t, 0)),
        out_shape=jax.ShapeDtypeStruct(x.shape, F32),
        scratch_shapes=[pltpu.VMEM((tm, D_MODEL), F32)],
        compiler_params=_params("parallel", "arbitrary"),
        name="attn_prompt",
    )(q, kpad, vpad, bias, sink, x, w_o, b_o)


def _attn_small_kernel(q_ref, kc_ref, vc_ref, kn_ref, vn_ref, bias_ref, sink_ref, x_ref, wo_ref, bo_ref,
                       o_ref, o_scr):
    step = pl.program_id(0)
    for j in range(SEG_PER_STEP):
        rows = slice(j * SEG, (j + 1) * SEG)
        keys = jnp.concatenate([kc_ref[j], kn_ref[rows]], axis=0).astype(BF16)
        vals = jnp.concatenate([vc_ref[j], vn_ref[rows]], axis=0).astype(BF16)
        variant = jnp.where(step * SEG_PER_STEP + j == META_SEG, 1, 0)
        _attend_groups(q_ref[rows], keys, vals, lambda kv: bias_ref[variant, kv], lambda kv: sink_ref[kv], o_scr,
                       j * SEG)
    o_ref[...] = x_ref[...] + _dot(o_scr[...].astype(BF16), wo_ref[...]) + bo_ref[...]


def _attn_small(q, kc, vc, kn, vn, bias, sink, x, w_o, b_o):
    tm = SEG_PER_STEP * SEG
    steps = x.shape[0] // tm
    cache_blocks = kc.shape[0] // SEG_PER_STEP
    cache_spec = pl.BlockSpec((SEG_PER_STEP, WINDOW, KV_DIM), lambda s: (jnp.minimum(s, cache_blocks - 1), 0, 0))
    return pl.pallas_call(
        _attn_small_kernel,
        grid=(steps,),
        in_specs=[
            pl.BlockSpec((tm, D_MODEL), lambda s: (s, 0)),
            cache_spec,
            cache_spec,
            pl.BlockSpec((tm, KV_DIM), lambda s: (s, 0)),
            pl.BlockSpec((tm, KV_DIM), lambda s: (s, 0)),
            _const_spec((2, N_KV, WINDOW + SEG, GQ * SEG)),
            _const_spec((N_KV, 1, GQ * SEG)),
            pl.BlockSpec((tm, D_MODEL), lambda s: (s, 0)),
            _const_spec((D_MODEL, D_MODEL)),
            _const_spec((1, D_MODEL)),
        ],
        out_specs=pl.BlockSpec((tm, D_MODEL), lambda s: (s, 0)),
        out_shape=jax.ShapeDtypeStruct(x.shape, F32),
        scratch_shapes=[pltpu.VMEM((tm, D_MODEL), F32)],
        compiler_params=_params("parallel"),
        name="attn_small",
    )(q, kc, vc, kn, vn, bias, sink, x, w_o, b_o)


def _moe_kernel(x_ref, g_ref, rwt_ref, rb_ref, w1_ref, w3_ref, w2_ref, gf_ref, o_ref,
                hn_scr, slot_scr, gate_scr, cnt_smem, acc_scr, yall_scr, *, rt):
    e = pl.program_id(1)
    tb = x_ref.shape[0]

    @pl.when(e == 0)
    def _route():
        hn = _rms(x_ref[...], g_ref[...]).astype(BF16)
        hn_scr[...] = hn
        logits = lax.dot_general(rwt_ref[...], hn, (((1,), (1,)), ((), ())),
                                 preferred_element_type=F32) + rb_ref[...]
        row = lax.broadcasted_iota(jnp.int32, logits.shape, 0).astype(F32)
        v1 = jnp.max(logits, axis=0, keepdims=True)
        i1 = jnp.min(jnp.where(logits == v1, row, float(N_EXPERTS)), axis=0, keepdims=True)
        rest = jnp.where(row == i1, -jnp.inf, logits)
        v2 = jnp.max(rest, axis=0, keepdims=True)
        i2 = jnp.min(jnp.where(rest == v2, row, float(N_EXPERTS)), axis=0, keepdims=True)
        e2 = jnp.exp(v2 - v1)
        den = 1.0 + e2
        gate_scr[...] = jnp.where(row == i1, 1.0 / den, 0.0) + jnp.where(row == i2, e2 / den, 0.0)
        chosen = (row == i1) | (row == i2)
        ind = jnp.where(chosen, 1.0, 0.0)
        earlier = (lax.broadcasted_iota(jnp.int32, (tb, tb), 0) < lax.broadcasted_iota(jnp.int32, (tb, tb), 1))
        rank = _dot(ind.astype(BF16), jnp.where(earlier, 1.0, 0.0).astype(BF16))
        slot_scr[...] = jnp.where(chosen, rank, -1.0).astype(jnp.int32)
        for k in range(N_EXPERTS):
            cnt_smem[k] = jnp.sum(ind[k:k + 1, :]).astype(jnp.int32)
        acc_scr[...] = jnp.zeros_like(acc_scr)

    slot = slot_scr[pl.ds(e, 1), :]
    gate = gate_scr[pl.ds(e, 1), :]

    def _scatter(onehot, rows):
        return lax.dot_general(onehot, rows, (((0,), (0,)), ((), ())), preferred_element_type=F32)

    def _expert_tile(i):
        packed_row = lax.broadcasted_iota(jnp.int32, (rt, tb), 0) + i * rt
        hit = packed_row == slot
        onehot = jnp.where(hit, 1.0, 0.0).astype(BF16)
        xe = _dot(onehot, hn_scr[...]).astype(BF16)
        a = _dot(xe, w1_ref[0])
        b = _dot(xe, w3_ref[0])
        y = _dot((a * jax.nn.sigmoid(a) * b).astype(BF16), w2_ref[0])
        g_row = jnp.sum(jnp.where(hit, gate, 0.0), axis=1, keepdims=True)
        return onehot, (y * g_row).astype(BF16)

    yall_scr[pl.ds(pl.multiple_of(e * rt, 16), rt), :] = _expert_tile(0)[1]

    def _overflow_tile(i, carry):
        acc_scr[...] += _scatter(*_expert_tile(i))
        return carry

    lax.fori_loop(1, (cnt_smem[e] + rt - 1) // rt, _overflow_tile, 0)

    @pl.when(e == N_EXPERTS - 1)
    def _finish():
        packed_row = lax.broadcasted_iota(jnp.int32, (rt, tb), 0)
        onehot = jnp.concatenate([jnp.where(packed_row == slot_scr[k:k + 1, :], 1.0, 0.0).astype(BF16)
                                  for k in range(N_EXPERTS)], axis=0)
        o_ref[...] = _rms(x_ref[...] + acc_scr[...] + _scatter(onehot, yall_scr[...]), gf_ref[...])


def _moe_final(x, g, rwt, rb, w1, w3, w2, gf, *, tb, rt):
    rows = x.shape[0]
    return pl.pallas_call(
        functools.partial(_moe_kernel, rt=rt),
        grid=(rows // tb, N_EXPERTS),
        in_specs=[
            pl.BlockSpec((tb, D_MODEL), lambda t, e: (t, 0)),
            _const_spec((1, D_MODEL)),
            _const_spec((N_EXPERTS, D_MODEL)),
            _const_spec((N_EXPERTS, 1)),
            pl.BlockSpec((1, D_MODEL, D_FF_E), lambda t, e: (e, 0, 0)),
            pl.BlockSpec((1, D_MODEL, D_FF_E), lambda t, e: (e, 0, 0)),
            pl.BlockSpec((1, D_FF_E, D_MODEL), lambda t, e: (e, 0, 0)),
            _const_spec((1, D_MODEL)),
        ],
        out_specs=pl.BlockSpec((tb, D_MODEL), lambda t, e: (t, 0)),
        out_shape=jax.ShapeDtypeStruct((rows, D_MODEL), F32),
        scratch_shapes=[
            pltpu.VMEM((tb, D_MODEL), BF16),
            pltpu.VMEM((N_EXPERTS, tb), jnp.int32),
            pltpu.VMEM((N_EXPERTS, tb), F32),
            pltpu.SMEM((N_EXPERTS,), jnp.int32),
            pltpu.VMEM((tb, D_MODEL), F32),
            pltpu.VMEM((N_EXPERTS * rt, D_MODEL), BF16),
        ],
        compiler_params=_params("parallel", "arbitrary"),
        name="moe_final",
    )(x, g, rwt, rb, w1, w3, w2, gf)


def _moe_route(hn, rwt_ref, rb_ref):
    tb = hn.shape[0]
    logits = lax.dot_general(rwt_ref[...], hn, (((1,), (1,)), ((), ())), preferred_element_type=F32) + rb_ref[...]
    row = lax.broadcasted_iota(jnp.int32, logits.shape, 0).astype(F32)
    v1 = jnp.max(logits, axis=0, keepdims=True)
    i1 = jnp.min(jnp.where(logits == v1, row, float(N_EXPERTS)), axis=0, keepdims=True)
    rest = jnp.where(row == i1, -jnp.inf, logits)
    v2 = jnp.max(rest, axis=0, keepdims=True)
    i2 = jnp.min(jnp.where(rest == v2, row, float(N_EXPERTS)), axis=0, keepdims=True)
    e2 = jnp.exp(v2 - v1)
    den = 1.0 + e2
    gate = jnp.where(row == i1, 1.0 / den, 0.0) + jnp.where(row == i2, e2 / den, 0.0)
    chosen = (row == i1) | (row == i2)
    earlier = (lax.broadcasted_iota(jnp.int32, (tb, tb), 0) < lax.broadcasted_iota(jnp.int32, (tb, tb), 1))
    rank = _dot(jnp.where(chosen, 1.0, 0.0).astype(BF16), jnp.where(earlier, 1.0, 0.0).astype(BF16))
    return gate, jnp.where(chosen, rank, -1.0).astype(jnp.int32)


def _moe_pack(hn, slot, gate, xs_ref, grow_ref, *, rt, tile):
    tb = hn.shape[0]
    packed_row = lax.broadcasted_iota(jnp.int32, (rt, tb), 0) + tile * rt
    for k in range(N_EXPERTS):
        hit = packed_row == slot[k:k + 1]
        xs_ref[k, 0] = _dot(jnp.where(hit, 1.0, 0.0).astype(BF16), hn).astype(BF16)
        grow_ref[k, 0] = jnp.sum(jnp.where(hit, gate[k:k + 1], 0.0), axis=1, keepdims=True)


def _moe_route_kernel(x_ref, g_ref, rwt_ref, rb_ref, xs_ref, grow_ref, slot_ref, gate_ref, *, rt):
    hn = _rms(x_ref[...], g_ref[...]).astype(BF16)
    gate, slot = _moe_route(hn, rwt_ref, rb_ref)
    slot_ref[0] = slot
    gate_ref[0] = gate
    _moe_pack(hn, slot, gate, xs_ref, grow_ref, rt=rt, tile=0)


def _moe_repack_kernel(x_ref, g_ref, slot_ref, gate_ref, xs_ref, grow_ref, *, rt, tile):
    hn = _rms(x_ref[...], g_ref[...]).astype(BF16)
    _moe_pack(hn, slot_ref[0], gate_ref[0], xs_ref, grow_ref, rt=rt, tile=tile)


def _moe_expert_kernel(xs_ref, grow_ref, w1_ref, w3_ref, w2_ref, ys_ref):
    rows = xs_ref.shape[1] * xs_ref.shape[2]
    xe = xs_ref[0].reshape(rows, D_MODEL)
    a = _dot(xe, w1_ref[0])
    b = _dot(xe, w3_ref[0])
    y = _dot((a * jax.nn.sigmoid(a) * b).astype(BF16), w2_ref[0])
    ys_ref[0] = (y * grow_ref[0].reshape(rows, 1)).astype(BF16).reshape(ys_ref.shape[1:])


def _moe_scatter(slot, ys_ref, *, rt, tile):
    tb = slot.shape[1]
    packed_row = lax.broadcasted_iota(jnp.int32, (rt, tb), 0) + tile * rt
    onehot = jnp.concatenate([jnp.where(packed_row == slot[k:k + 1], 1.0, 0.0).astype(BF16)
                              for k in range(N_EXPERTS)], axis=0)
    return lax.dot_general(onehot, ys_ref[:, 0].reshape(N_EXPERTS * rt, D_MODEL), (((0,), (0,)), ((), ())),
                           preferred_element_type=F32)


def _moe_combine_kernel(x_ref, slot_ref, ys_ref, gf_ref, o_ref, *, rt):
    o_ref[...] = _rms(x_ref[...] + _moe_scatter(slot_ref[0], ys_ref, rt=rt, tile=0), gf_ref[...])


def _moe_combine_more_kernel(x_ref, slot_ref, ys_ref, more_ref, gf_ref, o_ref, *, rt):
    o_ref[...] = _rms(x_ref[...] + more_ref[...] + _moe_scatter(slot_ref[0], ys_ref, rt=rt, tile=0), gf_ref[...])


def _moe_scatter_kernel(slot_ref, ys_ref, o_ref, *, rt, tile):
    o_ref[...] = _moe_scatter(slot_ref[0], ys_ref, rt=rt, tile=tile)


def _moe_staged(x, g, rwt, rb, w1, w3, w2, gf, *, tb, rt, blocks_per_step):
    rows = x.shape[0]
    nblk = rows // tb
    tiles = -(-tb // rt)
    x_spec = pl.BlockSpec((tb, D_MODEL), lambda b: (b, 0))
    route_spec = pl.BlockSpec((1, N_EXPERTS, tb), lambda b: (b, 0, 0))
    xs_spec = pl.BlockSpec((N_EXPERTS, 1, rt, D_MODEL), lambda b: (0, b, 0, 0))
    grow_spec = pl.BlockSpec((N_EXPERTS, 1, rt, 1), lambda b: (0, b, 0, 0))
    packed = [jax.ShapeDtypeStruct((N_EXPERTS, nblk, rt, D_MODEL), BF16),
              jax.ShapeDtypeStruct((N_EXPERTS, nblk, rt, 1), F32)]
    routed = [jax.ShapeDtypeStruct((nblk, N_EXPERTS, tb), jnp.int32), jax.ShapeDtypeStruct((nblk, N_EXPERTS, tb), F32)]

    xs, grow, slot, gate = pl.pallas_call(
        functools.partial(_moe_route_kernel, rt=rt),
        grid=(nblk,),
        in_specs=[x_spec, _const_spec((1, D_MODEL)), _const_spec((N_EXPERTS, D_MODEL)), _const_spec((N_EXPERTS, 1))],
        out_specs=[xs_spec, grow_spec, route_spec, route_spec],
        out_shape=packed + routed,
        compiler_params=_params("parallel"),
        name="moe_route",
    )(x, g, rwt, rb)

    def experts(xs, grow):
        nstep = nblk // blocks_per_step
        tile_spec = pl.BlockSpec((1, blocks_per_step, rt, D_MODEL), lambda e, j: (e, j, 0, 0))
        w_spec = pl.BlockSpec((1, D_MODEL, D_FF_E), lambda e, j: (e, 0, 0))
        return pl.pallas_call(
            _moe_expert_kernel,
            grid=(N_EXPERTS, nstep),
            in_specs=[tile_spec, pl.BlockSpec((1, blocks_per_step, rt, 1), lambda e, j: (e, j, 0, 0)),
                      w_spec, w_spec, pl.BlockSpec((1, D_FF_E, D_MODEL), lambda e, j: (e, 0, 0))],
            out_specs=tile_spec,
            out_shape=jax.ShapeDtypeStruct(xs.shape, BF16),
            compiler_params=_params("parallel", "arbitrary"),
            name="moe_experts",
        )(xs, grow, w1, w3, w2)

    ys = experts(xs, grow)

    def further_tile(tile):
        xs_t, grow_t = pl.pallas_call(
            functools.partial(_moe_repack_kernel, rt=rt, tile=tile),
            grid=(nblk,),
            in_specs=[x_spec, _const_spec((1, D_MODEL)), route_spec, route_spec],
            out_specs=[xs_spec, grow_spec],
            out_shape=packed,
            compiler_params=_params("parallel"),
            name="moe_repack",
        )(x, g, slot, gate)
        return pl.pallas_call(
            functools.partial(_moe_scatter_kernel, rt=rt, tile=tile),
            grid=(nblk,),
            in_specs=[route_spec, xs_spec],
            out_specs=x_spec,
            out_shape=jax.ShapeDtypeStruct(x.shape, F32),
            compiler_params=_params("parallel"),
            name="moe_scatter",
        )(slot, experts(xs_t, grow_t))

    def combine(more):
        kern, extra_in, extra_spec = ((_moe_combine_kernel, (), ()) if more is None
                                      else (_moe_combine_more_kernel, (more,), (x_spec,)))
        return pl.pallas_call(
            functools.partial(kern, rt=rt),
            grid=(nblk,),
            in_specs=[x_spec, route_spec, xs_spec, *extra_spec, _const_spec((1, D_MODEL))],
            out_specs=x_spec,
            out_shape=jax.ShapeDtypeStruct(x.shape, F32),
            compiler_params=_params("parallel"),
            name="moe_combine",
        )(x, slot, ys, *extra_in, gf)

    def with_further_tiles():
        more = further_tile(1)
        for tile in range(2, tiles):
            more = lax.cond(most > tile * rt, lambda m, tile=tile: m + further_tile(tile), lambda m: m, more)
        return combine(more)

    most = jnp.max(slot) + 1
    return lax.cond(most > rt, with_further_tiles, lambda: combine(None))


def _t5_bucket_np(rel):
    half = N_BUCKETS // 2
    max_exact = half // 2
    ret = np.where(rel > 0, half, 0)
    n = np.abs(rel)
    nf = np.maximum(n, 1).astype(np.float32)
    large = max_exact + (np.log(nf / np.float32(max_exact)) / np.float32(math.log(MAX_DIST / max_exact))
                         * np.float32(half - max_exact)).astype(np.int32)
    large = np.minimum(large, half - 1)
    return ret + np.where(n < max_exact, n, large)


def _bias_variants(table, bucket, keeps):
    nq, nk = bucket.shape
    onehot = jnp.asarray(np.eye(N_BUCKETS, dtype=np.float32)[bucket])
    bias = jnp.einsum('rjb,bh->hjr', onehot, table, precision=lax.Precision.HIGHEST)
    bias = jnp.stack([jnp.where(keep.T[None], bias, NEG) for keep in keeps], axis=0)
    bias = bias.reshape(len(keeps), N_KV, GQ, nk, nq)
    return jnp.transpose(bias, (0, 1, 3, 2, 4)).reshape(len(keeps), N_KV, nk, GQ * nq)


def _lane_sink(sink, nq):
    return jnp.broadcast_to(sink.reshape(N_KV, 1, GQ, 1), (N_KV, 1, GQ, nq)).reshape(N_KV, 1, GQ * nq)


def _prompt_bias(table):
    r = np.arange(TQ)[:, None]
    j = np.arange(TQ + WINDOW)[None, :]
    band = (j - CHUNK * (r // CHUNK) >= 0) & (j - CHUNK * (r // CHUNK) < WINDOW + CHUNK)
    first = band & (j >= WINDOW - N_META)
    return _bias_variants(table, _t5_bucket_np(j - WINDOW - r), [first, band])


def _small_bias(table):
    i = np.arange(SEG)[:, None]
    m = np.arange(WINDOW + SEG)[None, :]
    everything = np.ones((SEG, WINDOW + SEG), bool)
    meta_only = everything & (m >= WINDOW + SEG - N_META)
    return _bias_variants(table, _t5_bucket_np(m - WINDOW - i), [everything, meta_only])


def _pack_state(re, im):
    return jnp.concatenate([re[..., :HALF_STATE], im[..., :HALF_STATE], re[..., HALF_STATE:], im[..., HALF_STATE:]],
                           axis=-1)


def _cmul(ar, ai, br, bi):
    return ar * br - ai * bi, ar * bi + ai * br


def _s5_prepare(a_re, a_im, log_dt, b_re, b_im, c_re, c_im):
    dt = jnp.exp(log_dt)[:, None]
    mag = jnp.exp(a_re * dt)
    ab_re, ab_im = mag * jnp.cos(a_im * dt), mag * jnp.sin(a_im * dt)
    den = a_re * a_re + a_im * a_im
    num_re = ab_re - 1.0
    cf_re = (num_re * a_re + ab_im * a_im) / den
    cf_im = (ab_im * a_re - num_re * a_im) / den
    bb_re = cf_re[..., None] * b_re - cf_im[..., None] * b_im
    bb_im = cf_re[..., None] * b_im + cf_im[..., None] * b_re

    def group_diag(v):
        r, c = v.shape[1], v.shape[2]
        gh = S5_GROUPS // 2
        spread = np.tile(np.eye(c, dtype=np.float32), (1, gh))
        wide = jnp.einsum('hrc,cn->hrn', v.reshape(2, gh * r, c), spread, precision=lax.Precision.HIGHEST)
        keep = (np.arange(gh * r)[:, None] // r) == (np.arange(gh * c)[None, :] // c)
        return jnp.where(keep, wide, 0.0)

    def t(v):
        return jnp.swapaxes(v, 1, 2)

    bb = jnp.concatenate([group_diag(t(bb_re)), group_diag(t(bb_im))], axis=2).astype(BF16)
    cc = jnp.concatenate([group_diag(t(c_re)), -group_diag(t(c_im))], axis=1).astype(BF16)

    pr, pi = ab_re.reshape(1, S5_STATE), ab_im.reshape(1, S5_STATE)
    n = 1
    while n < CHUNK_ROWS:
        qr, qi = _cmul(pr, pi, pr[n - 1:n], pi[n - 1:n])
        pr, pi = jnp.concatenate([pr, qr], axis=0), jnp.concatenate([pi, qi], axis=0)
        n *= 2

    def tile8(v):
        return jnp.broadcast_to(v, (8,) + v.shape[1:])

    ab = jnp.stack([tile8(pr[0:1]), tile8(pi[0:1])])
    sub = np.arange(8)[:, None]
    hop, w = [], (pr[CHUNK_ROWS - 1:], pi[CHUNK_ROWS - 1:])
    for k in (1, 2, 4):
        hop += [jnp.where(sub >= k, w[0], 0.0), jnp.where(sub >= k, w[1], 0.0)]
        if k < 4:
            w = _cmul(*w, *w)
    hop = jnp.stack(hop + [tile8(pr[CHUNK_ROWS - 1:]), tile8(pi[CHUNK_ROWS - 1:])])
    pw = jnp.stack([jnp.broadcast_to(p[:, None, :], (CHUNK_ROWS, 8, S5_STATE)) for p in (pr, pi)])
    return bb, cc, ab, hop, pw


def kernel(x_prompt, x_sample, cache_conv, state_s5_re, state_s5_im, cache_swa_k, cache_swa_v, meta_tokens, rel_bias_table, norm_mix, norm_ffn, norm_final, w_in0, conv_w, conv_b, s5_a_re, s5_a_im, s5_log_dt, s5_b_re, s5_b_im, s5_c_re, s5_c_im, s5_d, s5_glu_w, s5_glu_b, w_out0, ffn_w1, ffn_w3, ffn_w2, w_qkv, b_qkv, attn_sink, w_o, b_o, router_w, router_b, moe_w1, moe_w3, moe_w2):
    nb, seq = x_prompt.shape[0], x_prompt.shape[1]
    nsb = x_sample.shape[0]
    assert x_sample.shape[1] == SEG and nsb == N_SAMPLE_SEG and seq % TM_MIX == 0

    def row(v):
        return v.reshape(1, -1)

    meta_seg = jnp.concatenate([jnp.zeros((SEG - N_META, D_MODEL), F32), meta_tokens], axis=0)
    filler = jnp.zeros(((N_SMALL_SEG - N_SAMPLE_SEG - 1) * SEG, D_MODEL), F32)
    xs = jnp.concatenate([x_sample.reshape(nsb * SEG, D_MODEL), meta_seg, filler], axis=0)
    n_extra = N_SMALL_SEG - N_SAMPLE_SEG

    bb, cc, ab, hop, pw = _s5_prepare(s5_a_re[0], s5_a_im[0], s5_log_dt[0], s5_b_re[0], s5_b_im[0], s5_c_re[0],
                                      s5_c_im[0])
    mixer_w = (row(norm_mix[0]), w_in0[0].astype(BF16), conv_w[0], row(conv_b[0]))
    mixer_w2 = (ab, hop, pw, bb, cc, row(s5_d[0]), s5_glu_w[0].astype(BF16), row(s5_glu_b[0]), w_out0[0].astype(BF16))
    cinit_s = jnp.concatenate([jnp.swapaxes(cache_conv[0], 0, 1), jnp.zeros((2, n_extra, CONV_CH), F32)], axis=1)
    sinit_s = _pack_state(state_s5_re[0].reshape(nsb, S5_STATE), state_s5_im[0].reshape(nsb, S5_STATE))
    sinit_s = jnp.concatenate([sinit_s, jnp.zeros((n_extra, 2 * S5_STATE), F32)], axis=0)
    hs, conv_s, state_s = _even_mixer(xs[None], *mixer_w, cinit_s, sinit_s, *mixer_w2, carried=False)
    hs = hs[0]
    cinit_p = jnp.broadcast_to(conv_s[:, META_SEG:META_SEG + 1], (2, nb, CONV_CH))
    sinit_p = jnp.broadcast_to(state_s[META_SEG:META_SEG + 1], (nb, 2 * S5_STATE))
    hp, conv_p, state_p = _even_mixer(x_prompt, *mixer_w, cinit_p, sinit_p, *mixer_w2, carried=True)
    hp = hp.reshape(nb * seq, D_MODEL)
    ffn_w = (row(norm_ffn[0]), ffn_w1[0].astype(BF16), ffn_w3[0].astype(BF16), ffn_w2[0].astype(BF16))
    hs = _ffn(hs, *ffn_w, tm=256)
    hp = _ffn(hp, *ffn_w, tm=512)

    qkv_w = (row(norm_mix[1]), w_qkv[0].astype(BF16), row(b_qkv[0]))
    qs, ks, vs = _qkv(hs, *qkv_w, tm=256)
    qp, kp, vp = _qkv(hp, *qkv_w, tm=512)
    wo = (w_o[0].astype(BF16), row(b_o[0]))
    hs = _attn_small(qs, cache_swa_k[0].reshape(nsb, WINDOW, KV_DIM), cache_swa_v[0].reshape(nsb, WINDOW, KV_DIM),
                     ks, vs, _small_bias(rel_bias_table), _lane_sink(attn_sink[0], SEG), hs, *wo)

    def padded(new, small):
        meta_rows = small[META_SEG * SEG:(META_SEG + 1) * SEG]
        front = jnp.concatenate([jnp.zeros((WINDOW - SEG, KV_DIM), F32), meta_rows], axis=0)
        front = jnp.broadcast_to(front[None], (nb, WINDOW, KV_DIM))
        return jnp.concatenate([front, new.reshape(nb, seq, KV_DIM)], axis=1).astype(BF16)

    hp = _attn_prompt(qp, padded(kp, ks), padded(vp, vs), _prompt_bias(rel_bias_table),
                      _lane_sink(attn_sink[0], TQ), hp, *wo)
    moe_w = (row(norm_ffn[1]), router_w[0].T.astype(BF16), router_b[0].reshape(N_EXPERTS, 1), moe_w1[0].astype(BF16),
             moe_w3[0].astype(BF16), moe_w2[0].astype(BF16), row(norm_final))
    ys = _moe_final(hs, *moe_w, tb=hs.shape[0], rt=352)
    yp = _moe_staged(hp, *moe_w, tb=1024, rt=288, blocks_per_step=4)

    n_real = nsb * SEG
    y_prompt = yp.reshape(nb, seq, D_MODEL)
    y_sample = ys[:n_real].reshape(nsb, SEG, D_MODEL)

    def split_state(st, n):
        st = st[:n].reshape(n, 2, 2, HALF_STATE)
        return tuple(st[:, :, ri].reshape(1, n, S5_GROUPS, S5_P) for ri in range(2))

    s5rp, s5ip = split_state(state_p, nb)
    s5rs, s5is = split_state(state_s, nsb)
    kp4 = kp.reshape(nb, seq, KV_DIM)[:, -WINDOW:].reshape(1, nb, WINDOW, N_KV, HEAD_DIM)
    vp4 = vp.reshape(nb, seq, KV_DIM)[:, -WINDOW:].reshape(1, nb, WINDOW, N_KV, HEAD_DIM)
    ks4 = ks[:n_real].reshape(1, nsb, SEG, N_KV, HEAD_DIM)
    vs4 = vs[:n_real].reshape(1, nsb, SEG, N_KV, HEAD_DIM)
    conv_p4 = jnp.swapaxes(conv_p, 0, 1)[None]
    conv_s4 = jnp.swapaxes(conv_s[:, :nsb], 0, 1)[None]
    return (y_prompt, y_sample, conv_p4, conv_s4, s5rp, s5ip, s5rs, s5is, kp4, vp4, ks4, vs4)
```

```python
import functools
import math

import numpy as np
import jax
import jax.numpy as jnp
from jax import lax
from jax.experimental import pallas as pl
from jax.experimental.pallas import tpu as pltpu

F32 = jnp.float32
BF16 = jnp.bfloat16

D_MODEL = 1024
CONV_CH = 512
S5_CH = 512
S5_GROUP = 16
S5_GROUPS = 32
S5_P = 64
S5_STATE = S5_GROUPS * S5_P
HALF_CH = S5_CH // 2
HALF_STATE = S5_STATE // 2
D_FF = 2816
N_HEADS = 16
N_KV = 2
GQ = N_HEADS // N_KV
HEAD_DIM = 64
KV_DIM = N_KV * HEAD_DIM
WINDOW = 128
CHUNK = 64
N_META = 16
N_BUCKETS = 32
MAX_DIST = 128
N_EXPERTS = 8
D_FF_E = 1024
EPS = 1e-6
NEG = -1e30

SEG = 32
N_SAMPLE_SEG = 32
META_SEG = N_SAMPLE_SEG
N_SMALL_SEG = 40
TM_MIX = 256
CHUNK_ROWS = SEG
SCAN_LANES = 512
TQ = 128
TQ_PER_STEP = 2
SEG_PER_STEP = 8
VMEM_LIMIT = 56 * 1024 * 1024


def _const_spec(shape):
    nd = len(shape)
    return pl.BlockSpec(shape, lambda *_: (0,) * nd, pipeline_mode=pl.Buffered(1))


def _params(*sem):
    return pltpu.CompilerParams(dimension_semantics=sem, vmem_limit_bytes=VMEM_LIMIT)


def _rms(x, g):
    return x * lax.rsqrt(jnp.mean(x * x, axis=-1, keepdims=True) + EPS) * g


def _dot(a, b):
    return jnp.dot(a, b, preferred_element_type=F32)


def _even_mixer_kernel(x_ref, gm_ref, perm_ref, unperm_ref, win_ref, cw_ref, cb_ref, cinit_ref, sinit_ref,
                       ab_ref, hop_ref, pw_ref, bb_ref, cc_ref, d_ref, gluw_ref, glub_ref, wout_ref,
                       h_ref, cout_ref, sout_ref, bu_ref, ccarry_ref, scarry_ref, *, carried, tiles_per_seq):
    if carried:
        @pl.when(pl.program_id(0) % tiles_per_seq == 0)
        def _load_initial_state():
            ccarry_ref[...] = cinit_ref[...]
            scarry_ref[...] = sinit_ref[...]

    streams = range(x_ref.shape[0])
    mid = [_mixer_project(b, x_ref, gm_ref, perm_ref, win_ref, cw_ref, cb_ref, cinit_ref, bb_ref, cout_ref, bu_ref,
                          ccarry_ref, carried=carried) for b in streams]
    for b in streams:
        _mixer_recur(b, sinit_ref, ab_ref, hop_ref, pw_ref, sout_ref, bu_ref, scarry_ref, carried=carried)
        _mixer_output(b, *mid[b], unperm_ref, cc_ref, d_ref, gluw_ref, glub_ref, wout_ref, h_ref, bu_ref)


def _mixer_project(b, x_ref, gm_ref, perm_ref, win_ref, cw_ref, cb_ref, cinit_ref, bb_ref, cout_ref, bu_ref,
                   ccarry_ref, *, carried):
    tm = TM_MIX
    x = x_ref[b]
    hn = _rms(x, gm_ref[...]).astype(BF16)
    hn = _dot(perm_ref[...], hn).astype(BF16)
    proj = _dot(hn, win_ref[...])
    g_b = proj[:, :CONV_CH]
    cin = proj[:, CONV_CH:2 * CONV_CH] * proj[:, 2 * CONV_CH:3 * CONV_CH]
    u = proj[:, 3 * CONV_CH:]
    ub = u.astype(BF16)
    for half in range(2):
        bu_ref[b, :, half * S5_STATE:(half + 1) * S5_STATE] = _dot(ub[:, half * HALF_CH:(half + 1) * HALF_CH],
                                                                  bb_ref[half])

    last, before_last = cin[tm - 8:tm], cin[tm - 16:tm - 8]
    if carried:
        first_chunk = lax.broadcasted_iota(jnp.int32, (8, CONV_CH), 0) == 0
        newer = jnp.where(first_chunk, ccarry_ref[1, b:b + 1, :], pltpu.roll(last, 1, 0))
        older = jnp.where(first_chunk, ccarry_ref[0, b:b + 1, :], pltpu.roll(before_last, 1, 0))
        for i, rows in enumerate((before_last, last)):
            ccarry_ref[i, b:b + 1, :] = rows[7:8]
            cout_ref[i, b:b + 1, :] = rows[7:8]
    else:
        older, newer = cinit_ref[0], cinit_ref[1]
        cout_ref[0] = before_last
        cout_ref[1] = last
    ext = jnp.concatenate([older, newer, cin], axis=0)
    cw = cw_ref[...]
    out_a = g_b * (cw[0:1] * ext[:tm] + cw[1:2] * ext[8:tm + 8] + cw[2:3] * cin + cb_ref[...])
    return x, u, out_a


def _mixer_recur(b, sinit_ref, ab_ref, hop_ref, pw_ref, sout_ref, bu_ref, scarry_ref, *, carried):
    for c in range(S5_STATE // SCAN_LANES):
        half, within = divmod(c * SCAN_LANES, HALF_STATE)
        nat = slice(c * SCAN_LANES, (c + 1) * SCAN_LANES)
        re_cols = slice(half * S5_STATE + within, half * S5_STATE + within + SCAN_LANES)
        im_cols = slice(re_cols.start + HALF_STATE, re_cols.stop + HALF_STATE)
        ar, ai = ab_ref[0, :, nat], ab_ref[1, :, nat]
        if carried:
            sr = si = jnp.zeros((8, SCAN_LANES), F32)
        else:
            sr, si = sinit_ref[:, re_cols], sinit_ref[:, im_cols]
        for g in range(CHUNK_ROWS):
            rows = slice(8 * g, 8 * g + 8)
            sr, si = (ar * sr - ai * si + bu_ref[b, rows, re_cols], ar * si + ai * sr + bu_ref[b, rows, im_cols])
            bu_ref[b, rows, re_cols] = sr
            bu_ref[b, rows, im_cols] = si
        if not carried:
            sout_ref[:, re_cols] = sr
            sout_ref[:, im_cols] = si
            continue
        first_chunk = lax.broadcasted_iota(jnp.int32, (8, SCAN_LANES), 0) == 0
        er = jnp.where(first_chunk, scarry_ref[b:b + 1, re_cols], pltpu.roll(sr, 1, 0))
        ei = jnp.where(first_chunk, scarry_ref[b:b + 1, im_cols], pltpu.roll(si, 1, 0))
        for k, sh in enumerate((1, 2, 4)):
            wr, wi = hop_ref[2 * k, :, nat], hop_ref[2 * k + 1, :, nat]
            tr, ti = pltpu.roll(er, sh, 0), pltpu.roll(ei, sh, 0)
            er, ei = er + wr * tr - wi * ti, ei + wr * ti + wi * tr
        wr, wi = hop_ref[6, :, nat], hop_ref[7, :, nat]
        fr, fi = wr * er - wi * ei + sr, wr * ei + wi * er + si
        for ref in (scarry_ref, sout_ref):
            ref[b:b + 1, re_cols] = fr[7:8]
            ref[b:b + 1, im_cols] = fi[7:8]
        for g in range(CHUNK_ROWS):
            rows = slice(8 * g, 8 * g + 8)
            pr, pi = pw_ref[0, g, :, nat], pw_ref[1, g, :, nat]
            bu_ref[b, rows, re_cols] += pr * er - pi * ei
            bu_ref[b, rows, im_cols] += pr * ei + pi * er


def _mixer_output(b, x, u, out_a, unperm_ref, cc_ref, d_ref, gluw_ref, glub_ref, wout_ref, h_ref, bu_ref):
    y = jnp.concatenate([_dot(bu_ref[b, :, half * S5_STATE:(half + 1) * S5_STATE].astype(BF16), cc_ref[half])
                         for half in range(2)], axis=-1) + d_ref[...] * u
    z = jax.nn.gelu(y)
    out_b = z * jax.nn.sigmoid(_dot(z.astype(BF16), gluw_ref[...]) + glub_ref[...])
    mix = jnp.concatenate([out_a, out_b], axis=-1).astype(BF16)
    mix = _dot(unperm_ref[...], mix).astype(BF16)
    h_ref[b] = x + _dot(mix, wout_ref[...])


def _even_mixer(x, gm, w_in, cw, cb, cinit, sinit, ab, hop, pw, bb, cc, d, glu_w, glu_b, w_out, *, carried):
    nb, rows = x.shape[0], x.shape[1]
    tm = TM_MIX
    chunks = tm // CHUNK_ROWS
    assert chunks == 8 and rows % tm == 0 and (carried or nb == 1)
    per_step = nb if carried else chunks
    tiles_per_seq = rows // tm if carried else 1
    nseq = sinit.shape[0]
    t_of = np.arange(tm)
    perm = np.zeros((tm, tm), np.float32)
    perm[8 * (t_of % CHUNK_ROWS) + t_of // CHUNK_ROWS, t_of] = 1.0
    kern = functools.partial(_even_mixer_kernel, carried=carried, tiles_per_seq=tiles_per_seq)
    return pl.pallas_call(
        kern,
        grid=(rows // tm,),
        in_specs=[
            pl.BlockSpec((nb, tm, D_MODEL), lambda t: (0, t, 0)),
            _const_spec((1, D_MODEL)),
            _const_spec((tm, tm)),
            _const_spec((tm, tm)),
            _const_spec((D_MODEL, 4 * CONV_CH)),
            _const_spec((3, CONV_CH)),
            _const_spec((1, CONV_CH)),
            pl.BlockSpec((2, per_step, CONV_CH), lambda t: (0, t // tiles_per_seq, 0)),
            pl.BlockSpec((per_step, 2 * S5_STATE), lambda t: (t // tiles_per_seq, 0)),
            _const_spec((2, 8, S5_STATE)),
            _const_spec((8, 8, S5_STATE)),
            _const_spec((2, CHUNK_ROWS, 8, S5_STATE)),
            _const_spec((2, HALF_CH, S5_STATE)),
            _const_spec((2, S5_STATE, HALF_CH)),
            _const_spec((1, S5_CH)),
            _const_spec((S5_CH, S5_CH)),
            _const_spec((1, S5_CH)),
            _const_spec((D_MODEL, D_MODEL)),
        ],
        out_specs=[
            pl.BlockSpec((nb, tm, D_MODEL), lambda t: (0, t, 0)),
            pl.BlockSpec((2, per_step, CONV_CH), lambda t: (0, t // tiles_per_seq, 0)),
            pl.BlockSpec((per_step, 2 * S5_STATE), lambda t: (t // tiles_per_seq, 0)),
        ],
        out_shape=[
            jax.ShapeDtypeStruct((nb, rows, D_MODEL), F32),
            jax.ShapeDtypeStruct((2, nseq, CONV_CH), F32),
            jax.ShapeDtypeStruct((nseq, 2 * S5_STATE), F32),
        ],
        scratch_shapes=[
            pltpu.VMEM((nb, tm, 2 * S5_STATE), F32),
            pltpu.VMEM((2, nb, CONV_CH), F32),
            pltpu.VMEM((nb, 2 * S5_STATE), F32),
        ],
        compiler_params=_params("arbitrary"),
        name="even_mixer",
    )(x, gm, jnp.asarray(perm, BF16), jnp.asarray(perm.T, BF16), w_in, cw, cb, cinit, sinit, ab, hop, pw, bb, cc, d,
      glu_w, glu_b, w_out)


def _ffn_kernel(x_ref, g_ref, w1_ref, w3_ref, w2_ref, o_ref):
    x = x_ref[...]
    hn = _rms(x, g_ref[...]).astype(BF16)
    a = _dot(hn, w1_ref[...])
    b = _dot(hn, w3_ref[...])
    o_ref[...] = x + _dot((a * jax.nn.sigmoid(a) * b).astype(BF16), w2_ref[...])


def _ffn(x, g, w1, w3, w2, *, tm):
    rows = x.shape[0]
    return pl.pallas_call(
        _ffn_kernel,
        grid=(rows // tm,),
        in_specs=[
            pl.BlockSpec((tm, D_MODEL), lambda t: (t, 0)),
            _const_spec((1, D_MODEL)),
            _const_spec((D_MODEL, D_FF)),
            _const_spec((D_MODEL, D_FF)),
            _const_spec((D_FF, D_MODEL)),
        ],
        out_specs=pl.BlockSpec((tm, D_MODEL), lambda t: (t, 0)),
        out_shape=jax.ShapeDtypeStruct((rows, D_MODEL), F32),
        compiler_params=_params("parallel"),
        name="ffn",
    )(x, g, w1, w3, w2)


def _qkv_kernel(x_ref, g_ref, w_ref, b_ref, q_ref, k_ref, v_ref):
    hn = _rms(x_ref[...], g_ref[...]).astype(BF16)
    qkv = _dot(hn, w_ref[...]) + b_ref[...]
    nq = N_HEADS * HEAD_DIM
    q_ref[...] = (qkv[:, :nq] * (HEAD_DIM ** -0.5)).astype(BF16)
    k_ref[...] = qkv[:, nq:nq + KV_DIM]
    v_ref[...] = qkv[:, nq + KV_DIM:]


def _qkv(x, g, w, b, *, tm):
    rows = x.shape[0]
    ncol = (N_HEADS + 2 * N_KV) * HEAD_DIM
    return pl.pallas_call(
        _qkv_kernel,
        grid=(rows // tm,),
        in_specs=[
            pl.BlockSpec((tm, D_MODEL), lambda t: (t, 0)),
            _const_spec((1, D_MODEL)),
            _const_spec((D_MODEL, ncol)),
            _const_spec((1, ncol)),
        ],
        out_specs=[
            pl.BlockSpec((tm, N_HEADS * HEAD_DIM), lambda t: (t, 0)),
            pl.BlockSpec((tm, KV_DIM), lambda t: (t, 0)),
            pl.BlockSpec((tm, KV_DIM), lambda t: (t, 0)),
        ],
        out_shape=[
            jax.ShapeDtypeStruct((rows, N_HEADS * HEAD_DIM), BF16),
            jax.ShapeDtypeStruct((rows, KV_DIM), F32),
            jax.ShapeDtypeStruct((rows, KV_DIM), F32),
        ],
        compiler_params=_params("parallel"),
        name="qkv",
    )(x, g, w, b)


def _attend_groups(q, keys, vals, bias_of_group, sink_of_group, o_scr, row0):
    nq = q.shape[0]
    for kv in range(N_KV):
        kv_cols = slice(kv * HEAD_DIM, (kv + 1) * HEAD_DIM)
        heads = [slice((kv * GQ + g) * HEAD_DIM, (kv * GQ + g + 1) * HEAD_DIM) for g in range(GQ)]
        qs = jnp.concatenate([q[:, cols] for cols in heads], axis=0)
        st = lax.dot_general(keys[:, kv_cols], qs, (((1,), (1,)), ((), ())),
                             preferred_element_type=F32) + bias_of_group(kv)
        sk = sink_of_group(kv)
        m = jnp.maximum(jnp.max(st, axis=0, keepdims=True), sk)
        p = jnp.exp(st - m)
        den = jnp.sum(p, axis=0, keepdims=True) + jnp.exp(sk - m)
        o = lax.dot_general((p * (1.0 / den)).astype(BF16), vals[:, kv_cols], (((0,), (0,)), ((), ())),
                            preferred_element_type=F32)
        for g, cols in enumerate(heads):
            o_scr[row0:row0 + nq, cols] = o[g * nq:(g + 1) * nq]


def _attn_prompt_kernel(q_ref, k_ref, v_ref, bias_ref, sink_ref, x_ref, wo_ref, bo_ref, o_ref, o_scr):
    t = pl.program_id(1)
    for u in range(TQ_PER_STEP):
        tile = t * TQ_PER_STEP + u
        start = pl.multiple_of(tile * TQ, TQ)
        keys = k_ref[0, pl.ds(start, TQ + WINDOW), :]
        vals = v_ref[0, pl.ds(start, TQ + WINDOW), :]
        variant = jnp.minimum(tile, 1)
        _attend_groups(q_ref[u * TQ:(u + 1) * TQ], keys, vals, lambda kv: bias_ref[variant, kv],
                       lambda kv: sink_ref[kv], o_scr, u * TQ)
    o_ref[...] = x_ref[...] + _dot(o_scr[...].astype(BF16), wo_ref[...]) + bo_ref[...]


def _attn_prompt(q, kpad, vpad, bias, sink, x, w_o, b_o):
    nb, lpad = kpad.shape[0], kpad.shape[1]
    tm = TQ_PER_STEP * TQ
    steps = (lpad - WINDOW) // tm
    return pl.pallas_call(
        _attn_prompt_kernel,
        grid=(nb, steps),
        in_specs=[
            pl.BlockSpec((tm, D_MODEL), lambda b, t: (b * steps + t, 0)),
            pl.BlockSpec((1, lpad, KV_DIM), lambda b, t: (b, 0, 0)),
            pl.BlockSpec((1, lpad, KV_DIM), lambda b, t: (b, 0, 0)),
            _const_spec((2, N_KV, TQ + WINDOW, GQ * TQ)),
            _const_spec((N_KV, 1, GQ * TQ)),
            pl.BlockSpec((tm, D_MODEL), lambda b, t: (b * steps + t, 0)),
            _const_spec((D_MODEL, D_MODEL)),
            _const_spec((1, D_MODEL)),
        ],
        out_specs=pl.BlockSpec((tm, D_MODEL), lambda b, t: (b * steps + t, 0)),
        out_shape=jax.ShapeDtypeStruct(x.shape, F32),
        scratch_shapes=[pltpu.VMEM((tm, D_MODEL), F32)],
        compiler_params=_params("parallel", "arbitrary"),
        name="attn_prompt",
    )(q, kpad, vpad, bias, sink, x, w_o, b_o)


def _attn_small_kernel(q_ref, kc_ref, vc_ref, kn_ref, vn_ref, bias_ref, sink_ref, x_ref, wo_ref, bo_ref,
                       o_ref, o_scr):
    step = pl.program_id(0)
    for j in range(SEG_PER_STEP):
        rows = slice(j * SEG, (j + 1) * SEG)
        keys = jnp.concatenate([kc_ref[j], kn_ref[rows]], axis=0).astype(BF16)
        vals = jnp.concatenate([vc_ref[j], vn_ref[rows]], axis=0).astype(BF16)
        variant = jnp.where(step * SEG_PER_STEP + j == META_SEG, 1, 0)
        _attend_groups(q_ref[rows], keys, vals, lambda kv: bias_ref[variant, kv], lambda kv: sink_ref[kv], o_scr,
                       j * SEG)
    o_ref[...] = x_ref[...] + _dot(o_scr[...].astype(BF16), wo_ref[...]) + bo_ref[...]


def _attn_small(q, kc, vc, kn, vn, bias, sink, x, w_o, b_o):
    tm = SEG_PER_STEP * SEG
    steps = x.shape[0] // tm
    cache_blocks = kc.shape[0] // SEG_PER_STEP
    cache_spec = pl.BlockSpec((SEG_PER_STEP, WINDOW, KV_DIM), lambda s: (jnp.minimum(s, cache_blocks - 1), 0, 0))
    return pl.pallas_call(
        _attn_small_kernel,
        grid=(steps,),
        in_specs=[
            pl.BlockSpec((tm, D_MODEL), lambda s: (s, 0)),
            cache_spec,
            cache_spec,
            pl.BlockSpec((tm, KV_DIM), lambda s: (s, 0)),
            pl.BlockSpec((tm, KV_DIM), lambda s: (s, 0)),
            _const_spec((2, N_KV, WINDOW + SEG, GQ * SEG)),
            _const_spec((N_KV, 1, GQ * SEG)),
            pl.BlockSpec((tm, D_MODEL), lambda s: (s, 0)),
            _const_spec((D_MODEL, D_MODEL)),
            _const_spec((1, D_MODEL)),
        ],
        out_specs=pl.BlockSpec((tm, D_MODEL), lambda s: (s, 0)),
        out_shape=jax.ShapeDtypeStruct(x.shape, F32),
        scratch_shapes=[pltpu.VMEM((tm, D_MODEL), F32)],
        compiler_params=_params("parallel"),
        name="attn_small",
    )(q, kc, vc, kn, vn, bias, sink, x, w_o, b_o)


def _moe_kernel(x_ref, g_ref, rwt_ref, rb_ref, w1_ref, w3_ref, w2_ref, gf_ref, o_ref,
                hn_scr, slot_scr, gate_scr, cnt_smem, acc_scr, yall_scr, *, rt):
    e = pl.program_id(1)
    tb = x_ref.shape[0]

    @pl.when(e == 0)
    def _route():
        hn = _rms(x_ref[...], g_ref[...]).astype(BF16)
        hn_scr[...] = hn
        logits = lax.dot_general(rwt_ref[...], hn, (((1,), (1,)), ((), ())),
                                 preferred_element_type=F32) + rb_ref[...]
        row = lax.broadcasted_iota(jnp.int32, logits.shape, 0).astype(F32)
        v1 = jnp.max(logits, axis=0, keepdims=True)
        i1 = jnp.min(jnp.where(logits == v1, row, float(N_EXPERTS)), axis=0, keepdims=True)
        rest = jnp.where(row == i1, -jnp.inf, logits)
        v2 = jnp.max(rest, axis=0, keepdims=True)
        i2 = jnp.min(jnp.where(rest == v2, row, float(N_EXPERTS)), axis=0, keepdims=True)
        e2 = jnp.exp(v2 - v1)
        den = 1.0 + e2
        gate_scr[...] = jnp.where(row == i1, 1.0 / den, 0.0) + jnp.where(row == i2, e2 / den, 0.0)
        chosen = (row == i1) | (row == i2)
        ind = jnp.where(chosen, 1.0, 0.0)
        earlier = (lax.broadcasted_iota(jnp.int32, (tb, tb), 0) < lax.broadcasted_iota(jnp.int32, (tb, tb), 1))
        rank = _dot(ind.astype(BF16), jnp.where(earlier, 1.0, 0.0).astype(BF16))
        slot_scr[...] = jnp.where(chosen, rank, -1.0).astype(jnp.int32)
        for k in range(N_EXPERTS):
            cnt_smem[k] = jnp.sum(ind[k:k + 1, :]).astype(jnp.int32)
        acc_scr[...] = jnp.zeros_like(acc_scr)

    slot = slot_scr[pl.ds(e, 1), :]
    gate = gate_scr[pl.ds(e, 1), :]

    def _scatter(onehot, rows):
        return lax.dot_general(onehot, rows, (((0,), (0,)), ((), ())), preferred_element_type=F32)

    def _expert_tile(i):
        packed_row = lax.broadcasted_iota(jnp.int32, (rt, tb), 0) + i * rt
        hit = packed_row == slot
        onehot = jnp.where(hit, 1.0, 0.0).astype(BF16)
        xe = _dot(onehot, hn_scr[...]).astype(BF16)
        a = _dot(xe, w1_ref[0])
        b = _dot(xe, w3_ref[0])
        y = _dot((a * jax.nn.sigmoid(a) * b).astype(BF16), w2_ref[0])
        g_row = jnp.sum(jnp.where(hit, gate, 0.0), axis=1, keepdims=True)
        return onehot, (y * g_row).astype(BF16)

    yall_scr[pl.ds(pl.multiple_of(e * rt, 16), rt), :] = _expert_tile(0)[1]

    def _overflow_tile(i, carry):
        acc_scr[...] += _scatter(*_expert_tile(i))
        return carry

    lax.fori_loop(1, (cnt_smem[e] + rt - 1) // rt, _overflow_tile, 0)

    @pl.when(e == N_EXPERTS - 1)
    def _finish():
        packed_row = lax.broadcasted_iota(jnp.int32, (rt, tb), 0)
        onehot = jnp.concatenate([jnp.where(packed_row == slot_scr[k:k + 1, :], 1.0, 0.0).astype(BF16)
                                  for k in range(N_EXPERTS)], axis=0)
        o_ref[...] = _rms(x_ref[...] + acc_scr[...] + _scatter(onehot, yall_scr[...]), gf_ref[...])


def _moe_final(x, g, rwt, rb, w1, w3, w2, gf, *, tb, rt):
    rows = x.shape[0]
    return pl.pallas_call(
        functools.partial(_moe_kernel, rt=rt),
        grid=(rows // tb, N_EXPERTS),
        in_specs=[
            pl.BlockSpec((tb, D_MODEL), lambda t, e: (t, 0)),
            _const_spec((1, D_MODEL)),
            _const_spec((N_EXPERTS, D_MODEL)),
            _const_spec((N_EXPERTS, 1)),
            pl.BlockSpec((1, D_MODEL, D_FF_E), lambda t, e: (e, 0, 0)),
            pl.BlockSpec((1, D_MODEL, D_FF_E), lambda t, e: (e, 0, 0)),
            pl.BlockSpec((1, D_FF_E, D_MODEL), lambda t, e: (e, 0, 0)),
            _const_spec((1, D_MODEL)),
        ],
        out_specs=pl.BlockSpec((tb, D_MODEL), lambda t, e: (t, 0)),
        out_shape=jax.ShapeDtypeStruct((rows, D_MODEL), F32),
        scratch_shapes=[
            pltpu.VMEM((tb, D_MODEL), BF16),
            pltpu.VMEM((N_EXPERTS, tb), jnp.int32),
            pltpu.VMEM((N_EXPERTS, tb), F32),
            pltpu.SMEM((N_EXPERTS,), jnp.int32),
            pltpu.VMEM((tb, D_MODEL), F32),
            pltpu.VMEM((N_EXPERTS * rt, D_MODEL), BF16),
        ],
        compiler_params=_params("parallel", "arbitrary"),
        name="moe_final",
    )(x, g, rwt, rb, w1, w3, w2, gf)


def _moe_route(hn, rwt_ref, rb_ref):
    tb = hn.shape[0]
    logits = lax.dot_general(rwt_ref[...], hn, (((1,), (1,)), ((), ())), preferred_element_type=F32) + rb_ref[...]
    row = lax.broadcasted_iota(jnp.int32, logits.shape, 0).astype(F32)
    v1 = jnp.max(logits, axis=0, keepdims=True)
    i1 = jnp.min(jnp.where(logits == v1, row, float(N_EXPERTS)), axis=0, keepdims=True)
    rest = jnp.where(row == i1, -jnp.inf, logits)
    v2 = jnp.max(rest, axis=0, keepdims=True)
    i2 = jnp.min(jnp.where(rest == v2, row, float(N_EXPERTS)), axis=0, keepdims=True)
    e2 = jnp.exp(v2 - v1)
    den = 1.0 + e2
    gate = jnp.where(row == i1, 1.0 / den, 0.0) + jnp.where(row == i2, e2 / den, 0.0)
    chosen = (row == i1) | (row == i2)
    earlier = (lax.broadcasted_iota(jnp.int32, (tb, tb), 0) < lax.broadcasted_iota(jnp.int32, (tb, tb), 1))
    rank = _dot(jnp.where(chosen, 1.0, 0.0).astype(BF16), jnp.where(earlier, 1.0, 0.0).astype(BF16))
    return gate, jnp.where(chosen, rank, -1.0).astype(jnp.int32)


def _moe_pack(hn, slot, gate, xs_ref, grow_ref, *, rt, tile):
    tb = hn.shape[0]
    packed_row = lax.broadcasted_iota(jnp.int32, (rt, tb), 0) + tile * rt
    for k in range(N_EXPERTS):
        hit = packed_row == slot[k:k + 1]
        xs_ref[k, 0] = _dot(jnp.where(hit, 1.0, 0.0).astype(BF16), hn).astype(BF16)
        grow_ref[k, 0] = jnp.sum(jnp.where(hit, gate[k:k + 1], 0.0), axis=1, keepdims=True)


def _moe_route_kernel(x_ref, g_ref, rwt_ref, rb_ref, xs_ref, grow_ref, slot_ref, gate_ref, *, rt):
    hn = _rms(x_ref[...], g_ref[...]).astype(BF16)
    gate, slot = _moe_route(hn, rwt_ref, rb_ref)
    slot_ref[0] = slot
    gate_ref[0] = gate
    _moe_pack(hn, slot, gate, xs_ref, grow_ref, rt=rt, tile=0)


def _moe_repack_kernel(x_ref, g_ref, slot_ref, gate_ref, xs_ref, grow_ref, *, rt, tile):
    hn = _rms(x_ref[...], g_ref[...]).astype(BF16)
    _moe_pack(hn, slot_ref[0], gate_ref[0], xs_ref, grow_ref, rt=rt, tile=tile)


def _moe_expert_kernel(xs_ref, grow_ref, w1_ref, w3_ref, w2_ref, ys_ref):
    rows = xs_ref.shape[1] * xs_ref.shape[2]
    xe = xs_ref[0].reshape(rows, D_MODEL)
    a = _dot(xe, w1_ref[0])
    b = _dot(xe, w3_ref[0])
    y = _dot((a * jax.nn.sigmoid(a) * b).astype(BF16), w2_ref[0])
    ys_ref[0] = (y * grow_ref[0].reshape(rows, 1)).astype(BF16).reshape(ys_ref.shape[1:])


def _moe_scatter(slot, ys_ref, *, rt, tile):
    tb = slot.shape[1]
    packed_row = lax.broadcasted_iota(jnp.int32, (rt, tb), 0) + tile * rt
    onehot = jnp.concatenate([jnp.where(packed_row == slot[k:k + 1], 1.0, 0.0).astype(BF16)
                              for k in range(N_EXPERTS)], axis=0)
    return lax.dot_general(onehot, ys_ref[:, 0].reshape(N_EXPERTS * rt, D_MODEL), (((0,), (0,)), ((), ())),
                           preferred_element_type=F32)


def _moe_combine_kernel(x_ref, slot_ref, ys_ref, gf_ref, o_ref, *, rt):
    o_ref[...] = _rms(x_ref[...] + _moe_scatter(slot_ref[0], ys_ref, rt=rt, tile=0), gf_ref[...])


def _moe_combine_more_kernel(x_ref, slot_ref, ys_ref, more_ref, gf_ref, o_ref, *, rt):
    o_ref[...] = _rms(x_ref[...] + more_ref[...] + _moe_scatter(slot_ref[0], ys_ref, rt=rt, tile=0), gf_ref[...])


def _moe_scatter_kernel(slot_ref, ys_ref, o_ref, *, rt, tile):
    o_ref[...] = _moe_scatter(slot_ref[0], ys_ref, rt=rt, tile=tile)


def _moe_staged(x, g, rwt, rb, w1, w3, w2, gf, *, tb, rt, blocks_per_step):
    rows = x.shape[0]
    nblk = rows // tb
    tiles = -(-tb // rt)
    x_spec = pl.BlockSpec((tb, D_MODEL), lambda b: (b, 0))
    route_spec = pl.BlockSpec((1, N_EXPERTS, tb), lambda b: (b, 0, 0))
    xs_spec = pl.BlockSpec((N_EXPERTS, 1, rt, D_MODEL), lambda b: (0, b, 0, 0))
    grow_spec = pl.BlockSpec((N_EXPERTS, 1, rt, 1), lambda b: (0, b, 0, 0))
    packed = [jax.ShapeDtypeStruct((N_EXPERTS, nblk, rt, D_MODEL), BF16),
              jax.ShapeDtypeStruct((N_EXPERTS, nblk, rt, 1), F32)]
    routed = [jax.ShapeDtypeStruct((nblk, N_EXPERTS, tb), jnp.int32), jax.ShapeDtypeStruct((nblk, N_EXPERTS, tb), F32)]

    xs, grow, slot, gate = pl.pallas_call(
        functools.partial(_moe_route_kernel, rt=rt),
        grid=(nblk,),
        in_specs=[x_spec, _const_spec((1, D_MODEL)), _const_spec((N_EXPERTS, D_MODEL)), _const_spec((N_EXPERTS, 1))],
        out_specs=[xs_spec, grow_spec, route_spec, route_spec],
        out_shape=packed + routed,
        compiler_params=_params("parallel"),
        name="moe_route",
    )(x, g, rwt, rb)

    def experts(xs, grow):
        nstep = nblk // blocks_per_step
        tile_spec = pl.BlockSpec((1, blocks_per_step, rt, D_MODEL), lambda e, j: (e, j, 0, 0))
        w_spec = pl.BlockSpec((1, D_MODEL, D_FF_E), lambda e, j: (e, 0, 0))
        return pl.pallas_call(
            _moe_expert_kernel,
            grid=(N_EXPERTS, nstep),
            in_specs=[tile_spec, pl.BlockSpec((1, blocks_per_step, rt, 1), lambda e, j: (e, j, 0, 0)),
                      w_spec, w_spec, pl.BlockSpec((1, D_FF_E, D_MODEL), lambda e, j: (e, 0, 0))],
            out_specs=tile_spec,
            out_shape=jax.ShapeDtypeStruct(xs.shape, BF16),
            compiler_params=_params("parallel", "arbitrary"),
            name="moe_experts",
        )(xs, grow, w1, w3, w2)

    ys = experts(xs, grow)

    def further_tile(tile):
        xs_t, grow_t = pl.pallas_call(
            functools.partial(_moe_repack_kernel, rt=rt, tile=tile),
            grid=(nblk,),
            in_specs=[x_spec, _const_spec((1, D_MODEL)), route_spec, route_spec],
            out_specs=[xs_spec, grow_spec],
            out_shape=packed,
            compiler_params=_params("parallel"),
            name="moe_repack",
        )(x, g, slot, gate)
        return pl.pallas_call(
            functools.partial(_moe_scatter_kernel, rt=rt, tile=tile),
            grid=(nblk,),
            in_specs=[route_spec, xs_spec],
            out_specs=x_spec,
            out_shape=jax.ShapeDtypeStruct(x.shape, F32),
            compiler_params=_params("parallel"),
            name="moe_scatter",
        )(slot, experts(xs_t, grow_t))

    def combine(more):
        kern, extra_in, extra_spec = ((_moe_combine_kernel, (), ()) if more is None
                                      else (_moe_combine_more_kernel, (more,), (x_spec,)))
        return pl.pallas_call(
            functools.partial(kern, rt=rt),
            grid=(nblk,),
            in_specs=[x_spec, route_spec, xs_spec, *extra_spec, _const_spec((1, D_MODEL))],
            out_specs=x_spec,
            out_shape=jax.ShapeDtypeStruct(x.shape, F32),
            compiler_params=_params("parallel"),
            name="moe_combine",
        )(x, slot, ys, *extra_in, gf)

    def with_further_tiles():
        more = further_tile(1)
        for tile in range(2, tiles):
            more = lax.cond(most > tile * rt, lambda m, tile=tile: m + further_tile(tile), lambda m: m, more)
        return combine(more)

    most = jnp.max(slot) + 1
    return lax.cond(most > rt, with_further_tiles, lambda: combine(None))


def _t5_bucket_np(rel):
    half = N_BUCKETS // 2
    max_exact = half // 2
    ret = np.where(rel > 0, half, 0)
    n = np.abs(rel)
    nf = np.maximum(n, 1).astype(np.float32)
    large = max_exact + (np.log(nf / np.float32(max_exact)) / np.float32(math.log(MAX_DIST / max_exact))
                         * np.float32(half - max_exact)).astype(np.int32)
    large = np.minimum(large, half - 1)
    return ret + np.where(n < max_exact, n, large)


def _bias_variants(table, bucket, keeps):
    nq, nk = bucket.shape
    onehot = jnp.asarray(np.eye(N_BUCKETS, dtype=np.float32)[bucket])
    bias = jnp.einsum('rjb,bh->hjr', onehot, table, precision=lax.Precision.HIGHEST)
    bias = jnp.stack([jnp.where(keep.T[None], bias, NEG) for keep in keeps], axis=0)
    bias = bias.reshape(len(keeps), N_KV, GQ, nk, nq)
    return jnp.transpose(bias, (0, 1, 3, 2, 4)).reshape(len(keeps), N_KV, nk, GQ * nq)


def _lane_sink(sink, nq):
    return jnp.broadcast_to(sink.reshape(N_KV, 1, GQ, 1), (N_KV, 1, GQ, nq)).reshape(N_KV, 1, GQ * nq)


def _prompt_bias(table):
    r = np.arange(TQ)[:, None]
    j = np.arange(TQ + WINDOW)[None, :]
    band = (j - CHUNK * (r // CHUNK) >= 0) & (j - CHUNK * (r // CHUNK) < WINDOW + CHUNK)
    first = band & (j >= WINDOW - N_META)
    return _bias_variants(table, _t5_bucket_np(j - WINDOW - r), [first, band])


def _small_bias(table):
    i = np.arange(SEG)[:, None]
    m = np.arange(WINDOW + SEG)[None, :]
    everything = np.ones((SEG, WINDOW + SEG), bool)
    meta_only = everything & (m >= WINDOW + SEG - N_META)
    return _bias_variants(table, _t5_bucket_np(m - WINDOW - i), [everything, meta_only])


def _pack_state(re, im):
    return jnp.concatenate([re[..., :HALF_STATE], im[..., :HALF_STATE], re[..., HALF_STATE:], im[..., HALF_STATE:]],
                           axis=-1)


def _cmul(ar, ai, br, bi):
    return ar * br - ai * bi, ar * bi + ai * br


def _s5_prepare(a_re, a_im, log_dt, b_re, b_im, c_re, c_im):
    dt = jnp.exp(log_dt)[:, None]
    mag = jnp.exp(a_re * dt)
    ab_re, ab_im = mag * jnp.cos(a_im * dt), mag * jnp.sin(a_im * dt)
    den = a_re * a_re + a_im * a_im
    num_re = ab_re - 1.0
    cf_re = (num_re * a_re + ab_im * a_im) / den
    cf_im = (ab_im * a_re - num_re * a_im) / den
    bb_re = cf_re[..., None] * b_re - cf_im[..., None] * b_im
    bb_im = cf_re[..., None] * b_im + cf_im[..., None] * b_re

    def group_diag(v):
        r, c = v.shape[1], v.shape[2]
        gh = S5_GROUPS // 2
        spread = np.tile(np.eye(c, dtype=np.float32), (1, gh))
        wide = jnp.einsum('hrc,cn->hrn', v.reshape(2, gh * r, c), spread, precision=lax.Precision.HIGHEST)
        keep = (np.arange(gh * r)[:, None] // r) == (np.arange(gh * c)[None, :] // c)
        return jnp.where(keep, wide, 0.0)

    def t(v):
        return jnp.swapaxes(v, 1, 2)

    bb = jnp.concatenate([group_diag(t(bb_re)), group_diag(t(bb_im))], axis=2).astype(BF16)
    cc = jnp.concatenate([group_diag(t(c_re)), -group_diag(t(c_im))], axis=1).astype(BF16)

    pr, pi = ab_re.reshape(1, S5_STATE), ab_im.reshape(1, S5_STATE)
    n = 1
    while n < CHUNK_ROWS:
        qr, qi = _cmul(pr, pi, pr[n - 1:n], pi[n - 1:n])
        pr, pi = jnp.concatenate([pr, qr], axis=0), jnp.concatenate([pi, qi], axis=0)
        n *= 2

    def tile8(v):
        return jnp.broadcast_to(v, (8,) + v.shape[1:])

    ab = jnp.stack([tile8(pr[0:1]), tile8(pi[0:1])])
    sub = np.arange(8)[:, None]
    hop, w = [], (pr[CHUNK_ROWS - 1:], pi[CHUNK_ROWS - 1:])
    for k in (1, 2, 4):
        hop += [jnp.where(sub >= k, w[0], 0.0), jnp.where(sub >= k, w[1], 0.0)]
        if k < 4:
            w = _cmul(*w, *w)
    hop = jnp.stack(hop + [tile8(pr[CHUNK_ROWS - 1:]), tile8(pi[CHUNK_ROWS - 1:])])
    pw = jnp.stack([jnp.broadcast_to(p[:, None, :], (CHUNK_ROWS, 8, S5_STATE)) for p in (pr, pi)])
    return bb, cc, ab, hop, pw


def kernel(x_prompt, x_sample, cache_conv, state_s5_re, state_s5_im, cache_swa_k, cache_swa_v, meta_tokens, rel_bias_table, norm_mix, norm_ffn, norm_final, w_in0, conv_w, conv_b, s5_a_re, s5_a_im, s5_log_dt, s5_b_re, s5_b_im, s5_c_re, s5_c_im, s5_d, s5_glu_w, s5_glu_b, w_out0, ffn_w1, ffn_w3, ffn_w2, w_qkv, b_qkv, attn_sink, w_o, b_o, router_w, router_b, moe_w1, moe_w3, moe_w2):
    nb, seq = x_prompt.shape[0], x_prompt.shape[1]
    nsb = x_sample.shape[0]
    assert x_sample.shape[1] == SEG and nsb == N_SAMPLE_SEG and seq % TM_MIX == 0

    def row(v):
        return v.reshape(1, -1)

    meta_seg = jnp.concatenate([jnp.zeros((SEG - N_META, D_MODEL), F32), meta_tokens], axis=0)
    filler = jnp.zeros(((N_SMALL_SEG - N_SAMPLE_SEG - 1) * SEG, D_MODEL), F32)
    xs = jnp.concatenate([x_sample.reshape(nsb * SEG, D_MODEL), meta_seg, filler], axis=0)
    n_extra = N_SMALL_SEG - N_SAMPLE_SEG

    bb, cc, ab, hop, pw = _s5_prepare(s5_a_re[0], s5_a_im[0], s5_log_dt[0], s5_b_re[0], s5_b_im[0], s5_c_re[0],
                                      s5_c_im[0])
    mixer_w = (row(norm_mix[0]), w_in0[0].astype(BF16), conv_w[0], row(conv_b[0]))
    mixer_w2 = (ab, hop, pw, bb, cc, row(s5_d[0]), s5_glu_w[0].astype(BF16), row(s5_glu_b[0]), w_out0[0].astype(BF16))
    cinit_s = jnp.concatenate([jnp.swapaxes(cache_conv[0], 0, 1), jnp.zeros((2, n_extra, CONV_CH), F32)], axis=1)
    sinit_s = _pack_state(state_s5_re[0].reshape(nsb, S5_STATE), state_s5_im[0].reshape(nsb, S5_STATE))
    sinit_s = jnp.concatenate([sinit_s, jnp.zeros((n_extra, 2 * S5_STATE), F32)], axis=0)
    hs, conv_s, state_s = _even_mixer(xs[None], *mixer_w, cinit_s, sinit_s, *mixer_w2, carried=False)
    hs = hs[0]
    cinit_p = jnp.broadcast_to(conv_s[:, META_SEG:META_SEG + 1], (2, nb, CONV_CH))
    sinit_p = jnp.broadcast_to(state_s[META_SEG:META_SEG + 1], (nb, 2 * S5_STATE))
    hp, conv_p, state_p = _even_mixer(x_prompt, *mixer_w, cinit_p, sinit_p, *mixer_w2, carried=True)
    hp = hp.reshape(nb * seq, D_MODEL)
    ffn_w = (row(norm_ffn[0]), ffn_w1[0].astype(BF16), ffn_w3[0].astype(BF16), ffn_w2[0].astype(BF16))
    hs = _ffn(hs, *ffn_w, tm=256)
    hp = _ffn(hp, *ffn_w, tm=512)

    qkv_w = (row(norm_mix[1]), w_qkv[0].astype(BF16), row(b_qkv[0]))
    qs, ks, vs = _qkv(hs, *qkv_w, tm=256)
    qp, kp, vp = _qkv(hp, *qkv_w, tm=512)
    wo = (w_o[0].astype(BF16), row(b_o[0]))
    hs = _attn_small(qs, cache_swa_k[0].reshape(nsb, WINDOW, KV_DIM), cache_swa_v[0].reshape(nsb, WINDOW, KV_DIM),
                     ks, vs, _small_bias(rel_bias_table), _lane_sink(attn_sink[0], SEG), hs, *wo)

    def padded(new, small):
        meta_rows = small[META_SEG * SEG:(META_SEG + 1) * SEG]
        front = jnp.concatenate([jnp.zeros((WINDOW - SEG, KV_DIM), F32), meta_rows], axis=0)
        front = jnp.broadcast_to(front[None], (nb, WINDOW, KV_DIM))
        return jnp.concatenate([front, new.reshape(nb, seq, KV_DIM)], axis=1).astype(BF16)

    hp = _attn_prompt(qp, padded(kp, ks), padded(vp, vs), _prompt_bias(rel_bias_table),
                      _lane_sink(attn_sink[0], TQ), hp, *wo)
    moe_w = (row(norm_ffn[1]), router_w[0].T.astype(BF16), router_b[0].reshape(N_EXPERTS, 1), moe_w1[0].astype(BF16),
             moe_w3[0].astype(BF16), moe_w2[0].astype(BF16), row(norm_final))
    ys = _moe_final(hs, *moe_w, tb=hs.shape[0], rt=352)
    yp = _moe_staged(hp, *moe_w, tb=1024, rt=320, blocks_per_step=4)

    n_real = nsb * SEG
    y_prompt = yp.reshape(nb, seq, D_MODEL)
    y_sample = ys[:n_real].reshape(nsb, SEG, D_MODEL)

    def split_state(st, n):
        st = st[:n].reshape(n, 2, 2, HALF_STATE)
        return tuple(st[:, :, ri].reshape(1, n, S5_GROUPS, S5_P) for ri in range(2))

    s5rp, s5ip = split_state(state_p, nb)
    s5rs, s5is = split_state(state_s, nsb)
    kp4 = kp.reshape(nb, seq, KV_DIM)[:, -WINDOW:].reshape(1, nb, WINDOW, N_KV, HEAD_DIM)
    vp4 = vp.reshape(nb, seq, KV_DIM)[:, -WINDOW:].reshape(1, nb, WINDOW, N_KV, HEAD_DIM)
    ks4 = ks[:n_real].reshape(1, nsb, SEG, N_KV, HEAD_DIM)
    vs4 = vs[:n_real].reshape(1, nsb, SEG, N_KV, HEAD_DIM)
    conv_p4 = jnp.swapaxes(conv_p, 0, 1)[None]
    conv_s4 = jnp.swapaxes(conv_s[:, :nsb], 0, 1)[None]
    return (y_prompt, y_sample, conv_p4, conv_s4, s5rp, s5ip, s5rs, s5is, kp4, vp4, ks4, vs4)
```

```python
import functools
import math

import numpy as np
import jax
import jax.numpy as jnp
from jax import lax
from jax.experimental import pallas as pl
from jax.experimental.pallas import tpu as pltpu

F32 = jnp.float32
BF16 = jnp.bfloat16

D_MODEL = 1024
CONV_CH = 512
S5_CH = 512
S5_GROUP = 16
S5_GROUPS = 32
S5_P = 64
S5_STATE = S5_GROUPS * S5_P
HALF_CH = S5_CH // 2
HALF_STATE = S5_STATE // 2
D_FF = 2816
N_HEADS = 16
N_KV = 2
GQ = N_HEADS // N_KV
HEAD_DIM = 64
KV_DIM = N_KV * HEAD_DIM
WINDOW = 128
CHUNK = 64
N_META = 16
N_BUCKETS = 32
MAX_DIST = 128
N_EXPERTS = 8
D_FF_E = 1024
EPS = 1e-6
NEG = -1e30

SEG = 32
N_SAMPLE_SEG = 32
META_SEG = N_SAMPLE_SEG
N_SMALL_SEG = 40
TM_MIX = 256
CHUNK_ROWS = SEG
SCAN_LANES = 512
TQ = 128
TQ_PER_STEP = 2
SEG_PER_STEP = 8
VMEM_LIMIT = 56 * 1024 * 1024


def _const_spec(shape):
    nd = len(shape)
    return pl.BlockSpec(shape, lambda *_: (0,) * nd, pipeline_mode=pl.Buffered(1))


def _params(*sem):
    return pltpu.CompilerParams(dimension_semantics=sem, vmem_limit_bytes=VMEM_LIMIT)


def _rms(x, g):
    return x * lax.rsqrt(jnp.mean(x * x, axis=-1, keepdims=True) + EPS) * g


def _dot(a, b):
    return jnp.dot(a, b, preferred_element_type=F32)


def _even_mixer_kernel(x_ref, gm_ref, perm_ref, unperm_ref, win_ref, cw_ref, cb_ref, cinit_ref, sinit_ref,
                       ab_ref, hop_ref, pw_ref, bb_ref, cc_ref, d_ref, gluw_ref, glub_ref, wout_ref,
                       h_ref, cout_ref, sout_ref, bu_ref, ccarry_ref, scarry_ref, *, carried, tiles_per_seq):
    if carried:
        @pl.when(pl.program_id(0) % tiles_per_seq == 0)
        def _load_initial_state():
            ccarry_ref[...] = cinit_ref[...]
            scarry_ref[...] = sinit_ref[...]

    streams = range(x_ref.shape[0])
    mid = [_mixer_project(b, x_ref, gm_ref, perm_ref, win_ref, cw_ref, cb_ref, cinit_ref, bb_ref, cout_ref, bu_ref,
                          ccarry_ref, carried=carried) for b in streams]
    for b in streams:
        _mixer_recur(b, sinit_ref, ab_ref, hop_ref, pw_ref, sout_ref, bu_ref, scarry_ref, carried=carried)
        _mixer_output(b, *mid[b], unperm_ref, cc_ref, d_ref, gluw_ref, glub_ref, wout_ref, h_ref, bu_ref)


def _mixer_project(b, x_ref, gm_ref, perm_ref, win_ref, cw_ref, cb_ref, cinit_ref, bb_ref, cout_ref, bu_ref,
                   ccarry_ref, *, carried):
    tm = TM_MIX
    x = x_ref[b]
    hn = _rms(x, gm_ref[...]).astype(BF16)
    hn = _dot(perm_ref[...], hn).astype(BF16)
    proj = _dot(hn, win_ref[...])
    g_b = proj[:, :CONV_CH]
    cin = proj[:, CONV_CH:2 * CONV_CH] * proj[:, 2 * CONV_CH:3 * CONV_CH]
    u = proj[:, 3 * CONV_CH:]
    ub = u.astype(BF16)
    for half in range(2):
        bu_ref[b, :, half * S5_STATE:(half + 1) * S5_STATE] = _dot(ub[:, half * HALF_CH:(half + 1) * HALF_CH],
                                                                  bb_ref[half])

    last, before_last = cin[tm - 8:tm], cin[tm - 16:tm - 8]
    if carried:
        first_chunk = lax.broadcasted_iota(jnp.int32, (8, CONV_CH), 0) == 0
        newer = jnp.where(first_chunk, ccarry_ref[1, b:b + 1, :], pltpu.roll(last, 1, 0))
        older = jnp.where(first_chunk, ccarry_ref[0, b:b + 1, :], pltpu.roll(before_last, 1, 0))
        for i, rows in enumerate((before_last, last)):
            ccarry_ref[i, b:b + 1, :] = rows[7:8]
            cout_ref[i, b:b + 1, :] = rows[7:8]
    else:
        older, newer = cinit_ref[0], cinit_ref[1]
        cout_ref[0] = before_last
        cout_ref[1] = last
    ext = jnp.concatenate([older, newer, cin], axis=0)
    cw = cw_ref[...]
    out_a = g_b * (cw[0:1] * ext[:tm] + cw[1:2] * ext[8:tm + 8] + cw[2:3] * cin + cb_ref[...])
    return x, u, out_a


def _mixer_recur(b, sinit_ref, ab_ref, hop_ref, pw_ref, sout_ref, bu_ref, scarry_ref, *, carried):
    for c in range(S5_STATE // SCAN_LANES):
        half, within = divmod(c * SCAN_LANES, HALF_STATE)
        nat = slice(c * SCAN_LANES, (c + 1) * SCAN_LANES)
        re_cols = slice(half * S5_STATE + within, half * S5_STATE + within + SCAN_LANES)
        im_cols = slice(re_cols.start + HALF_STATE, re_cols.stop + HALF_STATE)
        ar, ai = ab_ref[0, :, nat], ab_ref[1, :, nat]
        if carried:
            sr = si = jnp.zeros((8, SCAN_LANES), F32)
        else:
            sr, si = sinit_ref[:, re_cols], sinit_ref[:, im_cols]
        for g in range(CHUNK_ROWS):
            rows = slice(8 * g, 8 * g + 8)
            sr, si = (ar * sr - ai * si + bu_ref[b, rows, re_cols], ar * si + ai * sr + bu_ref[b, rows, im_cols])
            bu_ref[b, rows, re_cols] = sr
            bu_ref[b, rows, im_cols] = si
        if not carried:
            sout_ref[:, re_cols] = sr
            sout_ref[:, im_cols] = si
            continue
        first_chunk = lax.broadcasted_iota(jnp.int32, (8, SCAN_LANES), 0) == 0
        er = jnp.where(first_chunk, scarry_ref[b:b + 1, re_cols], pltpu.roll(sr, 1, 0))
        ei = jnp.where(first_chunk, scarry_ref[b:b + 1, im_cols], pltpu.roll(si, 1, 0))
        for k, sh in enumerate((1, 2, 4)):
            wr, wi = hop_ref[2 * k, :, nat], hop_ref[2 * k + 1, :, nat]
            tr, ti = pltpu.roll(er, sh, 0), pltpu.roll(ei, sh, 0)
            er, ei = er + wr * tr - wi * ti, ei + wr * ti + wi * tr
        wr, wi = hop_ref[6, :, nat], hop_ref[7, :, nat]
        fr, fi = wr * er - wi * ei + sr, wr * ei + wi * er + si
        for ref in (scarry_ref, sout_ref):
            ref[b:b + 1, re_cols] = fr[7:8]
            ref[b:b + 1, im_cols] = fi[7:8]
        for g in range(CHUNK_ROWS):
            rows = slice(8 * g, 8 * g + 8)
            pr, pi = pw_ref[0, g, :, nat], pw_ref[1, g, :, nat]
            bu_ref[b, rows, re_cols] += pr * er - pi * ei
            bu_ref[b, rows, im_cols] += pr * ei + pi * er


def _mixer_output(b, x, u, out_a, unperm_ref, cc_ref, d_ref, gluw_ref, glub_ref, wout_ref, h_ref, bu_ref):
    y = jnp.concatenate([_dot(bu_ref[b, :, half * S5_STATE:(half + 1) * S5_STATE].astype(BF16), cc_ref[half])
                         for half in range(2)], axis=-1) + d_ref[...] * u
    z = jax.nn.gelu(y)
    out_b = z * jax.nn.sigmoid(_dot(z.astype(BF16), gluw_ref[...]) + glub_ref[...])
    mix = jnp.concatenate([out_a, out_b], axis=-1).astype(BF16)
    mix = _dot(unperm_ref[...], mix).astype(BF16)
    h_ref[b] = x + _dot(mix, wout_ref[...])


def _even_mixer(x, gm, w_in, cw, cb, cinit, sinit, ab, hop, pw, bb, cc, d, glu_w, glu_b, w_out, *, carried):
    nb, rows = x.shape[0], x.shape[1]
    tm = TM_MIX
    chunks = tm // CHUNK_ROWS
    assert chunks == 8 and rows % tm == 0 and (carried or nb == 1)
    per_step = nb if carried else chunks
    tiles_per_seq = rows // tm if carried else 1
    nseq = sinit.shape[0]
    t_of = np.arange(tm)
    perm = np.zeros((tm, tm), np.float32)
    perm[8 * (t_of % CHUNK_ROWS) + t_of // CHUNK_ROWS, t_of] = 1.0
    kern = functools.partial(_even_mixer_kernel, carried=carried, tiles_per_seq=tiles_per_seq)
    return pl.pallas_call(
        kern,
        grid=(rows // tm,),
        in_specs=[
            pl.BlockSpec((nb, tm, D_MODEL), lambda t: (0, t, 0)),
            _const_spec((1, D_MODEL)),
            _const_spec((tm, tm)),
            _const_spec((tm, tm)),
            _const_spec((D_MODEL, 4 * CONV_CH)),
            _const_spec((3, CONV_CH)),
            _const_spec((1, CONV_CH)),
            pl.BlockSpec((2, per_step, CONV_CH), lambda t: (0, t // tiles_per_seq, 0)),
            pl.BlockSpec((per_step, 2 * S5_STATE), lambda t: (t // tiles_per_seq, 0)),
            _const_spec((2, 8, S5_STATE)),
            _const_spec((8, 8, S5_STATE)),
            _const_spec((2, CHUNK_ROWS, 8, S5_STATE)),
            _const_spec((2, HALF_CH, S5_STATE)),
            _const_spec((2, S5_STATE, HALF_CH)),
            _const_spec((1, S5_CH)),
            _const_spec((S5_CH, S5_CH)),
            _const_spec((1, S5_CH)),
            _const_spec((D_MODEL, D_MODEL)),
        ],
        out_specs=[
            pl.BlockSpec((nb, tm, D_MODEL), lambda t: (0, t, 0)),
            pl.BlockSpec((2, per_step, CONV_CH), lambda t: (0, t // tiles_per_seq, 0)),
            pl.BlockSpec((per_step, 2 * S5_STATE), lambda t: (t // tiles_per_seq, 0)),
        ],
        out_shape=[
            jax.ShapeDtypeStruct((nb, rows, D_MODEL), F32),
            jax.ShapeDtypeStruct((2, nseq, CONV_CH), F32),
            jax.ShapeDtypeStruct((nseq, 2 * S5_STATE), F32),
        ],
        scratch_shapes=[
            pltpu.VMEM((nb, tm, 2 * S5_STATE), F32),
            pltpu.VMEM((2, nb, CONV_CH), F32),
            pltpu.VMEM((nb, 2 * S5_STATE), F32),
        ],
        compiler_params=_params("arbitrary"),
        name="even_mixer",
    )(x, gm, jnp.asarray(perm, BF16), jnp.asarray(perm.T, BF16), w_in, cw, cb, cinit, sinit, ab, hop, pw, bb, cc, d,
      glu_w, glu_b, w_out)


def _ffn_kernel(x_ref, g_ref, w1_ref, w3_ref, w2_ref, o_ref):
    x = x_ref[...]
    hn = _rms(x, g_ref[...]).astype(BF16)
    a = _dot(hn, w1_ref[...])
    b = _dot(hn, w3_ref[...])
    o_ref[...] = x + _dot((a * jax.nn.sigmoid(a) * b).astype(BF16), w2_ref[...])


def _ffn(x, g, w1, w3, w2, *, tm):
    rows = x.shape[0]
    return pl.pallas_call(
        _ffn_kernel,
        grid=(rows // tm,),
        in_specs=[
            pl.BlockSpec((tm, D_MODEL), lambda t: (t, 0)),
            _const_spec((1, D_MODEL)),
            _const_spec((D_MODEL, D_FF)),
            _const_spec((D_MODEL, D_FF)),
            _const_spec((D_FF, D_MODEL)),
        ],
        out_specs=pl.BlockSpec((tm, D_MODEL), lambda t: (t, 0)),
        out_shape=jax.ShapeDtypeStruct((rows, D_MODEL), F32),
        compiler_params=_params("parallel"),
        name="ffn",
    )(x, g, w1, w3, w2)


def _qkv_kernel(x_ref, g_ref, w_ref, b_ref, q_ref, k_ref, v_ref):
    hn = _rms(x_ref[...], g_ref[...]).astype(BF16)
    qkv = _dot(hn, w_ref[...]) + b_ref[...]
    nq = N_HEADS * HEAD_DIM
    q_ref[...] = (qkv[:, :nq] * (HEAD_DIM ** -0.5)).astype(BF16)
    k_ref[...] = qkv[:, nq:nq + KV_DIM]
    v_ref[...] = qkv[:, nq + KV_DIM:]


def _qkv(x, g, w, b, *, tm):
    rows = x.shape[0]
    ncol = (N_HEADS + 2 * N_KV) * HEAD_DIM
    return pl.pallas_call(
        _qkv_kernel,
        grid=(rows // tm,),
        in_specs=[
            pl.BlockSpec((tm, D_MODEL), lambda t: (t, 0)),
            _const_spec((1, D_MODEL)),
            _const_spec((D_MODEL, ncol)),
            _const_spec((1, ncol)),
        ],
        out_specs=[
            pl.BlockSpec((tm, N_HEADS * HEAD_DIM), lambda t: (t, 0)),
            pl.BlockSpec((tm, KV_DIM), lambda t: (t, 0)),
            pl.BlockSpec((tm, KV_DIM), lambda t: (t, 0)),
        ],
        out_shape=[
            jax.ShapeDtypeStruct((rows, N_HEADS * HEAD_DIM), BF16),
            jax.ShapeDtypeStruct((rows, KV_DIM), F32),
            jax.ShapeDtypeStruct((rows, KV_DIM), F32),
        ],
        compiler_params=_params("parallel"),
        name="qkv",
    )(x, g, w, b)


def _attend_groups(q, keys, vals, bias_of_group, sink_of_group, o_scr, row0):
    nq = q.shape[0]
    for kv in range(N_KV):
        kv_cols = slice(kv * HEAD_DIM, (kv + 1) * HEAD_DIM)
        heads = [slice((kv * GQ + g) * HEAD_DIM, (kv * GQ + g + 1) * HEAD_DIM) for g in range(GQ)]
        qs = jnp.concatenate([q[:, cols] for cols in heads], axis=0)
        st = lax.dot_general(keys[:, kv_cols], qs, (((1,), (1,)), ((), ())),
                             preferred_element_type=F32) + bias_of_group(kv)
        sk = sink_of_group(kv)
        m = jnp.maximum(jnp.max(st, axis=0, keepdims=True), sk)
        p = jnp.exp(st - m)
        den = jnp.sum(p, axis=0, keepdims=True) + jnp.exp(sk - m)
        o = lax.dot_general((p * (1.0 / den)).astype(BF16), vals[:, kv_cols], (((0,), (0,)), ((), ())),
                            preferred_element_type=F32)
        for g, cols in enumerate(heads):
            o_scr[row0:row0 + nq, cols] = o[g * nq:(g + 1) * nq]


def _attn_prompt_kernel(q_ref, k_ref, v_ref, bias_ref, sink_ref, x_ref, wo_ref, bo_ref, o_ref, o_scr):
    t = pl.program_id(1)
    for u in range(TQ_PER_STEP):
        tile = t * TQ_PER_STEP + u
        start = pl.multiple_of(tile * TQ, TQ)
        keys = k_ref[0, pl.ds(start, TQ + WINDOW), :]
        vals = v_ref[0, pl.ds(start, TQ + WINDOW), :]
        variant = jnp.minimum(tile, 1)
        _attend_groups(q_ref[u * TQ:(u + 1) * TQ], keys, vals, lambda kv: bias_ref[variant, kv],
                       lambda kv: sink_ref[kv], o_scr, u * TQ)
    o_ref[...] = x_ref[...] + _dot(o_scr[...].astype(BF16), wo_ref[...]) + bo_ref[...]


def _attn_prompt(q, kpad, vpad, bias, sink, x, w_o, b_o):
    nb, lpad = kpad.shape[0], kpad.shape[1]
    tm = TQ_PER_STEP * TQ
    steps = (lpad - WINDOW) // tm
    return pl.pallas_call(
        _attn_prompt_kernel,
        grid=(nb, steps),
        in_specs=[
            pl.BlockSpec((tm, D_MODEL), lambda b, t: (b * steps + t, 0)),
            pl.BlockSpec((1, lpad, KV_DIM), lambda b, t: (b, 0, 0)),
            pl.BlockSpec((1, lpad, KV_DIM), lambda b, t: (b, 0, 0)),
            _const_spec((2, N_KV, TQ + WINDOW, GQ * TQ)),
            _const_spec((N_KV, 1, GQ * TQ)),
            pl.BlockSpec((tm, D_MODEL), lambda b, t: (b * steps + t, 0)),
            _const_spec((D_MODEL, D_MODEL)),
            _const_spec((1, D_MODEL)),
        ],
        out_specs=pl.BlockSpec((tm, D_MODEL), lambda b, t: (b * steps + t, 0)),
        out_shape=jax.ShapeDtypeStruct(x.shape, F32),
        scratch_shapes=[pltpu.VMEM((tm, D_MODEL), F32)],
        compiler_params=_params("parallel", "arbitrary"),
        name="attn_prompt",
    )(q, kpad, vpad, bias, sink, x, w_o, b_o)


def _attn_small_kernel(q_ref, kc_ref, vc_ref, kn_ref, vn_ref, bias_ref, sink_ref, x_ref, wo_ref, bo_ref,
                       o_ref, o_scr):
    step = pl.program_id(0)
    for j in range(SEG_PER_STEP):
        rows = slice(j * SEG, (j + 1) * SEG)
        keys = jnp.concatenate([kc_ref[j], kn_ref[rows]], axis=0).astype(BF16)
        vals = jnp.concatenate([vc_ref[j], vn_ref[rows]], axis=0).astype(BF16)
        variant = jnp.where(step * SEG_PER_STEP + j == META_SEG, 1, 0)
        _attend_groups(q_ref[rows], keys, vals, lambda kv: bias_ref[variant, kv], lambda kv: sink_ref[kv], o_scr,
                       j * SEG)
    o_ref[...] = x_ref[...] + _dot(o_scr[...].astype(BF16), wo_ref[...]) + bo_ref[...]


def _attn_small(q, kc, vc, kn, vn, bias, sink, x, w_o, b_o):
    tm = SEG_PER_STEP * SEG
    steps = x.shape[0] // tm
    cache_blocks = kc.shape[0] // SEG_PER_STEP
    cache_spec = pl.BlockSpec((SEG_PER_STEP, WINDOW, KV_DIM), lambda s: (jnp.minimum(s, cache_blocks - 1), 0, 0))
    return pl.pallas_call(
        _attn_small_kernel,
        grid=(steps,),
        in_specs=[
            pl.BlockSpec((tm, D_MODEL), lambda s: (s, 0)),
            cache_spec,
            cache_spec,
            pl.BlockSpec((tm, KV_DIM), lambda s: (s, 0)),
            pl.BlockSpec((tm, KV_DIM), lambda s: (s, 0)),
            _const_spec((2, N_KV, WINDOW + SEG, GQ * SEG)),
            _const_spec((N_KV, 1, GQ * SEG)),
            pl.BlockSpec((tm, D_MODEL), lambda s: (s, 0)),
            _const_spec((D_MODEL, D_MODEL)),
            _const_spec((1, D_MODEL)),
        ],
        out_specs=pl.BlockSpec((tm, D_MODEL), lambda s: (s, 0)),
        out_shape=jax.ShapeDtypeStruct(x.shape, F32),
        scratch_shapes=[pltpu.VMEM((tm, D_MODEL), F32)],
        compiler_params=_params("parallel"),
        name="attn_small",
    )(q, kc, vc, kn, vn, bias, sink, x, w_o, b_o)


def _moe_kernel(x_ref, g_ref, rwt_ref, rb_ref, w1_ref, w3_ref, w2_ref, gf_ref, o_ref,
                hn_scr, slot_scr, gate_scr, cnt_smem, acc_scr, yall_scr, *, rt):
    e = pl.program_id(1)
    tb = x_ref.shape[0]

    @pl.when(e == 0)
    def _route():
        hn = _rms(x_ref[...], g_ref[...]).astype(BF16)
        hn_scr[...] = hn
        logits = lax.dot_general(rwt_ref[...], hn, (((1,), (1,)), ((), ())),
                                 preferred_element_type=F32) + rb_ref[...]
        row = lax.broadcasted_iota(jnp.int32, logits.shape, 0).astype(F32)
        v1 = jnp.max(logits, axis=0, keepdims=True)
        i1 = jnp.min(jnp.where(logits == v1, row, float(N_EXPERTS)), axis=0, keepdims=True)
        rest = jnp.where(row == i1, -jnp.inf, logits)
        v2 = jnp.max(rest, axis=0, keepdims=True)
        i2 = jnp.min(jnp.where(rest == v2, row, float(N_EXPERTS)), axis=0, keepdims=True)
        e2 = jnp.exp(v2 - v1)
        den = 1.0 + e2
        gate_scr[...] = jnp.where(row == i1, 1.0 / den, 0.0) + jnp.where(row == i2, e2 / den, 0.0)
        chosen = (row == i1) | (row == i2)
        ind = jnp.where(chosen, 1.0, 0.0)
        earlier = (lax.broadcasted_iota(jnp.int32, (tb, tb), 0) < lax.broadcasted_iota(jnp.int32, (tb, tb), 1))
        rank = _dot(ind.astype(BF16), jnp.where(earlier, 1.0, 0.0).astype(BF16))
        slot_scr[...] = jnp.where(chosen, rank, -1.0).astype(jnp.int32)
        for k in range(N_EXPERTS):
            cnt_smem[k] = jnp.sum(ind[k:k + 1, :]).astype(jnp.int32)
        acc_scr[...] = jnp.zeros_like(acc_scr)

    slot = slot_scr[pl.ds(e, 1), :]
    gate = gate_scr[pl.ds(e, 1), :]

    def _scatter(onehot, rows):
        return lax.dot_general(onehot, rows, (((0,), (0,)), ((), ())), preferred_element_type=F32)

    def _expert_tile(i):
        packed_row = lax.broadcasted_iota(jnp.int32, (rt, tb), 0) + i * rt
        hit = packed_row == slot
        onehot = jnp.where(hit, 1.0, 0.0).astype(BF16)
        xe = _dot(onehot, hn_scr[...]).astype(BF16)
        a = _dot(xe, w1_ref[0])
        b = _dot(xe, w3_ref[0])
        y = _dot((a * jax.nn.sigmoid(a) * b).astype(BF16), w2_ref[0])
        g_row = jnp.sum(jnp.where(hit, gate, 0.0), axis=1, keepdims=True)
        return onehot, (y * g_row).astype(BF16)

    yall_scr[pl.ds(pl.multiple_of(e * rt, 16), rt), :] = _expert_tile(0)[1]

    def _overflow_tile(i, carry):
        acc_scr[...] += _scatter(*_expert_tile(i))
        return carry

    lax.fori_loop(1, (cnt_smem[e] + rt - 1) // rt, _overflow_tile, 0)

    @pl.when(e == N_EXPERTS - 1)
    def _finish():
        packed_row = lax.broadcasted_iota(jnp.int32, (rt, tb), 0)
        onehot = jnp.concatenate([jnp.where(packed_row == slot_scr[k:k + 1, :], 1.0, 0.0).astype(BF16)
                                  for k in range(N_EXPERTS)], axis=0)
        o_ref[...] = _rms(x_ref[...] + acc_scr[...] + _scatter(onehot, yall_scr[...]), gf_ref[...])


def _moe_final(x, g, rwt, rb, w1, w3, w2, gf, *, tb, rt):
    rows = x.shape[0]
    return pl.pallas_call(
        functools.partial(_moe_kernel, rt=rt),
        grid=(rows // tb, N_EXPERTS),
        in_specs=[
            pl.BlockSpec((tb, D_MODEL), lambda t, e: (t, 0)),
            _const_spec((1, D_MODEL)),
            _const_spec((N_EXPERTS, D_MODEL)),
            _const_spec((N_EXPERTS, 1)),
            pl.BlockSpec((1, D_MODEL, D_FF_E), lambda t, e: (e, 0, 0)),
            pl.BlockSpec((1, D_MODEL, D_FF_E), lambda t, e: (e, 0, 0)),
            pl.BlockSpec((1, D_FF_E, D_MODEL), lambda t, e: (e, 0, 0)),
            _const_spec((1, D_MODEL)),
        ],
        out_specs=pl.BlockSpec((tb, D_MODEL), lambda t, e: (t, 0)),
        out_shape=jax.ShapeDtypeStruct((rows, D_MODEL), F32),
        scratch_shapes=[
            pltpu.VMEM((tb, D_MODEL), BF16),
            pltpu.VMEM((N_EXPERTS, tb), jnp.int32),
            pltpu.VMEM((N_EXPERTS, tb), F32),
            pltpu.SMEM((N_EXPERTS,), jnp.int32),
            pltpu.VMEM((tb, D_MODEL), F32),
            pltpu.VMEM((N_EXPERTS * rt, D_MODEL), BF16),
        ],
        compiler_params=_params("parallel", "arbitrary"),
        name="moe_final",
    )(x, g, rwt, rb, w1, w3, w2, gf)


def _t5_bucket_np(rel):
    half = N_BUCKETS // 2
    max_exact = half // 2
    ret = np.where(rel > 0, half, 0)
    n = np.abs(rel)
    nf = np.maximum(n, 1).astype(np.float32)
    large = max_exact + (np.log(nf / np.float32(max_exact)) / np.float32(math.log(MAX_DIST / max_exact))
                         * np.float32(half - max_exact)).astype(np.int32)
    large = np.minimum(large, half - 1)
    return ret + np.where(n < max_exact, n, large)


def _bias_variants(table, bucket, keeps):
    nq, nk = bucket.shape
    onehot = jnp.asarray(np.eye(N_BUCKETS, dtype=np.float32)[bucket])
    bias = jnp.einsum('rjb,bh->hjr', onehot, table, precision=lax.Precision.HIGHEST)
    bias = jnp.stack([jnp.where(keep.T[None], bias, NEG) for keep in keeps], axis=0)
    bias = bias.reshape(len(keeps), N_KV, GQ, nk, nq)
    return jnp.transpose(bias, (0, 1, 3, 2, 4)).reshape(len(keeps), N_KV, nk, GQ * nq)


def _lane_sink(sink, nq):
    return jnp.broadcast_to(sink.reshape(N_KV, 1, GQ, 1), (N_KV, 1, GQ, nq)).reshape(N_KV, 1, GQ * nq)


def _prompt_bias(table):
    r = np.arange(TQ)[:, None]
    j = np.arange(TQ + WINDOW)[None, :]
    band = (j - CHUNK * (r // CHUNK) >= 0) & (j - CHUNK * (r // CHUNK) < WINDOW + CHUNK)
    first = band & (j >= WINDOW - N_META)
    return _bias_variants(table, _t5_bucket_np(j - WINDOW - r), [first, band])


def _small_bias(table):
    i = np.arange(SEG)[:, None]
    m = np.arange(WINDOW + SEG)[None, :]
    everything = np.ones((SEG, WINDOW + SEG), bool)
    meta_only = everything & (m >= WINDOW + SEG - N_META)
    return _bias_variants(table, _t5_bucket_np(m - WINDOW - i), [everything, meta_only])


def _pack_state(re, im):
    return jnp.concatenate([re[..., :HALF_STATE], im[..., :HALF_STATE], re[..., HALF_STATE:], im[..., HALF_STATE:]],
                           axis=-1)


def _cmul(ar, ai, br, bi):
    return ar * br - ai * bi, ar * bi + ai * br


def _s5_prepare(a_re, a_im, log_dt, b_re, b_im, c_re, c_im):
    dt = jnp.exp(log_dt)[:, None]
    mag = jnp.exp(a_re * dt)
    ab_re, ab_im = mag * jnp.cos(a_im * dt), mag * jnp.sin(a_im * dt)
    den = a_re * a_re + a_im * a_im
    num_re = ab_re - 1.0
    cf_re = (num_re * a_re + ab_im * a_im) / den
    cf_im = (ab_im * a_re - num_re * a_im) / den
    bb_re = cf_re[..., None] * b_re - cf_im[..., None] * b_im
    bb_im = cf_re[..., None] * b_im + cf_im[..., None] * b_re

    def group_diag(v):
        r, c = v.shape[1], v.shape[2]
        gh = S5_GROUPS // 2
        spread = np.tile(np.eye(c, dtype=np.float32), (1, gh))
        wide = jnp.einsum('hrc,cn->hrn', v.reshape(2, gh * r, c), spread, precision=lax.Precision.HIGHEST)
        keep = (np.arange(gh * r)[:, None] // r) == (np.arange(gh * c)[None, :] // c)
        return jnp.where(keep, wide, 0.0)

    def t(v):
        return jnp.swapaxes(v, 1, 2)

    bb = jnp.concatenate([group_diag(t(bb_re)), group_diag(t(bb_im))], axis=2).astype(BF16)
    cc = jnp.concatenate([group_diag(t(c_re)), -group_diag(t(c_im))], axis=1).astype(BF16)

    pr, pi = ab_re.reshape(1, S5_STATE), ab_im.reshape(1, S5_STATE)
    n = 1
    while n < CHUNK_ROWS:
        qr, qi = _cmul(pr, pi, pr[n - 1:n], pi[n - 1:n])
        pr, pi = jnp.concatenate([pr, qr], axis=0), jnp.concatenate([pi, qi], axis=0)
        n *= 2

    def tile8(v):
        return jnp.broadcast_to(v, (8,) + v.shape[1:])

    ab = jnp.stack([tile8(pr[0:1]), tile8(pi[0:1])])
    sub = np.arange(8)[:, None]
    hop, w = [], (pr[CHUNK_ROWS - 1:], pi[CHUNK_ROWS - 1:])
    for k in (1, 2, 4):
        hop += [jnp.where(sub >= k, w[0], 0.0), jnp.where(sub >= k, w[1], 0.0)]
        if k < 4:
            w = _cmul(*w, *w)
    hop = jnp.stack(hop + [tile8(pr[CHUNK_ROWS - 1:]), tile8(pi[CHUNK_ROWS - 1:])])
    pw = jnp.stack([jnp.broadcast_to(p[:, None, :], (CHUNK_ROWS, 8, S5_STATE)) for p in (pr, pi)])
    return bb, cc, ab, hop, pw


def kernel(x_prompt, x_sample, cache_conv, state_s5_re, state_s5_im, cache_swa_k, cache_swa_v, meta_tokens, rel_bias_table, norm_mix, norm_ffn, norm_final, w_in0, conv_w, conv_b, s5_a_re, s5_a_im, s5_log_dt, s5_b_re, s5_b_im, s5_c_re, s5_c_im, s5_d, s5_glu_w, s5_glu_b, w_out0, ffn_w1, ffn_w3, ffn_w2, w_qkv, b_qkv, attn_sink, w_o, b_o, router_w, router_b, moe_w1, moe_w3, moe_w2):
    nb, seq = x_prompt.shape[0], x_prompt.shape[1]
    nsb = x_sample.shape[0]
    assert x_sample.shape[1] == SEG and nsb == N_SAMPLE_SEG and seq % TM_MIX == 0

    def row(v):
        return v.reshape(1, -1)

    meta_seg = jnp.concatenate([jnp.zeros((SEG - N_META, D_MODEL), F32), meta_tokens], axis=0)
    filler = jnp.zeros(((N_SMALL_SEG - N_SAMPLE_SEG - 1) * SEG, D_MODEL), F32)
    xs = jnp.concatenate([x_sample.reshape(nsb * SEG, D_MODEL), meta_seg, filler], axis=0)
    n_extra = N_SMALL_SEG - N_SAMPLE_SEG

    bb, cc, ab, hop, pw = _s5_prepare(s5_a_re[0], s5_a_im[0], s5_log_dt[0], s5_b_re[0], s5_b_im[0], s5_c_re[0],
                                      s5_c_im[0])
    mixer_w = (row(norm_mix[0]), w_in0[0].astype(BF16), conv_w[0], row(conv_b[0]))
    mixer_w2 = (ab, hop, pw, bb, cc, row(s5_d[0]), s5_glu_w[0].astype(BF16), row(s5_glu_b[0]), w_out0[0].astype(BF16))
    cinit_s = jnp.concatenate([jnp.swapaxes(cache_conv[0], 0, 1), jnp.zeros((2, n_extra, CONV_CH), F32)], axis=1)
    sinit_s = _pack_state(state_s5_re[0].reshape(nsb, S5_STATE), state_s5_im[0].reshape(nsb, S5_STATE))
    sinit_s = jnp.concatenate([sinit_s, jnp.zeros((n_extra, 2 * S5_STATE), F32)], axis=0)
    hs, conv_s, state_s = _even_mixer(xs[None], *mixer_w, cinit_s, sinit_s, *mixer_w2, carried=False)
    hs = hs[0]
    cinit_p = jnp.broadcast_to(conv_s[:, META_SEG:META_SEG + 1], (2, nb, CONV_CH))
    sinit_p = jnp.broadcast_to(state_s[META_SEG:META_SEG + 1], (nb, 2 * S5_STATE))
    hp, conv_p, state_p = _even_mixer(x_prompt, *mixer_w, cinit_p, sinit_p, *mixer_w2, carried=True)
    hp = hp.reshape(nb * seq, D_MODEL)
    ffn_w = (row(norm_ffn[0]), ffn_w1[0].astype(BF16), ffn_w3[0].astype(BF16), ffn_w2[0].astype(BF16))
    hs = _ffn(hs, *ffn_w, tm=256)
    hp = _ffn(hp, *ffn_w, tm=512)

    qkv_w = (row(norm_mix[1]), w_qkv[0].astype(BF16), row(b_qkv[0]))
    qs, ks, vs = _qkv(hs, *qkv_w, tm=256)
    qp, kp, vp = _qkv(hp, *qkv_w, tm=512)
    wo = (w_o[0].astype(BF16), row(b_o[0]))
    hs = _attn_small(qs, cache_swa_k[0].reshape(nsb, WINDOW, KV_DIM), cache_swa_v[0].reshape(nsb, WINDOW, KV_DIM),
                     ks, vs, _small_bias(rel_bias_table), _lane_sink(attn_sink[0], SEG), hs, *wo)

    def padded(new, small):
        meta_rows = small[META_SEG * SEG:(META_SEG + 1) * SEG]
        front = jnp.concatenate([jnp.zeros((WINDOW - SEG, KV_DIM), F32), meta_rows], axis=0)
        front = jnp.broadcast_to(front[None], (nb, WINDOW, KV_DIM))
        return jnp.concatenate([front, new.reshape(nb, seq, KV_DIM)], axis=1).astype(BF16)

    hp = _attn_prompt(qp, padded(kp, ks), padded(vp, vs), _prompt_bias(rel_bias_table),
                      _lane_sink(attn_sink[0], TQ), hp, *wo)
    moe_w = (row(norm_ffn[1]), router_w[0].T.astype(BF16), router_b[0].reshape(N_EXPERTS, 1), moe_w1[0].astype(BF16),
             moe_w3[0].astype(BF16), moe_w2[0].astype(BF16), row(norm_final))
    ys = _moe_final(hs, *moe_w, tb=hs.shape[0], rt=352)
    yp = _moe_final(hp, *moe_w, tb=1024, rt=352)

    n_real = nsb * SEG
    y_prompt = yp.reshape(nb, seq, D_MODEL)
    y_sample = ys[:n_real].reshape(nsb, SEG, D_MODEL)

    def split_state(st, n):
        st = st[:n].reshape(n, 2, 2, HALF_STATE)
        return tuple(st[:, :, ri].reshape(1, n, S5_GROUPS, S5_P) for ri in range(2))

    s5rp, s5ip = split_state(state_p, nb)
    s5rs, s5is = split_state(state_s, nsb)
    kp4 = kp.reshape(nb, seq, KV_DIM)[:, -WINDOW:].reshape(1, nb, WINDOW, N_KV, HEAD_DIM)
    vp4 = vp.reshape(nb, seq, KV_DIM)[:, -WINDOW:].reshape(1, nb, WINDOW, N_KV, HEAD_DIM)
    ks4 = ks[:n_real].reshape(1, nsb, SEG, N_KV, HEAD_DIM)
    vs4 = vs[:n_real].reshape(1, nsb, SEG, N_KV, HEAD_DIM)
    conv_p4 = jnp.swapaxes(conv_p, 0, 1)[None]
    conv_s4 = jnp.swapaxes(conv_s[:, :nsb], 0, 1)[None]
    return (y_prompt, y_sample, conv_p4, conv_s4, s5rp, s5ip, s5rs, s5is, kp4, vp4, ks4, vs4)
```

```python
import functools
import math

import numpy as np
import jax
import jax.numpy as jnp
from jax import lax
from jax.experimental import pallas as pl
from jax.experimental.pallas import tpu as pltpu

F32 = jnp.float32
BF16 = jnp.bfloat16

D_MODEL = 1024
CONV_CH = 512
S5_CH = 512
S5_GROUP = 16
S5_GROUPS = 32
S5_P = 64
S5_STATE = S5_GROUPS * S5_P
HALF_CH = S5_CH // 2
HALF_STATE = S5_STATE // 2
D_FF = 2816
N_HEADS = 16
N_KV = 2
GQ = N_HEADS // N_KV
HEAD_DIM = 64
KV_DIM = N_KV * HEAD_DIM
WINDOW = 128
CHUNK = 64
N_META = 16
N_BUCKETS = 32
MAX_DIST = 128
N_EXPERTS = 8
D_FF_E = 1024
EPS = 1e-6
NEG = -1e30

SEG = 32
N_SAMPLE_SEG = 32
META_SEG = N_SAMPLE_SEG
N_SMALL_SEG = 40
TM_MIX = 256
CHUNK_ROWS = SEG
SCAN_LANES = 512
TQ = 128
TQ_PER_STEP = 2
SEG_PER_STEP = 8
VMEM_LIMIT = 56 * 1024 * 1024


def _const_spec(shape):
    nd = len(shape)
    return pl.BlockSpec(shape, lambda *_: (0,) * nd, pipeline_mode=pl.Buffered(1))


def _params(*sem):
    return pltpu.CompilerParams(dimension_semantics=sem, vmem_limit_bytes=VMEM_LIMIT)


def _rms(x, g):
    return x * lax.rsqrt(jnp.mean(x * x, axis=-1, keepdims=True) + EPS) * g


def _dot(a, b):
    return jnp.dot(a, b, preferred_element_type=F32)


def _even_mixer_kernel(x_ref, gm_ref, perm_ref, unperm_ref, win_ref, cw_ref, cb_ref, cinit_ref, sinit_ref,
                       ab_ref, hop_ref, pw_ref, bb_ref, cc_ref, d_ref, gluw_ref, glub_ref, wout_ref,
                       h_ref, cout_ref, sout_ref, bu_ref, ccarry_ref, scarry_ref, *, carried, tiles_per_seq):
    if carried:
        @pl.when(pl.program_id(0) % tiles_per_seq == 0)
        def _load_initial_state():
            ccarry_ref[...] = cinit_ref[...]
            scarry_ref[...] = sinit_ref[...]

    streams = range(x_ref.shape[0])
    mid = [_mixer_project(b, x_ref, gm_ref, perm_ref, win_ref, cw_ref, cb_ref, cinit_ref, bb_ref, cout_ref, bu_ref,
                          ccarry_ref, carried=carried) for b in streams]
    for b in streams:
        _mixer_recur(b, sinit_ref, ab_ref, hop_ref, pw_ref, sout_ref, bu_ref, scarry_ref, carried=carried)
        _mixer_output(b, *mid[b], unperm_ref, cc_ref, d_ref, gluw_ref, glub_ref, wout_ref, h_ref, bu_ref)


def _mixer_project(b, x_ref, gm_ref, perm_ref, win_ref, cw_ref, cb_ref, cinit_ref, bb_ref, cout_ref, bu_ref,
                   ccarry_ref, *, carried):
    tm = TM_MIX
    x = x_ref[b]
    hn = _rms(x, gm_ref[...]).astype(BF16)
    hn = _dot(perm_ref[...], hn).astype(BF16)
    proj = _dot(hn, win_ref[...])
    g_b = proj[:, :CONV_CH]
    cin = proj[:, CONV_CH:2 * CONV_CH] * proj[:, 2 * CONV_CH:3 * CONV_CH]
    u = proj[:, 3 * CONV_CH:]
    ub = u.astype(BF16)
    for half in range(2):
        bu_ref[b, :, half * S5_STATE:(half + 1) * S5_STATE] = _dot(ub[:, half * HALF_CH:(half + 1) * HALF_CH],
                                                                  bb_ref[half])

    last, before_last = cin[tm - 8:tm], cin[tm - 16:tm - 8]
    if carried:
        first_chunk = lax.broadcasted_iota(jnp.int32, (8, CONV_CH), 0) == 0
        newer = jnp.where(first_chunk, ccarry_ref[1, b:b + 1, :], pltpu.roll(last, 1, 0))
        older = jnp.where(first_chunk, ccarry_ref[0, b:b + 1, :], pltpu.roll(before_last, 1, 0))
        for i, rows in enumerate((before_last, last)):
            ccarry_ref[i, b:b + 1, :] = rows[7:8]
            cout_ref[i, b:b + 1, :] = rows[7:8]
    else:
        older, newer = cinit_ref[0], cinit_ref[1]
        cout_ref[0] = before_last
        cout_ref[1] = last
    ext = jnp.concatenate([older, newer, cin], axis=0)
    cw = cw_ref[...]
    out_a = g_b * (cw[0:1] * ext[:tm] + cw[1:2] * ext[8:tm + 8] + cw[2:3] * cin + cb_ref[...])
    return x, u, out_a


def _mixer_recur(b, sinit_ref, ab_ref, hop_ref, pw_ref, sout_ref, bu_ref, scarry_ref, *, carried):
    for c in range(S5_STATE // SCAN_LANES):
        half, within = divmod(c * SCAN_LANES, HALF_STATE)
        nat = slice(c * SCAN_LANES, (c + 1) * SCAN_LANES)
        re_cols = slice(half * S5_STATE + within, half * S5_STATE + within + SCAN_LANES)
        im_cols = slice(re_cols.start + HALF_STATE, re_cols.stop + HALF_STATE)
        ar, ai = ab_ref[0, :, nat], ab_ref[1, :, nat]
        if carried:
            sr = si = jnp.zeros((8, SCAN_LANES), F32)
        else:
            sr, si = sinit_ref[:, re_cols], sinit_ref[:, im_cols]
        for g in range(CHUNK_ROWS):
            rows = slice(8 * g, 8 * g + 8)
            sr, si = (ar * sr - ai * si + bu_ref[b, rows, re_cols], ar * si + ai * sr + bu_ref[b, rows, im_cols])
            bu_ref[b, rows, re_cols] = sr
            bu_ref[b, rows, im_cols] = si
        if not carried:
            sout_ref[:, re_cols] = sr
            sout_ref[:, im_cols] = si
            continue
        first_chunk = lax.broadcasted_iota(jnp.int32, (8, SCAN_LANES), 0) == 0
        er = jnp.where(first_chunk, scarry_ref[b:b + 1, re_cols], pltpu.roll(sr, 1, 0))
        ei = jnp.where(first_chunk, scarry_ref[b:b + 1, im_cols], pltpu.roll(si, 1, 0))
        for k, sh in enumerate((1, 2, 4)):
            wr, wi = hop_ref[2 * k, :, nat], hop_ref[2 * k + 1, :, nat]
            tr, ti = pltpu.roll(er, sh, 0), pltpu.roll(ei, sh, 0)
            er, ei = er + wr * tr - wi * ti, ei + wr * ti + wi * tr
        wr, wi = hop_ref[6, :, nat], hop_ref[7, :, nat]
        fr, fi = wr * er - wi * ei + sr, wr * ei + wi * er + si
        for ref in (scarry_ref, sout_ref):
            ref[b:b + 1, re_cols] = fr[7:8]
            ref[b:b + 1, im_cols] = fi[7:8]
        for g in range(CHUNK_ROWS):
            rows = slice(8 * g, 8 * g + 8)
            pr, pi = pw_ref[0, g, :, nat], pw_ref[1, g, :, nat]
            bu_ref[b, rows, re_cols] += pr * er - pi * ei
            bu_ref[b, rows, im_cols] += pr * ei + pi * er


def _mixer_output(b, x, u, out_a, unperm_ref, cc_ref, d_ref, gluw_ref, glub_ref, wout_ref, h_ref, bu_ref):
    y = jnp.concatenate([_dot(bu_ref[b, :, half * S5_STATE:(half + 1) * S5_STATE].astype(BF16), cc_ref[half])
                         for half in range(2)], axis=-1) + d_ref[...] * u
    z = jax.nn.gelu(y)
    out_b = z * jax.nn.sigmoid(_dot(z.astype(BF16), gluw_ref[...]) + glub_ref[...])
    mix = jnp.concatenate([out_a, out_b], axis=-1).astype(BF16)
    mix = _dot(unperm_ref[...], mix).astype(BF16)
    h_ref[b] = x + _dot(mix, wout_ref[...])


def _even_mixer(x, gm, w_in, cw, cb, cinit, sinit, ab, hop, pw, bb, cc, d, glu_w, glu_b, w_out, *, carried):
    nb, rows = x.shape[0], x.shape[1]
    tm = TM_MIX
    chunks = tm // CHUNK_ROWS
    assert chunks == 8 and rows % tm == 0 and (carried or nb == 1)
    per_step = nb if carried else chunks
    tiles_per_seq = rows // tm if carried else 1
    nseq = sinit.shape[0]
    t_of = np.arange(tm)
    perm = np.zeros((tm, tm), np.float32)
    perm[8 * (t_of % CHUNK_ROWS) + t_of // CHUNK_ROWS, t_of] = 1.0
    kern = functools.partial(_even_mixer_kernel, carried=carried, tiles_per_seq=tiles_per_seq)
    return pl.pallas_call(
        kern,
        grid=(rows // tm,),
        in_specs=[
            pl.BlockSpec((nb, tm, D_MODEL), lambda t: (0, t, 0)),
            _const_spec((1, D_MODEL)),
            _const_spec((tm, tm)),
            _const_spec((tm, tm)),
            _const_spec((D_MODEL, 4 * CONV_CH)),
            _const_spec((3, CONV_CH)),
            _const_spec((1, CONV_CH)),
            pl.BlockSpec((2, per_step, CONV_CH), lambda t: (0, t // tiles_per_seq, 0)),
            pl.BlockSpec((per_step, 2 * S5_STATE), lambda t: (t // tiles_per_seq, 0)),
            _const_spec((2, 8, S5_STATE)),
            _const_spec((8, 8, S5_STATE)),
            _const_spec((2, CHUNK_ROWS, 8, S5_STATE)),
            _const_spec((2, HALF_CH, S5_STATE)),
            _const_spec((2, S5_STATE, HALF_CH)),
            _const_spec((1, S5_CH)),
            _const_spec((S5_CH, S5_CH)),
            _const_spec((1, S5_CH)),
            _const_spec((D_MODEL, D_MODEL)),
        ],
        out_specs=[
            pl.BlockSpec((nb, tm, D_MODEL), lambda t: (0, t, 0)),
            pl.BlockSpec((2, per_step, CONV_CH), lambda t: (0, t // tiles_per_seq, 0)),
            pl.BlockSpec((per_step, 2 * S5_STATE), lambda t: (t // tiles_per_seq, 0)),
        ],
        out_shape=[
            jax.ShapeDtypeStruct((nb, rows, D_MODEL), F32),
            jax.ShapeDtypeStruct((2, nseq, CONV_CH), F32),
            jax.ShapeDtypeStruct((nseq, 2 * S5_STATE), F32),
        ],
        scratch_shapes=[
            pltpu.VMEM((nb, tm, 2 * S5_STATE), F32),
            pltpu.VMEM((2, nb, CONV_CH), F32),
            pltpu.VMEM((nb, 2 * S5_STATE), F32),
        ],
        compiler_params=_params("arbitrary"),
        name="even_mixer",
    )(x, gm, jnp.asarray(perm, BF16), jnp.asarray(perm.T, BF16), w_in, cw, cb, cinit, sinit, ab, hop, pw, bb, cc, d,
      glu_w, glu_b, w_out)


def _ffn_kernel(x_ref, g_ref, w1_ref, w3_ref, w2_ref, o_ref):
    x = x_ref[...]
    hn = _rms(x, g_ref[...]).astype(BF16)
    a = _dot(hn, w1_ref[...])
    b = _dot(hn, w3_ref[...])
    o_ref[...] = x + _dot((a * jax.nn.sigmoid(a) * b).astype(BF16), w2_ref[...])


def _ffn(x, g, w1, w3, w2, *, tm):
    rows = x.shape[0]
    return pl.pallas_call(
        _ffn_kernel,
        grid=(rows // tm,),
        in_specs=[
            pl.BlockSpec((tm, D_MODEL), lambda t: (t, 0)),
            _const_spec((1, D_MODEL)),
            _const_spec((D_MODEL, D_FF)),
            _const_spec((D_MODEL, D_FF)),
            _const_spec((D_FF, D_MODEL)),
        ],
        out_specs=pl.BlockSpec((tm, D_MODEL), lambda t: (t, 0)),
        out_shape=jax.ShapeDtypeStruct((rows, D_MODEL), F32),
        compiler_params=_params("parallel"),
        name="ffn",
    )(x, g, w1, w3, w2)


def _qkv_kernel(x_ref, g_ref, w_ref, b_ref, q_ref, k_ref, v_ref):
    hn = _rms(x_ref[...], g_ref[...]).astype(BF16)
    qkv = _dot(hn, w_ref[...]) + b_ref[...]
    nq = N_HEADS * HEAD_DIM
    q_ref[...] = (qkv[:, :nq] * (HEAD_DIM ** -0.5)).astype(BF16)
    k_ref[...] = qkv[:, nq:nq + KV_DIM]
    v_ref[...] = qkv[:, nq + KV_DIM:]


def _qkv(x, g, w, b, *, tm):
    rows = x.shape[0]
    ncol = (N_HEADS + 2 * N_KV) * HEAD_DIM
    return pl.pallas_call(
        _qkv_kernel,
        grid=(rows // tm,),
        in_specs=[
            pl.BlockSpec((tm, D_MODEL), lambda t: (t, 0)),
            _const_spec((1, D_MODEL)),
            _const_spec((D_MODEL, ncol)),
            _const_spec((1, ncol)),
        ],
        out_specs=[
            pl.BlockSpec((tm, N_HEADS * HEAD_DIM), lambda t: (t, 0)),
            pl.BlockSpec((tm, KV_DIM), lambda t: (t, 0)),
            pl.BlockSpec((tm, KV_DIM), lambda t: (t, 0)),
        ],
        out_shape=[
            jax.ShapeDtypeStruct((rows, N_HEADS * HEAD_DIM), BF16),
            jax.ShapeDtypeStruct((rows, KV_DIM), F32),
            jax.ShapeDtypeStruct((rows, KV_DIM), F32),
        ],
        compiler_params=_params("parallel"),
        name="qkv",
    )(x, g, w, b)


def _attend_groups(q, keys, vals, bias_of_group, sink_of_group, o_scr, row0):
    nq = q.shape[0]
    for kv in range(N_KV):
        kv_cols = slice(kv * HEAD_DIM, (kv + 1) * HEAD_DIM)
        heads = [slice((kv * GQ + g) * HEAD_DIM, (kv * GQ + g + 1) * HEAD_DIM) for g in range(GQ)]
        qs = jnp.concatenate([q[:, cols] for cols in heads], axis=0)
        st = lax.dot_general(keys[:, kv_cols], qs, (((1,), (1,)), ((), ())),
                             preferred_element_type=F32) + bias_of_group(kv)
        sk = sink_of_group(kv)
        m = jnp.maximum(jnp.max(st, axis=0, keepdims=True), sk)
        p = jnp.exp(st - m)
        den = jnp.sum(p, axis=0, keepdims=True) + jnp.exp(sk - m)
        o = lax.dot_general((p * (1.0 / den)).astype(BF16), vals[:, kv_cols], (((0,), (0,)), ((), ())),
                            preferred_element_type=F32)
        for g, cols in enumerate(heads):
            o_scr[row0:row0 + nq, cols] = o[g * nq:(g + 1) * nq]


def _attn_prompt_kernel(q_ref, k_ref, v_ref, bias_ref, sink_ref, x_ref, wo_ref, bo_ref, o_ref, o_scr):
    t = pl.program_id(1)
    for u in range(TQ_PER_STEP):
        tile = t * TQ_PER_STEP + u
        start = pl.multiple_of(tile * TQ, TQ)
        keys = k_ref[0, pl.ds(start, TQ + WINDOW), :]
        vals = v_ref[0, pl.ds(start, TQ + WINDOW), :]
        variant = jnp.minimum(tile, 1)
        _attend_groups(q_ref[u * TQ:(u + 1) * TQ], keys, vals, lambda kv: bias_ref[variant, kv],
                       lambda kv: sink_ref[kv], o_scr, u * TQ)
    o_ref[...] = x_ref[...] + _dot(o_scr[...].astype(BF16), wo_ref[...]) + bo_ref[...]


def _attn_prompt(q, kpad, vpad, bias, sink, x, w_o, b_o):
    nb, lpad = kpad.shape[0], kpad.shape[1]
    tm = TQ_PER_STEP * TQ
    steps = (lpad - WINDOW) // tm
    return pl.pallas_call(
        _attn_prompt_kernel,
        grid=(nb, steps),
        in_specs=[
            pl.BlockSpec((tm, D_MODEL), lambda b, t: (b * steps + t, 0)),
            pl.BlockSpec((1, lpad, KV_DIM), lambda b, t: (b, 0, 0)),
            pl.BlockSpec((1, lpad, KV_DIM), lambda b, t: (b, 0, 0)),
            _const_spec((2, N_KV, TQ + WINDOW, GQ * TQ)),
            _const_spec((N_KV, 1, GQ * TQ)),
            pl.BlockSpec((tm, D_MODEL), lambda b, t: (b * steps + t, 0)),
            _const_spec((D_MODEL, D_MODEL)),
            _const_spec((1, D_MODEL)),
        ],
        out_specs=pl.BlockSpec((tm, D_MODEL), lambda b, t: (b * steps + t, 0)),
        out_shape=jax.ShapeDtypeStruct(x.shape, F32),
        scratch_shapes=[pltpu.VMEM((tm, D_MODEL), F32)],
        compiler_params=_params("parallel", "arbitrary"),
        name="attn_prompt",
    )(q, kpad, vpad, bias, sink, x, w_o, b_o)


def _attn_small_kernel(q_ref, kc_ref, vc_ref, kn_ref, vn_ref, bias_ref, sink_ref, x_ref, wo_ref, bo_ref,
                       o_ref, o_scr):
    step = pl.program_id(0)
    for j in range(SEG_PER_STEP):
        rows = slice(j * SEG, (j + 1) * SEG)
        keys = jnp.concatenate([kc_ref[j], kn_ref[rows]], axis=0).astype(BF16)
        vals = jnp.concatenate([vc_ref[j], vn_ref[rows]], axis=0).astype(BF16)
        variant = jnp.where(step * SEG_PER_STEP + j == META_SEG, 1, 0)
        _attend_groups(q_ref[rows], keys, vals, lambda kv: bias_ref[variant, kv], lambda kv: sink_ref[kv], o_scr,
                       j * SEG)
    o_ref[...] = x_ref[...] + _dot(o_scr[...].astype(BF16), wo_ref[...]) + bo_ref[...]


def _attn_small(q, kc, vc, kn, vn, bias, sink, x, w_o, b_o):
    tm = SEG_PER_STEP * SEG
    steps = x.shape[0] // tm
    cache_blocks = kc.shape[0] // SEG_PER_STEP
    cache_spec = pl.BlockSpec((SEG_PER_STEP, WINDOW, KV_DIM), lambda s: (jnp.minimum(s, cache_blocks - 1), 0, 0))
    return pl.pallas_call(
        _attn_small_kernel,
        grid=(steps,),
        in_specs=[
            pl.BlockSpec((tm, D_MODEL), lambda s: (s, 0)),
            cache_spec,
            cache_spec,
            pl.BlockSpec((tm, KV_DIM), lambda s: (s, 0)),
            pl.BlockSpec((tm, KV_DIM), lambda s: (s, 0)),
            _const_spec((2, N_KV, WINDOW + SEG, GQ * SEG)),
            _const_spec((N_KV, 1, GQ * SEG)),
            pl.BlockSpec((tm, D_MODEL), lambda s: (s, 0)),
            _const_spec((D_MODEL, D_MODEL)),
            _const_spec((1, D_MODEL)),
        ],
        out_specs=pl.BlockSpec((tm, D_MODEL), lambda s: (s, 0)),
        out_shape=jax.ShapeDtypeStruct(x.shape, F32),
        scratch_shapes=[pltpu.VMEM((tm, D_MODEL), F32)],
        compiler_params=_params("parallel"),
        name="attn_small",
    )(q, kc, vc, kn, vn, bias, sink, x, w_o, b_o)


def _moe_kernel(x_ref, g_ref, rwt_ref, rb_ref, w1_ref, w3_ref, w2_ref, gf_ref, o_ref,
                hn_scr, slot_scr, gate_scr, cnt_smem, acc_scr, yall_scr, *, rt, rt_small):
    e = pl.program_id(1)
    tb = x_ref.shape[0]

    @pl.when(e == 0)
    def _route():
        hn = _rms(x_ref[...], g_ref[...]).astype(BF16)
        hn_scr[...] = hn
        logits = lax.dot_general(rwt_ref[...], hn, (((1,), (1,)), ((), ())),
                                 preferred_element_type=F32) + rb_ref[...]
        row = lax.broadcasted_iota(jnp.int32, logits.shape, 0).astype(F32)
        v1 = jnp.max(logits, axis=0, keepdims=True)
        i1 = jnp.min(jnp.where(logits == v1, row, float(N_EXPERTS)), axis=0, keepdims=True)
        rest = jnp.where(row == i1, -jnp.inf, logits)
        v2 = jnp.max(rest, axis=0, keepdims=True)
        i2 = jnp.min(jnp.where(rest == v2, row, float(N_EXPERTS)), axis=0, keepdims=True)
        e2 = jnp.exp(v2 - v1)
        den = 1.0 + e2
        gate_scr[...] = jnp.where(row == i1, 1.0 / den, 0.0) + jnp.where(row == i2, e2 / den, 0.0)
        chosen = (row == i1) | (row == i2)
        ind = jnp.where(chosen, 1.0, 0.0)
        earlier = (lax.broadcasted_iota(jnp.int32, (tb, tb), 0) < lax.broadcasted_iota(jnp.int32, (tb, tb), 1))
        rank = _dot(ind.astype(BF16), jnp.where(earlier, 1.0, 0.0).astype(BF16))
        slot_scr[...] = jnp.where(chosen, rank, -1.0).astype(jnp.int32)
        for k in range(N_EXPERTS):
            cnt_smem[k] = jnp.sum(ind[k:k + 1, :]).astype(jnp.int32)
        acc_scr[...] = jnp.zeros_like(acc_scr)

    slot = slot_scr[pl.ds(e, 1), :]
    gate = gate_scr[pl.ds(e, 1), :]

    def _scatter(onehot, rows):
        return lax.dot_general(onehot, rows, (((0,), (0,)), ((), ())), preferred_element_type=F32)

    def _expert_tile(i, rows=rt):
        packed_row = lax.broadcasted_iota(jnp.int32, (rows, tb), 0) + i * rt
        hit = packed_row == slot
        onehot = jnp.where(hit, 1.0, 0.0).astype(BF16)
        xe = _dot(onehot, hn_scr[...]).astype(BF16)
        a = _dot(xe, w1_ref[0])
        b = _dot(xe, w3_ref[0])
        y = _dot((a * jax.nn.sigmoid(a) * b).astype(BF16), w2_ref[0])
        g_row = jnp.sum(jnp.where(hit, gate, 0.0), axis=1, keepdims=True)
        return onehot, (y * g_row).astype(BF16)

    base = pl.multiple_of(e * rt, 16)
    few = cnt_smem[e] <= rt_small

    @pl.when(few)
    def _short_tile():
        yall_scr[pl.ds(base, rt_small), :] = _expert_tile(0, rt_small)[1]
        yall_scr[pl.ds(base + rt_small, rt - rt_small), :] = jnp.zeros((rt - rt_small, D_MODEL), BF16)

    @pl.when(jnp.logical_not(few))
    def _full_tile():
        yall_scr[pl.ds(base, rt), :] = _expert_tile(0)[1]

    def _overflow_tile(i, carry):
        acc_scr[...] += _scatter(*_expert_tile(i))
        return carry

    lax.fori_loop(1, (cnt_smem[e] + rt - 1) // rt, _overflow_tile, 0)

    @pl.when(e == N_EXPERTS - 1)
    def _finish():
        packed_row = lax.broadcasted_iota(jnp.int32, (rt, tb), 0)
        onehot = jnp.concatenate([jnp.where(packed_row == slot_scr[k:k + 1, :], 1.0, 0.0).astype(BF16)
                                  for k in range(N_EXPERTS)], axis=0)
        o_ref[...] = _rms(x_ref[...] + acc_scr[...] + _scatter(onehot, yall_scr[...]), gf_ref[...])


def _moe_final(x, g, rwt, rb, w1, w3, w2, gf, *, tb, rt, rt_small):
    rows = x.shape[0]
    return pl.pallas_call(
        functools.partial(_moe_kernel, rt=rt, rt_small=rt_small),
        grid=(rows // tb, N_EXPERTS),
        in_specs=[
            pl.BlockSpec((tb, D_MODEL), lambda t, e: (t, 0)),
            _const_spec((1, D_MODEL)),
            _const_spec((N_EXPERTS, D_MODEL)),
            _const_spec((N_EXPERTS, 1)),
            pl.BlockSpec((1, D_MODEL, D_FF_E), lambda t, e: (e, 0, 0)),
            pl.BlockSpec((1, D_MODEL, D_FF_E), lambda t, e: (e, 0, 0)),
            pl.BlockSpec((1, D_FF_E, D_MODEL), lambda t, e: (e, 0, 0)),
            _const_spec((1, D_MODEL)),
        ],
        out_specs=pl.BlockSpec((tb, D_MODEL), lambda t, e: (t, 0)),
        out_shape=jax.ShapeDtypeStruct((rows, D_MODEL), F32),
        scratch_shapes=[
            pltpu.VMEM((tb, D_MODEL), BF16),
            pltpu.VMEM((N_EXPERTS, tb), jnp.int32),
            pltpu.VMEM((N_EXPERTS, tb), F32),
            pltpu.SMEM((N_EXPERTS,), jnp.int32),
            pltpu.VMEM((tb, D_MODEL), F32),
            pltpu.VMEM((N_EXPERTS * rt, D_MODEL), BF16),
        ],
        compiler_params=_params("parallel", "arbitrary"),
        name="moe_final",
    )(x, g, rwt, rb, w1, w3, w2, gf)


def _t5_bucket_np(rel):
    half = N_BUCKETS // 2
    max_exact = half // 2
    ret = np.where(rel > 0, half, 0)
    n = np.abs(rel)
    nf = np.maximum(n, 1).astype(np.float32)
    large = max_exact + (np.log(nf / np.float32(max_exact)) / np.float32(math.log(MAX_DIST / max_exact))
                         * np.float32(half - max_exact)).astype(np.int32)
    large = np.minimum(large, half - 1)
    return ret + np.where(n < max_exact, n, large)


def _bias_variants(table, bucket, keeps):
    nq, nk = bucket.shape
    onehot = jnp.asarray(np.eye(N_BUCKETS, dtype=np.float32)[bucket])
    bias = jnp.einsum('rjb,bh->hjr', onehot, table, precision=lax.Precision.HIGHEST)
    bias = jnp.stack([jnp.where(keep.T[None], bias, NEG) for keep in keeps], axis=0)
    bias = bias.reshape(len(keeps), N_KV, GQ, nk, nq)
    return jnp.transpose(bias, (0, 1, 3, 2, 4)).reshape(len(keeps), N_KV, nk, GQ * nq)


def _lane_sink(sink, nq):
    return jnp.broadcast_to(sink.reshape(N_KV, 1, GQ, 1), (N_KV, 1, GQ, nq)).reshape(N_KV, 1, GQ * nq)


def _prompt_bias(table):
    r = np.arange(TQ)[:, None]
    j = np.arange(TQ + WINDOW)[None, :]
    band = (j - CHUNK * (r // CHUNK) >= 0) & (j - CHUNK * (r // CHUNK) < WINDOW + CHUNK)
    first = band & (j >= WINDOW - N_META)
    return _bias_variants(table, _t5_bucket_np(j - WINDOW - r), [first, band])


def _small_bias(table):
    i = np.arange(SEG)[:, None]
    m = np.arange(WINDOW + SEG)[None, :]
    everything = np.ones((SEG, WINDOW + SEG), bool)
    meta_only = everything & (m >= WINDOW + SEG - N_META)
    return _bias_variants(table, _t5_bucket_np(m - WINDOW - i), [everything, meta_only])


def _pack_state(re, im):
    return jnp.concatenate([re[..., :HALF_STATE], im[..., :HALF_STATE], re[..., HALF_STATE:], im[..., HALF_STATE:]],
                           axis=-1)


def _cmul(ar, ai, br, bi):
    return ar * br - ai * bi, ar * bi + ai * br


def _s5_prepare(a_re, a_im, log_dt, b_re, b_im, c_re, c_im):
    dt = jnp.exp(log_dt)[:, None]
    mag = jnp.exp(a_re * dt)
    ab_re, ab_im = mag * jnp.cos(a_im * dt), mag * jnp.sin(a_im * dt)
    den = a_re * a_re + a_im * a_im
    num_re = ab_re - 1.0
    cf_re = (num_re * a_re + ab_im * a_im) / den
    cf_im = (ab_im * a_re - num_re * a_im) / den
    bb_re = cf_re[..., None] * b_re - cf_im[..., None] * b_im
    bb_im = cf_re[..., None] * b_im + cf_im[..., None] * b_re

    def group_diag(v):
        r, c = v.shape[1], v.shape[2]
        gh = S5_GROUPS // 2
        spread = np.tile(np.eye(c, dtype=np.float32), (1, gh))
        wide = jnp.einsum('hrc,cn->hrn', v.reshape(2, gh * r, c), spread, precision=lax.Precision.HIGHEST)
        keep = (np.arange(gh * r)[:, None] // r) == (np.arange(gh * c)[None, :] // c)
        return jnp.where(keep, wide, 0.0)

    def t(v):
        return jnp.swapaxes(v, 1, 2)

    bb = jnp.concatenate([group_diag(t(bb_re)), group_diag(t(bb_im))], axis=2).astype(BF16)
    cc = jnp.concatenate([group_diag(t(c_re)), -group_diag(t(c_im))], axis=1).astype(BF16)

    pr, pi = ab_re.reshape(1, S5_STATE), ab_im.reshape(1, S5_STATE)
    n = 1
    while n < CHUNK_ROWS:
        qr, qi = _cmul(pr, pi, pr[n - 1:n], pi[n - 1:n])
        pr, pi = jnp.concatenate([pr, qr], axis=0), jnp.concatenate([pi, qi], axis=0)
        n *= 2

    def tile8(v):
        return jnp.broadcast_to(v, (8,) + v.shape[1:])

    ab = jnp.stack([tile8(pr[0:1]), tile8(pi[0:1])])
    sub = np.arange(8)[:, None]
    hop, w = [], (pr[CHUNK_ROWS - 1:], pi[CHUNK_ROWS - 1:])
    for k in (1, 2, 4):
        hop += [jnp.where(sub >= k, w[0], 0.0), jnp.where(sub >= k, w[1], 0.0)]
        if k < 4:
            w = _cmul(*w, *w)
    hop = jnp.stack(hop + [tile8(pr[CHUNK_ROWS - 1:]), tile8(pi[CHUNK_ROWS - 1:])])
    pw = jnp.stack([jnp.broadcast_to(p[:, None, :], (CHUNK_ROWS, 8, S5_STATE)) for p in (pr, pi)])
    return bb, cc, ab, hop, pw


def kernel(x_prompt, x_sample, cache_conv, state_s5_re, state_s5_im, cache_swa_k, cache_swa_v, meta_tokens, rel_bias_table, norm_mix, norm_ffn, norm_final, w_in0, conv_w, conv_b, s5_a_re, s5_a_im, s5_log_dt, s5_b_re, s5_b_im, s5_c_re, s5_c_im, s5_d, s5_glu_w, s5_glu_b, w_out0, ffn_w1, ffn_w3, ffn_w2, w_qkv, b_qkv, attn_sink, w_o, b_o, router_w, router_b, moe_w1, moe_w3, moe_w2):
    nb, seq = x_prompt.shape[0], x_prompt.shape[1]
    nsb = x_sample.shape[0]
    assert x_sample.shape[1] == SEG and nsb == N_SAMPLE_SEG and seq % TM_MIX == 0

    def row(v):
        return v.reshape(1, -1)

    meta_seg = jnp.concatenate([jnp.zeros((SEG - N_META, D_MODEL), F32), meta_tokens], axis=0)
    filler = jnp.zeros(((N_SMALL_SEG - N_SAMPLE_SEG - 1) * SEG, D_MODEL), F32)
    xs = jnp.concatenate([x_sample.reshape(nsb * SEG, D_MODEL), meta_seg, filler], axis=0)
    n_extra = N_SMALL_SEG - N_SAMPLE_SEG

    bb, cc, ab, hop, pw = _s5_prepare(s5_a_re[0], s5_a_im[0], s5_log_dt[0], s5_b_re[0], s5_b_im[0], s5_c_re[0],
                                      s5_c_im[0])
    mixer_w = (row(norm_mix[0]), w_in0[0].astype(BF16), conv_w[0], row(conv_b[0]))
    mixer_w2 = (ab, hop, pw, bb, cc, row(s5_d[0]), s5_glu_w[0].astype(BF16), row(s5_glu_b[0]), w_out0[0].astype(BF16))
    cinit_s = jnp.concatenate([jnp.swapaxes(cache_conv[0], 0, 1), jnp.zeros((2, n_extra, CONV_CH), F32)], axis=1)
    sinit_s = _pack_state(state_s5_re[0].reshape(nsb, S5_STATE), state_s5_im[0].reshape(nsb, S5_STATE))
    sinit_s = jnp.concatenate([sinit_s, jnp.zeros((n_extra, 2 * S5_STATE), F32)], axis=0)
    hs, conv_s, state_s = _even_mixer(xs[None], *mixer_w, cinit_s, sinit_s, *mixer_w2, carried=False)
    hs = hs[0]
    cinit_p = jnp.broadcast_to(conv_s[:, META_SEG:META_SEG + 1], (2, nb, CONV_CH))
    sinit_p = jnp.broadcast_to(state_s[META_SEG:META_SEG + 1], (nb, 2 * S5_STATE))
    hp, conv_p, state_p = _even_mixer(x_prompt, *mixer_w, cinit_p, sinit_p, *mixer_w2, carried=True)
    hp = hp.reshape(nb * seq, D_MODEL)
    ffn_w = (row(norm_ffn[0]), ffn_w1[0].astype(BF16), ffn_w3[0].astype(BF16), ffn_w2[0].astype(BF16))
    hs = _ffn(hs, *ffn_w, tm=256)
    hp = _ffn(hp, *ffn_w, tm=512)

    qkv_w = (row(norm_mix[1]), w_qkv[0].astype(BF16), row(b_qkv[0]))
    qs, ks, vs = _qkv(hs, *qkv_w, tm=256)
    qp, kp, vp = _qkv(hp, *qkv_w, tm=512)
    wo = (w_o[0].astype(BF16), row(b_o[0]))
    hs = _attn_small(qs, cache_swa_k[0].reshape(nsb, WINDOW, KV_DIM), cache_swa_v[0].reshape(nsb, WINDOW, KV_DIM),
                     ks, vs, _small_bias(rel_bias_table), _lane_sink(attn_sink[0], SEG), hs, *wo)

    def padded(new, small):
        meta_rows = small[META_SEG * SEG:(META_SEG + 1) * SEG]
        front = jnp.concatenate([jnp.zeros((WINDOW - SEG, KV_DIM), F32), meta_rows], axis=0)
        front = jnp.broadcast_to(front[None], (nb, WINDOW, KV_DIM))
        return jnp.concatenate([front, new.reshape(nb, seq, KV_DIM)], axis=1).astype(BF16)

    hp = _attn_prompt(qp, padded(kp, ks), padded(vp, vs), _prompt_bias(rel_bias_table),
                      _lane_sink(attn_sink[0], TQ), hp, *wo)
    moe_w = (row(norm_ffn[1]), router_w[0].T.astype(BF16), router_b[0].reshape(N_EXPERTS, 1), moe_w1[0].astype(BF16),
             moe_w3[0].astype(BF16), moe_w2[0].astype(BF16), row(norm_final))
    ys = _moe_final(hs, *moe_w, tb=hs.shape[0], rt=352, rt_small=256)
    yp = _moe_final(hp, *moe_w, tb=1024, rt=352, rt_small=256)

    n_real = nsb * SEG
    y_prompt = yp.reshape(nb, seq, D_MODEL)
    y_sample = ys[:n_real].reshape(nsb, SEG, D_MODEL)

    def split_state(st, n):
        st = st[:n].reshape(n, 2, 2, HALF_STATE)
        return tuple(st[:, :, ri].reshape(1, n, S5_GROUPS, S5_P) for ri in range(2))

    s5rp, s5ip = split_state(state_p, nb)
    s5rs, s5is = split_state(state_s, nsb)
    kp4 = kp.reshape(nb, seq, KV_DIM)[:, -WINDOW:].reshape(1, nb, WINDOW, N_KV, HEAD_DIM)
    vp4 = vp.reshape(nb, seq, KV_DIM)[:, -WINDOW:].reshape(1, nb, WINDOW, N_KV, HEAD_DIM)
    ks4 = ks[:n_real].reshape(1, nsb, SEG, N_KV, HEAD_DIM)
    vs4 = vs[:n_real].reshape(1, nsb, SEG, N_KV, HEAD_DIM)
    conv_p4 = jnp.swapaxes(conv_p, 0, 1)[None]
    conv_s4 = jnp.swapaxes(conv_s[:, :nsb], 0, 1)[None]
    return (y_prompt, y_sample, conv_p4, conv_s4, s5rp, s5ip, s5rs, s5is, kp4, vp4, ks4, vs4)
```

```python
import functools
import math

import numpy as np
import jax
import jax.numpy as jnp
from jax import lax
from jax.experimental import pallas as pl
from jax.experimental.pallas import tpu as pltpu

F32 = jnp.float32
BF16 = jnp.bfloat16

D_MODEL = 1024
CONV_CH = 512
S5_CH = 512
S5_GROUP = 16
S5_GROUPS = 32
S5_P = 64
S5_STATE = S5_GROUPS * S5_P
HALF_CH = S5_CH // 2
HALF_STATE = S5_STATE // 2
D_FF = 2816
N_HEADS = 16
N_KV = 2
GQ = N_HEADS // N_KV
HEAD_DIM = 64
KV_DIM = N_KV * HEAD_DIM
WINDOW = 128
CHUNK = 64
N_META = 16
N_BUCKETS = 32
MAX_DIST = 128
N_EXPERTS = 8
D_FF_E = 1024
EPS = 1e-6
NEG = -1e30

SEG = 32
N_SAMPLE_SEG = 32
META_SEG = N_SAMPLE_SEG
N_SMALL_SEG = 40
TM_MIX = 256
CHUNK_ROWS = SEG
SCAN_LANES = 512
TQ = 128
TQ_PER_STEP = 2
SEG_PER_STEP = 8
VMEM_LIMIT = 56 * 1024 * 1024


def _const_spec(shape):
    nd = len(shape)
    return pl.BlockSpec(shape, lambda *_: (0,) * nd, pipeline_mode=pl.Buffered(1))


def _params(*sem):
    return pltpu.CompilerParams(dimension_semantics=sem, vmem_limit_bytes=VMEM_LIMIT)


def _rms(x, g):
    return x * lax.rsqrt(jnp.mean(x * x, axis=-1, keepdims=True) + EPS) * g


def _dot(a, b):
    return jnp.dot(a, b, preferred_element_type=F32)


def _even_mixer_kernel(x_ref, gm_ref, perm_ref, unperm_ref, win_ref, cw_ref, cb_ref, cinit_ref, sinit_ref,
                       ab_ref, hop_ref, pw_ref, bb_ref, cc_ref, d_ref, gluw_ref, glub_ref, wout_ref,
                       h_ref, cout_ref, sout_ref, bu_ref, ccarry_ref, scarry_ref, *, carried, tiles_per_seq):
    if carried:
        @pl.when(pl.program_id(0) % tiles_per_seq == 0)
        def _load_initial_state():
            ccarry_ref[...] = cinit_ref[...]
            scarry_ref[...] = sinit_ref[...]

    streams = range(x_ref.shape[0])
    mid = [_mixer_project(b, x_ref, gm_ref, perm_ref, win_ref, cw_ref, cb_ref, cinit_ref, bb_ref, cout_ref, bu_ref,
                          ccarry_ref, carried=carried) for b in streams]
    for b in streams:
        _mixer_recur(b, sinit_ref, ab_ref, hop_ref, pw_ref, sout_ref, bu_ref, scarry_ref, carried=carried)
        _mixer_output(b, *mid[b], unperm_ref, cc_ref, d_ref, gluw_ref, glub_ref, wout_ref, h_ref, bu_ref)


def _mixer_project(b, x_ref, gm_ref, perm_ref, win_ref, cw_ref, cb_ref, cinit_ref, bb_ref, cout_ref, bu_ref,
                   ccarry_ref, *, carried):
    tm = TM_MIX
    x = x_ref[b]
    hn = _rms(x, gm_ref[...]).astype(BF16)
    hn = _dot(perm_ref[...], hn).astype(BF16)
    proj = _dot(hn, win_ref[...])
    g_b = proj[:, :CONV_CH]
    cin = proj[:, CONV_CH:2 * CONV_CH] * proj[:, 2 * CONV_CH:3 * CONV_CH]
    u = proj[:, 3 * CONV_CH:]
    ub = u.astype(BF16)
    for half in range(2):
        bu_ref[b, :, half * S5_STATE:(half + 1) * S5_STATE] = _dot(ub[:, half * HALF_CH:(half + 1) * HALF_CH],
                                                                  bb_ref[half])

    last, before_last = cin[tm - 8:tm], cin[tm - 16:tm - 8]
    if carried:
        first_chunk = lax.broadcasted_iota(jnp.int32, (8, CONV_CH), 0) == 0
        newer = jnp.where(first_chunk, ccarry_ref[1, b:b + 1, :], pltpu.roll(last, 1, 0))
        older = jnp.where(first_chunk, ccarry_ref[0, b:b + 1, :], pltpu.roll(before_last, 1, 0))
        for i, rows in enumerate((before_last, last)):
            ccarry_ref[i, b:b + 1, :] = rows[7:8]
            cout_ref[i, b:b + 1, :] = rows[7:8]
    else:
        older, newer = cinit_ref[0], cinit_ref[1]
        cout_ref[0] = before_last
        cout_ref[1] = last
    ext = jnp.concatenate([older, newer, cin], axis=0)
    cw = cw_ref[...]
    out_a = g_b * (cw[0:1] * ext[:tm] + cw[1:2] * ext[8:tm + 8] + cw[2:3] * cin + cb_ref[...])
    return x, u, out_a


def _mixer_recur(b, sinit_ref, ab_ref, hop_ref, pw_ref, sout_ref, bu_ref, scarry_ref, *, carried):
    for c in range(S5_STATE // SCAN_LANES):
        half, within = divmod(c * SCAN_LANES, HALF_STATE)
        nat = slice(c * SCAN_LANES, (c + 1) * SCAN_LANES)
        re_cols = slice(half * S5_STATE + within, half * S5_STATE + within + SCAN_LANES)
        im_cols = slice(re_cols.start + HALF_STATE, re_cols.stop + HALF_STATE)
        ar, ai = ab_ref[0, :, nat], ab_ref[1, :, nat]
        if carried:
            sr = si = jnp.zeros((8, SCAN_LANES), F32)
        else:
            sr, si = sinit_ref[:, re_cols], sinit_ref[:, im_cols]
        for g in range(CHUNK_ROWS):
            rows = slice(8 * g, 8 * g + 8)
            sr, si = (ar * sr - ai * si + bu_ref[b, rows, re_cols], ar * si + ai * sr + bu_ref[b, rows, im_cols])
            bu_ref[b, rows, re_cols] = sr
            bu_ref[b, rows, im_cols] = si
        if not carried:
            sout_ref[:, re_cols] = sr
            sout_ref[:, im_cols] = si
            continue
        first_chunk = lax.broadcasted_iota(jnp.int32, (8, SCAN_LANES), 0) == 0
        er = jnp.where(first_chunk, scarry_ref[b:b + 1, re_cols], pltpu.roll(sr, 1, 0))
        ei = jnp.where(first_chunk, scarry_ref[b:b + 1, im_cols], pltpu.roll(si, 1, 0))
        for k, sh in enumerate((1, 2, 4)):
            wr, wi = hop_ref[2 * k, :, nat], hop_ref[2 * k + 1, :, nat]
            tr, ti = pltpu.roll(er, sh, 0), pltpu.roll(ei, sh, 0)
            er, ei = er + wr * tr - wi * ti, ei + wr * ti + wi * tr
        wr, wi = hop_ref[6, :, nat], hop_ref[7, :, nat]
        fr, fi = wr * er - wi * ei + sr, wr * ei + wi * er + si
        for ref in (scarry_ref, sout_ref):
            ref[b:b + 1, re_cols] = fr[7:8]
            ref[b:b + 1, im_cols] = fi[7:8]
        for g in range(CHUNK_ROWS):
            rows = slice(8 * g, 8 * g + 8)
            pr, pi = pw_ref[0, g, :, nat], pw_ref[1, g, :, nat]
            bu_ref[b, rows, re_cols] += pr * er - pi * ei
            bu_ref[b, rows, im_cols] += pr * ei + pi * er


def _mixer_output(b, x, u, out_a, unperm_ref, cc_ref, d_ref, gluw_ref, glub_ref, wout_ref, h_ref, bu_ref):
    y = jnp.concatenate([_dot(bu_ref[b, :, half * S5_STATE:(half + 1) * S5_STATE].astype(BF16), cc_ref[half])
                         for half in range(2)], axis=-1) + d_ref[...] * u
    z = jax.nn.gelu(y)
    out_b = z * jax.nn.sigmoid(_dot(z.astype(BF16), gluw_ref[...]) + glub_ref[...])
    mix = jnp.concatenate([out_a, out_b], axis=-1).astype(BF16)
    mix = _dot(unperm_ref[...], mix).astype(BF16)
    h_ref[b] = x + _dot(mix, wout_ref[...])


def _even_mixer(x, gm, w_in, cw, cb, cinit, sinit, ab, hop, pw, bb, cc, d, glu_w, glu_b, w_out, *, carried):
    nb, rows = x.shape[0], x.shape[1]
    tm = TM_MIX
    chunks = tm // CHUNK_ROWS
    assert chunks == 8 and rows % tm == 0 and (carried or nb == 1)
    per_step = nb if carried else chunks
    tiles_per_seq = rows // tm if carried else 1
    nseq = sinit.shape[0]
    t_of = np.arange(tm)
    perm = np.zeros((tm, tm), np.float32)
    perm[8 * (t_of % CHUNK_ROWS) + t_of // CHUNK_ROWS, t_of] = 1.0
    kern = functools.partial(_even_mixer_kernel, carried=carried, tiles_per_seq=tiles_per_seq)
    return pl.pallas_call(
        kern,
        grid=(rows // tm,),
        in_specs=[
            pl.BlockSpec((nb, tm, D_MODEL), lambda t: (0, t, 0)),
            _const_spec((1, D_MODEL)),
            _const_spec((tm, tm)),
            _const_spec((tm, tm)),
            _const_spec((D_MODEL, 4 * CONV_CH)),
            _const_spec((3, CONV_CH)),
            _const_spec((1, CONV_CH)),
            pl.BlockSpec((2, per_step, CONV_CH), lambda t: (0, t // tiles_per_seq, 0)),
            pl.BlockSpec((per_step, 2 * S5_STATE), lambda t: (t // tiles_per_seq, 0)),
            _const_spec((2, 8, S5_STATE)),
            _const_spec((8, 8, S5_STATE)),
            _const_spec((2, CHUNK_ROWS, 8, S5_STATE)),
            _const_spec((2, HALF_CH, S5_STATE)),
            _const_spec((2, S5_STATE, HALF_CH)),
            _const_spec((1, S5_CH)),
            _const_spec((S5_CH, S5_CH)),
            _const_spec((1, S5_CH)),
            _const_spec((D_MODEL, D_MODEL)),
        ],
        out_specs=[
            pl.BlockSpec((nb, tm, D_MODEL), lambda t: (0, t, 0)),
            pl.BlockSpec((2, per_step, CONV_CH), lambda t: (0, t // tiles_per_seq, 0)),
            pl.BlockSpec((per_step, 2 * S5_STATE), lambda t: (t // tiles_per_seq, 0)),
        ],
        out_shape=[
            jax.ShapeDtypeStruct((nb, rows, D_MODEL), F32),
            jax.ShapeDtypeStruct((2, nseq, CONV_CH), F32),
            jax.ShapeDtypeStruct((nseq, 2 * S5_STATE), F32),
        ],
        scratch_shapes=[
            pltpu.VMEM((nb, tm, 2 * S5_STATE), F32),
            pltpu.VMEM((2, nb, CONV_CH), F32),
            pltpu.VMEM((nb, 2 * S5_STATE), F32),
        ],
        compiler_params=_params("arbitrary"),
        name="even_mixer",
    )(x, gm, jnp.asarray(perm, BF16), jnp.asarray(perm.T, BF16), w_in, cw, cb, cinit, sinit, ab, hop, pw, bb, cc, d,
      glu_w, glu_b, w_out)


def _ffn_kernel(x_ref, g_ref, w1_ref, w3_ref, w2_ref, o_ref):
    x = x_ref[...]
    hn = _rms(x, g_ref[...]).astype(BF16)
    a = _dot(hn, w1_ref[...])
    b = _dot(hn, w3_ref[...])
    o_ref[...] = x + _dot((a * jax.nn.sigmoid(a) * b).astype(BF16), w2_ref[...])


def _ffn(x, g, w1, w3, w2, *, tm):
    rows = x.shape[0]
    return pl.pallas_call(
        _ffn_kernel,
        grid=(rows // tm,),
        in_specs=[
            pl.BlockSpec((tm, D_MODEL), lambda t: (t, 0)),
            _const_spec((1, D_MODEL)),
            _const_spec((D_MODEL, D_FF)),
            _const_spec((D_MODEL, D_FF)),
            _const_spec((D_FF, D_MODEL)),
        ],
        out_specs=pl.BlockSpec((tm, D_MODEL), lambda t: (t, 0)),
        out_shape=jax.ShapeDtypeStruct((rows, D_MODEL), F32),
        compiler_params=_params("parallel"),
        name="ffn",
    )(x, g, w1, w3, w2)


def _qkv_kernel(x_ref, g_ref, w_ref, b_ref, q_ref, k_ref, v_ref):
    hn = _rms(x_ref[...], g_ref[...]).astype(BF16)
    qkv = _dot(hn, w_ref[...]) + b_ref[...]
    nq = N_HEADS * HEAD_DIM
    q_ref[...] = (qkv[:, :nq] * (HEAD_DIM ** -0.5)).astype(BF16)
    k_ref[...] = qkv[:, nq:nq + KV_DIM]
    v_ref[...] = qkv[:, nq + KV_DIM:]


def _qkv(x, g, w, b, *, tm):
    rows = x.shape[0]
    ncol = (N_HEADS + 2 * N_KV) * HEAD_DIM
    return pl.pallas_call(
        _qkv_kernel,
        grid=(rows // tm,),
        in_specs=[
            pl.BlockSpec((tm, D_MODEL), lambda t: (t, 0)),
            _const_spec((1, D_MODEL)),
            _const_spec((D_MODEL, ncol)),
            _const_spec((1, ncol)),
        ],
        out_specs=[
            pl.BlockSpec((tm, N_HEADS * HEAD_DIM), lambda t: (t, 0)),
            pl.BlockSpec((tm, KV_DIM), lambda t: (t, 0)),
            pl.BlockSpec((tm, KV_DIM), lambda t: (t, 0)),
        ],
        out_shape=[
            jax.ShapeDtypeStruct((rows, N_HEADS * HEAD_DIM), BF16),
            jax.ShapeDtypeStruct((rows, KV_DIM), F32),
            jax.ShapeDtypeStruct((rows, KV_DIM), F32),
        ],
        compiler_params=_params("parallel"),
        name="qkv",
    )(x, g, w, b)


def _attend_groups(q, keys, vals, bias_of_group, sink_of_group, o_scr, row0):
    nq = q.shape[0]
    for kv in range(N_KV):
        kv_cols = slice(kv * HEAD_DIM, (kv + 1) * HEAD_DIM)
        heads = [slice((kv * GQ + g) * HEAD_DIM, (kv * GQ + g + 1) * HEAD_DIM) for g in range(GQ)]
        qs = jnp.concatenate([q[:, cols] for cols in heads], axis=0)
        st = lax.dot_general(keys[:, kv_cols], qs, (((1,), (1,)), ((), ())),
                             preferred_element_type=F32) + bias_of_group(kv)
        sk = sink_of_group(kv)
        m = jnp.maximum(jnp.max(st, axis=0, keepdims=True), sk)
        p = jnp.exp(st - m)
        den = jnp.sum(p, axis=0, keepdims=True) + jnp.exp(sk - m)
        o = lax.dot_general((p * (1.0 / den)).astype(BF16), vals[:, kv_cols], (((0,), (0,)), ((), ())),
                            preferred_element_type=F32)
        for g, cols in enumerate(heads):
            o_scr[row0:row0 + nq, cols] = o[g * nq:(g + 1) * nq]


def _attn_prompt_kernel(q_ref, k_ref, v_ref, bias_ref, sink_ref, x_ref, wo_ref, bo_ref, o_ref, o_scr):
    t = pl.program_id(1)
    for u in range(TQ_PER_STEP):
        tile = t * TQ_PER_STEP + u
        start = pl.multiple_of(tile * TQ, TQ)
        keys = k_ref[0, pl.ds(start, TQ + WINDOW), :]
        vals = v_ref[0, pl.ds(start, TQ + WINDOW), :]
        variant = jnp.minimum(tile, 1)
        _attend_groups(q_ref[u * TQ:(u + 1) * TQ], keys, vals, lambda kv: bias_ref[variant, kv],
                       lambda kv: sink_ref[kv], o_scr, u * TQ)
    o_ref[...] = x_ref[...] + _dot(o_scr[...].astype(BF16), wo_ref[...]) + bo_ref[...]


def _attn_prompt(q, kpad, vpad, bias, sink, x, w_o, b_o):
    nb, lpad = kpad.shape[0], kpad.shape[1]
    tm = TQ_PER_STEP * TQ
    steps = (lpad - WINDOW) // tm
    return pl.pallas_call(
        _attn_prompt_kernel,
        grid=(nb, steps),
        in_specs=[
            pl.BlockSpec((tm, D_MODEL), lambda b, t: (b * steps + t, 0)),
            pl.BlockSpec((1, lpad, KV_DIM), lambda b, t: (b, 0, 0)),
            pl.BlockSpec((1, lpad, KV_DIM), lambda b, t: (b, 0, 0)),
            _const_spec((2, N_KV, TQ + WINDOW, GQ * TQ)),
            _const_spec((N_KV, 1, GQ * TQ)),
            pl.BlockSpec((tm, D_MODEL), lambda b, t: (b * steps + t, 0)),
            _const_spec((D_MODEL, D_MODEL)),
            _const_spec((1, D_MODEL)),
        ],
        out_specs=pl.BlockSpec((tm, D_MODEL), lambda b, t: (b * steps + t, 0)),
        out_shape=jax.ShapeDtypeStruct(x.shape, F32),
        scratch_shapes=[pltpu.VMEM((tm, D_MODEL), F32)],
        compiler_params=_params("parallel", "arbitrary"),
        name="attn_prompt",
    )(q, kpad, vpad, bias, sink, x, w_o, b_o)


def _attn_small_kernel(q_ref, kc_ref, vc_ref, kn_ref, vn_ref, bias_ref, sink_ref, x_ref, wo_ref, bo_ref,
                       o_ref, o_scr):
    step = pl.program_id(0)
    for j in range(SEG_PER_STEP):
        rows = slice(j * SEG, (j + 1) * SEG)
        keys = jnp.concatenate([kc_ref[j], kn_ref[rows]], axis=0).astype(BF16)
        vals = jnp.concatenate([vc_ref[j], vn_ref[rows]], axis=0).astype(BF16)
        variant = jnp.where(step * SEG_PER_STEP + j == META_SEG, 1, 0)
        _attend_groups(q_ref[rows], keys, vals, lambda kv: bias_ref[variant, kv], lambda kv: sink_ref[kv], o_scr,
                       j * SEG)
    o_ref[...] = x_ref[...] + _dot(o_scr[...].astype(BF16), wo_ref[...]) + bo_ref[...]


def _attn_small(q, kc, vc, kn, vn, bias, sink, x, w_o, b_o):
    tm = SEG_PER_STEP * SEG
    steps = x.shape[0] // tm
    cache_blocks = kc.shape[0] // SEG_PER_STEP
    cache_spec = pl.BlockSpec((SEG_PER_STEP, WINDOW, KV_DIM), lambda s: (jnp.minimum(s, cache_blocks - 1), 0, 0))
    return pl.pallas_call(
        _attn_small_kernel,
        grid=(steps,),
        in_specs=[
            pl.BlockSpec((tm, D_MODEL), lambda s: (s, 0)),
            cache_spec,
            cache_spec,
            pl.BlockSpec((tm, KV_DIM), lambda s: (s, 0)),
            pl.BlockSpec((tm, KV_DIM), lambda s: (s, 0)),
            _const_spec((2, N_KV, WINDOW + SEG, GQ * SEG)),
            _const_spec((N_KV, 1, GQ * SEG)),
            pl.BlockSpec((tm, D_MODEL), lambda s: (s, 0)),
            _const_spec((D_MODEL, D_MODEL)),
            _const_spec((1, D_MODEL)),
        ],
        out_specs=pl.BlockSpec((tm, D_MODEL), lambda s: (s, 0)),
        out_shape=jax.ShapeDtypeStruct(x.shape, F32),
        scratch_shapes=[pltpu.VMEM((tm, D_MODEL), F32)],
        compiler_params=_params("parallel"),
        name="attn_small",
    )(q, kc, vc, kn, vn, bias, sink, x, w_o, b_o)


def _moe_kernel(x_ref, g_ref, rwt_ref, rb_ref, w1_ref, w3_ref, w2_ref, gf_ref, o_ref,
                hn_scr, slot_scr, gate_scr, cnt_smem, acc_scr, yall_scr, *, tile_rows):
    e = pl.program_id(1)
    tb = x_ref.shape[0]
    rt = tile_rows[-1]

    @pl.when(e == 0)
    def _route():
        hn = _rms(x_ref[...], g_ref[...]).astype(BF16)
        hn_scr[...] = hn
        logits = lax.dot_general(rwt_ref[...], hn, (((1,), (1,)), ((), ())),
                                 preferred_element_type=F32) + rb_ref[...]
        row = lax.broadcasted_iota(jnp.int32, logits.shape, 0).astype(F32)
        v1 = jnp.max(logits, axis=0, keepdims=True)
        i1 = jnp.min(jnp.where(logits == v1, row, float(N_EXPERTS)), axis=0, keepdims=True)
        rest = jnp.where(row == i1, -jnp.inf, logits)
        v2 = jnp.max(rest, axis=0, keepdims=True)
        i2 = jnp.min(jnp.where(rest == v2, row, float(N_EXPERTS)), axis=0, keepdims=True)
        e2 = jnp.exp(v2 - v1)
        den = 1.0 + e2
        gate_scr[...] = jnp.where(row == i1, 1.0 / den, 0.0) + jnp.where(row == i2, e2 / den, 0.0)
        chosen = (row == i1) | (row == i2)
        ind = jnp.where(chosen, 1.0, 0.0)
        earlier = (lax.broadcasted_iota(jnp.int32, (tb, tb), 0) < lax.broadcasted_iota(jnp.int32, (tb, tb), 1))
        rank = _dot(ind.astype(BF16), jnp.where(earlier, 1.0, 0.0).astype(BF16))
        slot_scr[...] = jnp.where(chosen, rank, -1.0).astype(jnp.int32)
        for k in range(N_EXPERTS):
            cnt_smem[k] = jnp.sum(ind[k:k + 1, :]).astype(jnp.int32)
        acc_scr[...] = jnp.zeros_like(acc_scr)

    slot = slot_scr[pl.ds(e, 1), :]
    gate = gate_scr[pl.ds(e, 1), :]

    def _scatter(onehot, rows):
        return lax.dot_general(onehot, rows, (((0,), (0,)), ((), ())), preferred_element_type=F32)

    def _expert_tile(i, rows=rt):
        packed_row = lax.broadcasted_iota(jnp.int32, (rows, tb), 0) + i * rt
        hit = packed_row == slot
        onehot = jnp.where(hit, 1.0, 0.0).astype(BF16)
        xe = _dot(onehot, hn_scr[...]).astype(BF16)
        a = _dot(xe, w1_ref[0])
        b = _dot(xe, w3_ref[0])
        y = _dot((a * jax.nn.sigmoid(a) * b).astype(BF16), w2_ref[0])
        g_row = jnp.sum(jnp.where(hit, gate, 0.0), axis=1, keepdims=True)
        return onehot, (y * g_row).astype(BF16)

    base = pl.multiple_of(e * rt, 16)
    count = cnt_smem[e]
    for lower, rows in zip((-1,) + tile_rows, tile_rows):
        fits = (count > lower) & (count <= rows) if rows < rt else count > lower

        @pl.when(fits)
        def _first_tile(rows=rows):
            yall_scr[pl.ds(base, rows), :] = _expert_tile(0, rows)[1]
            if rows < rt:
                yall_scr[pl.ds(base + rows, rt - rows), :] = jnp.zeros((rt - rows, D_MODEL), BF16)

    def _overflow_tile(i, carry):
        acc_scr[...] += _scatter(*_expert_tile(i))
        return carry

    lax.fori_loop(1, (cnt_smem[e] + rt - 1) // rt, _overflow_tile, 0)

    @pl.when(e == N_EXPERTS - 1)
    def _finish():
        packed_row = lax.broadcasted_iota(jnp.int32, (rt, tb), 0)
        onehot = jnp.concatenate([jnp.where(packed_row == slot_scr[k:k + 1, :], 1.0, 0.0).astype(BF16)
                                  for k in range(N_EXPERTS)], axis=0)
        o_ref[...] = _rms(x_ref[...] + acc_scr[...] + _scatter(onehot, yall_scr[...]), gf_ref[...])


def _moe_final(x, g, rwt, rb, w1, w3, w2, gf, *, tb, tile_rows):
    rows, rt = x.shape[0], tile_rows[-1]
    return pl.pallas_call(
        functools.partial(_moe_kernel, tile_rows=tile_rows),
        grid=(rows // tb, N_EXPERTS),
        in_specs=[
            pl.BlockSpec((tb, D_MODEL), lambda t, e: (t, 0)),
            _const_spec((1, D_MODEL)),
            _const_spec((N_EXPERTS, D_MODEL)),
            _const_spec((N_EXPERTS, 1)),
            pl.BlockSpec((1, D_MODEL, D_FF_E), lambda t, e: (e, 0, 0)),
            pl.BlockSpec((1, D_MODEL, D_FF_E), lambda t, e: (e, 0, 0)),
            pl.BlockSpec((1, D_FF_E, D_MODEL), lambda t, e: (e, 0, 0)),
            _const_spec((1, D_MODEL)),
        ],
        out_specs=pl.BlockSpec((tb, D_MODEL), lambda t, e: (t, 0)),
        out_shape=jax.ShapeDtypeStruct((rows, D_MODEL), F32),
        scratch_shapes=[
            pltpu.VMEM((tb, D_MODEL), BF16),
            pltpu.VMEM((N_EXPERTS, tb), jnp.int32),
            pltpu.VMEM((N_EXPERTS, tb), F32),
            pltpu.SMEM((N_EXPERTS,), jnp.int32),
            pltpu.VMEM((tb, D_MODEL), F32),
            pltpu.VMEM((N_EXPERTS * rt, D_MODEL), BF16),
        ],
        compiler_params=_params("parallel", "arbitrary"),
        name="moe_final",
    )(x, g, rwt, rb, w1, w3, w2, gf)


def _t5_bucket_np(rel):
    half = N_BUCKETS // 2
    max_exact = half // 2
    ret = np.where(rel > 0, half, 0)
    n = np.abs(rel)
    nf = np.maximum(n, 1).astype(np.float32)
    large = max_exact + (np.log(nf / np.float32(max_exact)) / np.float32(math.log(MAX_DIST / max_exact))
                         * np.float32(half - max_exact)).astype(np.int32)
    large = np.minimum(large, half - 1)
    return ret + np.where(n < max_exact, n, large)


def _bias_variants(table, bucket, keeps):
    nq, nk = bucket.shape
    onehot = jnp.asarray(np.eye(N_BUCKETS, dtype=np.float32)[bucket])
    bias = jnp.einsum('rjb,bh->hjr', onehot, table, precision=lax.Precision.HIGHEST)
    bias = jnp.stack([jnp.where(keep.T[None], bias, NEG) for keep in keeps], axis=0)
    bias = bias.reshape(len(keeps), N_KV, GQ, nk, nq)
    return jnp.transpose(bias, (0, 1, 3, 2, 4)).reshape(len(keeps), N_KV, nk, GQ * nq)


def _lane_sink(sink, nq):
    return jnp.broadcast_to(sink.reshape(N_KV, 1, GQ, 1), (N_KV, 1, GQ, nq)).reshape(N_KV, 1, GQ * nq)


def _prompt_bias(table):
    r = np.arange(TQ)[:, None]
    j = np.arange(TQ + WINDOW)[None, :]
    band = (j - CHUNK * (r // CHUNK) >= 0) & (j - CHUNK * (r // CHUNK) < WINDOW + CHUNK)
    first = band & (j >= WINDOW - N_META)
    return _bias_variants(table, _t5_bucket_np(j - WINDOW - r), [first, band])


def _small_bias(table):
    i = np.arange(SEG)[:, None]
    m = np.arange(WINDOW + SEG)[None, :]
    everything = np.ones((SEG, WINDOW + SEG), bool)
    meta_only = everything & (m >= WINDOW + SEG - N_META)
    return _bias_variants(table, _t5_bucket_np(m - WINDOW - i), [everything, meta_only])


def _pack_state(re, im):
    return jnp.concatenate([re[..., :HALF_STATE], im[..., :HALF_STATE], re[..., HALF_STATE:], im[..., HALF_STATE:]],
                           axis=-1)


def _cmul(ar, ai, br, bi):
    return ar * br - ai * bi, ar * bi + ai * br


def _s5_prepare(a_re, a_im, log_dt, b_re, b_im, c_re, c_im):
    dt = jnp.exp(log_dt)[:, None]
    mag = jnp.exp(a_re * dt)
    ab_re, ab_im = mag * jnp.cos(a_im * dt), mag * jnp.sin(a_im * dt)
    den = a_re * a_re + a_im * a_im
    num_re = ab_re - 1.0
    cf_re = (num_re * a_re + ab_im * a_im) / den
    cf_im = (ab_im * a_re - num_re * a_im) / den
    bb_re = cf_re[..., None] * b_re - cf_im[..., None] * b_im
    bb_im = cf_re[..., None] * b_im + cf_im[..., None] * b_re

    def group_diag(v):
        r, c = v.shape[1], v.shape[2]
        gh = S5_GROUPS // 2
        spread = np.tile(np.eye(c, dtype=np.float32), (1, gh))
        wide = jnp.einsum('hrc,cn->hrn', v.reshape(2, gh * r, c), spread, precision=lax.Precision.HIGHEST)
        keep = (np.arange(gh * r)[:, None] // r) == (np.arange(gh * c)[None, :] // c)
        return jnp.where(keep, wide, 0.0)

    def t(v):
        return jnp.swapaxes(v, 1, 2)

    bb = jnp.concatenate([group_diag(t(bb_re)), group_diag(t(bb_im))], axis=2).astype(BF16)
    cc = jnp.concatenate([group_diag(t(c_re)), -group_diag(t(c_im))], axis=1).astype(BF16)

    pr, pi = ab_re.reshape(1, S5_STATE), ab_im.reshape(1, S5_STATE)
    n = 1
    while n < CHUNK_ROWS:
        qr, qi = _cmul(pr, pi, pr[n - 1:n], pi[n - 1:n])
        pr, pi = jnp.concatenate([pr, qr], axis=0), jnp.concatenate([pi, qi], axis=0)
        n *= 2

    def tile8(v):
        return jnp.broadcast_to(v, (8,) + v.shape[1:])

    ab = jnp.stack([tile8(pr[0:1]), tile8(pi[0:1])])
    sub = np.arange(8)[:, None]
    hop, w = [], (pr[CHUNK_ROWS - 1:], pi[CHUNK_ROWS - 1:])
    for k in (1, 2, 4):
        hop += [jnp.where(sub >= k, w[0], 0.0), jnp.where(sub >= k, w[1], 0.0)]
        if k < 4:
            w = _cmul(*w, *w)
    hop = jnp.stack(hop + [tile8(pr[CHUNK_ROWS - 1:]), tile8(pi[CHUNK_ROWS - 1:])])
    pw = jnp.stack([jnp.broadcast_to(p[:, None, :], (CHUNK_ROWS, 8, S5_STATE)) for p in (pr, pi)])
    return bb, cc, ab, hop, pw


def kernel(x_prompt, x_sample, cache_conv, state_s5_re, state_s5_im, cache_swa_k, cache_swa_v, meta_tokens, rel_bias_table, norm_mix, norm_ffn, norm_final, w_in0, conv_w, conv_b, s5_a_re, s5_a_im, s5_log_dt, s5_b_re, s5_b_im, s5_c_re, s5_c_im, s5_d, s5_glu_w, s5_glu_b, w_out0, ffn_w1, ffn_w3, ffn_w2, w_qkv, b_qkv, attn_sink, w_o, b_o, router_w, router_b, moe_w1, moe_w3, moe_w2):
    nb, seq = x_prompt.shape[0], x_prompt.shape[1]
    nsb = x_sample.shape[0]
    assert x_sample.shape[1] == SEG and nsb == N_SAMPLE_SEG and seq % TM_MIX == 0

    def row(v):
        return v.reshape(1, -1)

    meta_seg = jnp.concatenate([jnp.zeros((SEG - N_META, D_MODEL), F32), meta_tokens], axis=0)
    filler = jnp.zeros(((N_SMALL_SEG - N_SAMPLE_SEG - 1) * SEG, D_MODEL), F32)
    xs = jnp.concatenate([x_sample.reshape(nsb * SEG, D_MODEL), meta_seg, filler], axis=0)
    n_extra = N_SMALL_SEG - N_SAMPLE_SEG

    bb, cc, ab, hop, pw = _s5_prepare(s5_a_re[0], s5_a_im[0], s5_log_dt[0], s5_b_re[0], s5_b_im[0], s5_c_re[0],
                                      s5_c_im[0])
    mixer_w = (row(norm_mix[0]), w_in0[0].astype(BF16), conv_w[0], row(conv_b[0]))
    mixer_w2 = (ab, hop, pw, bb, cc, row(s5_d[0]), s5_glu_w[0].astype(BF16), row(s5_glu_b[0]), w_out0[0].astype(BF16))
    cinit_s = jnp.concatenate([jnp.swapaxes(cache_conv[0], 0, 1), jnp.zeros((2, n_extra, CONV_CH), F32)], axis=1)
    sinit_s = _pack_state(state_s5_re[0].reshape(nsb, S5_STATE), state_s5_im[0].reshape(nsb, S5_STATE))
    sinit_s = jnp.concatenate([sinit_s, jnp.zeros((n_extra, 2 * S5_STATE), F32)], axis=0)
    hs, conv_s, state_s = _even_mixer(xs[None], *mixer_w, cinit_s, sinit_s, *mixer_w2, carried=False)
    hs = hs[0]
    cinit_p = jnp.broadcast_to(conv_s[:, META_SEG:META_SEG + 1], (2, nb, CONV_CH))
    sinit_p = jnp.broadcast_to(state_s[META_SEG:META_SEG + 1], (nb, 2 * S5_STATE))
    hp, conv_p, state_p = _even_mixer(x_prompt, *mixer_w, cinit_p, sinit_p, *mixer_w2, carried=True)
    hp = hp.reshape(nb * seq, D_MODEL)
    ffn_w = (row(norm_ffn[0]), ffn_w1[0].astype(BF16), ffn_w3[0].astype(BF16), ffn_w2[0].astype(BF16))
    hs = _ffn(hs, *ffn_w, tm=256)
    hp = _ffn(hp, *ffn_w, tm=512)

    qkv_w = (row(norm_mix[1]), w_qkv[0].astype(BF16), row(b_qkv[0]))
    qs, ks, vs = _qkv(hs, *qkv_w, tm=256)
    qp, kp, vp = _qkv(hp, *qkv_w, tm=512)
    wo = (w_o[0].astype(BF16), row(b_o[0]))
    hs = _attn_small(qs, cache_swa_k[0].reshape(nsb, WINDOW, KV_DIM), cache_swa_v[0].reshape(nsb, WINDOW, KV_DIM),
                     ks, vs, _small_bias(rel_bias_table), _lane_sink(attn_sink[0], SEG), hs, *wo)

    def padded(new, small):
        meta_rows = small[META_SEG * SEG:(META_SEG + 1) * SEG]
        front = jnp.concatenate([jnp.zeros((WINDOW - SEG, KV_DIM), F32), meta_rows], axis=0)
        front = jnp.broadcast_to(front[None], (nb, WINDOW, KV_DIM))
        return jnp.concatenate([front, new.reshape(nb, seq, KV_DIM)], axis=1).astype(BF16)

    hp = _attn_prompt(qp, padded(kp, ks), padded(vp, vs), _prompt_bias(rel_bias_table),
                      _lane_sink(attn_sink[0], TQ), hp, *wo)
    moe_w = (row(norm_ffn[1]), router_w[0].T.astype(BF16), router_b[0].reshape(N_EXPERTS, 1), moe_w1[0].astype(BF16),
             moe_w3[0].astype(BF16), moe_w2[0].astype(BF16), row(norm_final))
    ys = _moe_final(hs, *moe_w, tb=hs.shape[0], tile_rows=(352,))
    yp = _moe_final(hp, *moe_w, tb=1024, tile_rows=(224, 288, 352))

    n_real = nsb * SEG
    y_prompt = yp.reshape(nb, seq, D_MODEL)
    y_sample = ys[:n_real].reshape(nsb, SEG, D_MODEL)

    def split_state(st, n):
        st = st[:n].reshape(n, 2, 2, HALF_STATE)
        return tuple(st[:, :, ri].reshape(1, n, S5_GROUPS, S5_P) for ri in range(2))

    s5rp, s5ip = split_state(state_p, nb)
    s5rs, s5is = split_state(state_s, nsb)
    kp4 = kp.reshape(nb, seq, KV_DIM)[:, -WINDOW:].reshape(1, nb, WINDOW, N_KV, HEAD_DIM)
    vp4 = vp.reshape(nb, seq, KV_DIM)[:, -WINDOW:].reshape(1, nb, WINDOW, N_KV, HEAD_DIM)
    ks4 = ks[:n_real].reshape(1, nsb, SEG, N_KV, HEAD_DIM)
    vs4 = vs[:n_real].reshape(1, nsb, SEG, N_KV, HEAD_DIM)
    conv_p4 = jnp.swapaxes(conv_p, 0, 1)[None]
    conv_s4 = jnp.swapaxes(conv_s[:, :nsb], 0, 1)[None]
    return (y_prompt, y_sample, conv_p4, conv_s4, s5rp, s5ip, s5rs, s5is, kp4, vp4, ks4, vs4)
```

```python
import functools
import math

import numpy as np
import jax
import jax.numpy as jnp
from jax import lax
from jax.experimental import pallas as pl
from jax.experimental.pallas import tpu as pltpu

F32 = jnp.float32
BF16 = jnp.bfloat16

D_MODEL = 1024
CONV_CH = 512
S5_CH = 512
S5_GROUP = 16
S5_GROUPS = 32
S5_P = 64
S5_STATE = S5_GROUPS * S5_P
HALF_CH = S5_CH // 2
HALF_STATE = S5_STATE // 2
D_FF = 2816
N_HEADS = 16
N_KV = 2
GQ = N_HEADS // N_KV
HEAD_DIM = 64
KV_DIM = N_KV * HEAD_DIM
WINDOW = 128
CHUNK = 64
N_META = 16
N_BUCKETS = 32
MAX_DIST = 128
N_EXPERTS = 8
D_FF_E = 1024
EPS = 1e-6
NEG = -1e30

SEG = 32
N_SAMPLE_SEG = 32
META_SEG = N_SAMPLE_SEG
N_SMALL_SEG = 40
TM_MIX = 256
CHUNK_ROWS = SEG
SCAN_LANES = 512
TQ = 128
TQ_PER_STEP = 8
SEG_PER_STEP = 8
VMEM_LIMIT = 56 * 1024 * 1024


def _const_spec(shape):
    nd = len(shape)
    return pl.BlockSpec(shape, lambda *_: (0,) * nd, pipeline_mode=pl.Buffered(1))


def _params(*sem):
    return pltpu.CompilerParams(dimension_semantics=sem, vmem_limit_bytes=VMEM_LIMIT)


def _rms(x, g):
    return x * lax.rsqrt(jnp.mean(x * x, axis=-1, keepdims=True) + EPS) * g


def _dot(a, b):
    return jnp.dot(a, b, preferred_element_type=F32)


def _even_mixer_kernel(x_ref, gm_ref, perm_ref, unperm_ref, win_ref, cw_ref, cb_ref, cinit_ref, sinit_ref,
                       ab_ref, hop_ref, pw_ref, bb_ref, cc_ref, d_ref, gluw_ref, glub_ref, wout_ref,
                       h_ref, cout_ref, sout_ref, bu_ref, ccarry_ref, scarry_ref, *, carried, tiles_per_seq):
    if carried:
        @pl.when(pl.program_id(0) % tiles_per_seq == 0)
        def _load_initial_state():
            ccarry_ref[...] = cinit_ref[...]
            scarry_ref[...] = sinit_ref[...]

    streams = range(x_ref.shape[0])
    mid = [_mixer_project(b, x_ref, gm_ref, perm_ref, win_ref, cw_ref, cb_ref, cinit_ref, bb_ref, cout_ref, bu_ref,
                          ccarry_ref, carried=carried) for b in streams]
    for b in streams:
        _mixer_recur(b, sinit_ref, ab_ref, hop_ref, pw_ref, sout_ref, bu_ref, scarry_ref, carried=carried)
        _mixer_output(b, *mid[b], unperm_ref, cc_ref, d_ref, gluw_ref, glub_ref, wout_ref, h_ref, bu_ref)


def _mixer_project(b, x_ref, gm_ref, perm_ref, win_ref, cw_ref, cb_ref, cinit_ref, bb_ref, cout_ref, bu_ref,
                   ccarry_ref, *, carried):
    tm = TM_MIX
    x = x_ref[b]
    hn = _rms(x, gm_ref[...]).astype(BF16)
    hn = _dot(perm_ref[...], hn).astype(BF16)
    proj = _dot(hn, win_ref[...])
    g_b = proj[:, :CONV_CH]
    cin = proj[:, CONV_CH:2 * CONV_CH] * proj[:, 2 * CONV_CH:3 * CONV_CH]
    u = proj[:, 3 * CONV_CH:]
    ub = u.astype(BF16)
    for half in range(2):
        bu_ref[b, :, half * S5_STATE:(half + 1) * S5_STATE] = _dot(ub[:, half * HALF_CH:(half + 1) * HALF_CH],
                                                                  bb_ref[half])

    last, before_last = cin[tm - 8:tm], cin[tm - 16:tm - 8]
    if carried:
        first_chunk = lax.broadcasted_iota(jnp.int32, (8, CONV_CH), 0) == 0
        newer = jnp.where(first_chunk, ccarry_ref[1, b:b + 1, :], pltpu.roll(last, 1, 0))
        older = jnp.where(first_chunk, ccarry_ref[0, b:b + 1, :], pltpu.roll(before_last, 1, 0))
        for i, rows in enumerate((before_last, last)):
            ccarry_ref[i, b:b + 1, :] = rows[7:8]
            cout_ref[i, b:b + 1, :] = rows[7:8]
    else:
        older, newer = cinit_ref[0], cinit_ref[1]
        cout_ref[0] = before_last
        cout_ref[1] = last
    ext = jnp.concatenate([older, newer, cin], axis=0)
    cw = cw_ref[...]
    out_a = g_b * (cw[0:1] * ext[:tm] + cw[1:2] * ext[8:tm + 8] + cw[2:3] * cin + cb_ref[...])
    return x, u, out_a


def _mixer_recur(b, sinit_ref, ab_ref, hop_ref, pw_ref, sout_ref, bu_ref, scarry_ref, *, carried):
    for c in range(S5_STATE // SCAN_LANES):
        half, within = divmod(c * SCAN_LANES, HALF_STATE)
        nat = slice(c * SCAN_LANES, (c + 1) * SCAN_LANES)
        re_cols = slice(half * S5_STATE + within, half * S5_STATE + within + SCAN_LANES)
        im_cols = slice(re_cols.start + HALF_STATE, re_cols.stop + HALF_STATE)
        ar, ai = ab_ref[0, :, nat], ab_ref[1, :, nat]
        if carried:
            sr = si = jnp.zeros((8, SCAN_LANES), F32)
        else:
            sr, si = sinit_ref[:, re_cols], sinit_ref[:, im_cols]
        for g in range(CHUNK_ROWS):
            rows = slice(8 * g, 8 * g + 8)
            sr, si = (ar * sr - ai * si + bu_ref[b, rows, re_cols], ar * si + ai * sr + bu_ref[b, rows, im_cols])
            bu_ref[b, rows, re_cols] = sr
            bu_ref[b, rows, im_cols] = si
        if not carried:
            sout_ref[:, re_cols] = sr
            sout_ref[:, im_cols] = si
            continue
        first_chunk = lax.broadcasted_iota(jnp.int32, (8, SCAN_LANES), 0) == 0
        er = jnp.where(first_chunk, scarry_ref[b:b + 1, re_cols], pltpu.roll(sr, 1, 0))
        ei = jnp.where(first_chunk, scarry_ref[b:b + 1, im_cols], pltpu.roll(si, 1, 0))
        for k, sh in enumerate((1, 2, 4)):
            wr, wi = hop_ref[2 * k, :, nat], hop_ref[2 * k + 1, :, nat]
            tr, ti = pltpu.roll(er, sh, 0), pltpu.roll(ei, sh, 0)
            er, ei = er + wr * tr - wi * ti, ei + wr * ti + wi * tr
        wr, wi = hop_ref[6, :, nat], hop_ref[7, :, nat]
        fr, fi = wr * er - wi * ei + sr, wr * ei + wi * er + si
        for ref in (scarry_ref, sout_ref):
            ref[b:b + 1, re_cols] = fr[7:8]
            ref[b:b + 1, im_cols] = fi[7:8]
        for g in range(CHUNK_ROWS):
            rows = slice(8 * g, 8 * g + 8)
            pr, pi = pw_ref[0, g, :, nat], pw_ref[1, g, :, nat]
            bu_ref[b, rows, re_cols] += pr * er - pi * ei
            bu_ref[b, rows, im_cols] += pr * ei + pi * er


def _mixer_output(b, x, u, out_a, unperm_ref, cc_ref, d_ref, gluw_ref, glub_ref, wout_ref, h_ref, bu_ref):
    y = jnp.concatenate([_dot(bu_ref[b, :, half * S5_STATE:(half + 1) * S5_STATE].astype(BF16), cc_ref[half])
                         for half in range(2)], axis=-1) + d_ref[...] * u
    z = jax.nn.gelu(y)
    out_b = z * jax.nn.sigmoid(_dot(z.astype(BF16), gluw_ref[...]) + glub_ref[...])
    mix = jnp.concatenate([out_a, out_b], axis=-1).astype(BF16)
    mix = _dot(unperm_ref[...], mix).astype(BF16)
    h_ref[b] = x + _dot(mix, wout_ref[...])


def _even_mixer(x, gm, w_in, cw, cb, cinit, sinit, ab, hop, pw, bb, cc, d, glu_w, glu_b, w_out, *, carried):
    nb, rows = x.shape[0], x.shape[1]
    tm = TM_MIX
    chunks = tm // CHUNK_ROWS
    assert chunks == 8 and rows % tm == 0 and (carried or nb == 1)
    per_step = nb if carried else chunks
    tiles_per_seq = rows // tm if carried else 1
    nseq = sinit.shape[0]
    t_of = np.arange(tm)
    perm = np.zeros((tm, tm), np.float32)
    perm[8 * (t_of % CHUNK_ROWS) + t_of // CHUNK_ROWS, t_of] = 1.0
    kern = functools.partial(_even_mixer_kernel, carried=carried, tiles_per_seq=tiles_per_seq)
    return pl.pallas_call(
        kern,
        grid=(rows // tm,),
        in_specs=[
            pl.BlockSpec((nb, tm, D_MODEL), lambda t: (0, t, 0)),
            _const_spec((1, D_MODEL)),
            _const_spec((tm, tm)),
            _const_spec((tm, tm)),
            _const_spec((D_MODEL, 4 * CONV_CH)),
            _const_spec((3, CONV_CH)),
            _const_spec((1, CONV_CH)),
            pl.BlockSpec((2, per_step, CONV_CH), lambda t: (0, t // tiles_per_seq, 0)),
            pl.BlockSpec((per_step, 2 * S5_STATE), lambda t: (t // tiles_per_seq, 0)),
            _const_spec((2, 8, S5_STATE)),
            _const_spec((8, 8, S5_STATE)),
            _const_spec((2, CHUNK_ROWS, 8, S5_STATE)),
            _const_spec((2, HALF_CH, S5_STATE)),
            _const_spec((2, S5_STATE, HALF_CH)),
            _const_spec((1, S5_CH)),
            _const_spec((S5_CH, S5_CH)),
            _const_spec((1, S5_CH)),
            _const_spec((D_MODEL, D_MODEL)),
        ],
        out_specs=[
            pl.BlockSpec((nb, tm, D_MODEL), lambda t: (0, t, 0)),
            pl.BlockSpec((2, per_step, CONV_CH), lambda t: (0, t // tiles_per_seq, 0)),
            pl.BlockSpec((per_step, 2 * S5_STATE), lambda t: (t // tiles_per_seq, 0)),
        ],
        out_shape=[
            jax.ShapeDtypeStruct((nb, rows, D_MODEL), F32),
            jax.ShapeDtypeStruct((2, nseq, CONV_CH), F32),
            jax.ShapeDtypeStruct((nseq, 2 * S5_STATE), F32),
        ],
        scratch_shapes=[
            pltpu.VMEM((nb, tm, 2 * S5_STATE), F32),
            pltpu.VMEM((2, nb, CONV_CH), F32),
            pltpu.VMEM((nb, 2 * S5_STATE), F32),
        ],
        compiler_params=_params("arbitrary"),
        name="even_mixer",
    )(x, gm, jnp.asarray(perm, BF16), jnp.asarray(perm.T, BF16), w_in, cw, cb, cinit, sinit, ab, hop, pw, bb, cc, d,
      glu_w, glu_b, w_out)


def _ffn_kernel(x_ref, g_ref, w1_ref, w3_ref, w2_ref, o_ref):
    x = x_ref[...]
    hn = _rms(x, g_ref[...]).astype(BF16)
    a = _dot(hn, w1_ref[...])
    b = _dot(hn, w3_ref[...])
    o_ref[...] = x + _dot((a * jax.nn.sigmoid(a) * b).astype(BF16), w2_ref[...])


def _ffn(x, g, w1, w3, w2, *, tm):
    rows = x.shape[0]
    return pl.pallas_call(
        _ffn_kernel,
        grid=(rows // tm,),
        in_specs=[
            pl.BlockSpec((tm, D_MODEL), lambda t: (t, 0)),
            _const_spec((1, D_MODEL)),
            _const_spec((D_MODEL, D_FF)),
            _const_spec((D_MODEL, D_FF)),
            _const_spec((D_FF, D_MODEL)),
        ],
        out_specs=pl.BlockSpec((tm, D_MODEL), lambda t: (t, 0)),
        out_shape=jax.ShapeDtypeStruct((rows, D_MODEL), F32),
        compiler_params=_params("parallel"),
        name="ffn",
    )(x, g, w1, w3, w2)


def _qkv_kernel(x_ref, g_ref, w_ref, b_ref, q_ref, k_ref, v_ref):
    hn = _rms(x_ref[...], g_ref[...]).astype(BF16)
    qkv = _dot(hn, w_ref[...]) + b_ref[...]
    nq = N_HEADS * HEAD_DIM
    q_ref[...] = (qkv[:, :nq] * (HEAD_DIM ** -0.5)).astype(BF16)
    k_ref[...] = qkv[:, nq:nq + KV_DIM]
    v_ref[...] = qkv[:, nq + KV_DIM:]


def _qkv(x, g, w, b, *, tm):
    rows = x.shape[0]
    ncol = (N_HEADS + 2 * N_KV) * HEAD_DIM
    return pl.pallas_call(
        _qkv_kernel,
        grid=(rows // tm,),
        in_specs=[
            pl.BlockSpec((tm, D_MODEL), lambda t: (t, 0)),
            _const_spec((1, D_MODEL)),
            _const_spec((D_MODEL, ncol)),
            _const_spec((1, ncol)),
        ],
        out_specs=[
            pl.BlockSpec((tm, N_HEADS * HEAD_DIM), lambda t: (t, 0)),
            pl.BlockSpec((tm, KV_DIM), lambda t: (t, 0)),
            pl.BlockSpec((tm, KV_DIM), lambda t: (t, 0)),
        ],
        out_shape=[
            jax.ShapeDtypeStruct((rows, N_HEADS * HEAD_DIM), BF16),
            jax.ShapeDtypeStruct((rows, KV_DIM), F32),
            jax.ShapeDtypeStruct((rows, KV_DIM), F32),
        ],
        compiler_params=_params("parallel"),
        name="qkv",
    )(x, g, w, b)


def _attend_groups(q, keys, vals, bias_of_group, sink_of_group, o_scr, row0):
    nq = q.shape[0]
    for kv in range(N_KV):
        kv_cols = slice(kv * HEAD_DIM, (kv + 1) * HEAD_DIM)
        heads = [slice((kv * GQ + g) * HEAD_DIM, (kv * GQ + g + 1) * HEAD_DIM) for g in range(GQ)]
        qs = jnp.concatenate([q[:, cols] for cols in heads], axis=0)
        st = lax.dot_general(keys[:, kv_cols], qs, (((1,), (1,)), ((), ())),
                             preferred_element_type=F32) + bias_of_group(kv)
        sk = sink_of_group(kv)
        m = jnp.maximum(jnp.max(st, axis=0, keepdims=True), sk)
        p = jnp.exp(st - m)
        den = jnp.sum(p, axis=0, keepdims=True) + jnp.exp(sk - m)
        o = lax.dot_general((p * (1.0 / den)).astype(BF16), vals[:, kv_cols], (((0,), (0,)), ((), ())),
                            preferred_element_type=F32)
        for g, cols in enumerate(heads):
            o_scr[row0:row0 + nq, cols] = o[g * nq:(g + 1) * nq]


def _attn_prompt_kernel(q_ref, k_ref, v_ref, bias_ref, sink_ref, x_ref, wo_ref, bo_ref, o_ref, o_scr):
    t = pl.program_id(1)
    for u in range(TQ_PER_STEP):
        tile = t * TQ_PER_STEP + u
        start = pl.multiple_of(tile * TQ, TQ)
        keys = k_ref[0, pl.ds(start, TQ + WINDOW), :]
        vals = v_ref[0, pl.ds(start, TQ + WINDOW), :]
        variant = jnp.minimum(tile, 1)
        _attend_groups(q_ref[u * TQ:(u + 1) * TQ], keys, vals, lambda kv: bias_ref[variant, kv],
                       lambda kv: sink_ref[kv], o_scr, u * TQ)
    o_ref[...] = x_ref[...] + _dot(o_scr[...].astype(BF16), wo_ref[...]) + bo_ref[...]


def _attn_prompt(q, kpad, vpad, bias, sink, x, w_o, b_o):
    nb, lpad = kpad.shape[0], kpad.shape[1]
    tm = TQ_PER_STEP * TQ
    steps = (lpad - WINDOW) // tm
    return pl.pallas_call(
        _attn_prompt_kernel,
        grid=(nb, steps),
        in_specs=[
            pl.BlockSpec((tm, D_MODEL), lambda b, t: (b * steps + t, 0)),
            pl.BlockSpec((1, lpad, KV_DIM), lambda b, t: (b, 0, 0)),
            pl.BlockSpec((1, lpad, KV_DIM), lambda b, t: (b, 0, 0)),
            _const_spec((2, N_KV, TQ + WINDOW, GQ * TQ)),
            _const_spec((N_KV, 1, GQ * TQ)),
            pl.BlockSpec((tm, D_MODEL), lambda b, t: (b * steps + t, 0)),
            _const_spec((D_MODEL, D_MODEL)),
            _const_spec((1, D_MODEL)),
        ],
        out_specs=pl.BlockSpec((tm, D_MODEL), lambda b, t: (b * steps + t, 0)),
        out_shape=jax.ShapeDtypeStruct(x.shape, F32),
        scratch_shapes=[pltpu.VMEM((tm, D_MODEL), F32)],
        compiler_params=_params("parallel", "arbitrary"),
        name="attn_prompt",
    )(q, kpad, vpad, bias, sink, x, w_o, b_o)


def _attn_small_kernel(q_ref, kc_ref, vc_ref, kn_ref, vn_ref, bias_ref, sink_ref, x_ref, wo_ref, bo_ref,
                       o_ref, o_scr):
    step = pl.program_id(0)
    for j in range(SEG_PER_STEP):
        rows = slice(j * SEG, (j + 1) * SEG)
        keys = jnp.concatenate([kc_ref[j], kn_ref[rows]], axis=0).astype(BF16)
        vals = jnp.concatenate([vc_ref[j], vn_ref[rows]], axis=0).astype(BF16)
        variant = jnp.where(step * SEG_PER_STEP + j == META_SEG, 1, 0)
        _attend_groups(q_ref[rows], keys, vals, lambda kv: bias_ref[variant, kv], lambda kv: sink_ref[kv], o_scr,
                       j * SEG)
    o_ref[...] = x_ref[...] + _dot(o_scr[...].astype(BF16), wo_ref[...]) + bo_ref[...]


def _attn_small(q, kc, vc, kn, vn, bias, sink, x, w_o, b_o):
    tm = SEG_PER_STEP * SEG
    steps = x.shape[0] // tm
    cache_blocks = kc.shape[0] // SEG_PER_STEP
    cache_spec = pl.BlockSpec((SEG_PER_STEP, WINDOW, KV_DIM), lambda s: (jnp.minimum(s, cache_blocks - 1), 0, 0))
    return pl.pallas_call(
        _attn_small_kernel,
        grid=(steps,),
        in_specs=[
            pl.BlockSpec((tm, D_MODEL), lambda s: (s, 0)),
            cache_spec,
            cache_spec,
            pl.BlockSpec((tm, KV_DIM), lambda s: (s, 0)),
            pl.BlockSpec((tm, KV_DIM), lambda s: (s, 0)),
            _const_spec((2, N_KV, WINDOW + SEG, GQ * SEG)),
            _const_spec((N_KV, 1, GQ * SEG)),
            pl.BlockSpec((tm, D_MODEL), lambda s: (s, 0)),
            _const_spec((D_MODEL, D_MODEL)),
            _const_spec((1, D_MODEL)),
        ],
        out_specs=pl.BlockSpec((tm, D_MODEL), lambda s: (s, 0)),
        out_shape=jax.ShapeDtypeStruct(x.shape, F32),
        scratch_shapes=[pltpu.VMEM((tm, D_MODEL), F32)],
        compiler_params=_params("parallel"),
        name="attn_small",
    )(q, kc, vc, kn, vn, bias, sink, x, w_o, b_o)


def _moe_kernel(x_ref, g_ref, rwt_ref, rb_ref, w1_ref, w3_ref, w2_ref, gf_ref, o_ref,
                hn_scr, slot_scr, gate_scr, cnt_smem, acc_scr, yall_scr, *, tile_rows):
    e = pl.program_id(1)
    tb = x_ref.shape[0]
    rt = tile_rows[-1]

    @pl.when(e == 0)
    def _route():
        hn = _rms(x_ref[...], g_ref[...]).astype(BF16)
        hn_scr[...] = hn
        logits = lax.dot_general(rwt_ref[...], hn, (((1,), (1,)), ((), ())),
                                 preferred_element_type=F32) + rb_ref[...]
        row = lax.broadcasted_iota(jnp.int32, logits.shape, 0).astype(F32)
        v1 = jnp.max(logits, axis=0, keepdims=True)
        i1 = jnp.min(jnp.where(logits == v1, row, float(N_EXPERTS)), axis=0, keepdims=True)
        rest = jnp.where(row == i1, -jnp.inf, logits)
        v2 = jnp.max(rest, axis=0, keepdims=True)
        i2 = jnp.min(jnp.where(rest == v2, row, float(N_EXPERTS)), axis=0, keepdims=True)
        e2 = jnp.exp(v2 - v1)
        den = 1.0 + e2
        gate_scr[...] = jnp.where(row == i1, 1.0 / den, 0.0) + jnp.where(row == i2, e2 / den, 0.0)
        chosen = (row == i1) | (row == i2)
        ind = jnp.where(chosen, 1.0, 0.0)
        earlier = (lax.broadcasted_iota(jnp.int32, (tb, tb), 0) < lax.broadcasted_iota(jnp.int32, (tb, tb), 1))
        rank = _dot(ind.astype(BF16), jnp.where(earlier, 1.0, 0.0).astype(BF16))
        slot_scr[...] = jnp.where(chosen, rank, -1.0).astype(jnp.int32)
        for k in range(N_EXPERTS):
            cnt_smem[k] = jnp.sum(ind[k:k + 1, :]).astype(jnp.int32)
        acc_scr[...] = jnp.zeros_like(acc_scr)

    slot = slot_scr[pl.ds(e, 1), :]
    gate = gate_scr[pl.ds(e, 1), :]

    def _scatter(onehot, rows):
        return lax.dot_general(onehot, rows, (((0,), (0,)), ((), ())), preferred_element_type=F32)

    def _expert_tile(i, rows=rt):
        packed_row = lax.broadcasted_iota(jnp.int32, (rows, tb), 0) + i * rt
        hit = packed_row == slot
        onehot = jnp.where(hit, 1.0, 0.0).astype(BF16)
        xe = _dot(onehot, hn_scr[...]).astype(BF16)
        a = _dot(xe, w1_ref[0])
        b = _dot(xe, w3_ref[0])
        y = _dot((a * jax.nn.sigmoid(a) * b).astype(BF16), w2_ref[0])
        g_row = jnp.sum(jnp.where(hit, gate, 0.0), axis=1, keepdims=True)
        return onehot, (y * g_row).astype(BF16)

    base = pl.multiple_of(e * rt, 16)
    count = cnt_smem[e]
    for lower, rows in zip((-1,) + tile_rows, tile_rows):
        fits = (count > lower) & (count <= rows) if rows < rt else count > lower

        @pl.when(fits)
        def _first_tile(rows=rows):
            yall_scr[pl.ds(base, rows), :] = _expert_tile(0, rows)[1]
            if rows < rt:
                yall_scr[pl.ds(base + rows, rt - rows), :] = jnp.zeros((rt - rows, D_MODEL), BF16)

    def _overflow_tile(i, carry):
        acc_scr[...] += _scatter(*_expert_tile(i))
        return carry

    lax.fori_loop(1, (cnt_smem[e] + rt - 1) // rt, _overflow_tile, 0)

    @pl.when(e == N_EXPERTS - 1)
    def _finish():
        packed_row = lax.broadcasted_iota(jnp.int32, (rt, tb), 0)
        onehot = jnp.concatenate([jnp.where(packed_row == slot_scr[k:k + 1, :], 1.0, 0.0).astype(BF16)
                                  for k in range(N_EXPERTS)], axis=0)
        o_ref[...] = _rms(x_ref[...] + acc_scr[...] + _scatter(onehot, yall_scr[...]), gf_ref[...])


def _moe_final(x, g, rwt, rb, w1, w3, w2, gf, *, tb, tile_rows):
    rows, rt = x.shape[0], tile_rows[-1]
    return pl.pallas_call(
        functools.partial(_moe_kernel, tile_rows=tile_rows),
        grid=(rows // tb, N_EXPERTS),
        in_specs=[
            pl.BlockSpec((tb, D_MODEL), lambda t, e: (t, 0)),
            _const_spec((1, D_MODEL)),
            _const_spec((N_EXPERTS, D_MODEL)),
            _const_spec((N_EXPERTS, 1)),
            pl.BlockSpec((1, D_MODEL, D_FF_E), lambda t, e: (e, 0, 0)),
            pl.BlockSpec((1, D_MODEL, D_FF_E), lambda t, e: (e, 0, 0)),
            pl.BlockSpec((1, D_FF_E, D_MODEL), lambda t, e: (e, 0, 0)),
            _const_spec((1, D_MODEL)),
        ],
        out_specs=pl.BlockSpec((tb, D_MODEL), lambda t, e: (t, 0)),
        out_shape=jax.ShapeDtypeStruct((rows, D_MODEL), F32),
        scratch_shapes=[
            pltpu.VMEM((tb, D_MODEL), BF16),
            pltpu.VMEM((N_EXPERTS, tb), jnp.int32),
            pltpu.VMEM((N_EXPERTS, tb), F32),
            pltpu.SMEM((N_EXPERTS,), jnp.int32),
            pltpu.VMEM((tb, D_MODEL), F32),
            pltpu.VMEM((N_EXPERTS * rt, D_MODEL), BF16),
        ],
        compiler_params=_params("parallel", "arbitrary"),
        name="moe_final",
    )(x, g, rwt, rb, w1, w3, w2, gf)


def _t5_bucket_np(rel):
    half = N_BUCKETS // 2
    max_exact = half // 2
    ret = np.where(rel > 0, half, 0)
    n = np.abs(rel)
    nf = np.maximum(n, 1).astype(np.float32)
    large = max_exact + (np.log(nf / np.float32(max_exact)) / np.float32(math.log(MAX_DIST / max_exact))
                         * np.float32(half - max_exact)).astype(np.int32)
    large = np.minimum(large, half - 1)
    return ret + np.where(n < max_exact, n, large)


def _bias_variants(table, bucket, keeps):
    nq, nk = bucket.shape
    onehot = jnp.asarray(np.eye(N_BUCKETS, dtype=np.float32)[bucket])
    bias = jnp.einsum('rjb,bh->hjr', onehot, table, precision=lax.Precision.HIGHEST)
    bias = jnp.stack([jnp.where(keep.T[None], bias, NEG) for keep in keeps], axis=0)
    bias = bias.reshape(len(keeps), N_KV, GQ, nk, nq)
    return jnp.transpose(bias, (0, 1, 3, 2, 4)).reshape(len(keeps), N_KV, nk, GQ * nq)


def _lane_sink(sink, nq):
    return jnp.broadcast_to(sink.reshape(N_KV, 1, GQ, 1), (N_KV, 1, GQ, nq)).reshape(N_KV, 1, GQ * nq)


def _prompt_bias(table):
    r = np.arange(TQ)[:, None]
    j = np.arange(TQ + WINDOW)[None, :]
    band = (j - CHUNK * (r // CHUNK) >= 0) & (j - CHUNK * (r // CHUNK) < WINDOW + CHUNK)
    first = band & (j >= WINDOW - N_META)
    return _bias_variants(table, _t5_bucket_np(j - WINDOW - r), [first, band])


def _small_bias(table):
    i = np.arange(SEG)[:, None]
    m = np.arange(WINDOW + SEG)[None, :]
    everything = np.ones((SEG, WINDOW + SEG), bool)
    meta_only = everything & (m >= WINDOW + SEG - N_META)
    return _bias_variants(table, _t5_bucket_np(m - WINDOW - i), [everything, meta_only])


def _pack_state(re, im):
    return jnp.concatenate([re[..., :HALF_STATE], im[..., :HALF_STATE], re[..., HALF_STATE:], im[..., HALF_STATE:]],
                           axis=-1)


def _cmul(ar, ai, br, bi):
    return ar * br - ai * bi, ar * bi + ai * br


def _s5_prepare(a_re, a_im, log_dt, b_re, b_im, c_re, c_im):
    dt = jnp.exp(log_dt)[:, None]
    mag = jnp.exp(a_re * dt)
    ab_re, ab_im = mag * jnp.cos(a_im * dt), mag * jnp.sin(a_im * dt)
    den = a_re * a_re + a_im * a_im
    num_re = ab_re - 1.0
    cf_re = (num_re * a_re + ab_im * a_im) / den
    cf_im = (ab_im * a_re - num_re * a_im) / den
    bb_re = cf_re[..., None] * b_re - cf_im[..., None] * b_im
    bb_im = cf_re[..., None] * b_im + cf_im[..., None] * b_re

    def group_diag(v):
        r, c = v.shape[1], v.shape[2]
        gh = S5_GROUPS // 2
        spread = np.tile(np.eye(c, dtype=np.float32), (1, gh))
        wide = jnp.einsum('hrc,cn->hrn', v.reshape(2, gh * r, c), spread, precision=lax.Precision.HIGHEST)
        keep = (np.arange(gh * r)[:, None] // r) == (np.arange(gh * c)[None, :] // c)
        return jnp.where(keep, wide, 0.0)

    def t(v):
        return jnp.swapaxes(v, 1, 2)

    bb = jnp.concatenate([group_diag(t(bb_re)), group_diag(t(bb_im))], axis=2).astype(BF16)
    cc = jnp.concatenate([group_diag(t(c_re)), -group_diag(t(c_im))], axis=1).astype(BF16)

    pr, pi = ab_re.reshape(1, S5_STATE), ab_im.reshape(1, S5_STATE)
    n = 1
    while n < CHUNK_ROWS:
        qr, qi = _cmul(pr, pi, pr[n - 1:n], pi[n - 1:n])
        pr, pi = jnp.concatenate([pr, qr], axis=0), jnp.concatenate([pi, qi], axis=0)
        n *= 2

    def tile8(v):
        return jnp.broadcast_to(v, (8,) + v.shape[1:])

    ab = jnp.stack([tile8(pr[0:1]), tile8(pi[0:1])])
    sub = np.arange(8)[:, None]
    hop, w = [], (pr[CHUNK_ROWS - 1:], pi[CHUNK_ROWS - 1:])
    for k in (1, 2, 4):
        hop += [jnp.where(sub >= k, w[0], 0.0), jnp.where(sub >= k, w[1], 0.0)]
        if k < 4:
            w = _cmul(*w, *w)
    hop = jnp.stack(hop + [tile8(pr[CHUNK_ROWS - 1:]), tile8(pi[CHUNK_ROWS - 1:])])
    pw = jnp.stack([jnp.broadcast_to(p[:, None, :], (CHUNK_ROWS, 8, S5_STATE)) for p in (pr, pi)])
    return bb, cc, ab, hop, pw


def kernel(x_prompt, x_sample, cache_conv, state_s5_re, state_s5_im, cache_swa_k, cache_swa_v, meta_tokens, rel_bias_table, norm_mix, norm_ffn, norm_final, w_in0, conv_w, conv_b, s5_a_re, s5_a_im, s5_log_dt, s5_b_re, s5_b_im, s5_c_re, s5_c_im, s5_d, s5_glu_w, s5_glu_b, w_out0, ffn_w1, ffn_w3, ffn_w2, w_qkv, b_qkv, attn_sink, w_o, b_o, router_w, router_b, moe_w1, moe_w3, moe_w2):
    nb, seq = x_prompt.shape[0], x_prompt.shape[1]
    nsb = x_sample.shape[0]
    assert x_sample.shape[1] == SEG and nsb == N_SAMPLE_SEG and seq % TM_MIX == 0

    def row(v):
        return v.reshape(1, -1)

    meta_seg = jnp.concatenate([jnp.zeros((SEG - N_META, D_MODEL), F32), meta_tokens], axis=0)
    filler = jnp.zeros(((N_SMALL_SEG - N_SAMPLE_SEG - 1) * SEG, D_MODEL), F32)
    xs = jnp.concatenate([x_sample.reshape(nsb * SEG, D_MODEL), meta_seg, filler], axis=0)
    n_extra = N_SMALL_SEG - N_SAMPLE_SEG

    bb, cc, ab, hop, pw = _s5_prepare(s5_a_re[0], s5_a_im[0], s5_log_dt[0], s5_b_re[0], s5_b_im[0], s5_c_re[0],
                                      s5_c_im[0])
    mixer_w = (row(norm_mix[0]), w_in0[0].astype(BF16), conv_w[0], row(conv_b[0]))
    mixer_w2 = (ab, hop, pw, bb, cc, row(s5_d[0]), s5_glu_w[0].astype(BF16), row(s5_glu_b[0]), w_out0[0].astype(BF16))
    cinit_s = jnp.concatenate([jnp.swapaxes(cache_conv[0], 0, 1), jnp.zeros((2, n_extra, CONV_CH), F32)], axis=1)
    sinit_s = _pack_state(state_s5_re[0].reshape(nsb, S5_STATE), state_s5_im[0].reshape(nsb, S5_STATE))
    sinit_s = jnp.concatenate([sinit_s, jnp.zeros((n_extra, 2 * S5_STATE), F32)], axis=0)
    hs, conv_s, state_s = _even_mixer(xs[None], *mixer_w, cinit_s, sinit_s, *mixer_w2, carried=False)
    hs = hs[0]
    cinit_p = jnp.broadcast_to(conv_s[:, META_SEG:META_SEG + 1], (2, nb, CONV_CH))
    sinit_p = jnp.broadcast_to(state_s[META_SEG:META_SEG + 1], (nb, 2 * S5_STATE))
    hp, conv_p, state_p = _even_mixer(x_prompt, *mixer_w, cinit_p, sinit_p, *mixer_w2, carried=True)
    hp = hp.reshape(nb * seq, D_MODEL)
    ffn_w = (row(norm_ffn[0]), ffn_w1[0].astype(BF16), ffn_w3[0].astype(BF16), ffn_w2[0].astype(BF16))
    hs = _ffn(hs, *ffn_w, tm=256)
    hp = _ffn(hp, *ffn_w, tm=512)

    qkv_w = (row(norm_mix[1]), w_qkv[0].astype(BF16), row(b_qkv[0]))
    qs, ks, vs = _qkv(hs, *qkv_w, tm=256)
    qp, kp, vp = _qkv(hp, *qkv_w, tm=512)
    wo = (w_o[0].astype(BF16), row(b_o[0]))
    hs = _attn_small(qs, cache_swa_k[0].reshape(nsb, WINDOW, KV_DIM), cache_swa_v[0].reshape(nsb, WINDOW, KV_DIM),
                     ks, vs, _small_bias(rel_bias_table), _lane_sink(attn_sink[0], SEG), hs, *wo)

    def padded(new, small):
        meta_rows = small[META_SEG * SEG:(META_SEG + 1) * SEG]
        front = jnp.concatenate([jnp.zeros((WINDOW - SEG, KV_DIM), F32), meta_rows], axis=0)
        front = jnp.broadcast_to(front[None], (nb, WINDOW, KV_DIM))
        return jnp.concatenate([front, new.reshape(nb, seq, KV_DIM)], axis=1).astype(BF16)

    hp = _attn_prompt(qp, padded(kp, ks), padded(vp, vs), _prompt_bias(rel_bias_table),
                      _lane_sink(attn_sink[0], TQ), hp, *wo)
    moe_w = (row(norm_ffn[1]), router_w[0].T.astype(BF16), router_b[0].reshape(N_EXPERTS, 1), moe_w1[0].astype(BF16),
             moe_w3[0].astype(BF16), moe_w2[0].astype(BF16), row(norm_final))
    ys = _moe_final(hs, *moe_w, tb=hs.shape[0], tile_rows=(352,))
    yp = _moe_final(hp, *moe_w, tb=1024, tile_rows=(224, 288, 352))

    n_real = nsb * SEG
    y_prompt = yp.reshape(nb, seq, D_MODEL)
    y_sample = ys[:n_real].reshape(nsb, SEG, D_MODEL)

    def split_state(st, n):
        st = st[:n].reshape(n, 2, 2, HALF_STATE)
        return tuple(st[:, :, ri].reshape(1, n, S5_GROUPS, S5_P) for ri in range(2))

    s5rp, s5ip = split_state(state_p, nb)
    s5rs, s5is = split_state(state_s, nsb)
    kp4 = kp.reshape(nb, seq, KV_DIM)[:, -WINDOW:].reshape(1, nb, WINDOW, N_KV, HEAD_DIM)
    vp4 = vp.reshape(nb, seq, KV_DIM)[:, -WINDOW:].reshape(1, nb, WINDOW, N_KV, HEAD_DIM)
    ks4 = ks[:n_real].reshape(1, nsb, SEG, N_KV, HEAD_DIM)
    vs4 = vs[:n_real].reshape(1, nsb, SEG, N_KV, HEAD_DIM)
    conv_p4 = jnp.swapaxes(conv_p, 0, 1)[None]
    conv_s4 = jnp.swapaxes(conv_s[:, :nsb], 0, 1)[None]
    return (y_prompt, y_sample, conv_p4, conv_s4, s5rp, s5ip, s5rs, s5is, kp4, vp4, ks4, vs4)
```

```python
import functools
import math

import numpy as np
import jax
import jax.numpy as jnp
from jax import lax
from jax.experimental import pallas as pl
from jax.experimental.pallas import tpu as pltpu

F32 = jnp.float32
BF16 = jnp.bfloat16

D_MODEL = 1024
CONV_CH = 512
S5_CH = 512
S5_GROUP = 16
S5_GROUPS = 32
S5_P = 64
S5_STATE = S5_GROUPS * S5_P
HALF_CH = S5_CH // 2
HALF_STATE = S5_STATE // 2
D_FF = 2816
N_HEADS = 16
N_KV = 2
GQ = N_HEADS // N_KV
HEAD_DIM = 64
KV_DIM = N_KV * HEAD_DIM
WINDOW = 128
CHUNK = 64
N_META = 16
N_BUCKETS = 32
MAX_DIST = 128
N_EXPERTS = 8
D_FF_E = 1024
EPS = 1e-6
NEG = -1e30

SEG = 32
N_SAMPLE_SEG = 32
META_SEG = N_SAMPLE_SEG
N_SMALL_SEG = 40
TM_MIX = 256
CHUNK_ROWS = SEG
SCAN_LANES = 512
TQ = 128
TQ_PER_STEP = 8
SEG_PER_STEP = 8
VMEM_LIMIT = 56 * 1024 * 1024


def _const_spec(shape):
    nd = len(shape)
    return pl.BlockSpec(shape, lambda *_: (0,) * nd, pipeline_mode=pl.Buffered(1))


def _params(*sem):
    return pltpu.CompilerParams(dimension_semantics=sem, vmem_limit_bytes=VMEM_LIMIT)


def _rms(x, g):
    return x * lax.rsqrt(jnp.mean(x * x, axis=-1, keepdims=True) + EPS) * g


def _dot(a, b):
    return jnp.dot(a, b, preferred_element_type=F32)


def _even_mixer_kernel(x_ref, gm_ref, perm_ref, unperm_ref, win_ref, cw_ref, cb_ref, cinit_ref, sinit_ref,
                       ab_ref, hop_ref, pw_ref, bb_ref, cc_ref, d_ref, gluw_ref, glub_ref, wout_ref,
                       h_ref, cout_ref, sout_ref, bu_ref, ccarry_ref, scarry_ref, *, carried, tiles_per_seq):
    if carried:
        @pl.when(pl.program_id(0) % tiles_per_seq == 0)
        def _load_initial_state():
            ccarry_ref[...] = cinit_ref[...]
            scarry_ref[...] = sinit_ref[...]

    streams = range(x_ref.shape[0])
    mid = [_mixer_project(b, x_ref, gm_ref, perm_ref, win_ref, cw_ref, cb_ref, cinit_ref, bb_ref, cout_ref, bu_ref,
                          ccarry_ref, carried=carried) for b in streams]
    for b in streams:
        _mixer_recur(b, sinit_ref, ab_ref, hop_ref, pw_ref, sout_ref, bu_ref, scarry_ref, carried=carried)
        _mixer_output(b, *mid[b], unperm_ref, cc_ref, d_ref, gluw_ref, glub_ref, wout_ref, h_ref, bu_ref)


def _mixer_project(b, x_ref, gm_ref, perm_ref, win_ref, cw_ref, cb_ref, cinit_ref, bb_ref, cout_ref, bu_ref,
                   ccarry_ref, *, carried):
    tm = TM_MIX
    x = x_ref[b]
    hn = _rms(x, gm_ref[...]).astype(BF16)
    hn = _dot(perm_ref[...], hn).astype(BF16)
    proj = _dot(hn, win_ref[...])
    g_b = proj[:, :CONV_CH]
    cin = proj[:, CONV_CH:2 * CONV_CH] * proj[:, 2 * CONV_CH:3 * CONV_CH]
    u = proj[:, 3 * CONV_CH:]
    ub = u.astype(BF16)
    for half in range(2):
        bu_ref[b, :, half * S5_STATE:(half + 1) * S5_STATE] = _dot(ub[:, half * HALF_CH:(half + 1) * HALF_CH],
                                                                  bb_ref[half])

    last, before_last = cin[tm - 8:tm], cin[tm - 16:tm - 8]
    if carried:
        first_chunk = lax.broadcasted_iota(jnp.int32, (8, CONV_CH), 0) == 0
        newer = jnp.where(first_chunk, ccarry_ref[1, b:b + 1, :], pltpu.roll(last, 1, 0))
        older = jnp.where(first_chunk, ccarry_ref[0, b:b + 1, :], pltpu.roll(before_last, 1, 0))
        for i, rows in enumerate((before_last, last)):
            ccarry_ref[i, b:b + 1, :] = rows[7:8]
            cout_ref[i, b:b + 1, :] = rows[7:8]
    else:
        older, newer = cinit_ref[0], cinit_ref[1]
        cout_ref[0] = before_last
        cout_ref[1] = last
    ext = jnp.concatenate([older, newer, cin], axis=0)
    cw = cw_ref[...]
    out_a = g_b * (cw[0:1] * ext[:tm] + cw[1:2] * ext[8:tm + 8] + cw[2:3] * cin + cb_ref[...])
    return x, u, out_a


def _mixer_recur(b, sinit_ref, ab_ref, hop_ref, pw_ref, sout_ref, bu_ref, scarry_ref, *, carried):
    for c in range(S5_STATE // SCAN_LANES):
        half, within = divmod(c * SCAN_LANES, HALF_STATE)
        nat = slice(c * SCAN_LANES, (c + 1) * SCAN_LANES)
        re_cols = slice(half * S5_STATE + within, half * S5_STATE + within + SCAN_LANES)
        im_cols = slice(re_cols.start + HALF_STATE, re_cols.stop + HALF_STATE)
        ar, ai = ab_ref[0, :, nat], ab_ref[1, :, nat]
        if carried:
            sr = si = jnp.zeros((8, SCAN_LANES), F32)
        else:
            sr, si = sinit_ref[:, re_cols], sinit_ref[:, im_cols]
        for g in range(CHUNK_ROWS):
            rows = slice(8 * g, 8 * g + 8)
            sr, si = (ar * sr - ai * si + bu_ref[b, rows, re_cols], ar * si + ai * sr + bu_ref[b, rows, im_cols])
            bu_ref[b, rows, re_cols] = sr
            bu_ref[b, rows, im_cols] = si
        if not carried:
            sout_ref[:, re_cols] = sr
            sout_ref[:, im_cols] = si
            continue
        first_chunk = lax.broadcasted_iota(jnp.int32, (8, SCAN_LANES), 0) == 0
        er = jnp.where(first_chunk, scarry_ref[b:b + 1, re_cols], pltpu.roll(sr, 1, 0))
        ei = jnp.where(first_chunk, scarry_ref[b:b + 1, im_cols], pltpu.roll(si, 1, 0))
        for k, sh in enumerate((1, 2, 4)):
            wr, wi = hop_ref[2 * k, :, nat], hop_ref[2 * k + 1, :, nat]
            tr, ti = pltpu.roll(er, sh, 0), pltpu.roll(ei, sh, 0)
            er, ei = er + wr * tr - wi * ti, ei + wr * ti + wi * tr
        wr, wi = hop_ref[6, :, nat], hop_ref[7, :, nat]
        fr, fi = wr * er - wi * ei + sr, wr * ei + wi * er + si
        for ref in (scarry_ref, sout_ref):
            ref[b:b + 1, re_cols] = fr[7:8]
            ref[b:b + 1, im_cols] = fi[7:8]
        for g in range(CHUNK_ROWS):
            rows = slice(8 * g, 8 * g + 8)
            pr, pi = pw_ref[0, g, :, nat], pw_ref[1, g, :, nat]
            bu_ref[b, rows, re_cols] += pr * er - pi * ei
            bu_ref[b, rows, im_cols] += pr * ei + pi * er


def _mixer_output(b, x, u, out_a, unperm_ref, cc_ref, d_ref, gluw_ref, glub_ref, wout_ref, h_ref, bu_ref):
    y = jnp.concatenate([_dot(bu_ref[b, :, half * S5_STATE:(half + 1) * S5_STATE].astype(BF16), cc_ref[half])
                         for half in range(2)], axis=-1) + d_ref[...] * u
    z = jax.nn.gelu(y)
    out_b = z * jax.nn.sigmoid(_dot(z.astype(BF16), gluw_ref[...]) + glub_ref[...])
    mix = jnp.concatenate([out_a, out_b], axis=-1).astype(BF16)
    mix = _dot(unperm_ref[...], mix).astype(BF16)
    h_ref[b] = x + _dot(mix, wout_ref[...])


def _even_mixer(x, gm, w_in, cw, cb, cinit, sinit, ab, hop, pw, bb, cc, d, glu_w, glu_b, w_out, *, carried):
    nb, rows = x.shape[0], x.shape[1]
    tm = TM_MIX
    chunks = tm // CHUNK_ROWS
    assert chunks == 8 and rows % tm == 0 and (carried or nb == 1)
    per_step = nb if carried else chunks
    tiles_per_seq = rows // tm if carried else 1
    nseq = sinit.shape[0]
    t_of = np.arange(tm)
    perm = np.zeros((tm, tm), np.float32)
    perm[8 * (t_of % CHUNK_ROWS) + t_of // CHUNK_ROWS, t_of] = 1.0
    kern = functools.partial(_even_mixer_kernel, carried=carried, tiles_per_seq=tiles_per_seq)
    return pl.pallas_call(
        kern,
        grid=(rows // tm,),
        in_specs=[
            pl.BlockSpec((nb, tm, D_MODEL), lambda t: (0, t, 0)),
            _const_spec((1, D_MODEL)),
            _const_spec((tm, tm)),
            _const_spec((tm, tm)),
            _const_spec((D_MODEL, 4 * CONV_CH)),
            _const_spec((3, CONV_CH)),
            _const_spec((1, CONV_CH)),
            pl.BlockSpec((2, per_step, CONV_CH), lambda t: (0, t // tiles_per_seq, 0)),
            pl.BlockSpec((per_step, 2 * S5_STATE), lambda t: (t // tiles_per_seq, 0)),
            _const_spec((2, 8, S5_STATE)),
            _const_spec((8, 8, S5_STATE)),
            _const_spec((2, CHUNK_ROWS, 8, S5_STATE)),
            _const_spec((2, HALF_CH, S5_STATE)),
            _const_spec((2, S5_STATE, HALF_CH)),
            _const_spec((1, S5_CH)),
            _const_spec((S5_CH, S5_CH)),
            _const_spec((1, S5_CH)),
            _const_spec((D_MODEL, D_MODEL)),
        ],
        out_specs=[
            pl.BlockSpec((nb, tm, D_MODEL), lambda t: (0, t, 0)),
            pl.BlockSpec((2, per_step, CONV_CH), lambda t: (0, t // tiles_per_seq, 0)),
            pl.BlockSpec((per_step, 2 * S5_STATE), lambda t: (t // tiles_per_seq, 0)),
        ],
        out_shape=[
            jax.ShapeDtypeStruct((nb, rows, D_MODEL), F32),
            jax.ShapeDtypeStruct((2, nseq, CONV_CH), F32),
            jax.ShapeDtypeStruct((nseq, 2 * S5_STATE), F32),
        ],
        scratch_shapes=[
            pltpu.VMEM((nb, tm, 2 * S5_STATE), F32),
            pltpu.VMEM((2, nb, CONV_CH), F32),
            pltpu.VMEM((nb, 2 * S5_STATE), F32),
        ],
        compiler_params=_params("arbitrary"),
        name="even_mixer",
    )(x, gm, jnp.asarray(perm, BF16), jnp.asarray(perm.T, BF16), w_in, cw, cb, cinit, sinit, ab, hop, pw, bb, cc, d,
      glu_w, glu_b, w_out)


def _ffn_kernel(x_ref, g_ref, w1_ref, w3_ref, w2_ref, o_ref):
    x = x_ref[...]
    hn = _rms(x, g_ref[...]).astype(BF16)
    a = _dot(hn, w1_ref[...])
    b = _dot(hn, w3_ref[...])
    o_ref[...] = x + _dot((a * jax.nn.sigmoid(a) * b).astype(BF16), w2_ref[...])


def _ffn(x, g, w1, w3, w2, *, tm):
    rows = x.shape[0]
    return pl.pallas_call(
        _ffn_kernel,
        grid=(rows // tm,),
        in_specs=[
            pl.BlockSpec((tm, D_MODEL), lambda t: (t, 0)),
            _const_spec((1, D_MODEL)),
            _const_spec((D_MODEL, D_FF)),
            _const_spec((D_MODEL, D_FF)),
            _const_spec((D_FF, D_MODEL)),
        ],
        out_specs=pl.BlockSpec((tm, D_MODEL), lambda t: (t, 0)),
        out_shape=jax.ShapeDtypeStruct((rows, D_MODEL), F32),
        compiler_params=_params("parallel"),
        name="ffn",
    )(x, g, w1, w3, w2)


def _qkv_kernel(x_ref, g_ref, w_ref, b_ref, q_ref, k_ref, v_ref):
    hn = _rms(x_ref[...], g_ref[...]).astype(BF16)
    qkv = _dot(hn, w_ref[...]) + b_ref[...]
    nq = N_HEADS * HEAD_DIM
    q_ref[...] = (qkv[:, :nq] * (HEAD_DIM ** -0.5)).astype(BF16)
    k_ref[...] = qkv[:, nq:nq + KV_DIM]
    v_ref[...] = qkv[:, nq + KV_DIM:]


def _qkv(x, g, w, b, *, tm):
    rows = x.shape[0]
    ncol = (N_HEADS + 2 * N_KV) * HEAD_DIM
    return pl.pallas_call(
        _qkv_kernel,
        grid=(rows // tm,),
        in_specs=[
            pl.BlockSpec((tm, D_MODEL), lambda t: (t, 0)),
            _const_spec((1, D_MODEL)),
            _const_spec((D_MODEL, ncol)),
            _const_spec((1, ncol)),
        ],
        out_specs=[
            pl.BlockSpec((tm, N_HEADS * HEAD_DIM), lambda t: (t, 0)),
            pl.BlockSpec((tm, KV_DIM), lambda t: (t, 0)),
            pl.BlockSpec((tm, KV_DIM), lambda t: (t, 0)),
        ],
        out_shape=[
            jax.ShapeDtypeStruct((rows, N_HEADS * HEAD_DIM), BF16),
            jax.ShapeDtypeStruct((rows, KV_DIM), F32),
            jax.ShapeDtypeStruct((rows, KV_DIM), F32),
        ],
        compiler_params=_params("parallel"),
        name="qkv",
    )(x, g, w, b)


def _attend_groups(q, keys, vals, bias_of_group, sink_of_group, o_scr, row0):
    nq = q.shape[0]
    for kv in range(N_KV):
        kv_cols = slice(kv * HEAD_DIM, (kv + 1) * HEAD_DIM)
        heads = [slice((kv * GQ + g) * HEAD_DIM, (kv * GQ + g + 1) * HEAD_DIM) for g in range(GQ)]
        qs = jnp.concatenate([q[:, cols] for cols in heads], axis=0)
        st = lax.dot_general(keys[:, kv_cols], qs, (((1,), (1,)), ((), ())),
                             preferred_element_type=F32) + bias_of_group(kv)
        sk = sink_of_group(kv)
        m = jnp.maximum(jnp.max(st, axis=0, keepdims=True), sk)
        p = jnp.exp(st - m)
        den = jnp.sum(p, axis=0, keepdims=True) + jnp.exp(sk - m)
        o = lax.dot_general((p * (1.0 / den)).astype(BF16), vals[:, kv_cols], (((0,), (0,)), ((), ())),
                            preferred_element_type=F32)
        for g, cols in enumerate(heads):
            o_scr[row0:row0 + nq, cols] = o[g * nq:(g + 1) * nq]


def _attn_prompt_kernel(q_ref, k_ref, v_ref, bias_ref, sink_ref, x_ref, wo_ref, bo_ref, o_ref, o_scr):
    t = pl.program_id(1)
    for u in range(TQ_PER_STEP):
        tile = t * TQ_PER_STEP + u
        start = pl.multiple_of(tile * TQ, TQ)
        keys = k_ref[0, pl.ds(start, TQ + WINDOW), :]
        vals = v_ref[0, pl.ds(start, TQ + WINDOW), :]
        variant = jnp.minimum(tile, 1)
        _attend_groups(q_ref[u * TQ:(u + 1) * TQ], keys, vals, lambda kv: bias_ref[variant, kv],
                       lambda kv: sink_ref[kv], o_scr, u * TQ)
    o_ref[...] = x_ref[...] + _dot(o_scr[...].astype(BF16), wo_ref[...]) + bo_ref[...]


def _attn_prompt(q, kpad, vpad, bias, sink, x, w_o, b_o):
    nb, lpad = kpad.shape[0], kpad.shape[1]
    tm = TQ_PER_STEP * TQ
    steps = (lpad - WINDOW) // tm
    return pl.pallas_call(
        _attn_prompt_kernel,
        grid=(nb, steps),
        in_specs=[
            pl.BlockSpec((tm, D_MODEL), lambda b, t: (b * steps + t, 0)),
            pl.BlockSpec((1, lpad, KV_DIM), lambda b, t: (b, 0, 0)),
            pl.BlockSpec((1, lpad, KV_DIM), lambda b, t: (b, 0, 0)),
            _const_spec((2, N_KV, TQ + WINDOW, GQ * TQ)),
            _const_spec((N_KV, 1, GQ * TQ)),
            pl.BlockSpec((tm, D_MODEL), lambda b, t: (b * steps + t, 0)),
            _const_spec((D_MODEL, D_MODEL)),
            _const_spec((1, D_MODEL)),
        ],
        out_specs=pl.BlockSpec((tm, D_MODEL), lambda b, t: (b * steps + t, 0)),
        out_shape=jax.ShapeDtypeStruct(x.shape, F32),
        scratch_shapes=[pltpu.VMEM((tm, D_MODEL), F32)],
        compiler_params=_params("parallel", "arbitrary"),
        name="attn_prompt",
    )(q, kpad, vpad, bias, sink, x, w_o, b_o)


def _attn_small_kernel(q_ref, kc_ref, vc_ref, kn_ref, vn_ref, bias_ref, sink_ref, x_ref, wo_ref, bo_ref,
                       o_ref, o_scr):
    step = pl.program_id(0)
    for j in range(SEG_PER_STEP):
        rows = slice(j * SEG, (j + 1) * SEG)
        keys = jnp.concatenate([kc_ref[j], kn_ref[rows]], axis=0).astype(BF16)
        vals = jnp.concatenate([vc_ref[j], vn_ref[rows]], axis=0).astype(BF16)
        variant = jnp.where(step * SEG_PER_STEP + j == META_SEG, 1, 0)
        _attend_groups(q_ref[rows], keys, vals, lambda kv: bias_ref[variant, kv], lambda kv: sink_ref[kv], o_scr,
                       j * SEG)
    o_ref[...] = x_ref[...] + _dot(o_scr[...].astype(BF16), wo_ref[...]) + bo_ref[...]


def _attn_small(q, kc, vc, kn, vn, bias, sink, x, w_o, b_o):
    tm = SEG_PER_STEP * SEG
    steps = x.shape[0] // tm
    cache_blocks = kc.shape[0] // SEG_PER_STEP
    cache_spec = pl.BlockSpec((SEG_PER_STEP, WINDOW, KV_DIM), lambda s: (jnp.minimum(s, cache_blocks - 1), 0, 0))
    return pl.pallas_call(
        _attn_small_kernel,
        grid=(steps,),
        in_specs=[
            pl.BlockSpec((tm, D_MODEL), lambda s: (s, 0)),
            cache_spec,
            cache_spec,
            pl.BlockSpec((tm, KV_DIM), lambda s: (s, 0)),
            pl.BlockSpec((tm, KV_DIM), lambda s: (s, 0)),
            _const_spec((2, N_KV, WINDOW + SEG, GQ * SEG)),
            _const_spec((N_KV, 1, GQ * SEG)),
            pl.BlockSpec((tm, D_MODEL), lambda s: (s, 0)),
            _const_spec((D_MODEL, D_MODEL)),
            _const_spec((1, D_MODEL)),
        ],
        out_specs=pl.BlockSpec((tm, D_MODEL), lambda s: (s, 0)),
        out_shape=jax.ShapeDtypeStruct(x.shape, F32),
        scratch_shapes=[pltpu.VMEM((tm, D_MODEL), F32)],
        compiler_params=_params("parallel"),
        name="attn_small",
    )(q, kc, vc, kn, vn, bias, sink, x, w_o, b_o)


def _moe_kernel(x_ref, g_ref, rwt_ref, rb_ref, w1_ref, w3_ref, w2_ref, gf_ref, o_ref,
                hn_scr, slot_scr, gate_scr, cnt_smem, acc_scr, yall_scr, *, subs, tile_rows):
    e = pl.program_id(1)
    tb = x_ref.shape[0]
    ts = tb // subs
    cap = tile_rows[-1]
    sub_cols = [slice(s * ts, (s + 1) * ts) for s in range(subs)]

    @pl.when(e == 0)
    def _route():
        hn = _rms(x_ref[...], g_ref[...]).astype(BF16)
        hn_scr[...] = hn
        logits = lax.dot_general(rwt_ref[...], hn, (((1,), (1,)), ((), ())),
                                 preferred_element_type=F32) + rb_ref[...]
        row = lax.broadcasted_iota(jnp.int32, logits.shape, 0).astype(F32)
        v1 = jnp.max(logits, axis=0, keepdims=True)
        i1 = jnp.min(jnp.where(logits == v1, row, float(N_EXPERTS)), axis=0, keepdims=True)
        rest = jnp.where(row == i1, -jnp.inf, logits)
        v2 = jnp.max(rest, axis=0, keepdims=True)
        i2 = jnp.min(jnp.where(rest == v2, row, float(N_EXPERTS)), axis=0, keepdims=True)
        e2 = jnp.exp(v2 - v1)
        den = 1.0 + e2
        gate_scr[...] = jnp.where(row == i1, 1.0 / den, 0.0) + jnp.where(row == i2, e2 / den, 0.0)
        chosen = (row == i1) | (row == i2)
        ind = jnp.where(chosen, 1.0, 0.0)
        before = lax.broadcasted_iota(jnp.int32, (ts, ts), 0) < lax.broadcasted_iota(jnp.int32, (ts, ts), 1)
        before = jnp.where(before, 1.0, 0.0).astype(BF16)
        for s, cols in enumerate(sub_cols):
            rank = _dot(ind[:, cols].astype(BF16), before)
            slot_scr[:, cols] = jnp.where(chosen[:, cols], rank, -1.0).astype(jnp.int32)
            for k in range(N_EXPERTS):
                cnt_smem[k * subs + s] = jnp.sum(ind[k:k + 1, cols]).astype(jnp.int32)
        acc_scr[...] = jnp.zeros_like(acc_scr)

    def _scatter(onehot, rows):
        return lax.dot_general(onehot, rows, (((0,), (0,)), ((), ())), preferred_element_type=F32)

    def _expert_tile(i, rows=cap):
        packed_row = lax.broadcasted_iota(jnp.int32, (rows, ts), 0) + i * cap
        hits = [packed_row == slot_scr[pl.ds(e, 1), cols] for cols in sub_cols]
        onehots = [jnp.where(hit, 1.0, 0.0).astype(BF16) for hit in hits]
        xe = jnp.concatenate([_dot(onehot, hn_scr[cols, :]) for onehot, cols in zip(onehots, sub_cols)],
                             axis=0).astype(BF16)
        a = _dot(xe, w1_ref[0])
        b = _dot(xe, w3_ref[0])
        y = _dot((a * jax.nn.sigmoid(a) * b).astype(BF16), w2_ref[0])
        g_row = jnp.concatenate([jnp.sum(jnp.where(hit, gate_scr[pl.ds(e, 1), cols], 0.0), axis=1, keepdims=True)
                                 for hit, cols in zip(hits, sub_cols)], axis=0)
        ys = (y * g_row).astype(BF16)
        return onehots, [ys[s * rows:(s + 1) * rows] for s in range(subs)]

    base = pl.multiple_of(e * cap, 16)
    count = cnt_smem[e * subs]
    for s in range(1, subs):
        count = jnp.maximum(count, cnt_smem[e * subs + s])
    for lower, rows in zip((-1,) + tile_rows, tile_rows):
        fits = (count > lower) & (count <= rows) if rows < cap else count > lower

        @pl.when(fits)
        def _first_tile(rows=rows):
            for s, ys in enumerate(_expert_tile(0, rows)[1]):
                yall_scr[s, pl.ds(base, rows), :] = ys
                if rows < cap:
                    yall_scr[s, pl.ds(base + rows, cap - rows), :] = jnp.zeros((cap - rows, D_MODEL), BF16)

    def _overflow_tile(i, carry):
        onehots, yss = _expert_tile(i)
        for onehot, ys, cols in zip(onehots, yss, sub_cols):
            acc_scr[cols, :] += _scatter(onehot, ys)
        return carry

    lax.fori_loop(1, (count + cap - 1) // cap, _overflow_tile, 0)

    @pl.when(e == N_EXPERTS - 1)
    def _finish():
        packed_row = lax.broadcasted_iota(jnp.int32, (cap, ts), 0)
        for s, cols in enumerate(sub_cols):
            onehot = jnp.concatenate([jnp.where(packed_row == slot_scr[k:k + 1, cols], 1.0, 0.0).astype(BF16)
                                      for k in range(N_EXPERTS)], axis=0)
            o_ref[cols, :] = _rms(x_ref[cols, :] + acc_scr[cols, :] + _scatter(onehot, yall_scr[s]), gf_ref[...])


def _moe_final(x, g, rwt, rb, w1, w3, w2, gf, *, tb, subs, tile_rows):
    rows, cap = x.shape[0], tile_rows[-1]
    return pl.pallas_call(
        functools.partial(_moe_kernel, subs=subs, tile_rows=tile_rows),
        grid=(rows // tb, N_EXPERTS),
        in_specs=[
            pl.BlockSpec((tb, D_MODEL), lambda t, e: (t, 0)),
            _const_spec((1, D_MODEL)),
            _const_spec((N_EXPERTS, D_MODEL)),
            _const_spec((N_EXPERTS, 1)),
            pl.BlockSpec((1, D_MODEL, D_FF_E), lambda t, e: (e, 0, 0)),
            pl.BlockSpec((1, D_MODEL, D_FF_E), lambda t, e: (e, 0, 0)),
            pl.BlockSpec((1, D_FF_E, D_MODEL), lambda t, e: (e, 0, 0)),
            _const_spec((1, D_MODEL)),
        ],
        out_specs=pl.BlockSpec((tb, D_MODEL), lambda t, e: (t, 0)),
        out_shape=jax.ShapeDtypeStruct((rows, D_MODEL), F32),
        scratch_shapes=[
            pltpu.VMEM((tb, D_MODEL), BF16),
            pltpu.VMEM((N_EXPERTS, tb), jnp.int32),
            pltpu.VMEM((N_EXPERTS, tb), F32),
            pltpu.SMEM((N_EXPERTS * subs,), jnp.int32),
            pltpu.VMEM((tb, D_MODEL), F32),
            pltpu.VMEM((subs, N_EXPERTS * cap, D_MODEL), BF16),
        ],
        compiler_params=_params("parallel", "arbitrary"),
        name="moe_final",
    )(x, g, rwt, rb, w1, w3, w2, gf)


def _t5_bucket_np(rel):
    half = N_BUCKETS // 2
    max_exact = half // 2
    ret = np.where(rel > 0, half, 0)
    n = np.abs(rel)
    nf = np.maximum(n, 1).astype(np.float32)
    large = max_exact + (np.log(nf / np.float32(max_exact)) / np.float32(math.log(MAX_DIST / max_exact))
                         * np.float32(half - max_exact)).astype(np.int32)
    large = np.minimum(large, half - 1)
    return ret + np.where(n < max_exact, n, large)


def _bias_variants(table, bucket, keeps):
    nq, nk = bucket.shape
    onehot = jnp.asarray(np.eye(N_BUCKETS, dtype=np.float32)[bucket])
    bias = jnp.einsum('rjb,bh->hjr', onehot, table, precision=lax.Precision.HIGHEST)
    bias = jnp.stack([jnp.where(keep.T[None], bias, NEG) for keep in keeps], axis=0)
    bias = bias.reshape(len(keeps), N_KV, GQ, nk, nq)
    return jnp.transpose(bias, (0, 1, 3, 2, 4)).reshape(len(keeps), N_KV, nk, GQ * nq)


def _lane_sink(sink, nq):
    return jnp.broadcast_to(sink.reshape(N_KV, 1, GQ, 1), (N_KV, 1, GQ, nq)).reshape(N_KV, 1, GQ * nq)


def _prompt_bias(table):
    r = np.arange(TQ)[:, None]
    j = np.arange(TQ + WINDOW)[None, :]
    band = (j - CHUNK * (r // CHUNK) >= 0) & (j - CHUNK * (r // CHUNK) < WINDOW + CHUNK)
    first = band & (j >= WINDOW - N_META)
    return _bias_variants(table, _t5_bucket_np(j - WINDOW - r), [first, band])


def _small_bias(table):
    i = np.arange(SEG)[:, None]
    m = np.arange(WINDOW + SEG)[None, :]
    everything = np.ones((SEG, WINDOW + SEG), bool)
    meta_only = everything & (m >= WINDOW + SEG - N_META)
    return _bias_variants(table, _t5_bucket_np(m - WINDOW - i), [everything, meta_only])


def _pack_state(re, im):
    return jnp.concatenate([re[..., :HALF_STATE], im[..., :HALF_STATE], re[..., HALF_STATE:], im[..., HALF_STATE:]],
                           axis=-1)


def _cmul(ar, ai, br, bi):
    return ar * br - ai * bi, ar * bi + ai * br


def _s5_prepare(a_re, a_im, log_dt, b_re, b_im, c_re, c_im):
    dt = jnp.exp(log_dt)[:, None]
    mag = jnp.exp(a_re * dt)
    ab_re, ab_im = mag * jnp.cos(a_im * dt), mag * jnp.sin(a_im * dt)
    den = a_re * a_re + a_im * a_im
    num_re = ab_re - 1.0
    cf_re = (num_re * a_re + ab_im * a_im) / den
    cf_im = (ab_im * a_re - num_re * a_im) / den
    bb_re = cf_re[..., None] * b_re - cf_im[..., None] * b_im
    bb_im = cf_re[..., None] * b_im + cf_im[..., None] * b_re

    def group_diag(v):
        r, c = v.shape[1], v.shape[2]
        gh = S5_GROUPS // 2
        spread = np.tile(np.eye(c, dtype=np.float32), (1, gh))
        wide = jnp.einsum('hrc,cn->hrn', v.reshape(2, gh * r, c), spread, precision=lax.Precision.HIGHEST)
        keep = (np.arange(gh * r)[:, None] // r) == (np.arange(gh * c)[None, :] // c)
        return jnp.where(keep, wide, 0.0)

    def t(v):
        return jnp.swapaxes(v, 1, 2)

    bb = jnp.concatenate([group_diag(t(bb_re)), group_diag(t(bb_im))], axis=2).astype(BF16)
    cc = jnp.concatenate([group_diag(t(c_re)), -group_diag(t(c_im))], axis=1).astype(BF16)

    pr, pi = ab_re.reshape(1, S5_STATE), ab_im.reshape(1, S5_STATE)
    n = 1
    while n < CHUNK_ROWS:
        qr, qi = _cmul(pr, pi, pr[n - 1:n], pi[n - 1:n])
        pr, pi = jnp.concatenate([pr, qr], axis=0), jnp.concatenate([pi, qi], axis=0)
        n *= 2

    def tile8(v):
        return jnp.broadcast_to(v, (8,) + v.shape[1:])

    ab = jnp.stack([tile8(pr[0:1]), tile8(pi[0:1])])
    sub = np.arange(8)[:, None]
    hop, w = [], (pr[CHUNK_ROWS - 1:], pi[CHUNK_ROWS - 1:])
    for k in (1, 2, 4):
        hop += [jnp.where(sub >= k, w[0], 0.0), jnp.where(sub >= k, w[1], 0.0)]
        if k < 4:
            w = _cmul(*w, *w)
    hop = jnp.stack(hop + [tile8(pr[CHUNK_ROWS - 1:]), tile8(pi[CHUNK_ROWS - 1:])])
    pw = jnp.stack([jnp.broadcast_to(p[:, None, :], (CHUNK_ROWS, 8, S5_STATE)) for p in (pr, pi)])
    return bb, cc, ab, hop, pw


def kernel(x_prompt, x_sample, cache_conv, state_s5_re, state_s5_im, cache_swa_k, cache_swa_v, meta_tokens, rel_bias_table, norm_mix, norm_ffn, norm_final, w_in0, conv_w, conv_b, s5_a_re, s5_a_im, s5_log_dt, s5_b_re, s5_b_im, s5_c_re, s5_c_im, s5_d, s5_glu_w, s5_glu_b, w_out0, ffn_w1, ffn_w3, ffn_w2, w_qkv, b_qkv, attn_sink, w_o, b_o, router_w, router_b, moe_w1, moe_w3, moe_w2):
    nb, seq = x_prompt.shape[0], x_prompt.shape[1]
    nsb = x_sample.shape[0]
    assert x_sample.shape[1] == SEG and nsb == N_SAMPLE_SEG and seq % TM_MIX == 0

    def row(v):
        return v.reshape(1, -1)

    meta_seg = jnp.concatenate([jnp.zeros((SEG - N_META, D_MODEL), F32), meta_tokens], axis=0)
    filler = jnp.zeros(((N_SMALL_SEG - N_SAMPLE_SEG - 1) * SEG, D_MODEL), F32)
    xs = jnp.concatenate([x_sample.reshape(nsb * SEG, D_MODEL), meta_seg, filler], axis=0)
    n_extra = N_SMALL_SEG - N_SAMPLE_SEG

    bb, cc, ab, hop, pw = _s5_prepare(s5_a_re[0], s5_a_im[0], s5_log_dt[0], s5_b_re[0], s5_b_im[0], s5_c_re[0],
                                      s5_c_im[0])
    mixer_w = (row(norm_mix[0]), w_in0[0].astype(BF16), conv_w[0], row(conv_b[0]))
    mixer_w2 = (ab, hop, pw, bb, cc, row(s5_d[0]), s5_glu_w[0].astype(BF16), row(s5_glu_b[0]), w_out0[0].astype(BF16))
    cinit_s = jnp.concatenate([jnp.swapaxes(cache_conv[0], 0, 1), jnp.zeros((2, n_extra, CONV_CH), F32)], axis=1)
    sinit_s = _pack_state(state_s5_re[0].reshape(nsb, S5_STATE), state_s5_im[0].reshape(nsb, S5_STATE))
    sinit_s = jnp.concatenate([sinit_s, jnp.zeros((n_extra, 2 * S5_STATE), F32)], axis=0)
    hs, conv_s, state_s = _even_mixer(xs[None], *mixer_w, cinit_s, sinit_s, *mixer_w2, carried=False)
    hs = hs[0]
    cinit_p = jnp.broadcast_to(conv_s[:, META_SEG:META_SEG + 1], (2, nb, CONV_CH))
    sinit_p = jnp.broadcast_to(state_s[META_SEG:META_SEG + 1], (nb, 2 * S5_STATE))
    hp, conv_p, state_p = _even_mixer(x_prompt, *mixer_w, cinit_p, sinit_p, *mixer_w2, carried=True)
    hp = hp.reshape(nb * seq, D_MODEL)
    ffn_w = (row(norm_ffn[0]), ffn_w1[0].astype(BF16), ffn_w3[0].astype(BF16), ffn_w2[0].astype(BF16))
    hs = _ffn(hs, *ffn_w, tm=256)
    hp = _ffn(hp, *ffn_w, tm=512)

    qkv_w = (row(norm_mix[1]), w_qkv[0].astype(BF16), row(b_qkv[0]))
    qs, ks, vs = _qkv(hs, *qkv_w, tm=256)
    qp, kp, vp = _qkv(hp, *qkv_w, tm=512)
    wo = (w_o[0].astype(BF16), row(b_o[0]))
    hs = _attn_small(qs, cache_swa_k[0].reshape(nsb, WINDOW, KV_DIM), cache_swa_v[0].reshape(nsb, WINDOW, KV_DIM),
                     ks, vs, _small_bias(rel_bias_table), _lane_sink(attn_sink[0], SEG), hs, *wo)

    def padded(new, small):
        meta_rows = small[META_SEG * SEG:(META_SEG + 1) * SEG]
        front = jnp.concatenate([jnp.zeros((WINDOW - SEG, KV_DIM), F32), meta_rows], axis=0)
        front = jnp.broadcast_to(front[None], (nb, WINDOW, KV_DIM))
        return jnp.concatenate([front, new.reshape(nb, seq, KV_DIM)], axis=1).astype(BF16)

    hp = _attn_prompt(qp, padded(kp, ks), padded(vp, vs), _prompt_bias(rel_bias_table),
                      _lane_sink(attn_sink[0], TQ), hp, *wo)
    moe_w = (row(norm_ffn[1]), router_w[0].T.astype(BF16), router_b[0].reshape(N_EXPERTS, 1), moe_w1[0].astype(BF16),
             moe_w3[0].astype(BF16), moe_w2[0].astype(BF16), row(norm_final))
    ys = _moe_final(hs, *moe_w, tb=hs.shape[0], subs=1, tile_rows=(352,))
    yp = _moe_final(hp, *moe_w, tb=1024, subs=2, tile_rows=(112, 144, 176, 208))

    n_real = nsb * SEG
    y_prompt = yp.reshape(nb, seq, D_MODEL)
    y_sample = ys[:n_real].reshape(nsb, SEG, D_MODEL)

    def split_state(st, n):
        st = st[:n].reshape(n, 2, 2, HALF_STATE)
        return tuple(st[:, :, ri].reshape(1, n, S5_GROUPS, S5_P) for ri in range(2))

    s5rp, s5ip = split_state(state_p, nb)
    s5rs, s5is = split_state(state_s, nsb)
    kp4 = kp.reshape(nb, seq, KV_DIM)[:, -WINDOW:].reshape(1, nb, WINDOW, N_KV, HEAD_DIM)
    vp4 = vp.reshape(nb, seq, KV_DIM)[:, -WINDOW:].reshape(1, nb, WINDOW, N_KV, HEAD_DIM)
    ks4 = ks[:n_real].reshape(1, nsb, SEG, N_KV, HEAD_DIM)
    vs4 = vs[:n_real].reshape(1, nsb, SEG, N_KV, HEAD_DIM)
    conv_p4 = jnp.swapaxes(conv_p, 0, 1)[None]
    conv_s4 = jnp.swapaxes(conv_s[:, :nsb], 0, 1)[None]
    return (y_prompt, y_sample, conv_p4, conv_s4, s5rp, s5ip, s5rs, s5is, kp4, vp4, ks4, vs4)
```

```python
import functools
import math

import numpy as np
import jax
import jax.numpy as jnp
from jax import lax
from jax.experimental import pallas as pl
from jax.experimental.pallas import tpu as pltpu

F32 = jnp.float32
BF16 = jnp.bfloat16

D_MODEL = 1024
CONV_CH = 512
S5_CH = 512
S5_GROUP = 16
S5_GROUPS = 32
S5_P = 64
S5_STATE = S5_GROUPS * S5_P
HALF_CH = S5_CH // 2
HALF_STATE = S5_STATE // 2
D_FF = 2816
N_HEADS = 16
N_KV = 2
GQ = N_HEADS // N_KV
HEAD_DIM = 64
KV_DIM = N_KV * HEAD_DIM
WINDOW = 128
CHUNK = 64
N_META = 16
N_BUCKETS = 32
MAX_DIST = 128
N_EXPERTS = 8
D_FF_E = 1024
EPS = 1e-6
NEG = -1e30

SEG = 32
N_SAMPLE_SEG = 32
META_SEG = N_SAMPLE_SEG
N_SMALL_SEG = 40
TM_MIX = 256
CHUNK_ROWS = SEG
SCAN_LANES = 512
TQ = 128
TQ_PER_STEP = 8
SEG_PER_STEP = 8
VMEM_LIMIT = 56 * 1024 * 1024


def _const_spec(shape):
    nd = len(shape)
    return pl.BlockSpec(shape, lambda *_: (0,) * nd, pipeline_mode=pl.Buffered(1))


def _params(*sem):
    return pltpu.CompilerParams(dimension_semantics=sem, vmem_limit_bytes=VMEM_LIMIT)


def _rms(x, g):
    return x * lax.rsqrt(jnp.mean(x * x, axis=-1, keepdims=True) + EPS) * g


def _dot(a, b):
    return jnp.dot(a, b, preferred_element_type=F32)


def _even_mixer_kernel(x_ref, gm_ref, perm_ref, unperm_ref, win_ref, cw_ref, cb_ref, cinit_ref, sinit_ref,
                       ab_ref, hop_ref, pw_ref, bb_ref, cc_ref, d_ref, gluw_ref, glub_ref, wout_ref,
                       h_ref, cout_ref, sout_ref, bu_ref, ccarry_ref, scarry_ref, *, carried, tiles_per_seq):
    if carried:
        @pl.when(pl.program_id(0) % tiles_per_seq == 0)
        def _load_initial_state():
            ccarry_ref[...] = cinit_ref[...]
            scarry_ref[...] = sinit_ref[...]

    streams = range(x_ref.shape[0])
    mid = [_mixer_project(b, x_ref, gm_ref, perm_ref, win_ref, cw_ref, cb_ref, cinit_ref, bb_ref, cout_ref, bu_ref,
                          ccarry_ref, carried=carried) for b in streams]
    for b in streams:
        _mixer_recur(b, sinit_ref, ab_ref, hop_ref, pw_ref, sout_ref, bu_ref, scarry_ref, carried=carried)
        _mixer_output(b, *mid[b], unperm_ref, cc_ref, d_ref, gluw_ref, glub_ref, wout_ref, h_ref, bu_ref)


def _mixer_project(b, x_ref, gm_ref, perm_ref, win_ref, cw_ref, cb_ref, cinit_ref, bb_ref, cout_ref, bu_ref,
                   ccarry_ref, *, carried):
    tm = TM_MIX
    x = x_ref[b]
    hn = _rms(x, gm_ref[...]).astype(BF16)
    hn = _dot(perm_ref[...], hn).astype(BF16)
    proj = _dot(hn, win_ref[...])
    g_b = proj[:, :CONV_CH]
    cin = proj[:, CONV_CH:2 * CONV_CH] * proj[:, 2 * CONV_CH:3 * CONV_CH]
    u = proj[:, 3 * CONV_CH:]
    ub = u.astype(BF16)
    for half in range(2):
        bu_ref[b, :, half * S5_STATE:(half + 1) * S5_STATE] = _dot(ub[:, half * HALF_CH:(half + 1) * HALF_CH],
                                                                  bb_ref[half])

    last, before_last = cin[tm - 8:tm], cin[tm - 16:tm - 8]
    if carried:
        first_chunk = lax.broadcasted_iota(jnp.int32, (8, CONV_CH), 0) == 0
        newer = jnp.where(first_chunk, ccarry_ref[1, b:b + 1, :], pltpu.roll(last, 1, 0))
        older = jnp.where(first_chunk, ccarry_ref[0, b:b + 1, :], pltpu.roll(before_last, 1, 0))
        for i, rows in enumerate((before_last, last)):
            ccarry_ref[i, b:b + 1, :] = rows[7:8]
            cout_ref[i, b:b + 1, :] = rows[7:8]
    else:
        older, newer = cinit_ref[0], cinit_ref[1]
        cout_ref[0] = before_last
        cout_ref[1] = last
    ext = jnp.concatenate([older, newer, cin], axis=0)
    cw = cw_ref[...]
    out_a = g_b * (cw[0:1] * ext[:tm] + cw[1:2] * ext[8:tm + 8] + cw[2:3] * cin + cb_ref[...])
    return x, u, out_a


def _mixer_recur(b, sinit_ref, ab_ref, hop_ref, pw_ref, sout_ref, bu_ref, scarry_ref, *, carried):
    for c in range(S5_STATE // SCAN_LANES):
        half, within = divmod(c * SCAN_LANES, HALF_STATE)
        nat = slice(c * SCAN_LANES, (c + 1) * SCAN_LANES)
        re_cols = slice(half * S5_STATE + within, half * S5_STATE + within + SCAN_LANES)
        im_cols = slice(re_cols.start + HALF_STATE, re_cols.stop + HALF_STATE)
        ar, ai = ab_ref[0, :, nat], ab_ref[1, :, nat]
        if carried:
            sr = si = jnp.zeros((8, SCAN_LANES), F32)
        else:
            sr, si = sinit_ref[:, re_cols], sinit_ref[:, im_cols]
        for g in range(CHUNK_ROWS):
            rows = slice(8 * g, 8 * g + 8)
            sr, si = (ar * sr - ai * si + bu_ref[b, rows, re_cols], ar * si + ai * sr + bu_ref[b, rows, im_cols])
            bu_ref[b, rows, re_cols] = sr
            bu_ref[b, rows, im_cols] = si
        if not carried:
            sout_ref[:, re_cols] = sr
            sout_ref[:, im_cols] = si
            continue
        first_chunk = lax.broadcasted_iota(jnp.int32, (8, SCAN_LANES), 0) == 0
        er = jnp.where(first_chunk, scarry_ref[b:b + 1, re_cols], pltpu.roll(sr, 1, 0))
        ei = jnp.where(first_chunk, scarry_ref[b:b + 1, im_cols], pltpu.roll(si, 1, 0))
        for k, sh in enumerate((1, 2, 4)):
            wr, wi = hop_ref[2 * k, :, nat], hop_ref[2 * k + 1, :, nat]
            tr, ti = pltpu.roll(er, sh, 0), pltpu.roll(ei, sh, 0)
            er, ei = er + wr * tr - wi * ti, ei + wr * ti + wi * tr
        wr, wi = hop_ref[6, :, nat], hop_ref[7, :, nat]
        fr, fi = wr * er - wi * ei + sr, wr * ei + wi * er + si
        for ref in (scarry_ref, sout_ref):
            ref[b:b + 1, re_cols] = fr[7:8]
            ref[b:b + 1, im_cols] = fi[7:8]
        for g in range(CHUNK_ROWS):
            rows = slice(8 * g, 8 * g + 8)
            pr, pi = pw_ref[0, g, :, nat], pw_ref[1, g, :, nat]
            bu_ref[b, rows, re_cols] += pr * er - pi * ei
            bu_ref[b, rows, im_cols] += pr * ei + pi * er


def _mixer_output(b, x, u, out_a, unperm_ref, cc_ref, d_ref, gluw_ref, glub_ref, wout_ref, h_ref, bu_ref):
    y = jnp.concatenate([_dot(bu_ref[b, :, half * S5_STATE:(half + 1) * S5_STATE].astype(BF16), cc_ref[half])
                         for half in range(2)], axis=-1) + d_ref[...] * u
    z = jax.nn.gelu(y)
    out_b = z * jax.nn.sigmoid(_dot(z.astype(BF16), gluw_ref[...]) + glub_ref[...])
    mix = jnp.concatenate([out_a, out_b], axis=-1).astype(BF16)
    mix = _dot(unperm_ref[...], mix).astype(BF16)
    h_ref[b] = x + _dot(mix, wout_ref[...])


def _even_mixer(x, gm, w_in, cw, cb, cinit, sinit, ab, hop, pw, bb, cc, d, glu_w, glu_b, w_out, *, carried):
    nb, rows = x.shape[0], x.shape[1]
    tm = TM_MIX
    chunks = tm // CHUNK_ROWS
    assert chunks == 8 and rows % tm == 0 and (carried or nb == 1)
    per_step = nb if carried else chunks
    tiles_per_seq = rows // tm if carried else 1
    nseq = sinit.shape[0]
    t_of = np.arange(tm)
    perm = np.zeros((tm, tm), np.float32)
    perm[8 * (t_of % CHUNK_ROWS) + t_of // CHUNK_ROWS, t_of] = 1.0
    kern = functools.partial(_even_mixer_kernel, carried=carried, tiles_per_seq=tiles_per_seq)
    return pl.pallas_call(
        kern,
        grid=(rows // tm,),
        in_specs=[
            pl.BlockSpec((nb, tm, D_MODEL), lambda t: (0, t, 0)),
            _const_spec((1, D_MODEL)),
            _const_spec((tm, tm)),
            _const_spec((tm, tm)),
            _const_spec((D_MODEL, 4 * CONV_CH)),
            _const_spec((3, CONV_CH)),
            _const_spec((1, CONV_CH)),
            pl.BlockSpec((2, per_step, CONV_CH), lambda t: (0, t // tiles_per_seq, 0)),
            pl.BlockSpec((per_step, 2 * S5_STATE), lambda t: (t // tiles_per_seq, 0)),
            _const_spec((2, 8, S5_STATE)),
            _const_spec((8, 8, S5_STATE)),
            _const_spec((2, CHUNK_ROWS, 8, S5_STATE)),
            _const_spec((2, HALF_CH, S5_STATE)),
            _const_spec((2, S5_STATE, HALF_CH)),
            _const_spec((1, S5_CH)),
            _const_spec((S5_CH, S5_CH)),
            _const_spec((1, S5_CH)),
            _const_spec((D_MODEL, D_MODEL)),
        ],
        out_specs=[
            pl.BlockSpec((nb, tm, D_MODEL), lambda t: (0, t, 0)),
            pl.BlockSpec((2, per_step, CONV_CH), lambda t: (0, t // tiles_per_seq, 0)),
            pl.BlockSpec((per_step, 2 * S5_STATE), lambda t: (t // tiles_per_seq, 0)),
        ],
        out_shape=[
            jax.ShapeDtypeStruct((nb, rows, D_MODEL), F32),
            jax.ShapeDtypeStruct((2, nseq, CONV_CH), F32),
            jax.ShapeDtypeStruct((nseq, 2 * S5_STATE), F32),
        ],
        scratch_shapes=[
            pltpu.VMEM((nb, tm, 2 * S5_STATE), F32),
            pltpu.VMEM((2, nb, CONV_CH), F32),
            pltpu.VMEM((nb, 2 * S5_STATE), F32),
        ],
        compiler_params=_params("arbitrary"),
        name="even_mixer",
    )(x, gm, jnp.asarray(perm, BF16), jnp.asarray(perm.T, BF16), w_in, cw, cb, cinit, sinit, ab, hop, pw, bb, cc, d,
      glu_w, glu_b, w_out)


def _ffn_kernel(x_ref, g_ref, w1_ref, w3_ref, w2_ref, o_ref):
    x = x_ref[...]
    hn = _rms(x, g_ref[...]).astype(BF16)
    a = _dot(hn, w1_ref[...])
    b = _dot(hn, w3_ref[...])
    o_ref[...] = x + _dot((a * jax.nn.sigmoid(a) * b).astype(BF16), w2_ref[...])


def _ffn(x, g, w1, w3, w2, *, tm):
    rows = x.shape[0]
    return pl.pallas_call(
        _ffn_kernel,
        grid=(rows // tm,),
        in_specs=[
            pl.BlockSpec((tm, D_MODEL), lambda t: (t, 0)),
            _const_spec((1, D_MODEL)),
            _const_spec((D_MODEL, D_FF)),
            _const_spec((D_MODEL, D_FF)),
            _const_spec((D_FF, D_MODEL)),
        ],
        out_specs=pl.BlockSpec((tm, D_MODEL), lambda t: (t, 0)),
        out_shape=jax.ShapeDtypeStruct((rows, D_MODEL), F32),
        compiler_params=_params("parallel"),
        name="ffn",
    )(x, g, w1, w3, w2)


def _qkv_kernel(x_ref, g_ref, w_ref, b_ref, q_ref, k_ref, v_ref):
    hn = _rms(x_ref[...], g_ref[...]).astype(BF16)
    qkv = _dot(hn, w_ref[...]) + b_ref[...]
    nq = N_HEADS * HEAD_DIM
    q_ref[...] = (qkv[:, :nq] * (HEAD_DIM ** -0.5)).astype(BF16)
    k_ref[...] = qkv[:, nq:nq + KV_DIM]
    v_ref[...] = qkv[:, nq + KV_DIM:]


def _qkv(x, g, w, b, *, tm):
    rows = x.shape[0]
    ncol = (N_HEADS + 2 * N_KV) * HEAD_DIM
    return pl.pallas_call(
        _qkv_kernel,
        grid=(rows // tm,),
        in_specs=[
            pl.BlockSpec((tm, D_MODEL), lambda t: (t, 0)),
            _const_spec((1, D_MODEL)),
            _const_spec((D_MODEL, ncol)),
            _const_spec((1, ncol)),
        ],
        out_specs=[
            pl.BlockSpec((tm, N_HEADS * HEAD_DIM), lambda t: (t, 0)),
            pl.BlockSpec((tm, KV_DIM), lambda t: (t, 0)),
            pl.BlockSpec((tm, KV_DIM), lambda t: (t, 0)),
        ],
        out_shape=[
            jax.ShapeDtypeStruct((rows, N_HEADS * HEAD_DIM), BF16),
            jax.ShapeDtypeStruct((rows, KV_DIM), F32),
            jax.ShapeDtypeStruct((rows, KV_DIM), F32),
        ],
        compiler_params=_params("parallel"),
        name="qkv",
    )(x, g, w, b)


def _attend_groups(q, keys, vals, bias_of_group, sink_of_group, o_scr, row0):
    nq = q.shape[0]
    for kv in range(N_KV):
        kv_cols = slice(kv * HEAD_DIM, (kv + 1) * HEAD_DIM)
        heads = [slice((kv * GQ + g) * HEAD_DIM, (kv * GQ + g + 1) * HEAD_DIM) for g in range(GQ)]
        qs = jnp.concatenate([q[:, cols] for cols in heads], axis=0)
        st = lax.dot_general(keys[:, kv_cols], qs, (((1,), (1,)), ((), ())),
                             preferred_element_type=F32) + bias_of_group(kv)
        sk = sink_of_group(kv)
        m = jnp.maximum(jnp.max(st, axis=0, keepdims=True), sk)
        p = jnp.exp(st - m)
        den = jnp.sum(p, axis=0, keepdims=True) + jnp.exp(sk - m)
        o = lax.dot_general((p * (1.0 / den)).astype(BF16), vals[:, kv_cols], (((0,), (0,)), ((), ())),
                            preferred_element_type=F32)
        for g, cols in enumerate(heads):
            o_scr[row0:row0 + nq, cols] = o[g * nq:(g + 1) * nq]


def _attn_prompt_kernel(q_ref, k_ref, v_ref, bias_ref, sink_ref, x_ref, wo_ref, bo_ref, o_ref, o_scr):
    t = pl.program_id(1)
    for u in range(TQ_PER_STEP):
        tile = t * TQ_PER_STEP + u
        start = pl.multiple_of(tile * TQ, TQ)
        keys = k_ref[0, pl.ds(start, TQ + WINDOW), :]
        vals = v_ref[0, pl.ds(start, TQ + WINDOW), :]
        variant = jnp.minimum(tile, 1)
        _attend_groups(q_ref[u * TQ:(u + 1) * TQ], keys, vals, lambda kv: bias_ref[variant, kv],
                       lambda kv: sink_ref[kv], o_scr, u * TQ)
    o_ref[...] = x_ref[...] + _dot(o_scr[...].astype(BF16), wo_ref[...]) + bo_ref[...]


def _attn_prompt(q, kpad, vpad, bias, sink, x, w_o, b_o):
    nb, lpad = kpad.shape[0], kpad.shape[1]
    tm = TQ_PER_STEP * TQ
    steps = (lpad - WINDOW) // tm
    return pl.pallas_call(
        _attn_prompt_kernel,
        grid=(nb, steps),
        in_specs=[
            pl.BlockSpec((tm, D_MODEL), lambda b, t: (b * steps + t, 0)),
            pl.BlockSpec((1, lpad, KV_DIM), lambda b, t: (b, 0, 0)),
            pl.BlockSpec((1, lpad, KV_DIM), lambda b, t: (b, 0, 0)),
            _const_spec((2, N_KV, TQ + WINDOW, GQ * TQ)),
            _const_spec((N_KV, 1, GQ * TQ)),
            pl.BlockSpec((tm, D_MODEL), lambda b, t: (b * steps + t, 0)),
            _const_spec((D_MODEL, D_MODEL)),
            _const_spec((1, D_MODEL)),
        ],
        out_specs=pl.BlockSpec((tm, D_MODEL), lambda b, t: (b * steps + t, 0)),
        out_shape=jax.ShapeDtypeStruct(x.shape, F32),
        scratch_shapes=[pltpu.VMEM((tm, D_MODEL), F32)],
        compiler_params=_params("parallel", "arbitrary"),
        name="attn_prompt",
    )(q, kpad, vpad, bias, sink, x, w_o, b_o)


def _attn_small_kernel(q_ref, kc_ref, vc_ref, kn_ref, vn_ref, bias_ref, sink_ref, x_ref, wo_ref, bo_ref,
                       o_ref, o_scr):
    step = pl.program_id(0)
    for j in range(SEG_PER_STEP):
        rows = slice(j * SEG, (j + 1) * SEG)
        keys = jnp.concatenate([kc_ref[j], kn_ref[rows]], axis=0).astype(BF16)
        vals = jnp.concatenate([vc_ref[j], vn_ref[rows]], axis=0).astype(BF16)
        variant = jnp.where(step * SEG_PER_STEP + j == META_SEG, 1, 0)
        _attend_groups(q_ref[rows], keys, vals, lambda kv: bias_ref[variant, kv], lambda kv: sink_ref[kv], o_scr,
                       j * SEG)
    o_ref[...] = x_ref[...] + _dot(o_scr[...].astype(BF16), wo_ref[...]) + bo_ref[...]


def _attn_small(q, kc, vc, kn, vn, bias, sink, x, w_o, b_o):
    tm = SEG_PER_STEP * SEG
    steps = x.shape[0] // tm
    cache_blocks = kc.shape[0] // SEG_PER_STEP
    cache_spec = pl.BlockSpec((SEG_PER_STEP, WINDOW, KV_DIM), lambda s: (jnp.minimum(s, cache_blocks - 1), 0, 0))
    return pl.pallas_call(
        _attn_small_kernel,
        grid=(steps,),
        in_specs=[
            pl.BlockSpec((tm, D_MODEL), lambda s: (s, 0)),
            cache_spec,
            cache_spec,
            pl.BlockSpec((tm, KV_DIM), lambda s: (s, 0)),
            pl.BlockSpec((tm, KV_DIM), lambda s: (s, 0)),
            _const_spec((2, N_KV, WINDOW + SEG, GQ * SEG)),
            _const_spec((N_KV, 1, GQ * SEG)),
            pl.BlockSpec((tm, D_MODEL), lambda s: (s, 0)),
            _const_spec((D_MODEL, D_MODEL)),
            _const_spec((1, D_MODEL)),
        ],
        out_specs=pl.BlockSpec((tm, D_MODEL), lambda s: (s, 0)),
        out_shape=jax.ShapeDtypeStruct(x.shape, F32),
        scratch_shapes=[pltpu.VMEM((tm, D_MODEL), F32)],
        compiler_params=_params("parallel"),
        name="attn_small",
    )(q, kc, vc, kn, vn, bias, sink, x, w_o, b_o)


def _moe_kernel(x_ref, g_ref, rwt_ref, rb_ref, w1_ref, w3_ref, w2_ref, gf_ref, o_ref,
                hn_scr, slot_scr, gate_scr, cnt_smem, acc_scr, yall_scr, *, subs, tile_rows):
    e = pl.program_id(1)
    tb = x_ref.shape[0]
    ts = tb // subs
    cap = tile_rows[-1]
    sub_cols = [slice(s * ts, (s + 1) * ts) for s in range(subs)]

    @pl.when(e == 0)
    def _route():
        hn = _rms(x_ref[...], g_ref[...]).astype(BF16)
        hn_scr[...] = hn
        logits = lax.dot_general(rwt_ref[...], hn, (((1,), (1,)), ((), ())),
                                 preferred_element_type=F32) + rb_ref[...]
        row = lax.broadcasted_iota(jnp.int32, logits.shape, 0).astype(F32)
        v1 = jnp.max(logits, axis=0, keepdims=True)
        i1 = jnp.min(jnp.where(logits == v1, row, float(N_EXPERTS)), axis=0, keepdims=True)
        rest = jnp.where(row == i1, -jnp.inf, logits)
        v2 = jnp.max(rest, axis=0, keepdims=True)
        i2 = jnp.min(jnp.where(rest == v2, row, float(N_EXPERTS)), axis=0, keepdims=True)
        e2 = jnp.exp(v2 - v1)
        den = 1.0 + e2
        gate_scr[...] = jnp.where(row == i1, 1.0 / den, 0.0) + jnp.where(row == i2, e2 / den, 0.0)
        chosen = (row == i1) | (row == i2)
        ind = jnp.where(chosen, 1.0, 0.0)
        before = lax.broadcasted_iota(jnp.int32, (ts, ts), 0) < lax.broadcasted_iota(jnp.int32, (ts, ts), 1)
        before = jnp.where(before, 1.0, 0.0).astype(BF16)
        for s, cols in enumerate(sub_cols):
            rank = _dot(ind[:, cols].astype(BF16), before)
            slot_scr[:, cols] = jnp.where(chosen[:, cols], rank, -1.0).astype(jnp.int32)
            for k in range(N_EXPERTS):
                cnt_smem[k * subs + s] = jnp.sum(ind[k:k + 1, cols]).astype(jnp.int32)
        acc_scr[...] = jnp.zeros_like(acc_scr)

    def _scatter(onehot, rows):
        return lax.dot_general(onehot, rows, (((0,), (0,)), ((), ())), preferred_element_type=F32)

    def _expert_tile(i, rows=cap):
        packed_row = lax.broadcasted_iota(jnp.int32, (rows, ts), 0) + i * cap
        hits = [packed_row == slot_scr[pl.ds(e, 1), cols] for cols in sub_cols]
        onehots = [jnp.where(hit, 1.0, 0.0).astype(BF16) for hit in hits]
        xe = jnp.concatenate([_dot(onehot, hn_scr[cols, :]) for onehot, cols in zip(onehots, sub_cols)],
                             axis=0).astype(BF16)
        a = _dot(xe, w1_ref[0])
        b = _dot(xe, w3_ref[0])
        y = _dot((a * jax.nn.sigmoid(a) * b).astype(BF16), w2_ref[0])
        g_row = jnp.concatenate([jnp.sum(jnp.where(hit, gate_scr[pl.ds(e, 1), cols], 0.0), axis=1, keepdims=True)
                                 for hit, cols in zip(hits, sub_cols)], axis=0)
        ys = (y * g_row).astype(BF16)
        return onehots, [ys[s * rows:(s + 1) * rows] for s in range(subs)]

    base = pl.multiple_of(e * cap, 16)
    count = cnt_smem[e * subs]
    for s in range(1, subs):
        count = jnp.maximum(count, cnt_smem[e * subs + s])
    for lower, rows in zip((-1,) + tile_rows, tile_rows):
        fits = (count > lower) & (count <= rows) if rows < cap else count > lower

        @pl.when(fits)
        def _first_tile(rows=rows):
            for s, ys in enumerate(_expert_tile(0, rows)[1]):
                yall_scr[s, pl.ds(base, rows), :] = ys
                if rows < cap:
                    yall_scr[s, pl.ds(base + rows, cap - rows), :] = jnp.zeros((cap - rows, D_MODEL), BF16)

    def _overflow_tile(i, carry):
        onehots, yss = _expert_tile(i)
        for onehot, ys, cols in zip(onehots, yss, sub_cols):
            acc_scr[cols, :] += _scatter(onehot, ys)
        return carry

    lax.fori_loop(1, (count + cap - 1) // cap, _overflow_tile, 0)

    @pl.when(e == N_EXPERTS - 1)
    def _finish():
        packed_row = lax.broadcasted_iota(jnp.int32, (cap, ts), 0)
        for s, cols in enumerate(sub_cols):
            onehot = jnp.concatenate([jnp.where(packed_row == slot_scr[k:k + 1, cols], 1.0, 0.0).astype(BF16)
                                      for k in range(N_EXPERTS)], axis=0)
            o_ref[cols, :] = _rms(x_ref[cols, :] + acc_scr[cols, :] + _scatter(onehot, yall_scr[s]), gf_ref[...])


def _moe_final(x, g, rwt, rb, w1, w3, w2, gf, *, tb, subs, tile_rows):
    rows, cap = x.shape[0], tile_rows[-1]
    return pl.pallas_call(
        functools.partial(_moe_kernel, subs=subs, tile_rows=tile_rows),
        grid=(rows // tb, N_EXPERTS),
        in_specs=[
            pl.BlockSpec((tb, D_MODEL), lambda t, e: (t, 0)),
            _const_spec((1, D_MODEL)),
            _const_spec((N_EXPERTS, D_MODEL)),
            _const_spec((N_EXPERTS, 1)),
            pl.BlockSpec((1, D_MODEL, D_FF_E), lambda t, e: (e, 0, 0)),
            pl.BlockSpec((1, D_MODEL, D_FF_E), lambda t, e: (e, 0, 0)),
            pl.BlockSpec((1, D_FF_E, D_MODEL), lambda t, e: (e, 0, 0)),
            _const_spec((1, D_MODEL)),
        ],
        out_specs=pl.BlockSpec((tb, D_MODEL), lambda t, e: (t, 0)),
        out_shape=jax.ShapeDtypeStruct((rows, D_MODEL), F32),
        scratch_shapes=[
            pltpu.VMEM((tb, D_MODEL), BF16),
            pltpu.VMEM((N_EXPERTS, tb), jnp.int32),
            pltpu.VMEM((N_EXPERTS, tb), F32),
            pltpu.SMEM((N_EXPERTS * subs,), jnp.int32),
            pltpu.VMEM((tb, D_MODEL), F32),
            pltpu.VMEM((subs, N_EXPERTS * cap, D_MODEL), BF16),
        ],
        compiler_params=_params("parallel", "arbitrary"),
        name="moe_final",
    )(x, g, rwt, rb, w1, w3, w2, gf)


def _t5_bucket_np(rel):
    half = N_BUCKETS // 2
    max_exact = half // 2
    ret = np.where(rel > 0, half, 0)
    n = np.abs(rel)
    nf = np.maximum(n, 1).astype(np.float32)
    large = max_exact + (np.log(nf / np.float32(max_exact)) / np.float32(math.log(MAX_DIST / max_exact))
                         * np.float32(half - max_exact)).astype(np.int32)
    large = np.minimum(large, half - 1)
    return ret + np.where(n < max_exact, n, large)


def _bias_variants(table, bucket, keeps):
    nq, nk = bucket.shape
    onehot = jnp.asarray(np.eye(N_BUCKETS, dtype=np.float32)[bucket])
    bias = jnp.einsum('rjb,bh->hjr', onehot, table, precision=lax.Precision.HIGHEST)
    bias = jnp.stack([jnp.where(keep.T[None], bias, NEG) for keep in keeps], axis=0)
    bias = bias.reshape(len(keeps), N_KV, GQ, nk, nq)
    return jnp.transpose(bias, (0, 1, 3, 2, 4)).reshape(len(keeps), N_KV, nk, GQ * nq)


def _lane_sink(sink, nq):
    return jnp.broadcast_to(sink.reshape(N_KV, 1, GQ, 1), (N_KV, 1, GQ, nq)).reshape(N_KV, 1, GQ * nq)


def _prompt_bias(table):
    r = np.arange(TQ)[:, None]
    j = np.arange(TQ + WINDOW)[None, :]
    band = (j - CHUNK * (r // CHUNK) >= 0) & (j - CHUNK * (r // CHUNK) < WINDOW + CHUNK)
    first = band & (j >= WINDOW - N_META)
    return _bias_variants(table, _t5_bucket_np(j - WINDOW - r), [first, band])


def _small_bias(table):
    i = np.arange(SEG)[:, None]
    m = np.arange(WINDOW + SEG)[None, :]
    everything = np.ones((SEG, WINDOW + SEG), bool)
    meta_only = everything & (m >= WINDOW + SEG - N_META)
    return _bias_variants(table, _t5_bucket_np(m - WINDOW - i), [everything, meta_only])


def _pack_state(re, im):
    return jnp.concatenate([re[..., :HALF_STATE], im[..., :HALF_STATE], re[..., HALF_STATE:], im[..., HALF_STATE:]],
                           axis=-1)


def _cmul(ar, ai, br, bi):
    return ar * br - ai * bi, ar * bi + ai * br


def _s5_prepare(a_re, a_im, log_dt, b_re, b_im, c_re, c_im):
    dt = jnp.exp(log_dt)[:, None]
    mag = jnp.exp(a_re * dt)
    ab_re, ab_im = mag * jnp.cos(a_im * dt), mag * jnp.sin(a_im * dt)
    den = a_re * a_re + a_im * a_im
    num_re = ab_re - 1.0
    cf_re = (num_re * a_re + ab_im * a_im) / den
    cf_im = (ab_im * a_re - num_re * a_im) / den
    bb_re = cf_re[..., None] * b_re - cf_im[..., None] * b_im
    bb_im = cf_re[..., None] * b_im + cf_im[..., None] * b_re

    def group_diag(v):
        r, c = v.shape[1], v.shape[2]
        gh = S5_GROUPS // 2
        spread = np.tile(np.eye(c, dtype=np.float32), (1, gh))
        wide = jnp.einsum('hrc,cn->hrn', v.reshape(2, gh * r, c), spread, precision=lax.Precision.HIGHEST)
        keep = (np.arange(gh * r)[:, None] // r) == (np.arange(gh * c)[None, :] // c)
        return jnp.where(keep, wide, 0.0)

    def t(v):
        return jnp.swapaxes(v, 1, 2)

    bb = jnp.concatenate([group_diag(t(bb_re)), group_diag(t(bb_im))], axis=2).astype(BF16)
    cc = jnp.concatenate([group_diag(t(c_re)), -group_diag(t(c_im))], axis=1).astype(BF16)

    pr, pi = ab_re.reshape(1, S5_STATE), ab_im.reshape(1, S5_STATE)
    n = 1
    while n < CHUNK_ROWS:
        qr, qi = _cmul(pr, pi, pr[n - 1:n], pi[n - 1:n])
        pr, pi = jnp.concatenate([pr, qr], axis=0), jnp.concatenate([pi, qi], axis=0)
        n *= 2

    def tile8(v):
        return jnp.broadcast_to(v, (8,) + v.shape[1:])

    ab = jnp.stack([tile8(pr[0:1]), tile8(pi[0:1])])
    sub = np.arange(8)[:, None]
    hop, w = [], (pr[CHUNK_ROWS - 1:], pi[CHUNK_ROWS - 1:])
    for k in (1, 2, 4):
        hop += [jnp.where(sub >= k, w[0], 0.0), jnp.where(sub >= k, w[1], 0.0)]
        if k < 4:
            w = _cmul(*w, *w)
    hop = jnp.stack(hop + [tile8(pr[CHUNK_ROWS - 1:]), tile8(pi[CHUNK_ROWS - 1:])])
    pw = jnp.stack([jnp.broadcast_to(p[:, None, :], (CHUNK_ROWS, 8, S5_STATE)) for p in (pr, pi)])
    return bb, cc, ab, hop, pw


def kernel(x_prompt, x_sample, cache_conv, state_s5_re, state_s5_im, cache_swa_k, cache_swa_v, meta_tokens, rel_bias_table, norm_mix, norm_ffn, norm_final, w_in0, conv_w, conv_b, s5_a_re, s5_a_im, s5_log_dt, s5_b_re, s5_b_im, s5_c_re, s5_c_im, s5_d, s5_glu_w, s5_glu_b, w_out0, ffn_w1, ffn_w3, ffn_w2, w_qkv, b_qkv, attn_sink, w_o, b_o, router_w, router_b, moe_w1, moe_w3, moe_w2):
    nb, seq = x_prompt.shape[0], x_prompt.shape[1]
    nsb = x_sample.shape[0]
    assert x_sample.shape[1] == SEG and nsb == N_SAMPLE_SEG and seq % TM_MIX == 0

    def row(v):
        return v.reshape(1, -1)

    meta_seg = jnp.concatenate([jnp.zeros((SEG - N_META, D_MODEL), F32), meta_tokens], axis=0)
    filler = jnp.zeros(((N_SMALL_SEG - N_SAMPLE_SEG - 1) * SEG, D_MODEL), F32)
    xs = jnp.concatenate([x_sample.reshape(nsb * SEG, D_MODEL), meta_seg, filler], axis=0)
    n_extra = N_SMALL_SEG - N_SAMPLE_SEG

    bb, cc, ab, hop, pw = _s5_prepare(s5_a_re[0], s5_a_im[0], s5_log_dt[0], s5_b_re[0], s5_b_im[0], s5_c_re[0],
                                      s5_c_im[0])
    mixer_w = (row(norm_mix[0]), w_in0[0].astype(BF16), conv_w[0], row(conv_b[0]))
    mixer_w2 = (ab, hop, pw, bb, cc, row(s5_d[0]), s5_glu_w[0].astype(BF16), row(s5_glu_b[0]), w_out0[0].astype(BF16))
    cinit_s = jnp.concatenate([jnp.swapaxes(cache_conv[0], 0, 1), jnp.zeros((2, n_extra, CONV_CH), F32)], axis=1)
    sinit_s = _pack_state(state_s5_re[0].reshape(nsb, S5_STATE), state_s5_im[0].reshape(nsb, S5_STATE))
    sinit_s = jnp.concatenate([sinit_s, jnp.zeros((n_extra, 2 * S5_STATE), F32)], axis=0)
    hs, conv_s, state_s = _even_mixer(xs[None], *mixer_w, cinit_s, sinit_s, *mixer_w2, carried=False)
    hs = hs[0]
    cinit_p = jnp.broadcast_to(conv_s[:, META_SEG:META_SEG + 1], (2, nb, CONV_CH))
    sinit_p = jnp.broadcast_to(state_s[META_SEG:META_SEG + 1], (nb, 2 * S5_STATE))
    hp, conv_p, state_p = _even_mixer(x_prompt, *mixer_w, cinit_p, sinit_p, *mixer_w2, carried=True)
    hp = hp.reshape(nb * seq, D_MODEL)
    ffn_w = (row(norm_ffn[0]), ffn_w1[0].astype(BF16), ffn_w3[0].astype(BF16), ffn_w2[0].astype(BF16))
    hs = _ffn(hs, *ffn_w, tm=640)
    hp = _ffn(hp, *ffn_w, tm=512)

    qkv_w = (row(norm_mix[1]), w_qkv[0].astype(BF16), row(b_qkv[0]))
    qs, ks, vs = _qkv(hs, *qkv_w, tm=640)
    qp, kp, vp = _qkv(hp, *qkv_w, tm=1024)
    wo = (w_o[0].astype(BF16), row(b_o[0]))
    hs = _attn_small(qs, cache_swa_k[0].reshape(nsb, WINDOW, KV_DIM), cache_swa_v[0].reshape(nsb, WINDOW, KV_DIM),
                     ks, vs, _small_bias(rel_bias_table), _lane_sink(attn_sink[0], SEG), hs, *wo)

    def padded(new, small):
        meta_rows = small[META_SEG * SEG:(META_SEG + 1) * SEG]
        front = jnp.concatenate([jnp.zeros((WINDOW - SEG, KV_DIM), F32), meta_rows], axis=0)
        front = jnp.broadcast_to(front[None], (nb, WINDOW, KV_DIM))
        return jnp.concatenate([front, new.reshape(nb, seq, KV_DIM)], axis=1).astype(BF16)

    hp = _attn_prompt(qp, padded(kp, ks), padded(vp, vs), _prompt_bias(rel_bias_table),
                      _lane_sink(attn_sink[0], TQ), hp, *wo)
    moe_w = (row(norm_ffn[1]), router_w[0].T.astype(BF16), router_b[0].reshape(N_EXPERTS, 1), moe_w1[0].astype(BF16),
             moe_w3[0].astype(BF16), moe_w2[0].astype(BF16), row(norm_final))
    ys = _moe_final(hs, *moe_w, tb=hs.shape[0], subs=1, tile_rows=(352,))
    yp = _moe_final(hp, *moe_w, tb=1024, subs=2, tile_rows=(112, 144, 176, 208))

    n_real = nsb * SEG
    y_prompt = yp.reshape(nb, seq, D_MODEL)
    y_sample = ys[:n_real].reshape(nsb, SEG, D_MODEL)

    def split_state(st, n):
        st = st[:n].reshape(n, 2, 2, HALF_STATE)
        return tuple(st[:, :, ri].reshape(1, n, S5_GROUPS, S5_P) for ri in range(2))

    s5rp, s5ip = split_state(state_p, nb)
    s5rs, s5is = split_state(state_s, nsb)
    kp4 = kp.reshape(nb, seq, KV_DIM)[:, -WINDOW:].reshape(1, nb, WINDOW, N_KV, HEAD_DIM)
    vp4 = vp.reshape(nb, seq, KV_DIM)[:, -WINDOW:].reshape(1, nb, WINDOW, N_KV, HEAD_DIM)
    ks4 = ks[:n_real].reshape(1, nsb, SEG, N_KV, HEAD_DIM)
    vs4 = vs[:n_real].reshape(1, nsb, SEG, N_KV, HEAD_DIM)
    conv_p4 = jnp.swapaxes(conv_p, 0, 1)[None]
    conv_s4 = jnp.swapaxes(conv_s[:, :nsb], 0, 1)[None]
    return (y_prompt, y_sample, conv_p4, conv_s4, s5rp, s5ip, s5rs, s5is, kp4, vp4, ks4, vs4)
```

```python
import functools
import math

import numpy as np
import jax
import jax.numpy as jnp
from jax import lax
from jax.experimental import pallas as pl
from jax.experimental.pallas import tpu as pltpu

F32 = jnp.float32
BF16 = jnp.bfloat16

D_MODEL = 1024
CONV_CH = 512
S5_CH = 512
S5_GROUP = 16
S5_GROUPS = 32
S5_P = 64
S5_STATE = S5_GROUPS * S5_P
HALF_CH = S5_CH // 2
HALF_STATE = S5_STATE // 2
D_FF = 2816
N_HEADS = 16
N_KV = 2
GQ = N_HEADS // N_KV
HEAD_DIM = 64
KV_DIM = N_KV * HEAD_DIM
WINDOW = 128
CHUNK = 64
N_META = 16
N_BUCKETS = 32
MAX_DIST = 128
N_EXPERTS = 8
D_FF_E = 1024
EPS = 1e-6
NEG = -1e30

SEG = 32
N_SAMPLE_SEG = 32
META_SEG = N_SAMPLE_SEG
N_SMALL_SEG = 40
TM_MIX = 256
CHUNK_ROWS = SEG
SCAN_LANES = 512
TQ = 128
TQ_PER_STEP = 8
SEG_PER_STEP = 8
VMEM_LIMIT = 56 * 1024 * 1024


def _const_spec(shape):
    nd = len(shape)
    return pl.BlockSpec(shape, lambda *_: (0,) * nd, pipeline_mode=pl.Buffered(1))


def _params(*sem):
    return pltpu.CompilerParams(dimension_semantics=sem, vmem_limit_bytes=VMEM_LIMIT)


def _rms(x, g):
    return x * lax.rsqrt(jnp.mean(x * x, axis=-1, keepdims=True) + EPS) * g


def _dot(a, b):
    return jnp.dot(a, b, preferred_element_type=F32)


def _even_mixer_kernel(x_ref, gm_ref, perm_ref, unperm_ref, win_ref, cw_ref, cb_ref, cinit_ref, sinit_ref,
                       ab_ref, hop_ref, pw_ref, bb_ref, cc_ref, d_ref, gluw_ref, glub_ref, wout_ref,
                       h_ref, cout_ref, sout_ref, bu_ref, ccarry_ref, scarry_ref, *, carried, tiles_per_seq):
    if carried:
        @pl.when(pl.program_id(0) % tiles_per_seq == 0)
        def _load_initial_state():
            ccarry_ref[...] = cinit_ref[...]
            scarry_ref[...] = sinit_ref[...]

    streams = range(x_ref.shape[0])
    mid = [_mixer_project(b, x_ref, gm_ref, perm_ref, win_ref, cw_ref, cb_ref, cinit_ref, bb_ref, cout_ref, bu_ref,
                          ccarry_ref, carried=carried) for b in streams]
    for b in streams:
        _mixer_recur(b, sinit_ref, ab_ref, hop_ref, pw_ref, sout_ref, bu_ref, scarry_ref, carried=carried)
        _mixer_output(b, *mid[b], unperm_ref, cc_ref, d_ref, gluw_ref, glub_ref, wout_ref, h_ref, bu_ref)


def _mixer_project(b, x_ref, gm_ref, perm_ref, win_ref, cw_ref, cb_ref, cinit_ref, bb_ref, cout_ref, bu_ref,
                   ccarry_ref, *, carried):
    tm = TM_MIX
    x = x_ref[b]
    hn = _rms(x, gm_ref[...]).astype(BF16)
    hn = _dot(perm_ref[...], hn).astype(BF16)
    proj = _dot(hn, win_ref[...])
    g_b = proj[:, :CONV_CH]
    cin = proj[:, CONV_CH:2 * CONV_CH] * proj[:, 2 * CONV_CH:3 * CONV_CH]
    u = proj[:, 3 * CONV_CH:]
    ub = u.astype(BF16)
    for half in range(2):
        bu_ref[b, :, half * S5_STATE:(half + 1) * S5_STATE] = _dot(ub[:, half * HALF_CH:(half + 1) * HALF_CH],
                                                                  bb_ref[half])

    last, before_last = cin[tm - 8:tm], cin[tm - 16:tm - 8]
    if carried:
        first_chunk = lax.broadcasted_iota(jnp.int32, (8, CONV_CH), 0) == 0
        newer = jnp.where(first_chunk, ccarry_ref[1, b:b + 1, :], pltpu.roll(last, 1, 0))
        older = jnp.where(first_chunk, ccarry_ref[0, b:b + 1, :], pltpu.roll(before_last, 1, 0))
        for i, rows in enumerate((before_last, last)):
            ccarry_ref[i, b:b + 1, :] = rows[7:8]
            cout_ref[i, b:b + 1, :] = rows[7:8]
    else:
        older, newer = cinit_ref[0], cinit_ref[1]
        cout_ref[0] = before_last
        cout_ref[1] = last
    ext = jnp.concatenate([older, newer, cin], axis=0)
    cw = cw_ref[...]
    out_a = g_b * (cw[0:1] * ext[:tm] + cw[1:2] * ext[8:tm + 8] + cw[2:3] * cin + cb_ref[...])
    return x, u, out_a


def _mixer_recur(b, sinit_ref, ab_ref, hop_ref, pw_ref, sout_ref, bu_ref, scarry_ref, *, carried):
    for c in range(S5_STATE // SCAN_LANES):
        half, within = divmod(c * SCAN_LANES, HALF_STATE)
        nat = slice(c * SCAN_LANES, (c + 1) * SCAN_LANES)
        re_cols = slice(half * S5_STATE + within, half * S5_STATE + within + SCAN_LANES)
        im_cols = slice(re_cols.start + HALF_STATE, re_cols.stop + HALF_STATE)
        ar, ai = ab_ref[0, :, nat], ab_ref[1, :, nat]
        if carried:
            sr = si = jnp.zeros((8, SCAN_LANES), F32)
        else:
            sr, si = sinit_ref[:, re_cols], sinit_ref[:, im_cols]
        for g in range(CHUNK_ROWS):
            rows = slice(8 * g, 8 * g + 8)
            sr, si = (ar * sr - ai * si + bu_ref[b, rows, re_cols], ar * si + ai * sr + bu_ref[b, rows, im_cols])
            bu_ref[b, rows, re_cols] = sr
            bu_ref[b, rows, im_cols] = si
        if not carried:
            sout_ref[:, re_cols] = sr
            sout_ref[:, im_cols] = si
            continue
        first_chunk = lax.broadcasted_iota(jnp.int32, (8, SCAN_LANES), 0) == 0
        er = jnp.where(first_chunk, scarry_ref[b:b + 1, re_cols], pltpu.roll(sr, 1, 0))
        ei = jnp.where(first_chunk, scarry_ref[b:b + 1, im_cols], pltpu.roll(si, 1, 0))
        for k, sh in enumerate((1, 2, 4)):
            wr, wi = hop_ref[2 * k, :, nat], hop_ref[2 * k + 1, :, nat]
            tr, ti = pltpu.roll(er, sh, 0), pltpu.roll(ei, sh, 0)
            er, ei = er + wr * tr - wi * ti, ei + wr * ti + wi * tr
        wr, wi = hop_ref[6, :, nat], hop_ref[7, :, nat]
        fr, fi = wr * er - wi * ei + sr, wr * ei + wi * er + si
        for ref in (scarry_ref, sout_ref):
            ref[b:b + 1, re_cols] = fr[7:8]
            ref[b:b + 1, im_cols] = fi[7:8]
        for g in range(CHUNK_ROWS):
            rows = slice(8 * g, 8 * g + 8)
            pr, pi = pw_ref[0, g, :, nat], pw_ref[1, g, :, nat]
            bu_ref[b, rows, re_cols] += pr * er - pi * ei
            bu_ref[b, rows, im_cols] += pr * ei + pi * er


def _mixer_output(b, x, u, out_a, unperm_ref, cc_ref, d_ref, gluw_ref, glub_ref, wout_ref, h_ref, bu_ref):
    y = jnp.concatenate([_dot(bu_ref[b, :, half * S5_STATE:(half + 1) * S5_STATE].astype(BF16), cc_ref[half])
                         for half in range(2)], axis=-1) + d_ref[...] * u
    z = jax.nn.gelu(y)
    out_b = z * jax.nn.sigmoid(_dot(z.astype(BF16), gluw_ref[...]) + glub_ref[...])
    mix = jnp.concatenate([out_a, out_b], axis=-1).astype(BF16)
    mix = _dot(unperm_ref[...], mix).astype(BF16)
    h_ref[b] = x + _dot(mix, wout_ref[...])


def _even_mixer(x, gm, w_in, cw, cb, cinit, sinit, ab, hop, pw, bb, cc, d, glu_w, glu_b, w_out, *, carried):
    nb, rows = x.shape[0], x.shape[1]
    tm = TM_MIX
    chunks = tm // CHUNK_ROWS
    assert chunks == 8 and rows % tm == 0 and (carried or nb == 1)
    per_step = nb if carried else chunks
    tiles_per_seq = rows // tm if carried else 1
    nseq = sinit.shape[0]
    t_of = np.arange(tm)
    perm = np.zeros((tm, tm), np.float32)
    perm[8 * (t_of % CHUNK_ROWS) + t_of // CHUNK_ROWS, t_of] = 1.0
    kern = functools.partial(_even_mixer_kernel, carried=carried, tiles_per_seq=tiles_per_seq)
    return pl.pallas_call(
        kern,
        grid=(rows // tm,),
        in_specs=[
            pl.BlockSpec((nb, tm, D_MODEL), lambda t: (0, t, 0)),
            _const_spec((1, D_MODEL)),
            _const_spec((tm, tm)),
            _const_spec((tm, tm)),
            _const_spec((D_MODEL, 4 * CONV_CH)),
            _const_spec((3, CONV_CH)),
            _const_spec((1, CONV_CH)),
            pl.BlockSpec((2, per_step, CONV_CH), lambda t: (0, t // tiles_per_seq, 0)),
            pl.BlockSpec((per_step, 2 * S5_STATE), lambda t: (t // tiles_per_seq, 0)),
            _const_spec((2, 8, S5_STATE)),
            _const_spec((8, 8, S5_STATE)),
            _const_spec((2, CHUNK_ROWS, 8, S5_STATE)),
            _const_spec((2, HALF_CH, S5_STATE)),
            _const_spec((2, S5_STATE, HALF_CH)),
            _const_spec((1, S5_CH)),
            _const_spec((S5_CH, S5_CH)),
            _const_spec((1, S5_CH)),
            _const_spec((D_MODEL, D_MODEL)),
        ],
        out_specs=[
            pl.BlockSpec((nb, tm, D_MODEL), lambda t: (0, t, 0)),
            pl.BlockSpec((2, per_step, CONV_CH), lambda t: (0, t // tiles_per_seq, 0)),
            pl.BlockSpec((per_step, 2 * S5_STATE), lambda t: (t // tiles_per_seq, 0)),
        ],
        out_shape=[
            jax.ShapeDtypeStruct((nb, rows, D_MODEL), F32),
            jax.ShapeDtypeStruct((2, nseq, CONV_CH), F32),
            jax.ShapeDtypeStruct((nseq, 2 * S5_STATE), F32),
        ],
        scratch_shapes=[
            pltpu.VMEM((nb, tm, 2 * S5_STATE), F32),
            pltpu.VMEM((2, nb, CONV_CH), F32),
            pltpu.VMEM((nb, 2 * S5_STATE), F32),
        ],
        compiler_params=_params("arbitrary"),
        name="even_mixer",
    )(x, gm, jnp.asarray(perm, BF16), jnp.asarray(perm.T, BF16), w_in, cw, cb, cinit, sinit, ab, hop, pw, bb, cc, d,
      glu_w, glu_b, w_out)


def _ffn_kernel(x_ref, g_ref, w1_ref, w3_ref, w2_ref, o_ref):
    x = x_ref[...]
    hn = _rms(x, g_ref[...]).astype(BF16)
    a = _dot(hn, w1_ref[...])
    b = _dot(hn, w3_ref[...])
    o_ref[...] = x + _dot((a * jax.nn.sigmoid(a) * b).astype(BF16), w2_ref[...])


def _ffn(x, g, w1, w3, w2, *, tm):
    rows = x.shape[0]
    return pl.pallas_call(
        _ffn_kernel,
        grid=(rows // tm,),
        in_specs=[
            pl.BlockSpec((tm, D_MODEL), lambda t: (t, 0)),
            _const_spec((1, D_MODEL)),
            _const_spec((D_MODEL, D_FF)),
            _const_spec((D_MODEL, D_FF)),
            _const_spec((D_FF, D_MODEL)),
        ],
        out_specs=pl.BlockSpec((tm, D_MODEL), lambda t: (t, 0)),
        out_shape=jax.ShapeDtypeStruct((rows, D_MODEL), F32),
        compiler_params=_params("parallel"),
        name="ffn",
    )(x, g, w1, w3, w2)


def _qkv_kernel(x_ref, g_ref, w_ref, b_ref, q_ref, k_ref, v_ref):
    hn = _rms(x_ref[...], g_ref[...]).astype(BF16)
    qkv = _dot(hn, w_ref[...]) + b_ref[...]
    nq = N_HEADS * HEAD_DIM
    q_ref[...] = (qkv[:, :nq] * (HEAD_DIM ** -0.5)).astype(BF16)
    k_ref[...] = qkv[:, nq:nq + KV_DIM]
    v_ref[...] = qkv[:, nq + KV_DIM:]


def _qkv(x, g, w, b, *, tm):
    rows = x.shape[0]
    ncol = (N_HEADS + 2 * N_KV) * HEAD_DIM
    return pl.pallas_call(
        _qkv_kernel,
        grid=(rows // tm,),
        in_specs=[
            pl.BlockSpec((tm, D_MODEL), lambda t: (t, 0)),
            _const_spec((1, D_MODEL)),
            _const_spec((D_MODEL, ncol)),
            _const_spec((1, ncol)),
        ],
        out_specs=[
            pl.BlockSpec((tm, N_HEADS * HEAD_DIM), lambda t: (t, 0)),
            pl.BlockSpec((tm, KV_DIM), lambda t: (t, 0)),
            pl.BlockSpec((tm, KV_DIM), lambda t: (t, 0)),
        ],
        out_shape=[
            jax.ShapeDtypeStruct((rows, N_HEADS * HEAD_DIM), BF16),
            jax.ShapeDtypeStruct((rows, KV_DIM), F32),
            jax.ShapeDtypeStruct((rows, KV_DIM), F32),
        ],
        compiler_params=_params("parallel"),
        name="qkv",
    )(x, g, w, b)


def _attend_groups(q, keys, vals, bias_of_group, sink_of_group, o_scr, row0):
    nq = q.shape[0]
    for kv in range(N_KV):
        kv_cols = slice(kv * HEAD_DIM, (kv + 1) * HEAD_DIM)
        heads = [slice((kv * GQ + g) * HEAD_DIM, (kv * GQ + g + 1) * HEAD_DIM) for g in range(GQ)]
        qs = jnp.concatenate([q[:, cols] for cols in heads], axis=0)
        st = lax.dot_general(keys[:, kv_cols], qs, (((1,), (1,)), ((), ())),
                             preferred_element_type=F32) + bias_of_group(kv)
        sk = sink_of_group(kv)
        m = jnp.maximum(jnp.max(st, axis=0, keepdims=True), sk)
        p = jnp.exp(st - m)
        den = jnp.sum(p, axis=0, keepdims=True) + jnp.exp(sk - m)
        o = lax.dot_general((p * (1.0 / den)).astype(BF16), vals[:, kv_cols], (((0,), (0,)), ((), ())),
                            preferred_element_type=F32)
        for g, cols in enumerate(heads):
            o_scr[row0:row0 + nq, cols] = o[g * nq:(g + 1) * nq]


def _attn_prompt_kernel(q_ref, k_ref, v_ref, bias_ref, sink_ref, x_ref, wo_ref, bo_ref, o_ref, o_scr):
    t = pl.program_id(1)
    for u in range(TQ_PER_STEP):
        tile = t * TQ_PER_STEP + u
        start = pl.multiple_of(tile * TQ, TQ)
        keys = k_ref[0, pl.ds(start, TQ + WINDOW), :]
        vals = v_ref[0, pl.ds(start, TQ + WINDOW), :]
        variant = jnp.minimum(tile, 1)
        _attend_groups(q_ref[u * TQ:(u + 1) * TQ], keys, vals, lambda kv: bias_ref[variant, kv],
                       lambda kv: sink_ref[kv], o_scr, u * TQ)
    o_ref[...] = x_ref[...] + _dot(o_scr[...].astype(BF16), wo_ref[...]) + bo_ref[...]


def _attn_prompt(q, kpad, vpad, bias, sink, x, w_o, b_o):
    nb, lpad = kpad.shape[0], kpad.shape[1]
    tm = TQ_PER_STEP * TQ
    steps = (lpad - WINDOW) // tm
    return pl.pallas_call(
        _attn_prompt_kernel,
        grid=(nb, steps),
        in_specs=[
            pl.BlockSpec((tm, D_MODEL), lambda b, t: (b * steps + t, 0)),
            pl.BlockSpec((1, lpad, KV_DIM), lambda b, t: (b, 0, 0)),
            pl.BlockSpec((1, lpad, KV_DIM), lambda b, t: (b, 0, 0)),
            _const_spec((2, N_KV, TQ + WINDOW, GQ * TQ)),
            _const_spec((N_KV, 1, GQ * TQ)),
            pl.BlockSpec((tm, D_MODEL), lambda b, t: (b * steps + t, 0)),
            _const_spec((D_MODEL, D_MODEL)),
            _const_spec((1, D_MODEL)),
        ],
        out_specs=pl.BlockSpec((tm, D_MODEL), lambda b, t: (b * steps + t, 0)),
        out_shape=jax.ShapeDtypeStruct(x.shape, F32),
        scratch_shapes=[pltpu.VMEM((tm, D_MODEL), F32)],
        compiler_params=_params("parallel", "arbitrary"),
        name="attn_prompt",
    )(q, kpad, vpad, bias, sink, x, w_o, b_o)


def _attn_small_kernel(q_ref, kc_ref, vc_ref, kn_ref, vn_ref, bias_ref, sink_ref, x_ref, wo_ref, bo_ref,
                       o_ref, o_scr):
    step = pl.program_id(0)
    for j in range(SEG_PER_STEP):
        rows = slice(j * SEG, (j + 1) * SEG)
        keys = jnp.concatenate([kc_ref[j], kn_ref[rows]], axis=0).astype(BF16)
        vals = jnp.concatenate([vc_ref[j], vn_ref[rows]], axis=0).astype(BF16)
        variant = jnp.where(step * SEG_PER_STEP + j == META_SEG, 1, 0)
        _attend_groups(q_ref[rows], keys, vals, lambda kv: bias_ref[variant, kv], lambda kv: sink_ref[kv], o_scr,
                       j * SEG)
    o_ref[...] = x_ref[...] + _dot(o_scr[...].astype(BF16), wo_ref[...]) + bo_ref[...]


def _attn_small(q, kc, vc, kn, vn, bias, sink, x, w_o, b_o):
    tm = SEG_PER_STEP * SEG
    steps = x.shape[0] // tm
    cache_blocks = kc.shape[0] // SEG_PER_STEP
    cache_spec = pl.BlockSpec((SEG_PER_STEP, WINDOW, KV_DIM), lambda s: (jnp.minimum(s, cache_blocks - 1), 0, 0))
    return pl.pallas_call(
        _attn_small_kernel,
        grid=(steps,),
        in_specs=[
            pl.BlockSpec((tm, D_MODEL), lambda s: (s, 0)),
            cache_spec,
            cache_spec,
            pl.BlockSpec((tm, KV_DIM), lambda s: (s, 0)),
            pl.BlockSpec((tm, KV_DIM), lambda s: (s, 0)),
            _const_spec((2, N_KV, WINDOW + SEG, GQ * SEG)),
            _const_spec((N_KV, 1, GQ * SEG)),
            pl.BlockSpec((tm, D_MODEL), lambda s: (s, 0)),
            _const_spec((D_MODEL, D_MODEL)),
            _const_spec((1, D_MODEL)),
        ],
        out_specs=pl.BlockSpec((tm, D_MODEL), lambda s: (s, 0)),
        out_shape=jax.ShapeDtypeStruct(x.shape, F32),
        scratch_shapes=[pltpu.VMEM((tm, D_MODEL), F32)],
        compiler_params=_params("parallel"),
        name="attn_small",
    )(q, kc, vc, kn, vn, bias, sink, x, w_o, b_o)


def _expert_of_step(block, step):
    return jnp.where(block % 2 == 0, step, N_EXPERTS - 1 - step)


def _moe_kernel(x_ref, g_ref, rwt_ref, rb_ref, w1_ref, w3_ref, w2_ref, gf_ref, o_ref,
                hn_scr, slot_scr, gate_scr, cnt_smem, acc_scr, yall_scr, *, subs, tile_rows):
    step = pl.program_id(1)
    e = _expert_of_step(pl.program_id(0), step)
    tb = x_ref.shape[0]
    ts = tb // subs
    cap = tile_rows[-1]
    sub_cols = [slice(s * ts, (s + 1) * ts) for s in range(subs)]

    @pl.when(step == 0)
    def _route():
        hn = _rms(x_ref[...], g_ref[...]).astype(BF16)
        hn_scr[...] = hn
        logits = lax.dot_general(rwt_ref[...], hn, (((1,), (1,)), ((), ())),
                                 preferred_element_type=F32) + rb_ref[...]
        row = lax.broadcasted_iota(jnp.int32, logits.shape, 0).astype(F32)
        v1 = jnp.max(logits, axis=0, keepdims=True)
        i1 = jnp.min(jnp.where(logits == v1, row, float(N_EXPERTS)), axis=0, keepdims=True)
        rest = jnp.where(row == i1, -jnp.inf, logits)
        v2 = jnp.max(rest, axis=0, keepdims=True)
        i2 = jnp.min(jnp.where(rest == v2, row, float(N_EXPERTS)), axis=0, keepdims=True)
        e2 = jnp.exp(v2 - v1)
        den = 1.0 + e2
        gate_scr[...] = jnp.where(row == i1, 1.0 / den, 0.0) + jnp.where(row == i2, e2 / den, 0.0)
        chosen = (row == i1) | (row == i2)
        ind = jnp.where(chosen, 1.0, 0.0)
        before = lax.broadcasted_iota(jnp.int32, (ts, ts), 0) < lax.broadcasted_iota(jnp.int32, (ts, ts), 1)
        before = jnp.where(before, 1.0, 0.0).astype(BF16)
        for s, cols in enumerate(sub_cols):
            rank = _dot(ind[:, cols].astype(BF16), before)
            slot_scr[:, cols] = jnp.where(chosen[:, cols], rank, -1.0).astype(jnp.int32)
            for k in range(N_EXPERTS):
                cnt_smem[k * subs + s] = jnp.sum(ind[k:k + 1, cols]).astype(jnp.int32)
        acc_scr[...] = jnp.zeros_like(acc_scr)

    def _scatter(onehot, rows):
        return lax.dot_general(onehot, rows, (((0,), (0,)), ((), ())), preferred_element_type=F32)

    def _expert_tile(i, rows=cap):
        packed_row = lax.broadcasted_iota(jnp.int32, (rows, ts), 0) + i * cap
        hits = [packed_row == slot_scr[pl.ds(e, 1), cols] for cols in sub_cols]
        onehots = [jnp.where(hit, 1.0, 0.0).astype(BF16) for hit in hits]
        xe = jnp.concatenate([_dot(onehot, hn_scr[cols, :]) for onehot, cols in zip(onehots, sub_cols)],
                             axis=0).astype(BF16)
        a = _dot(xe, w1_ref[0])
        b = _dot(xe, w3_ref[0])
        y = _dot((a * jax.nn.sigmoid(a) * b).astype(BF16), w2_ref[0])
        g_row = jnp.concatenate([jnp.sum(jnp.where(hit, gate_scr[pl.ds(e, 1), cols], 0.0), axis=1, keepdims=True)
                                 for hit, cols in zip(hits, sub_cols)], axis=0)
        ys = (y * g_row).astype(BF16)
        return onehots, [ys[s * rows:(s + 1) * rows] for s in range(subs)]

    base = pl.multiple_of(e * cap, 16)
    count = cnt_smem[e * subs]
    for s in range(1, subs):
        count = jnp.maximum(count, cnt_smem[e * subs + s])
    for lower, rows in zip((-1,) + tile_rows, tile_rows):
        fits = (count > lower) & (count <= rows) if rows < cap else count > lower

        @pl.when(fits)
        def _first_tile(rows=rows):
            for s, ys in enumerate(_expert_tile(0, rows)[1]):
                yall_scr[s, pl.ds(base, rows), :] = ys
                if rows < cap:
                    yall_scr[s, pl.ds(base + rows, cap - rows), :] = jnp.zeros((cap - rows, D_MODEL), BF16)

    def _overflow_tile(i, carry):
        onehots, yss = _expert_tile(i)
        for onehot, ys, cols in zip(onehots, yss, sub_cols):
            acc_scr[cols, :] += _scatter(onehot, ys)
        return carry

    lax.fori_loop(1, (count + cap - 1) // cap, _overflow_tile, 0)

    @pl.when(step == N_EXPERTS - 1)
    def _finish():
        packed_row = lax.broadcasted_iota(jnp.int32, (cap, ts), 0)
        for s, cols in enumerate(sub_cols):
            onehot = jnp.concatenate([jnp.where(packed_row == slot_scr[k:k + 1, cols], 1.0, 0.0).astype(BF16)
                                      for k in range(N_EXPERTS)], axis=0)
            o_ref[cols, :] = _rms(x_ref[cols, :] + acc_scr[cols, :] + _scatter(onehot, yall_scr[s]), gf_ref[...])


def _moe_final(x, g, rwt, rb, w1, w3, w2, gf, *, tb, subs, tile_rows):
    rows, cap = x.shape[0], tile_rows[-1]
    return pl.pallas_call(
        functools.partial(_moe_kernel, subs=subs, tile_rows=tile_rows),
        grid=(rows // tb, N_EXPERTS),
        in_specs=[
            pl.BlockSpec((tb, D_MODEL), lambda t, e: (t, 0)),
            _const_spec((1, D_MODEL)),
            _const_spec((N_EXPERTS, D_MODEL)),
            _const_spec((N_EXPERTS, 1)),
            pl.BlockSpec((1, D_MODEL, D_FF_E), lambda t, s: (_expert_of_step(t, s), 0, 0)),
            pl.BlockSpec((1, D_MODEL, D_FF_E), lambda t, s: (_expert_of_step(t, s), 0, 0)),
            pl.BlockSpec((1, D_FF_E, D_MODEL), lambda t, s: (_expert_of_step(t, s), 0, 0)),
            _const_spec((1, D_MODEL)),
        ],
        out_specs=pl.BlockSpec((tb, D_MODEL), lambda t, e: (t, 0)),
        out_shape=jax.ShapeDtypeStruct((rows, D_MODEL), F32),
        scratch_shapes=[
            pltpu.VMEM((tb, D_MODEL), BF16),
            pltpu.VMEM((N_EXPERTS, tb), jnp.int32),
            pltpu.VMEM((N_EXPERTS, tb), F32),
            pltpu.SMEM((N_EXPERTS * subs,), jnp.int32),
            pltpu.VMEM((tb, D_MODEL), F32),
            pltpu.VMEM((subs, N_EXPERTS * cap, D_MODEL), BF16),
        ],
        compiler_params=_params("parallel", "arbitrary"),
        name="moe_final",
    )(x, g, rwt, rb, w1, w3, w2, gf)


def _t5_bucket_np(rel):
    half = N_BUCKETS // 2
    max_exact = half // 2
    ret = np.where(rel > 0, half, 0)
    n = np.abs(rel)
    nf = np.maximum(n, 1).astype(np.float32)
    large = max_exact + (np.log(nf / np.float32(max_exact)) / np.float32(math.log(MAX_DIST / max_exact))
                         * np.float32(half - max_exact)).astype(np.int32)
    large = np.minimum(large, half - 1)
    return ret + np.where(n < max_exact, n, large)


def _bias_variants(table, bucket, keeps):
    nq, nk = bucket.shape
    onehot = jnp.asarray(np.eye(N_BUCKETS, dtype=np.float32)[bucket])
    bias = jnp.einsum('rjb,bh->hjr', onehot, table, precision=lax.Precision.HIGHEST)
    bias = jnp.stack([jnp.where(keep.T[None], bias, NEG) for keep in keeps], axis=0)
    bias = bias.reshape(len(keeps), N_KV, GQ, nk, nq)
    return jnp.transpose(bias, (0, 1, 3, 2, 4)).reshape(len(keeps), N_KV, nk, GQ * nq)


def _lane_sink(sink, nq):
    return jnp.broadcast_to(sink.reshape(N_KV, 1, GQ, 1), (N_KV, 1, GQ, nq)).reshape(N_KV, 1, GQ * nq)


def _prompt_bias(table):
    r = np.arange(TQ)[:, None]
    j = np.arange(TQ + WINDOW)[None, :]
    band = (j - CHUNK * (r // CHUNK) >= 0) & (j - CHUNK * (r // CHUNK) < WINDOW + CHUNK)
    first = band & (j >= WINDOW - N_META)
    return _bias_variants(table, _t5_bucket_np(j - WINDOW - r), [first, band])


def _small_bias(table):
    i = np.arange(SEG)[:, None]
    m = np.arange(WINDOW + SEG)[None, :]
    everything = np.ones((SEG, WINDOW + SEG), bool)
    meta_only = everything & (m >= WINDOW + SEG - N_META)
    return _bias_variants(table, _t5_bucket_np(m - WINDOW - i), [everything, meta_only])


def _pack_state(re, im):
    return jnp.concatenate([re[..., :HALF_STATE], im[..., :HALF_STATE], re[..., HALF_STATE:], im[..., HALF_STATE:]],
                           axis=-1)


def _cmul(ar, ai, br, bi):
    return ar * br - ai * bi, ar * bi + ai * br


def _s5_prepare(a_re, a_im, log_dt, b_re, b_im, c_re, c_im):
    dt = jnp.exp(log_dt)[:, None]
    mag = jnp.exp(a_re * dt)
    ab_re, ab_im = mag * jnp.cos(a_im * dt), mag * jnp.sin(a_im * dt)
    den = a_re * a_re + a_im * a_im
    num_re = ab_re - 1.0
    cf_re = (num_re * a_re + ab_im * a_im) / den
    cf_im = (ab_im * a_re - num_re * a_im) / den
    bb_re = cf_re[..., None] * b_re - cf_im[..., None] * b_im
    bb_im = cf_re[..., None] * b_im + cf_im[..., None] * b_re

    def group_diag(v):
        r, c = v.shape[1], v.shape[2]
        gh = S5_GROUPS // 2
        spread = np.tile(np.eye(c, dtype=np.float32), (1, gh))
        wide = jnp.einsum('hrc,cn->hrn', v.reshape(2, gh * r, c), spread, precision=lax.Precision.HIGHEST)
        keep = (np.arange(gh * r)[:, None] // r) == (np.arange(gh * c)[None, :] // c)
        return jnp.where(keep, wide, 0.0)

    def t(v):
        return jnp.swapaxes(v, 1, 2)

    bb = jnp.concatenate([group_diag(t(bb_re)), group_diag(t(bb_im))], axis=2).astype(BF16)
    cc = jnp.concatenate([group_diag(t(c_re)), -group_diag(t(c_im))], axis=1).astype(BF16)

    pr, pi = ab_re.reshape(1, S5_STATE), ab_im.reshape(1, S5_STATE)
    n = 1
    while n < CHUNK_ROWS:
        qr, qi = _cmul(pr, pi, pr[n - 1:n], pi[n - 1:n])
        pr, pi = jnp.concatenate([pr, qr], axis=0), jnp.concatenate([pi, qi], axis=0)
        n *= 2

    def tile8(v):
        return jnp.broadcast_to(v, (8,) + v.shape[1:])

    ab = jnp.stack([tile8(pr[0:1]), tile8(pi[0:1])])
    sub = np.arange(8)[:, None]
    hop, w = [], (pr[CHUNK_ROWS - 1:], pi[CHUNK_ROWS - 1:])
    for k in (1, 2, 4):
        hop += [jnp.where(sub >= k, w[0], 0.0), jnp.where(sub >= k, w[1], 0.0)]
        if k < 4:
            w = _cmul(*w, *w)
    hop = jnp.stack(hop + [tile8(pr[CHUNK_ROWS - 1:]), tile8(pi[CHUNK_ROWS - 1:])])
    pw = jnp.stack([jnp.broadcast_to(p[:, None, :], (CHUNK_ROWS, 8, S5_STATE)) for p in (pr, pi)])
    return bb, cc, ab, hop, pw


def kernel(x_prompt, x_sample, cache_conv, state_s5_re, state_s5_im, cache_swa_k, cache_swa_v, meta_tokens, rel_bias_table, norm_mix, norm_ffn, norm_final, w_in0, conv_w, conv_b, s5_a_re, s5_a_im, s5_log_dt, s5_b_re, s5_b_im, s5_c_re, s5_c_im, s5_d, s5_glu_w, s5_glu_b, w_out0, ffn_w1, ffn_w3, ffn_w2, w_qkv, b_qkv, attn_sink, w_o, b_o, router_w, router_b, moe_w1, moe_w3, moe_w2):
    nb, seq = x_prompt.shape[0], x_prompt.shape[1]
    nsb = x_sample.shape[0]
    assert x_sample.shape[1] == SEG and nsb == N_SAMPLE_SEG and seq % TM_MIX == 0

    def row(v):
        return v.reshape(1, -1)

    meta_seg = jnp.concatenate([jnp.zeros((SEG - N_META, D_MODEL), F32), meta_tokens], axis=0)
    filler = jnp.zeros(((N_SMALL_SEG - N_SAMPLE_SEG - 1) * SEG, D_MODEL), F32)
    xs = jnp.concatenate([x_sample.reshape(nsb * SEG, D_MODEL), meta_seg, filler], axis=0)
    n_extra = N_SMALL_SEG - N_SAMPLE_SEG

    bb, cc, ab, hop, pw = _s5_prepare(s5_a_re[0], s5_a_im[0], s5_log_dt[0], s5_b_re[0], s5_b_im[0], s5_c_re[0],
                                      s5_c_im[0])
    mixer_w = (row(norm_mix[0]), w_in0[0].astype(BF16), conv_w[0], row(conv_b[0]))
    mixer_w2 = (ab, hop, pw, bb, cc, row(s5_d[0]), s5_glu_w[0].astype(BF16), row(s5_glu_b[0]), w_out0[0].astype(BF16))
    cinit_s = jnp.concatenate([jnp.swapaxes(cache_conv[0], 0, 1), jnp.zeros((2, n_extra, CONV_CH), F32)], axis=1)
    sinit_s = _pack_state(state_s5_re[0].reshape(nsb, S5_STATE), state_s5_im[0].reshape(nsb, S5_STATE))
    sinit_s = jnp.concatenate([sinit_s, jnp.zeros((n_extra, 2 * S5_STATE), F32)], axis=0)
    hs, conv_s, state_s = _even_mixer(xs[None], *mixer_w, cinit_s, sinit_s, *mixer_w2, carried=False)
    hs = hs[0]
    cinit_p = jnp.broadcast_to(conv_s[:, META_SEG:META_SEG + 1], (2, nb, CONV_CH))
    sinit_p = jnp.broadcast_to(state_s[META_SEG:META_SEG + 1], (nb, 2 * S5_STATE))
    hp, conv_p, state_p = _even_mixer(x_prompt, *mixer_w, cinit_p, sinit_p, *mixer_w2, carried=True)
    hp = hp.reshape(nb * seq, D_MODEL)
    ffn_w = (row(norm_ffn[0]), ffn_w1[0].astype(BF16), ffn_w3[0].astype(BF16), ffn_w2[0].astype(BF16))
    hs = _ffn(hs, *ffn_w, tm=256)
    hp = _ffn(hp, *ffn_w, tm=512)

    qkv_w = (row(norm_mix[1]), w_qkv[0].astype(BF16), row(b_qkv[0]))
    qs, ks, vs = _qkv(hs, *qkv_w, tm=640)
    qp, kp, vp = _qkv(hp, *qkv_w, tm=1024)
    wo = (w_o[0].astype(BF16), row(b_o[0]))
    hs = _attn_small(qs, cache_swa_k[0].reshape(nsb, WINDOW, KV_DIM), cache_swa_v[0].reshape(nsb, WINDOW, KV_DIM),
                     ks, vs, _small_bias(rel_bias_table), _lane_sink(attn_sink[0], SEG), hs, *wo)

    def padded(new, small):
        meta_rows = small[META_SEG * SEG:(META_SEG + 1) * SEG]
        front = jnp.concatenate([jnp.zeros((WINDOW - SEG, KV_DIM), F32), meta_rows], axis=0)
        front = jnp.broadcast_to(front[None], (nb, WINDOW, KV_DIM))
        return jnp.concatenate([front, new.reshape(nb, seq, KV_DIM)], axis=1).astype(BF16)

    hp = _attn_prompt(qp, padded(kp, ks), padded(vp, vs), _prompt_bias(rel_bias_table),
                      _lane_sink(attn_sink[0], TQ), hp, *wo)
    moe_w = (row(norm_ffn[1]), router_w[0].T.astype(BF16), router_b[0].reshape(N_EXPERTS, 1), moe_w1[0].astype(BF16),
             moe_w3[0].astype(BF16), moe_w2[0].astype(BF16), row(norm_final))
    ys = _moe_final(hs, *moe_w, tb=hs.shape[0], subs=1, tile_rows=(352,))
    yp = _moe_final(hp, *moe_w, tb=1024, subs=2, tile_rows=(112, 144, 176, 208))

    n_real = nsb * SEG
    y_prompt = yp.reshape(nb, seq, D_MODEL)
    y_sample = ys[:n_real].reshape(nsb, SEG, D_MODEL)

    def split_state(st, n):
        st = st[:n].reshape(n, 2, 2, HALF_STATE)
        return tuple(st[:, :, ri].reshape(1, n, S5_GROUPS, S5_P) for ri in range(2))

    s5rp, s5ip = split_state(state_p, nb)
    s5rs, s5is = split_state(state_s, nsb)
    kp4 = kp.reshape(nb, seq, KV_DIM)[:, -WINDOW:].reshape(1, nb, WINDOW, N_KV, HEAD_DIM)
    vp4 = vp.reshape(nb, seq, KV_DIM)[:, -WINDOW:].reshape(1, nb, WINDOW, N_KV, HEAD_DIM)
    ks4 = ks[:n_real].reshape(1, nsb, SEG, N_KV, HEAD_DIM)
    vs4 = vs[:n_real].reshape(1, nsb, SEG, N_KV, HEAD_DIM)
    conv_p4 = jnp.swapaxes(conv_p, 0, 1)[None]
    conv_s4 = jnp.swapaxes(conv_s[:, :nsb], 0, 1)[None]
    return (y_prompt, y_sample, conv_p4, conv_s4, s5rp, s5ip, s5rs, s5is, kp4, vp4, ks4, vs4)
```

```python
import functools
import math

import numpy as np
import jax
import jax.numpy as jnp
from jax import lax
from jax.experimental import pallas as pl
from jax.experimental.pallas import tpu as pltpu

F32 = jnp.float32
BF16 = jnp.bfloat16

D_MODEL = 1024
CONV_CH = 512
S5_CH = 512
S5_GROUP = 16
S5_GROUPS = 32
S5_P = 64
S5_STATE = S5_GROUPS * S5_P
HALF_CH = S5_CH // 2
HALF_STATE = S5_STATE // 2
D_FF = 2816
N_HEADS = 16
N_KV = 2
GQ = N_HEADS // N_KV
HEAD_DIM = 64
KV_DIM = N_KV * HEAD_DIM
WINDOW = 128
CHUNK = 64
N_META = 16
N_BUCKETS = 32
MAX_DIST = 128
N_EXPERTS = 8
D_FF_E = 1024
EPS = 1e-6
NEG = -1e30

SEG = 32
N_SAMPLE_SEG = 32
META_SEG = N_SAMPLE_SEG
N_SMALL_SEG = 40
TM_MIX = 256
CHUNK_ROWS = SEG
SCAN_LANES = 512
TQ = 128
TQ_PER_STEP = 8
SEG_PER_STEP = 8
VMEM_LIMIT = 56 * 1024 * 1024


def _const_spec(shape):
    nd = len(shape)
    return pl.BlockSpec(shape, lambda *_: (0,) * nd, pipeline_mode=pl.Buffered(1))


def _params(*sem):
    return pltpu.CompilerParams(dimension_semantics=sem, vmem_limit_bytes=VMEM_LIMIT)


def _rms(x, g):
    return x * lax.rsqrt(jnp.mean(x * x, axis=-1, keepdims=True) + EPS) * g


def _dot(a, b):
    return jnp.dot(a, b, preferred_element_type=F32)


def _even_mixer_kernel(x_ref, gm_ref, perm_ref, unperm_ref, win_ref, cw_ref, cb_ref, cinit_ref, sinit_ref,
                       ab_ref, hop_ref, pw_ref, bb_ref, cc_ref, d_ref, gluw_ref, glub_ref, wout_ref,
                       h_ref, cout_ref, sout_ref, bu_ref, ccarry_ref, scarry_ref, *, carried, tiles_per_seq):
    if carried:
        @pl.when(pl.program_id(0) % tiles_per_seq == 0)
        def _load_initial_state():
            ccarry_ref[...] = cinit_ref[...]
            scarry_ref[...] = sinit_ref[...]

    streams = range(x_ref.shape[0])
    mid = [_mixer_project(b, x_ref, gm_ref, perm_ref, win_ref, cw_ref, cb_ref, cinit_ref, bb_ref, cout_ref, bu_ref,
                          ccarry_ref, carried=carried) for b in streams]
    for b in streams:
        _mixer_recur(b, sinit_ref, ab_ref, hop_ref, pw_ref, sout_ref, bu_ref, scarry_ref, carried=carried)
        _mixer_output(b, *mid[b], unperm_ref, cc_ref, d_ref, gluw_ref, glub_ref, wout_ref, h_ref, bu_ref)


def _mixer_project(b, x_ref, gm_ref, perm_ref, win_ref, cw_ref, cb_ref, cinit_ref, bb_ref, cout_ref, bu_ref,
                   ccarry_ref, *, carried):
    tm = TM_MIX
    x = x_ref[b]
    hn = _rms(x, gm_ref[...]).astype(BF16)
    hn = _dot(perm_ref[...], hn).astype(BF16)
    proj = _dot(hn, win_ref[...])
    g_b = proj[:, :CONV_CH]
    cin = proj[:, CONV_CH:2 * CONV_CH] * proj[:, 2 * CONV_CH:3 * CONV_CH]
    u = proj[:, 3 * CONV_CH:]
    ub = u.astype(BF16)
    for half in range(2):
        bu_ref[b, :, half * S5_STATE:(half + 1) * S5_STATE] = _dot(ub[:, half * HALF_CH:(half + 1) * HALF_CH],
                                                                  bb_ref[half])

    last, before_last = cin[tm - 8:tm], cin[tm - 16:tm - 8]
    if carried:
        first_chunk = lax.broadcasted_iota(jnp.int32, (8, CONV_CH), 0) == 0
        newer = jnp.where(first_chunk, ccarry_ref[1, b:b + 1, :], pltpu.roll(last, 1, 0))
        older = jnp.where(first_chunk, ccarry_ref[0, b:b + 1, :], pltpu.roll(before_last, 1, 0))
        for i, rows in enumerate((before_last, last)):
            ccarry_ref[i, b:b + 1, :] = rows[7:8]
            cout_ref[i, b:b + 1, :] = rows[7:8]
    else:
        older, newer = cinit_ref[0], cinit_ref[1]
        cout_ref[0] = before_last
        cout_ref[1] = last
    ext = jnp.concatenate([older, newer, cin], axis=0)
    cw = cw_ref[...]
    out_a = g_b * (cw[0:1] * ext[:tm] + cw[1:2] * ext[8:tm + 8] + cw[2:3] * cin + cb_ref[...])
    return x, u, out_a


def _mixer_recur(b, sinit_ref, ab_ref, hop_ref, pw_ref, sout_ref, bu_ref, scarry_ref, *, carried):
    for c in range(S5_STATE // SCAN_LANES):
        half, within = divmod(c * SCAN_LANES, HALF_STATE)
        nat = slice(c * SCAN_LANES, (c + 1) * SCAN_LANES)
        re_cols = slice(half * S5_STATE + within, half * S5_STATE + within + SCAN_LANES)
        im_cols = slice(re_cols.start + HALF_STATE, re_cols.stop + HALF_STATE)
        ar, ai = ab_ref[0, :, nat], ab_ref[1, :, nat]
        if carried:
            sr = si = jnp.zeros((8, SCAN_LANES), F32)
        else:
            sr, si = sinit_ref[:, re_cols], sinit_ref[:, im_cols]
        for g in range(CHUNK_ROWS):
            rows = slice(8 * g, 8 * g + 8)
            sr, si = (ar * sr - ai * si + bu_ref[b, rows, re_cols], ar * si + ai * sr + bu_ref[b, rows, im_cols])
            bu_ref[b, rows, re_cols] = sr
            bu_ref[b, rows, im_cols] = si
        if not carried:
            sout_ref[:, re_cols] = sr
            sout_ref[:, im_cols] = si
            continue
        first_chunk = lax.broadcasted_iota(jnp.int32, (8, SCAN_LANES), 0) == 0
        er = jnp.where(first_chunk, scarry_ref[b:b + 1, re_cols], pltpu.roll(sr, 1, 0))
        ei = jnp.where(first_chunk, scarry_ref[b:b + 1, im_cols], pltpu.roll(si, 1, 0))
        for k, sh in enumerate((1, 2, 4)):
            wr, wi = hop_ref[2 * k, :, nat], hop_ref[2 * k + 1, :, nat]
            tr, ti = pltpu.roll(er, sh, 0), pltpu.roll(ei, sh, 0)
            er, ei = er + wr * tr - wi * ti, ei + wr * ti + wi * tr
        wr, wi = hop_ref[6, :, nat], hop_ref[7, :, nat]
        fr, fi = wr * er - wi * ei + sr, wr * ei + wi * er + si
        for ref in (scarry_ref, sout_ref):
            ref[b:b + 1, re_cols] = fr[7:8]
            ref[b:b + 1, im_cols] = fi[7:8]
        for g in range(CHUNK_ROWS):
            rows = slice(8 * g, 8 * g + 8)
            pr, pi = pw_ref[0, g, :, nat], pw_ref[1, g, :, nat]
            bu_ref[b, rows, re_cols] += pr * er - pi * ei
            bu_ref[b, rows, im_cols] += pr * ei + pi * er


def _mixer_output(b, x, u, out_a, unperm_ref, cc_ref, d_ref, gluw_ref, glub_ref, wout_ref, h_ref, bu_ref):
    y = jnp.concatenate([_dot(bu_ref[b, :, half * S5_STATE:(half + 1) * S5_STATE].astype(BF16), cc_ref[half])
                         for half in range(2)], axis=-1) + d_ref[...] * u
    z = jax.nn.gelu(y)
    out_b = z * jax.nn.sigmoid(_dot(z.astype(BF16), gluw_ref[...]) + glub_ref[...])
    mix = jnp.concatenate([out_a, out_b], axis=-1).astype(BF16)
    mix = _dot(unperm_ref[...], mix).astype(BF16)
    h_ref[b] = x + _dot(mix, wout_ref[...])


def _even_mixer(x, gm, w_in, cw, cb, cinit, sinit, ab, hop, pw, bb, cc, d, glu_w, glu_b, w_out, *, carried):
    nb, rows = x.shape[0], x.shape[1]
    tm = TM_MIX
    chunks = tm // CHUNK_ROWS
    assert chunks == 8 and rows % tm == 0 and (carried or nb == 1)
    per_step = nb if carried else chunks
    tiles_per_seq = rows // tm if carried else 1
    nseq = sinit.shape[0]
    t_of = np.arange(tm)
    perm = np.zeros((tm, tm), np.float32)
    perm[8 * (t_of % CHUNK_ROWS) + t_of // CHUNK_ROWS, t_of] = 1.0
    kern = functools.partial(_even_mixer_kernel, carried=carried, tiles_per_seq=tiles_per_seq)
    return pl.pallas_call(
        kern,
        grid=(rows // tm,),
        in_specs=[
            pl.BlockSpec((nb, tm, D_MODEL), lambda t: (0, t, 0)),
            _const_spec((1, D_MODEL)),
            _const_spec((tm, tm)),
            _const_spec((tm, tm)),
            _const_spec((D_MODEL, 4 * CONV_CH)),
            _const_spec((3, CONV_CH)),
            _const_spec((1, CONV_CH)),
            pl.BlockSpec((2, per_step, CONV_CH), lambda t: (0, t // tiles_per_seq, 0)),
            pl.BlockSpec((per_step, 2 * S5_STATE), lambda t: (t // tiles_per_seq, 0)),
            _const_spec((2, 8, S5_STATE)),
            _const_spec((8, 8, S5_STATE)),
            _const_spec((2, CHUNK_ROWS, 8, S5_STATE)),
            _const_spec((2, HALF_CH, S5_STATE)),
            _const_spec((2, S5_STATE, HALF_CH)),
            _const_spec((1, S5_CH)),
            _const_spec((S5_CH, S5_CH)),
            _const_spec((1, S5_CH)),
            _const_spec((D_MODEL, D_MODEL)),
        ],
        out_specs=[
            pl.BlockSpec((nb, tm, D_MODEL), lambda t: (0, t, 0)),
            pl.BlockSpec((2, per_step, CONV_CH), lambda t: (0, t // tiles_per_seq, 0)),
            pl.BlockSpec((per_step, 2 * S5_STATE), lambda t: (t // tiles_per_seq, 0)),
        ],
        out_shape=[
            jax.ShapeDtypeStruct((nb, rows, D_MODEL), F32),
            jax.ShapeDtypeStruct((2, nseq, CONV_CH), F32),
            jax.ShapeDtypeStruct((nseq, 2 * S5_STATE), F32),
        ],
        scratch_shapes=[
            pltpu.VMEM((nb, tm, 2 * S5_STATE), F32),
            pltpu.VMEM((2, nb, CONV_CH), F32),
            pltpu.VMEM((nb, 2 * S5_STATE), F32),
        ],
        compiler_params=_params("arbitrary"),
        name="even_mixer",
    )(x, gm, jnp.asarray(perm, BF16), jnp.asarray(perm.T, BF16), w_in, cw, cb, cinit, sinit, ab, hop, pw, bb, cc, d,
      glu_w, glu_b, w_out)


def _ffn_kernel(x_ref, g_ref, w1_ref, w3_ref, w2_ref, o_ref):
    x = x_ref[...]
    hn = _rms(x, g_ref[...]).astype(BF16)
    a = _dot(hn, w1_ref[...])
    b = _dot(hn, w3_ref[...])
    o_ref[...] = x + _dot((a * jax.nn.sigmoid(a) * b).astype(BF16), w2_ref[...])


def _ffn(x, g, w1, w3, w2, *, tm):
    rows = x.shape[0]
    return pl.pallas_call(
        _ffn_kernel,
        grid=(rows // tm,),
        in_specs=[
            pl.BlockSpec((tm, D_MODEL), lambda t: (t, 0)),
            _const_spec((1, D_MODEL)),
            _const_spec((D_MODEL, D_FF)),
            _const_spec((D_MODEL, D_FF)),
            _const_spec((D_FF, D_MODEL)),
        ],
        out_specs=pl.BlockSpec((tm, D_MODEL), lambda t: (t, 0)),
        out_shape=jax.ShapeDtypeStruct((rows, D_MODEL), F32),
        compiler_params=_params("parallel"),
        name="ffn",
    )(x, g, w1, w3, w2)


def _qkv_kernel(x_ref, g_ref, w_ref, b_ref, q_ref, k_ref, v_ref):
    hn = _rms(x_ref[...], g_ref[...]).astype(BF16)
    qkv = _dot(hn, w_ref[...]) + b_ref[...]
    nq = N_HEADS * HEAD_DIM
    q_ref[...] = (qkv[:, :nq] * (HEAD_DIM ** -0.5)).astype(BF16)
    k_ref[...] = qkv[:, nq:nq + KV_DIM]
    v_ref[...] = qkv[:, nq + KV_DIM:]


def _qkv(x, g, w, b, *, tm):
    rows = x.shape[0]
    ncol = (N_HEADS + 2 * N_KV) * HEAD_DIM
    return pl.pallas_call(
        _qkv_kernel,
        grid=(rows // tm,),
        in_specs=[
            pl.BlockSpec((tm, D_MODEL), lambda t: (t, 0)),
            _const_spec((1, D_MODEL)),
            _const_spec((D_MODEL, ncol)),
            _const_spec((1, ncol)),
        ],
        out_specs=[
            pl.BlockSpec((tm, N_HEADS * HEAD_DIM), lambda t: (t, 0)),
            pl.BlockSpec((tm, KV_DIM), lambda t: (t, 0)),
            pl.BlockSpec((tm, KV_DIM), lambda t: (t, 0)),
        ],
        out_shape=[
            jax.ShapeDtypeStruct((rows, N_HEADS * HEAD_DIM), BF16),
            jax.ShapeDtypeStruct((rows, KV_DIM), F32),
            jax.ShapeDtypeStruct((rows, KV_DIM), F32),
        ],
        compiler_params=_params("parallel"),
        name="qkv",
    )(x, g, w, b)


def _attend_groups(q, keys, vals, bias_of_group, sink_of_group, o_scr, row0):
    nq = q.shape[0]
    for kv in range(N_KV):
        kv_cols = slice(kv * HEAD_DIM, (kv + 1) * HEAD_DIM)
        heads = [slice((kv * GQ + g) * HEAD_DIM, (kv * GQ + g + 1) * HEAD_DIM) for g in range(GQ)]
        qs = jnp.concatenate([q[:, cols] for cols in heads], axis=0)
        st = lax.dot_general(keys[:, kv_cols], qs, (((1,), (1,)), ((), ())),
                             preferred_element_type=F32) + bias_of_group(kv)
        sk = sink_of_group(kv)
        m = jnp.maximum(jnp.max(st, axis=0, keepdims=True), sk)
        p = jnp.exp(st - m)
        den = jnp.sum(p, axis=0, keepdims=True) + jnp.exp(sk - m)
        o = lax.dot_general((p * (1.0 / den)).astype(BF16), vals[:, kv_cols], (((0,), (0,)), ((), ())),
                            preferred_element_type=F32)
        for g, cols in enumerate(heads):
            o_scr[row0:row0 + nq, cols] = o[g * nq:(g + 1) * nq]


def _attn_prompt_kernel(q_ref, k_ref, v_ref, bias_ref, sink_ref, x_ref, wo_ref, bo_ref, o_ref, o_scr):
    t = pl.program_id(1)
    for u in range(TQ_PER_STEP):
        tile = t * TQ_PER_STEP + u
        start = pl.multiple_of(tile * TQ, TQ)
        keys = k_ref[0, pl.ds(start, TQ + WINDOW), :]
        vals = v_ref[0, pl.ds(start, TQ + WINDOW), :]
        variant = jnp.minimum(tile, 1)
        _attend_groups(q_ref[u * TQ:(u + 1) * TQ], keys, vals, lambda kv: bias_ref[variant, kv],
                       lambda kv: sink_ref[kv], o_scr, u * TQ)
    o_ref[...] = x_ref[...] + _dot(o_scr[...].astype(BF16), wo_ref[...]) + bo_ref[...]


def _attn_prompt(q, kpad, vpad, bias, sink, x, w_o, b_o):
    nb, lpad = kpad.shape[0], kpad.shape[1]
    tm = TQ_PER_STEP * TQ
    steps = (lpad - WINDOW) // tm
    return pl.pallas_call(
        _attn_prompt_kernel,
        grid=(nb, steps),
        in_specs=[
            pl.BlockSpec((tm, D_MODEL), lambda b, t: (b * steps + t, 0)),
            pl.BlockSpec((1, lpad, KV_DIM), lambda b, t: (b, 0, 0)),
            pl.BlockSpec((1, lpad, KV_DIM), lambda b, t: (b, 0, 0)),
            _const_spec((2, N_KV, TQ + WINDOW, GQ * TQ)),
            _const_spec((N_KV, 1, GQ * TQ)),
            pl.BlockSpec((tm, D_MODEL), lambda b, t: (b * steps + t, 0)),
            _const_spec((D_MODEL, D_MODEL)),
            _const_spec((1, D_MODEL)),
        ],
        out_specs=pl.BlockSpec((tm, D_MODEL), lambda b, t: (b * steps + t, 0)),
        out_shape=jax.ShapeDtypeStruct(x.shape, F32),
        scratch_shapes=[pltpu.VMEM((tm, D_MODEL), F32)],
        compiler_params=_params("parallel", "arbitrary"),
        name="attn_prompt",
    )(q, kpad, vpad, bias, sink, x, w_o, b_o)


def _attn_small_kernel(q_ref, kc_ref, vc_ref, kn_ref, vn_ref, bias_ref, sink_ref, x_ref, wo_ref, bo_ref,
                       o_ref, o_scr):
    step = pl.program_id(0)
    for j in range(SEG_PER_STEP):
        rows = slice(j * SEG, (j + 1) * SEG)
        keys = jnp.concatenate([kc_ref[j], kn_ref[rows]], axis=0).astype(BF16)
        vals = jnp.concatenate([vc_ref[j], vn_ref[rows]], axis=0).astype(BF16)
        variant = jnp.where(step * SEG_PER_STEP + j == META_SEG, 1, 0)
        _attend_groups(q_ref[rows], keys, vals, lambda kv: bias_ref[variant, kv], lambda kv: sink_ref[kv], o_scr,
                       j * SEG)
    o_ref[...] = x_ref[...] + _dot(o_scr[...].astype(BF16), wo_ref[...]) + bo_ref[...]


def _attn_small(q, kc, vc, kn, vn, bias, sink, x, w_o, b_o):
    tm = SEG_PER_STEP * SEG
    steps = x.shape[0] // tm
    cache_blocks = kc.shape[0] // SEG_PER_STEP
    cache_spec = pl.BlockSpec((SEG_PER_STEP, WINDOW, KV_DIM), lambda s: (jnp.minimum(s, cache_blocks - 1), 0, 0))
    return pl.pallas_call(
        _attn_small_kernel,
        grid=(steps,),
        in_specs=[
            pl.BlockSpec((tm, D_MODEL), lambda s: (s, 0)),
            cache_spec,
            cache_spec,
            pl.BlockSpec((tm, KV_DIM), lambda s: (s, 0)),
            pl.BlockSpec((tm, KV_DIM), lambda s: (s, 0)),
            _const_spec((2, N_KV, WINDOW + SEG, GQ * SEG)),
            _const_spec((N_KV, 1, GQ * SEG)),
            pl.BlockSpec((tm, D_MODEL), lambda s: (s, 0)),
            _const_spec((D_MODEL, D_MODEL)),
            _const_spec((1, D_MODEL)),
        ],
        out_specs=pl.BlockSpec((tm, D_MODEL), lambda s: (s, 0)),
        out_shape=jax.ShapeDtypeStruct(x.shape, F32),
        scratch_shapes=[pltpu.VMEM((tm, D_MODEL), F32)],
        compiler_params=_params("parallel"),
        name="attn_small",
    )(q, kc, vc, kn, vn, bias, sink, x, w_o, b_o)


def _expert_of_step(block, step):
    return jnp.where(block % 2 == 0, step, N_EXPERTS - 1 - step)


def _moe_kernel(x_ref, g_ref, rwt_ref, rb_ref, w1_hbm, w3_hbm, w2_hbm, gf_ref, o_ref,
                hn_scr, slot_scr, gate_scr, cnt_smem, acc_scr, yall_scr, w1_buf, w3_buf, w2_buf, w_sem, *,
                subs, tile_rows):
    block = pl.program_id(0)
    tb = x_ref.shape[0]
    ts = tb // subs
    cap = tile_rows[-1]
    sub_cols = [slice(s * ts, (s + 1) * ts) for s in range(subs)]

    def _weights(e, slot):
        return [pltpu.make_async_copy(hbm.at[e], buf.at[slot], w_sem.at[i, slot])
                for i, (hbm, buf) in enumerate(((w1_hbm, w1_buf), (w3_hbm, w3_buf), (w2_hbm, w2_buf)))]

    @pl.when(block == 0)
    def _first_weights():
        for copy in _weights(_expert_of_step(block, 0), 0):
            copy.start()
        for copy in _weights(_expert_of_step(block, 0), 0):
            copy.wait()

    def _route():
        hn = _rms(x_ref[...], g_ref[...]).astype(BF16)
        hn_scr[...] = hn
        logits = lax.dot_general(rwt_ref[...], hn, (((1,), (1,)), ((), ())),
                                 preferred_element_type=F32) + rb_ref[...]
        row = lax.broadcasted_iota(jnp.int32, logits.shape, 0).astype(F32)
        v1 = jnp.max(logits, axis=0, keepdims=True)
        i1 = jnp.min(jnp.where(logits == v1, row, float(N_EXPERTS)), axis=0, keepdims=True)
        rest = jnp.where(row == i1, -jnp.inf, logits)
        v2 = jnp.max(rest, axis=0, keepdims=True)
        i2 = jnp.min(jnp.where(rest == v2, row, float(N_EXPERTS)), axis=0, keepdims=True)
        e2 = jnp.exp(v2 - v1)
        den = 1.0 + e2
        gate_scr[...] = jnp.where(row == i1, 1.0 / den, 0.0) + jnp.where(row == i2, e2 / den, 0.0)
        chosen = (row == i1) | (row == i2)
        ind = jnp.where(chosen, 1.0, 0.0)
        before = lax.broadcasted_iota(jnp.int32, (ts, ts), 0) < lax.broadcasted_iota(jnp.int32, (ts, ts), 1)
        before = jnp.where(before, 1.0, 0.0).astype(BF16)
        for s, cols in enumerate(sub_cols):
            rank = _dot(ind[:, cols].astype(BF16), before)
            slot_scr[:, cols] = jnp.where(chosen[:, cols], rank, -1.0).astype(jnp.int32)
            for k in range(N_EXPERTS):
                cnt_smem[k * subs + s] = jnp.sum(ind[k:k + 1, cols]).astype(jnp.int32)
        acc_scr[...] = jnp.zeros_like(acc_scr)

    _route()

    def _scatter(onehot, rows):
        return lax.dot_general(onehot, rows, (((0,), (0,)), ((), ())), preferred_element_type=F32)

    def _expert_tile(e, slot, i, rows=cap):
        packed_row = lax.broadcasted_iota(jnp.int32, (rows, ts), 0) + i * cap
        hits = [packed_row == slot_scr[pl.ds(e, 1), cols] for cols in sub_cols]
        onehots = [jnp.where(hit, 1.0, 0.0).astype(BF16) for hit in hits]
        xe = jnp.concatenate([_dot(onehot, hn_scr[cols, :]) for onehot, cols in zip(onehots, sub_cols)],
                             axis=0).astype(BF16)
        a = _dot(xe, w1_buf[slot])
        b = _dot(xe, w3_buf[slot])
        y = _dot((a * jax.nn.sigmoid(a) * b).astype(BF16), w2_buf[slot])
        g_row = jnp.concatenate([jnp.sum(jnp.where(hit, gate_scr[pl.ds(e, 1), cols], 0.0), axis=1, keepdims=True)
                                 for hit, cols in zip(hits, sub_cols)], axis=0)
        ys = (y * g_row).astype(BF16)
        return onehots, [ys[s * rows:(s + 1) * rows] for s in range(subs)]

    def _expert_step(step, carry):
        e = _expert_of_step(block, step)
        slot = (block + step) % 2
        nxt = _expert_of_step(block, jnp.minimum(step + 1, N_EXPERTS - 1))

        @pl.when(step < N_EXPERTS - 1)
        def _prefetch():
            for copy in _weights(nxt, 1 - slot):
                copy.start()

        base = pl.multiple_of(e * cap, 16)
        count = cnt_smem[e * subs]
        for s in range(1, subs):
            count = jnp.maximum(count, cnt_smem[e * subs + s])
        for lower, rows in zip((-1,) + tile_rows, tile_rows):
            fits = (count > lower) & (count <= rows) if rows < cap else count > lower

            @pl.when(fits)
            def _first_tile(rows=rows):
                for s, ys in enumerate(_expert_tile(e, slot, 0, rows)[1]):
                    yall_scr[s, pl.ds(base, rows), :] = ys
                    if rows < cap:
                        yall_scr[s, pl.ds(base + rows, cap - rows), :] = jnp.zeros((cap - rows, D_MODEL), BF16)

        def _overflow_tile(i, c):
            onehots, yss = _expert_tile(e, slot, i)
            for onehot, ys, cols in zip(onehots, yss, sub_cols):
                acc_scr[cols, :] += _scatter(onehot, ys)
            return c

        lax.fori_loop(1, (count + cap - 1) // cap, _overflow_tile, 0)

        @pl.when(step < N_EXPERTS - 1)
        def _await():
            for copy in _weights(nxt, 1 - slot):
                copy.wait()

        return carry

    lax.fori_loop(0, N_EXPERTS, _expert_step, 0)

    packed_row = lax.broadcasted_iota(jnp.int32, (cap, ts), 0)
    for s, cols in enumerate(sub_cols):
        onehot = jnp.concatenate([jnp.where(packed_row == slot_scr[k:k + 1, cols], 1.0, 0.0).astype(BF16)
                                  for k in range(N_EXPERTS)], axis=0)
        o_ref[cols, :] = _rms(x_ref[cols, :] + acc_scr[cols, :] + _scatter(onehot, yall_scr[s]), gf_ref[...])


def _moe_final(x, g, rwt, rb, w1, w3, w2, gf, *, tb, subs, tile_rows):
    rows, cap = x.shape[0], tile_rows[-1]
    return pl.pallas_call(
        functools.partial(_moe_kernel, subs=subs, tile_rows=tile_rows),
        grid=(rows // tb,),
        in_specs=[
            pl.BlockSpec((tb, D_MODEL), lambda t: (t, 0)),
            _const_spec((1, D_MODEL)),
            _const_spec((N_EXPERTS, D_MODEL)),
            _const_spec((N_EXPERTS, 1)),
            pl.BlockSpec(memory_space=pl.ANY),
            pl.BlockSpec(memory_space=pl.ANY),
            pl.BlockSpec(memory_space=pl.ANY),
            _const_spec((1, D_MODEL)),
        ],
        out_specs=pl.BlockSpec((tb, D_MODEL), lambda t: (t, 0)),
        out_shape=jax.ShapeDtypeStruct((rows, D_MODEL), F32),
        scratch_shapes=[
            pltpu.VMEM((tb, D_MODEL), BF16),
            pltpu.VMEM((N_EXPERTS, tb), jnp.int32),
            pltpu.VMEM((N_EXPERTS, tb), F32),
            pltpu.SMEM((N_EXPERTS * subs,), jnp.int32),
            pltpu.VMEM((tb, D_MODEL), F32),
            pltpu.VMEM((subs, N_EXPERTS * cap, D_MODEL), BF16),
            pltpu.VMEM((2, D_MODEL, D_FF_E), BF16),
            pltpu.VMEM((2, D_MODEL, D_FF_E), BF16),
            pltpu.VMEM((2, D_FF_E, D_MODEL), BF16),
            pltpu.SemaphoreType.DMA((3, 2)),
        ],
        compiler_params=_params("arbitrary"),
        name="moe_final",
    )(x, g, rwt, rb, w1, w3, w2, gf)


def _t5_bucket_np(rel):
    half = N_BUCKETS // 2
    max_exact = half // 2
    ret = np.where(rel > 0, half, 0)
    n = np.abs(rel)
    nf = np.maximum(n, 1).astype(np.float32)
    large = max_exact + (np.log(nf / np.float32(max_exact)) / np.float32(math.log(MAX_DIST / max_exact))
                         * np.float32(half - max_exact)).astype(np.int32)
    large = np.minimum(large, half - 1)
    return ret + np.where(n < max_exact, n, large)


def _bias_variants(table, bucket, keeps):
    nq, nk = bucket.shape
    onehot = jnp.asarray(np.eye(N_BUCKETS, dtype=np.float32)[bucket])
    bias = jnp.einsum('rjb,bh->hjr', onehot, table, precision=lax.Precision.HIGHEST)
    bias = jnp.stack([jnp.where(keep.T[None], bias, NEG) for keep in keeps], axis=0)
    bias = bias.reshape(len(keeps), N_KV, GQ, nk, nq)
    return jnp.transpose(bias, (0, 1, 3, 2, 4)).reshape(len(keeps), N_KV, nk, GQ * nq)


def _lane_sink(sink, nq):
    return jnp.broadcast_to(sink.reshape(N_KV, 1, GQ, 1), (N_KV, 1, GQ, nq)).reshape(N_KV, 1, GQ * nq)


def _prompt_bias(table):
    r = np.arange(TQ)[:, None]
    j = np.arange(TQ + WINDOW)[None, :]
    band = (j - CHUNK * (r // CHUNK) >= 0) & (j - CHUNK * (r // CHUNK) < WINDOW + CHUNK)
    first = band & (j >= WINDOW - N_META)
    return _bias_variants(table, _t5_bucket_np(j - WINDOW - r), [first, band])


def _small_bias(table):
    i = np.arange(SEG)[:, None]
    m = np.arange(WINDOW + SEG)[None, :]
    everything = np.ones((SEG, WINDOW + SEG), bool)
    meta_only = everything & (m >= WINDOW + SEG - N_META)
    return _bias_variants(table, _t5_bucket_np(m - WINDOW - i), [everything, meta_only])


def _pack_state(re, im):
    return jnp.concatenate([re[..., :HALF_STATE], im[..., :HALF_STATE], re[..., HALF_STATE:], im[..., HALF_STATE:]],
                           axis=-1)


def _cmul(ar, ai, br, bi):
    return ar * br - ai * bi, ar * bi + ai * br


def _s5_prepare(a_re, a_im, log_dt, b_re, b_im, c_re, c_im):
    dt = jnp.exp(log_dt)[:, None]
    mag = jnp.exp(a_re * dt)
    ab_re, ab_im = mag * jnp.cos(a_im * dt), mag * jnp.sin(a_im * dt)
    den = a_re * a_re + a_im * a_im
    num_re = ab_re - 1.0
    cf_re = (num_re * a_re + ab_im * a_im) / den
    cf_im = (ab_im * a_re - num_re * a_im) / den
    bb_re = cf_re[..., None] * b_re - cf_im[..., None] * b_im
    bb_im = cf_re[..., None] * b_im + cf_im[..., None] * b_re

    def group_diag(v):
        r, c = v.shape[1], v.shape[2]
        gh = S5_GROUPS // 2
        spread = np.tile(np.eye(c, dtype=np.float32), (1, gh))
        wide = jnp.einsum('hrc,cn->hrn', v.reshape(2, gh * r, c), spread, precision=lax.Precision.HIGHEST)
        keep = (np.arange(gh * r)[:, None] // r) == (np.arange(gh * c)[None, :] // c)
        return jnp.where(keep, wide, 0.0)

    def t(v):
        return jnp.swapaxes(v, 1, 2)

    bb = jnp.concatenate([group_diag(t(bb_re)), group_diag(t(bb_im))], axis=2).astype(BF16)
    cc = jnp.concatenate([group_diag(t(c_re)), -group_diag(t(c_im))], axis=1).astype(BF16)

    pr, pi = ab_re.reshape(1, S5_STATE), ab_im.reshape(1, S5_STATE)
    n = 1
    while n < CHUNK_ROWS:
        qr, qi = _cmul(pr, pi, pr[n - 1:n], pi[n - 1:n])
        pr, pi = jnp.concatenate([pr, qr], axis=0), jnp.concatenate([pi, qi], axis=0)
        n *= 2

    def tile8(v):
        return jnp.broadcast_to(v, (8,) + v.shape[1:])

    ab = jnp.stack([tile8(pr[0:1]), tile8(pi[0:1])])
    sub = np.arange(8)[:, None]
    hop, w = [], (pr[CHUNK_ROWS - 1:], pi[CHUNK_ROWS - 1:])
    for k in (1, 2, 4):
        hop += [jnp.where(sub >= k, w[0], 0.0), jnp.where(sub >= k, w[1], 0.0)]
        if k < 4:
            w = _cmul(*w, *w)
    hop = jnp.stack(hop + [tile8(pr[CHUNK_ROWS - 1:]), tile8(pi[CHUNK_ROWS - 1:])])
    pw = jnp.stack([jnp.broadcast_to(p[:, None, :], (CHUNK_ROWS, 8, S5_STATE)) for p in (pr, pi)])
    return bb, cc, ab, hop, pw


def kernel(x_prompt, x_sample, cache_conv, state_s5_re, state_s5_im, cache_swa_k, cache_swa_v, meta_tokens, rel_bias_table, norm_mix, norm_ffn, norm_final, w_in0, conv_w, conv_b, s5_a_re, s5_a_im, s5_log_dt, s5_b_re, s5_b_im, s5_c_re, s5_c_im, s5_d, s5_glu_w, s5_glu_b, w_out0, ffn_w1, ffn_w3, ffn_w2, w_qkv, b_qkv, attn_sink, w_o, b_o, router_w, router_b, moe_w1, moe_w3, moe_w2):
    nb, seq = x_prompt.shape[0], x_prompt.shape[1]
    nsb = x_sample.shape[0]
    assert x_sample.shape[1] == SEG and nsb == N_SAMPLE_SEG and seq % TM_MIX == 0

    def row(v):
        return v.reshape(1, -1)

    meta_seg = jnp.concatenate([jnp.zeros((SEG - N_META, D_MODEL), F32), meta_tokens], axis=0)
    filler = jnp.zeros(((N_SMALL_SEG - N_SAMPLE_SEG - 1) * SEG, D_MODEL), F32)
    xs = jnp.concatenate([x_sample.reshape(nsb * SEG, D_MODEL), meta_seg, filler], axis=0)
    n_extra = N_SMALL_SEG - N_SAMPLE_SEG

    bb, cc, ab, hop, pw = _s5_prepare(s5_a_re[0], s5_a_im[0], s5_log_dt[0], s5_b_re[0], s5_b_im[0], s5_c_re[0],
                                      s5_c_im[0])
    mixer_w = (row(norm_mix[0]), w_in0[0].astype(BF16), conv_w[0], row(conv_b[0]))
    mixer_w2 = (ab, hop, pw, bb, cc, row(s5_d[0]), s5_glu_w[0].astype(BF16), row(s5_glu_b[0]), w_out0[0].astype(BF16))
    cinit_s = jnp.concatenate([jnp.swapaxes(cache_conv[0], 0, 1), jnp.zeros((2, n_extra, CONV_CH), F32)], axis=1)
    sinit_s = _pack_state(state_s5_re[0].reshape(nsb, S5_STATE), state_s5_im[0].reshape(nsb, S5_STATE))
    sinit_s = jnp.concatenate([sinit_s, jnp.zeros((n_extra, 2 * S5_STATE), F32)], axis=0)
    hs, conv_s, state_s = _even_mixer(xs[None], *mixer_w, cinit_s, sinit_s, *mixer_w2, carried=False)
    hs = hs[0]
    cinit_p = jnp.broadcast_to(conv_s[:, META_SEG:META_SEG + 1], (2, nb, CONV_CH))
    sinit_p = jnp.broadcast_to(state_s[META_SEG:META_SEG + 1], (nb, 2 * S5_STATE))
    hp, conv_p, state_p = _even_mixer(x_prompt, *mixer_w, cinit_p, sinit_p, *mixer_w2, carried=True)
    hp = hp.reshape(nb * seq, D_MODEL)
    ffn_w = (row(norm_ffn[0]), ffn_w1[0].astype(BF16), ffn_w3[0].astype(BF16), ffn_w2[0].astype(BF16))
    hs = _ffn(hs, *ffn_w, tm=256)
    hp = _ffn(hp, *ffn_w, tm=512)

    qkv_w = (row(norm_mix[1]), w_qkv[0].astype(BF16), row(b_qkv[0]))
    qs, ks, vs = _qkv(hs, *qkv_w, tm=640)
    qp, kp, vp = _qkv(hp, *qkv_w, tm=1024)
    wo = (w_o[0].astype(BF16), row(b_o[0]))
    hs = _attn_small(qs, cache_swa_k[0].reshape(nsb, WINDOW, KV_DIM), cache_swa_v[0].reshape(nsb, WINDOW, KV_DIM),
                     ks, vs, _small_bias(rel_bias_table), _lane_sink(attn_sink[0], SEG), hs, *wo)

    def padded(new, small):
        meta_rows = small[META_SEG * SEG:(META_SEG + 1) * SEG]
        front = jnp.concatenate([jnp.zeros((WINDOW - SEG, KV_DIM), F32), meta_rows], axis=0)
        front = jnp.broadcast_to(front[None], (nb, WINDOW, KV_DIM))
        return jnp.concatenate([front, new.reshape(nb, seq, KV_DIM)], axis=1).astype(BF16)

    hp = _attn_prompt(qp, padded(kp, ks), padded(vp, vs), _prompt_bias(rel_bias_table),
                      _lane_sink(attn_sink[0], TQ), hp, *wo)
    moe_w = (row(norm_ffn[1]), router_w[0].T.astype(BF16), router_b[0].reshape(N_EXPERTS, 1), moe_w1[0].astype(BF16),
             moe_w3[0].astype(BF16), moe_w2[0].astype(BF16), row(norm_final))
    ys = _moe_final(hs, *moe_w, tb=hs.shape[0], subs=1, tile_rows=(352,))
    yp = _moe_final(hp, *moe_w, tb=1024, subs=2, tile_rows=(112, 144, 176, 208))

    n_real = nsb * SEG
    y_prompt = yp.reshape(nb, seq, D_MODEL)
    y_sample = ys[:n_real].reshape(nsb, SEG, D_MODEL)

    def split_state(st, n):
        st = st[:n].reshape(n, 2, 2, HALF_STATE)
        return tuple(st[:, :, ri].reshape(1, n, S5_GROUPS, S5_P) for ri in range(2))

    s5rp, s5ip = split_state(state_p, nb)
    s5rs, s5is = split_state(state_s, nsb)
    kp4 = kp.reshape(nb, seq, KV_DIM)[:, -WINDOW:].reshape(1, nb, WINDOW, N_KV, HEAD_DIM)
    vp4 = vp.reshape(nb, seq, KV_DIM)[:, -WINDOW:].reshape(1, nb, WINDOW, N_KV, HEAD_DIM)
    ks4 = ks[:n_real].reshape(1, nsb, SEG, N_KV, HEAD_DIM)
    vs4 = vs[:n_real].reshape(1, nsb, SEG, N_KV, HEAD_DIM)
    conv_p4 = jnp.swapaxes(conv_p, 0, 1)[None]
    conv_s4 = jnp.swapaxes(conv_s[:, :nsb], 0, 1)[None]
    return (y_prompt, y_sample, conv_p4, conv_s4, s5rp, s5ip, s5rs, s5is, kp4, vp4, ks4, vs4)
```

```python
import functools
import math

import numpy as np
import jax
import jax.numpy as jnp
from jax import lax
from jax.experimental import pallas as pl
from jax.experimental.pallas import tpu as pltpu

F32 = jnp.float32
BF16 = jnp.bfloat16

D_MODEL = 1024
CONV_CH = 512
S5_CH = 512
S5_GROUP = 16
S5_GROUPS = 32
S5_P = 64
S5_STATE = S5_GROUPS * S5_P
HALF_CH = S5_CH // 2
HALF_STATE = S5_STATE // 2
D_FF = 2816
N_HEADS = 16
N_KV = 2
GQ = N_HEADS // N_KV
HEAD_DIM = 64
KV_DIM = N_KV * HEAD_DIM
WINDOW = 128
CHUNK = 64
N_META = 16
N_BUCKETS = 32
MAX_DIST = 128
N_EXPERTS = 8
D_FF_E = 1024
EPS = 1e-6
NEG = -1e30

SEG = 32
N_SAMPLE_SEG = 32
META_SEG = N_SAMPLE_SEG
N_SMALL_SEG = 40
TM_MIX = 256
CHUNK_ROWS = SEG
SCAN_LANES = 512
TQ = 128
TQ_PER_STEP = 8
SEG_PER_STEP = 8
VMEM_LIMIT = 56 * 1024 * 1024


def _const_spec(shape):
    nd = len(shape)
    return pl.BlockSpec(shape, lambda *_: (0,) * nd, pipeline_mode=pl.Buffered(1))


def _params(*sem):
    return pltpu.CompilerParams(dimension_semantics=sem, vmem_limit_bytes=VMEM_LIMIT)


def _rms(x, g):
    return x * lax.rsqrt(jnp.mean(x * x, axis=-1, keepdims=True) + EPS) * g


def _dot(a, b):
    return jnp.dot(a, b, preferred_element_type=F32)


def _even_mixer_kernel(x_ref, gm_ref, perm_ref, unperm_ref, win_ref, cw_ref, cb_ref, cinit_ref, sinit_ref,
                       ab_ref, hop_ref, pw_ref, bb_ref, cc_ref, d_ref, gluw_ref, glub_ref, wout_ref,
                       h_ref, cout_ref, sout_ref, bu_ref, ccarry_ref, scarry_ref, *, carried, tiles_per_seq):
    if carried:
        @pl.when(pl.program_id(0) % tiles_per_seq == 0)
        def _load_initial_state():
            ccarry_ref[...] = cinit_ref[...]
            scarry_ref[...] = sinit_ref[...]

    streams = range(x_ref.shape[0])
    mid = [_mixer_project(b, x_ref, gm_ref, perm_ref, win_ref, cw_ref, cb_ref, cinit_ref, bb_ref, cout_ref, bu_ref,
                          ccarry_ref, carried=carried) for b in streams]
    for b in streams:
        _mixer_recur(b, sinit_ref, ab_ref, hop_ref, pw_ref, sout_ref, bu_ref, scarry_ref, carried=carried)
        _mixer_output(b, *mid[b], unperm_ref, cc_ref, d_ref, gluw_ref, glub_ref, wout_ref, h_ref, bu_ref)


def _mixer_project(b, x_ref, gm_ref, perm_ref, win_ref, cw_ref, cb_ref, cinit_ref, bb_ref, cout_ref, bu_ref,
                   ccarry_ref, *, carried):
    tm = TM_MIX
    x = x_ref[b]
    hn = _rms(x, gm_ref[...]).astype(BF16)
    hn = _dot(perm_ref[...], hn).astype(BF16)
    proj = _dot(hn, win_ref[...])
    g_b = proj[:, :CONV_CH]
    cin = proj[:, CONV_CH:2 * CONV_CH] * proj[:, 2 * CONV_CH:3 * CONV_CH]
    u = proj[:, 3 * CONV_CH:]
    ub = u.astype(BF16)
    for half in range(2):
        bu_ref[b, :, half * S5_STATE:(half + 1) * S5_STATE] = _dot(ub[:, half * HALF_CH:(half + 1) * HALF_CH],
                                                                  bb_ref[half])

    last, before_last = cin[tm - 8:tm], cin[tm - 16:tm - 8]
    if carried:
        first_chunk = lax.broadcasted_iota(jnp.int32, (8, CONV_CH), 0) == 0
        newer = jnp.where(first_chunk, ccarry_ref[1, b:b + 1, :], pltpu.roll(last, 1, 0))
        older = jnp.where(first_chunk, ccarry_ref[0, b:b + 1, :], pltpu.roll(before_last, 1, 0))
        for i, rows in enumerate((before_last, last)):
            ccarry_ref[i, b:b + 1, :] = rows[7:8]
            cout_ref[i, b:b + 1, :] = rows[7:8]
    else:
        older, newer = cinit_ref[0], cinit_ref[1]
        cout_ref[0] = before_last
        cout_ref[1] = last
    ext = jnp.concatenate([older, newer, cin], axis=0)
    cw = cw_ref[...]
    out_a = g_b * (cw[0:1] * ext[:tm] + cw[1:2] * ext[8:tm + 8] + cw[2:3] * cin + cb_ref[...])
    return x, u, out_a


def _mixer_recur(b, sinit_ref, ab_ref, hop_ref, pw_ref, sout_ref, bu_ref, scarry_ref, *, carried):
    for c in range(S5_STATE // SCAN_LANES):
        half, within = divmod(c * SCAN_LANES, HALF_STATE)
        nat = slice(c * SCAN_LANES, (c + 1) * SCAN_LANES)
        re_cols = slice(half * S5_STATE + within, half * S5_STATE + within + SCAN_LANES)
        im_cols = slice(re_cols.start + HALF_STATE, re_cols.stop + HALF_STATE)
        ar, ai = ab_ref[0, :, nat], ab_ref[1, :, nat]
        if carried:
            sr = si = jnp.zeros((8, SCAN_LANES), F32)
        else:
            sr, si = sinit_ref[:, re_cols], sinit_ref[:, im_cols]
        for g in range(CHUNK_ROWS):
            rows = slice(8 * g, 8 * g + 8)
            sr, si = (ar * sr - ai * si + bu_ref[b, rows, re_cols], ar * si + ai * sr + bu_ref[b, rows, im_cols])
            bu_ref[b, rows, re_cols] = sr
            bu_ref[b, rows, im_cols] = si
        if not carried:
            sout_ref[:, re_cols] = sr
            sout_ref[:, im_cols] = si
            continue
        first_chunk = lax.broadcasted_iota(jnp.int32, (8, SCAN_LANES), 0) == 0
        er = jnp.where(first_chunk, scarry_ref[b:b + 1, re_cols], pltpu.roll(sr, 1, 0))
        ei = jnp.where(first_chunk, scarry_ref[b:b + 1, im_cols], pltpu.roll(si, 1, 0))
        for k, sh in enumerate((1, 2, 4)):
            wr, wi = hop_ref[2 * k, :, nat], hop_ref[2 * k + 1, :, nat]
            tr, ti = pltpu.roll(er, sh, 0), pltpu.roll(ei, sh, 0)
            er, ei = er + wr * tr - wi * ti, ei + wr * ti + wi * tr
        wr, wi = hop_ref[6, :, nat], hop_ref[7, :, nat]
        fr, fi = wr * er - wi * ei + sr, wr * ei + wi * er + si
        for ref in (scarry_ref, sout_ref):
            ref[b:b + 1, re_cols] = fr[7:8]
            ref[b:b + 1, im_cols] = fi[7:8]
        for g in range(CHUNK_ROWS):
            rows = slice(8 * g, 8 * g + 8)
            pr, pi = pw_ref[0, g, :, nat], pw_ref[1, g, :, nat]
            bu_ref[b, rows, re_cols] += pr * er - pi * ei
            bu_ref[b, rows, im_cols] += pr * ei + pi * er


def _mixer_output(b, x, u, out_a, unperm_ref, cc_ref, d_ref, gluw_ref, glub_ref, wout_ref, h_ref, bu_ref):
    y = jnp.concatenate([_dot(bu_ref[b, :, half * S5_STATE:(half + 1) * S5_STATE].astype(BF16), cc_ref[half])
                         for half in range(2)], axis=-1) + d_ref[...] * u
    z = jax.nn.gelu(y)
    out_b = z * jax.nn.sigmoid(_dot(z.astype(BF16), gluw_ref[...]) + glub_ref[...])
    mix = jnp.concatenate([out_a, out_b], axis=-1).astype(BF16)
    mix = _dot(unperm_ref[...], mix).astype(BF16)
    h_ref[b] = x + _dot(mix, wout_ref[...])


def _even_mixer(x, gm, w_in, cw, cb, cinit, sinit, ab, hop, pw, bb, cc, d, glu_w, glu_b, w_out, *, carried):
    nb, rows = x.shape[0], x.shape[1]
    tm = TM_MIX
    chunks = tm // CHUNK_ROWS
    assert chunks == 8 and rows % tm == 0 and (carried or nb == 1)
    per_step = nb if carried else chunks
    tiles_per_seq = rows // tm if carried else 1
    nseq = sinit.shape[0]
    t_of = np.arange(tm)
    perm = np.zeros((tm, tm), np.float32)
    perm[8 * (t_of % CHUNK_ROWS) + t_of // CHUNK_ROWS, t_of] = 1.0
    kern = functools.partial(_even_mixer_kernel, carried=carried, tiles_per_seq=tiles_per_seq)
    return pl.pallas_call(
        kern,
        grid=(rows // tm,),
        in_specs=[
            pl.BlockSpec((nb, tm, D_MODEL), lambda t: (0, t, 0)),
            _const_spec((1, D_MODEL)),
            _const_spec((tm, tm)),
            _const_spec((tm, tm)),
            _const_spec((D_MODEL, 4 * CONV_CH)),
            _const_spec((3, CONV_CH)),
            _const_spec((1, CONV_CH)),
            pl.BlockSpec((2, per_step, CONV_CH), lambda t: (0, t // tiles_per_seq, 0)),
            pl.BlockSpec((per_step, 2 * S5_STATE), lambda t: (t // tiles_per_seq, 0)),
            _const_spec((2, 8, S5_STATE)),
            _const_spec((8, 8, S5_STATE)),
            _const_spec((2, CHUNK_ROWS, 8, S5_STATE)),
            _const_spec((2, HALF_CH, S5_STATE)),
            _const_spec((2, S5_STATE, HALF_CH)),
            _const_spec((1, S5_CH)),
            _const_spec((S5_CH, S5_CH)),
            _const_spec((1, S5_CH)),
            _const_spec((D_MODEL, D_MODEL)),
        ],
        out_specs=[
            pl.BlockSpec((nb, tm, D_MODEL), lambda t: (0, t, 0)),
            pl.BlockSpec((2, per_step, CONV_CH), lambda t: (0, t // tiles_per_seq, 0)),
            pl.BlockSpec((per_step, 2 * S5_STATE), lambda t: (t // tiles_per_seq, 0)),
        ],
        out_shape=[
            jax.ShapeDtypeStruct((nb, rows, D_MODEL), F32),
            jax.ShapeDtypeStruct((2, nseq, CONV_CH), F32),
            jax.ShapeDtypeStruct((nseq, 2 * S5_STATE), F32),
        ],
        scratch_shapes=[
            pltpu.VMEM((nb, tm, 2 * S5_STATE), F32),
            pltpu.VMEM((2, nb, CONV_CH), F32),
            pltpu.VMEM((nb, 2 * S5_STATE), F32),
        ],
        compiler_params=_params("arbitrary"),
        name="even_mixer",
    )(x, gm, jnp.asarray(perm, BF16), jnp.asarray(perm.T, BF16), w_in, cw, cb, cinit, sinit, ab, hop, pw, bb, cc, d,
      glu_w, glu_b, w_out)


def _ffn_kernel(x_ref, g_ref, w1_ref, w3_ref, w2_ref, o_ref):
    x = x_ref[...]
    hn = _rms(x, g_ref[...]).astype(BF16)
    a = _dot(hn, w1_ref[...])
    b = _dot(hn, w3_ref[...])
    o_ref[...] = x + _dot((a * jax.nn.sigmoid(a) * b).astype(BF16), w2_ref[...])


def _ffn(x, g, w1, w3, w2, *, tm):
    rows = x.shape[0]
    return pl.pallas_call(
        _ffn_kernel,
        grid=(rows // tm,),
        in_specs=[
            pl.BlockSpec((tm, D_MODEL), lambda t: (t, 0)),
            _const_spec((1, D_MODEL)),
            _const_spec((D_MODEL, D_FF)),
            _const_spec((D_MODEL, D_FF)),
            _const_spec((D_FF, D_MODEL)),
        ],
        out_specs=pl.BlockSpec((tm, D_MODEL), lambda t: (t, 0)),
        out_shape=jax.ShapeDtypeStruct((rows, D_MODEL), F32),
        compiler_params=_params("parallel"),
        name="ffn",
    )(x, g, w1, w3, w2)


def _qkv_kernel(x_ref, g_ref, w_ref, b_ref, q_ref, k_ref, v_ref):
    hn = _rms(x_ref[...], g_ref[...]).astype(BF16)
    qkv = _dot(hn, w_ref[...]) + b_ref[...]
    nq = N_HEADS * HEAD_DIM
    q_ref[...] = (qkv[:, :nq] * (HEAD_DIM ** -0.5)).astype(BF16)
    k_ref[...] = qkv[:, nq:nq + KV_DIM]
    v_ref[...] = qkv[:, nq + KV_DIM:]


def _qkv(x, g, w, b, *, tm):
    rows = x.shape[0]
    ncol = (N_HEADS + 2 * N_KV) * HEAD_DIM
    return pl.pallas_call(
        _qkv_kernel,
        grid=(rows // tm,),
        in_specs=[
            pl.BlockSpec((tm, D_MODEL), lambda t: (t, 0)),
            _const_spec((1, D_MODEL)),
            _const_spec((D_MODEL, ncol)),
            _const_spec((1, ncol)),
        ],
        out_specs=[
            pl.BlockSpec((tm, N_HEADS * HEAD_DIM), lambda t: (t, 0)),
            pl.BlockSpec((tm, KV_DIM), lambda t: (t, 0)),
            pl.BlockSpec((tm, KV_DIM), lambda t: (t, 0)),
        ],
        out_shape=[
            jax.ShapeDtypeStruct((rows, N_HEADS * HEAD_DIM), BF16),
            jax.ShapeDtypeStruct((rows, KV_DIM), F32),
            jax.ShapeDtypeStruct((rows, KV_DIM), F32),
        ],
        compiler_params=_params("parallel"),
        name="qkv",
    )(x, g, w, b)


def _attend_groups(q, keys, vals, bias_of_group, sink_of_group, o_scr, row0):
    nq = q.shape[0]
    for kv in range(N_KV):
        kv_cols = slice(kv * HEAD_DIM, (kv + 1) * HEAD_DIM)
        heads = [slice((kv * GQ + g) * HEAD_DIM, (kv * GQ + g + 1) * HEAD_DIM) for g in range(GQ)]
        qs = jnp.concatenate([q[:, cols] for cols in heads], axis=0)
        st = lax.dot_general(keys[:, kv_cols], qs, (((1,), (1,)), ((), ())),
                             preferred_element_type=F32) + bias_of_group(kv)
        sk = sink_of_group(kv)
        m = jnp.maximum(jnp.max(st, axis=0, keepdims=True), sk)
        p = jnp.exp(st - m)
        den = jnp.sum(p, axis=0, keepdims=True) + jnp.exp(sk - m)
        o = lax.dot_general((p * (1.0 / den)).astype(BF16), vals[:, kv_cols], (((0,), (0,)), ((), ())),
                            preferred_element_type=F32)
        for g, cols in enumerate(heads):
            o_scr[row0:row0 + nq, cols] = o[g * nq:(g + 1) * nq]


def _attn_prompt_kernel(q_ref, k_ref, v_ref, bias_ref, sink_ref, x_ref, wo_ref, bo_ref, o_ref, o_scr):
    t = pl.program_id(1)
    for u in range(TQ_PER_STEP):
        tile = t * TQ_PER_STEP + u
        start = pl.multiple_of(tile * TQ, TQ)
        keys = k_ref[0, pl.ds(start, TQ + WINDOW), :]
        vals = v_ref[0, pl.ds(start, TQ + WINDOW), :]
        variant = jnp.minimum(tile, 1)
        _attend_groups(q_ref[u * TQ:(u + 1) * TQ], keys, vals, lambda kv: bias_ref[variant, kv],
                       lambda kv: sink_ref[kv], o_scr, u * TQ)
    o_ref[...] = x_ref[...] + _dot(o_scr[...].astype(BF16), wo_ref[...]) + bo_ref[...]


def _attn_prompt(q, kpad, vpad, bias, sink, x, w_o, b_o):
    nb, lpad = kpad.shape[0], kpad.shape[1]
    tm = TQ_PER_STEP * TQ
    steps = (lpad - WINDOW) // tm
    return pl.pallas_call(
        _attn_prompt_kernel,
        grid=(nb, steps),
        in_specs=[
            pl.BlockSpec((tm, D_MODEL), lambda b, t: (b * steps + t, 0)),
            pl.BlockSpec((1, lpad, KV_DIM), lambda b, t: (b, 0, 0)),
            pl.BlockSpec((1, lpad, KV_DIM), lambda b, t: (b, 0, 0)),
            _const_spec((2, N_KV, TQ + WINDOW, GQ * TQ)),
            _const_spec((N_KV, 1, GQ * TQ)),
            pl.BlockSpec((tm, D_MODEL), lambda b, t: (b * steps + t, 0)),
            _const_spec((D_MODEL, D_MODEL)),
            _const_spec((1, D_MODEL)),
        ],
        out_specs=pl.BlockSpec((tm, D_MODEL), lambda b, t: (b * steps + t, 0)),
        out_shape=jax.ShapeDtypeStruct(x.shape, F32),
        scratch_shapes=[pltpu.VMEM((tm, D_MODEL), F32)],
        compiler_params=_params("parallel", "arbitrary"),
        name="attn_prompt",
    )(q, kpad, vpad, bias, sink, x, w_o, b_o)


def _attn_small_kernel(q_ref, kc_ref, vc_ref, kn_ref, vn_ref, bias_ref, sink_ref, x_ref, wo_ref, bo_ref,
                       o_ref, o_scr):
    step = pl.program_id(0)
    for j in range(SEG_PER_STEP):
        rows = slice(j * SEG, (j + 1) * SEG)
        keys = jnp.concatenate([kc_ref[j], kn_ref[rows]], axis=0).astype(BF16)
        vals = jnp.concatenate([vc_ref[j], vn_ref[rows]], axis=0).astype(BF16)
        variant = jnp.where(step * SEG_PER_STEP + j == META_SEG, 1, 0)
        _attend_groups(q_ref[rows], keys, vals, lambda kv: bias_ref[variant, kv], lambda kv: sink_ref[kv], o_scr,
                       j * SEG)
    o_ref[...] = x_ref[...] + _dot(o_scr[...].astype(BF16), wo_ref[...]) + bo_ref[...]


def _attn_small(q, kc, vc, kn, vn, bias, sink, x, w_o, b_o):
    tm = SEG_PER_STEP * SEG
    steps = x.shape[0] // tm
    cache_blocks = kc.shape[0] // SEG_PER_STEP
    cache_spec = pl.BlockSpec((SEG_PER_STEP, WINDOW, KV_DIM), lambda s: (jnp.minimum(s, cache_blocks - 1), 0, 0))
    return pl.pallas_call(
        _attn_small_kernel,
        grid=(steps,),
        in_specs=[
            pl.BlockSpec((tm, D_MODEL), lambda s: (s, 0)),
            cache_spec,
            cache_spec,
            pl.BlockSpec((tm, KV_DIM), lambda s: (s, 0)),
            pl.BlockSpec((tm, KV_DIM), lambda s: (s, 0)),
            _const_spec((2, N_KV, WINDOW + SEG, GQ * SEG)),
            _const_spec((N_KV, 1, GQ * SEG)),
            pl.BlockSpec((tm, D_MODEL), lambda s: (s, 0)),
            _const_spec((D_MODEL, D_MODEL)),
            _const_spec((1, D_MODEL)),
        ],
        out_specs=pl.BlockSpec((tm, D_MODEL), lambda s: (s, 0)),
        out_shape=jax.ShapeDtypeStruct(x.shape, F32),
        scratch_shapes=[pltpu.VMEM((tm, D_MODEL), F32)],
        compiler_params=_params("parallel"),
        name="attn_small",
    )(q, kc, vc, kn, vn, bias, sink, x, w_o, b_o)


def _expert_of_step(block, step):
    return jnp.where(block % 2 == 0, step, N_EXPERTS - 1 - step)


def _moe_kernel(x_ref, g_ref, rwt_ref, rb_ref, w1_ref, w3_ref, w2_ref, gf_ref, o_ref,
                hn_scr, slot_scr, gate_scr, cnt_smem, acc_scr, yall_scr, *, subs, tile_rows):
    step = pl.program_id(1)
    e = _expert_of_step(pl.program_id(0), step)
    tb = x_ref.shape[0]
    ts = tb // subs
    cap = tile_rows[-1]
    sub_cols = [slice(s * ts, (s + 1) * ts) for s in range(subs)]

    @pl.when(step == 0)
    def _route():
        hn = _rms(x_ref[...], g_ref[...]).astype(BF16)
        hn_scr[...] = hn
        logits = lax.dot_general(rwt_ref[...], hn, (((1,), (1,)), ((), ())),
                                 preferred_element_type=F32) + rb_ref[...]
        row = lax.broadcasted_iota(jnp.int32, logits.shape, 0).astype(F32)
        v1 = jnp.max(logits, axis=0, keepdims=True)
        i1 = jnp.min(jnp.where(logits == v1, row, float(N_EXPERTS)), axis=0, keepdims=True)
        rest = jnp.where(row == i1, -jnp.inf, logits)
        v2 = jnp.max(rest, axis=0, keepdims=True)
        i2 = jnp.min(jnp.where(rest == v2, row, float(N_EXPERTS)), axis=0, keepdims=True)
        e2 = jnp.exp(v2 - v1)
        den = 1.0 + e2
        gate_scr[...] = jnp.where(row == i1, 1.0 / den, 0.0) + jnp.where(row == i2, e2 / den, 0.0)
        chosen = (row == i1) | (row == i2)
        ind = jnp.where(chosen, 1.0, 0.0)
        before = lax.broadcasted_iota(jnp.int32, (ts, ts), 0) < lax.broadcasted_iota(jnp.int32, (ts, ts), 1)
        before = jnp.where(before, 1.0, 0.0).astype(BF16)
        for s, cols in enumerate(sub_cols):
            rank = _dot(ind[:, cols].astype(BF16), before)
            slot_scr[:, cols] = jnp.where(chosen[:, cols], rank, -1.0).astype(jnp.int32)
            for k in range(N_EXPERTS):
                cnt_smem[k * subs + s] = jnp.sum(ind[k:k + 1, cols]).astype(jnp.int32)
        acc_scr[...] = jnp.zeros_like(acc_scr)

    def _scatter(onehot, rows):
        return lax.dot_general(onehot, rows, (((0,), (0,)), ((), ())), preferred_element_type=F32)

    def _expert_tile(i, rows=cap):
        packed_row = lax.broadcasted_iota(jnp.int32, (rows, ts), 0) + i * cap
        hits = [packed_row == slot_scr[pl.ds(e, 1), cols] for cols in sub_cols]
        onehots = [jnp.where(hit, 1.0, 0.0).astype(BF16) for hit in hits]
        xe = jnp.concatenate([_dot(onehot, hn_scr[cols, :]) for onehot, cols in zip(onehots, sub_cols)],
                             axis=0).astype(BF16)
        a = _dot(xe, w1_ref[0])
        b = _dot(xe, w3_ref[0])
        y = _dot((a * jax.nn.sigmoid(a) * b).astype(BF16), w2_ref[0])
        g_row = jnp.concatenate([jnp.sum(jnp.where(hit, gate_scr[pl.ds(e, 1), cols], 0.0), axis=1, keepdims=True)
                                 for hit, cols in zip(hits, sub_cols)], axis=0)
        ys = (y * g_row).astype(BF16)
        return onehots, [ys[s * rows:(s + 1) * rows] for s in range(subs)]

    base = pl.multiple_of(e * cap, 16)
    count = cnt_smem[e * subs]
    for s in range(1, subs):
        count = jnp.maximum(count, cnt_smem[e * subs + s])
    for lower, rows in zip((-1,) + tile_rows, tile_rows):
        fits = (count > lower) & (count <= rows) if rows < cap else count > lower

        @pl.when(fits)
        def _first_tile(rows=rows):
            for s, ys in enumerate(_expert_tile(0, rows)[1]):
                yall_scr[s, pl.ds(base, rows), :] = ys
                if rows < cap:
                    yall_scr[s, pl.ds(base + rows, cap - rows), :] = jnp.zeros((cap - rows, D_MODEL), BF16)

    def _overflow_tile(i, carry):
        onehots, yss = _expert_tile(i)
        for onehot, ys, cols in zip(onehots, yss, sub_cols):
            acc_scr[cols, :] += _scatter(onehot, ys)
        return carry

    lax.fori_loop(1, (count + cap - 1) // cap, _overflow_tile, 0)

    @pl.when(step == N_EXPERTS - 1)
    def _finish():
        packed_row = lax.broadcasted_iota(jnp.int32, (cap, ts), 0)
        for s, cols in enumerate(sub_cols):
            onehot = jnp.concatenate([jnp.where(packed_row == slot_scr[k:k + 1, cols], 1.0, 0.0).astype(BF16)
                                      for k in range(N_EXPERTS)], axis=0)
            o_ref[cols, :] = _rms(x_ref[cols, :] + acc_scr[cols, :] + _scatter(onehot, yall_scr[s]), gf_ref[...])


def _moe_final(x, g, rwt, rb, w1, w3, w2, gf, *, tb, subs, tile_rows):
    rows, cap = x.shape[0], tile_rows[-1]
    return pl.pallas_call(
        functools.partial(_moe_kernel, subs=subs, tile_rows=tile_rows),
        grid=(rows // tb, N_EXPERTS),
        in_specs=[
            pl.BlockSpec((tb, D_MODEL), lambda t, e: (t, 0)),
            _const_spec((1, D_MODEL)),
            _const_spec((N_EXPERTS, D_MODEL)),
            _const_spec((N_EXPERTS, 1)),
            pl.BlockSpec((1, D_MODEL, D_FF_E), lambda t, s: (_expert_of_step(t, s), 0, 0)),
            pl.BlockSpec((1, D_MODEL, D_FF_E), lambda t, s: (_expert_of_step(t, s), 0, 0)),
            pl.BlockSpec((1, D_FF_E, D_MODEL), lambda t, s: (_expert_of_step(t, s), 0, 0)),
            _const_spec((1, D_MODEL)),
        ],
        out_specs=pl.BlockSpec((tb, D_MODEL), lambda t, e: (t, 0)),
        out_shape=jax.ShapeDtypeStruct((rows, D_MODEL), F32),
        scratch_shapes=[
            pltpu.VMEM((tb, D_MODEL), BF16),
            pltpu.VMEM((N_EXPERTS, tb), jnp.int32),
            pltpu.VMEM((N_EXPERTS, tb), F32),
            pltpu.SMEM((N_EXPERTS * subs,), jnp.int32),
            pltpu.VMEM((tb, D_MODEL), F32),
            pltpu.VMEM((subs, N_EXPERTS * cap, D_MODEL), BF16),
        ],
        compiler_params=_params("parallel", "arbitrary"),
        name="moe_final",
    )(x, g, rwt, rb, w1, w3, w2, gf)


def _t5_bucket_np(rel):
    half = N_BUCKETS // 2
    max_exact = half // 2
    ret = np.where(rel > 0, half, 0)
    n = np.abs(rel)
    nf = np.maximum(n, 1).astype(np.float32)
    large = max_exact + (np.log(nf / np.float32(max_exact)) / np.float32(math.log(MAX_DIST / max_exact))
                         * np.float32(half - max_exact)).astype(np.int32)
    large = np.minimum(large, half - 1)
    return ret + np.where(n < max_exact, n, large)


def _bias_variants(table, bucket, keeps):
    nq, nk = bucket.shape
    onehot = jnp.asarray(np.eye(N_BUCKETS, dtype=np.float32)[bucket])
    bias = jnp.einsum('rjb,bh->hjr', onehot, table, precision=lax.Precision.HIGHEST)
    bias = jnp.stack([jnp.where(keep.T[None], bias, NEG) for keep in keeps], axis=0)
    bias = bias.reshape(len(keeps), N_KV, GQ, nk, nq)
    return jnp.transpose(bias, (0, 1, 3, 2, 4)).reshape(len(keeps), N_KV, nk, GQ * nq)


def _lane_sink(sink, nq):
    return jnp.broadcast_to(sink.reshape(N_KV, 1, GQ, 1), (N_KV, 1, GQ, nq)).reshape(N_KV, 1, GQ * nq)


def _prompt_bias(table):
    r = np.arange(TQ)[:, None]
    j = np.arange(TQ + WINDOW)[None, :]
    band = (j - CHUNK * (r // CHUNK) >= 0) & (j - CHUNK * (r // CHUNK) < WINDOW + CHUNK)
    first = band & (j >= WINDOW - N_META)
    return _bias_variants(table, _t5_bucket_np(j - WINDOW - r), [first, band])


def _small_bias(table):
    i = np.arange(SEG)[:, None]
    m = np.arange(WINDOW + SEG)[None, :]
    everything = np.ones((SEG, WINDOW + SEG), bool)
    meta_only = everything & (m >= WINDOW + SEG - N_META)
    return _bias_variants(table, _t5_bucket_np(m - WINDOW - i), [everything, meta_only])


def _pack_state(re, im):
    return jnp.concatenate([re[..., :HALF_STATE], im[..., :HALF_STATE], re[..., HALF_STATE:], im[..., HALF_STATE:]],
                           axis=-1)


def _cmul(ar, ai, br, bi):
    return ar * br - ai * bi, ar * bi + ai * br


def _s5_prepare(a_re, a_im, log_dt, b_re, b_im, c_re, c_im):
    dt = jnp.exp(log_dt)[:, None]
    mag = jnp.exp(a_re * dt)
    ab_re, ab_im = mag * jnp.cos(a_im * dt), mag * jnp.sin(a_im * dt)
    den = a_re * a_re + a_im * a_im
    num_re = ab_re - 1.0
    cf_re = (num_re * a_re + ab_im * a_im) / den
    cf_im = (ab_im * a_re - num_re * a_im) / den
    bb_re = cf_re[..., None] * b_re - cf_im[..., None] * b_im
    bb_im = cf_re[..., None] * b_im + cf_im[..., None] * b_re

    def group_diag(v):
        r, c = v.shape[1], v.shape[2]
        gh = S5_GROUPS // 2
        spread = np.tile(np.eye(c, dtype=np.float32), (1, gh))
        wide = jnp.einsum('hrc,cn->hrn', v.reshape(2, gh * r, c), spread, precision=lax.Precision.HIGHEST)
        keep = (np.arange(gh * r)[:, None] // r) == (np.arange(gh * c)[None, :] // c)
        return jnp.where(keep, wide, 0.0)

    def t(v):
        return jnp.swapaxes(v, 1, 2)

    bb = jnp.concatenate([group_diag(t(bb_re)), group_diag(t(bb_im))], axis=2).astype(BF16)
    cc = jnp.concatenate([group_diag(t(c_re)), -group_diag(t(c_im))], axis=1).astype(BF16)

    pr, pi = ab_re.reshape(1, S5_STATE), ab_im.reshape(1, S5_STATE)
    n = 1
    while n < CHUNK_ROWS:
        qr, qi = _cmul(pr, pi, pr[n - 1:n], pi[n - 1:n])
        pr, pi = jnp.concatenate([pr, qr], axis=0), jnp.concatenate([pi, qi], axis=0)
        n *= 2

    def tile8(v):
        return jnp.broadcast_to(v, (8,) + v.shape[1:])

    ab = jnp.stack([tile8(pr[0:1]), tile8(pi[0:1])])
    sub = np.arange(8)[:, None]
    hop, w = [], (pr[CHUNK_ROWS - 1:], pi[CHUNK_ROWS - 1:])
    for k in (1, 2, 4):
        hop += [jnp.where(sub >= k, w[0], 0.0), jnp.where(sub >= k, w[1], 0.0)]
        if k < 4:
            w = _cmul(*w, *w)
    hop = jnp.stack(hop + [tile8(pr[CHUNK_ROWS - 1:]), tile8(pi[CHUNK_ROWS - 1:])])
    pw = jnp.stack([jnp.broadcast_to(p[:, None, :], (CHUNK_ROWS, 8, S5_STATE)) for p in (pr, pi)])
    return bb, cc, ab, hop, pw


def kernel(x_prompt, x_sample, cache_conv, state_s5_re, state_s5_im, cache_swa_k, cache_swa_v, meta_tokens, rel_bias_table, norm_mix, norm_ffn, norm_final, w_in0, conv_w, conv_b, s5_a_re, s5_a_im, s5_log_dt, s5_b_re, s5_b_im, s5_c_re, s5_c_im, s5_d, s5_glu_w, s5_glu_b, w_out0, ffn_w1, ffn_w3, ffn_w2, w_qkv, b_qkv, attn_sink, w_o, b_o, router_w, router_b, moe_w1, moe_w3, moe_w2):
    nb, seq = x_prompt.shape[0], x_prompt.shape[1]
    nsb = x_sample.shape[0]
    assert x_sample.shape[1] == SEG and nsb == N_SAMPLE_SEG and seq % TM_MIX == 0

    def row(v):
        return v.reshape(1, -1)

    meta_seg = jnp.concatenate([jnp.zeros((SEG - N_META, D_MODEL), F32), meta_tokens], axis=0)
    filler = jnp.zeros(((N_SMALL_SEG - N_SAMPLE_SEG - 1) * SEG, D_MODEL), F32)
    xs = jnp.concatenate([x_sample.reshape(nsb * SEG, D_MODEL), meta_seg, filler], axis=0)
    n_extra = N_SMALL_SEG - N_SAMPLE_SEG

    bb, cc, ab, hop, pw = _s5_prepare(s5_a_re[0], s5_a_im[0], s5_log_dt[0], s5_b_re[0], s5_b_im[0], s5_c_re[0],
                                      s5_c_im[0])
    mixer_w = (row(norm_mix[0]), w_in0[0].astype(BF16), conv_w[0], row(conv_b[0]))
    mixer_w2 = (ab, hop, pw, bb, cc, row(s5_d[0]), s5_glu_w[0].astype(BF16), row(s5_glu_b[0]), w_out0[0].astype(BF16))
    cinit_s = jnp.concatenate([jnp.swapaxes(cache_conv[0], 0, 1), jnp.zeros((2, n_extra, CONV_CH), F32)], axis=1)
    sinit_s = _pack_state(state_s5_re[0].reshape(nsb, S5_STATE), state_s5_im[0].reshape(nsb, S5_STATE))
    sinit_s = jnp.concatenate([sinit_s, jnp.zeros((n_extra, 2 * S5_STATE), F32)], axis=0)
    hs, conv_s, state_s = _even_mixer(xs[None], *mixer_w, cinit_s, sinit_s, *mixer_w2, carried=False)
    hs = hs[0]
    cinit_p = jnp.broadcast_to(conv_s[:, META_SEG:META_SEG + 1], (2, nb, CONV_CH))
    sinit_p = jnp.broadcast_to(state_s[META_SEG:META_SEG + 1], (nb, 2 * S5_STATE))
    hp, conv_p, state_p = _even_mixer(x_prompt, *mixer_w, cinit_p, sinit_p, *mixer_w2, carried=True)
    hp = hp.reshape(nb * seq, D_MODEL)
    ffn_w = (row(norm_ffn[0]), ffn_w1[0].astype(BF16), ffn_w3[0].astype(BF16), ffn_w2[0].astype(BF16))
    hs = _ffn(hs, *ffn_w, tm=256)
    hp = _ffn(hp, *ffn_w, tm=512)

    qkv_w = (row(norm_mix[1]), w_qkv[0].astype(BF16), row(b_qkv[0]))
    qs, ks, vs = _qkv(hs, *qkv_w, tm=640)
    qp, kp, vp = _qkv(hp, *qkv_w, tm=1024)
    wo = (w_o[0].astype(BF16), row(b_o[0]))
    hs = _attn_small(qs, cache_swa_k[0].reshape(nsb, WINDOW, KV_DIM), cache_swa_v[0].reshape(nsb, WINDOW, KV_DIM),
                     ks, vs, _small_bias(rel_bias_table), _lane_sink(attn_sink[0], SEG), hs, *wo)

    def padded(new, small):
        meta_rows = small[META_SEG * SEG:(META_SEG + 1) * SEG]
        front = jnp.concatenate([jnp.zeros((WINDOW - SEG, KV_DIM), F32), meta_rows], axis=0)
        front = jnp.broadcast_to(front[None], (nb, WINDOW, KV_DIM))
        return jnp.concatenate([front, new.reshape(nb, seq, KV_DIM)], axis=1).astype(BF16)

    hp = _attn_prompt(qp, padded(kp, ks), padded(vp, vs), _prompt_bias(rel_bias_table),
                      _lane_sink(attn_sink[0], TQ), hp, *wo)
    moe_w = (row(norm_ffn[1]), router_w[0].T.astype(BF16), router_b[0].reshape(N_EXPERTS, 1), moe_w1[0].astype(BF16),
             moe_w3[0].astype(BF16), moe_w2[0].astype(BF16), row(norm_final))
    ys = _moe_final(hs, *moe_w, tb=hs.shape[0], subs=1, tile_rows=(352,))
    yp = _moe_final(hp, *moe_w, tb=1024, subs=2, tile_rows=(96, 112, 128, 144, 160, 176, 192, 208))

    n_real = nsb * SEG
    y_prompt = yp.reshape(nb, seq, D_MODEL)
    y_sample = ys[:n_real].reshape(nsb, SEG, D_MODEL)

    def split_state(st, n):
        st = st[:n].reshape(n, 2, 2, HALF_STATE)
        return tuple(st[:, :, ri].reshape(1, n, S5_GROUPS, S5_P) for ri in range(2))

    s5rp, s5ip = split_state(state_p, nb)
    s5rs, s5is = split_state(state_s, nsb)
    kp4 = kp.reshape(nb, seq, KV_DIM)[:, -WINDOW:].reshape(1, nb, WINDOW, N_KV, HEAD_DIM)
    vp4 = vp.reshape(nb, seq, KV_DIM)[:, -WINDOW:].reshape(1, nb, WINDOW, N_KV, HEAD_DIM)
    ks4 = ks[:n_real].reshape(1, nsb, SEG, N_KV, HEAD_DIM)
    vs4 = vs[:n_real].reshape(1, nsb, SEG, N_KV, HEAD_DIM)
    conv_p4 = jnp.swapaxes(conv_p, 0, 1)[None]
    conv_s4 = jnp.swapaxes(conv_s[:, :nsb], 0, 1)[None]
    return (y_prompt, y_sample, conv_p4, conv_s4, s5rp, s5ip, s5rs, s5is, kp4, vp4, ks4, vs4)
```

```python
import functools
import math

import numpy as np
import jax
import jax.numpy as jnp
from jax import lax
from jax.experimental import pallas as pl
from jax.experimental.pallas import tpu as pltpu

F32 = jnp.float32
BF16 = jnp.bfloat16

D_MODEL = 1024
CONV_CH = 512
S5_CH = 512
S5_GROUP = 16
S5_GROUPS = 32
S5_P = 64
S5_STATE = S5_GROUPS * S5_P
HALF_CH = S5_CH // 2
HALF_STATE = S5_STATE // 2
D_FF = 2816
N_HEADS = 16
N_KV = 2
GQ = N_HEADS // N_KV
HEAD_DIM = 64
KV_DIM = N_KV * HEAD_DIM
WINDOW = 128
CHUNK = 64
N_META = 16
N_BUCKETS = 32
MAX_DIST = 128
N_EXPERTS = 8
D_FF_E = 1024
EPS = 1e-6
NEG = -1e30

SUBLANES = 8
BF16_TILE_ROWS = 16
SEG = 32
N_SAMPLE_SEG = 32
META_SEG = N_SAMPLE_SEG
N_SMALL_SEG = 40
TM_MIX = 256
CHUNK_ROWS = SEG
SCAN_LANES = 512
TQ = 128
TQ_PER_STEP = 8
SEG_PER_STEP = 8
VMEM_LIMIT = 56 * 1024 * 1024


def _const_spec(shape):
    nd = len(shape)
    return pl.BlockSpec(shape, lambda *_: (0,) * nd, pipeline_mode=pl.Buffered(1))


def _params(*sem):
    return pltpu.CompilerParams(dimension_semantics=sem, vmem_limit_bytes=VMEM_LIMIT)


def _rms(x, g):
    return x * lax.rsqrt(jnp.mean(x * x, axis=-1, keepdims=True) + EPS) * g


def _dot(a, b):
    return jnp.dot(a, b, preferred_element_type=F32)


def _even_mixer_kernel(x_ref, gm_ref, perm_ref, unperm_ref, win_ref, cw_ref, cb_ref, cinit_ref, sinit_ref,
                       ab_ref, hop_ref, pw_ref, bb_ref, cc_ref, d_ref, gluw_ref, glub_ref, wout_ref,
                       h_ref, cout_ref, sout_ref, bu_ref, ccarry_ref, scarry_ref, *, carried, tiles_per_seq):
    if carried:
        @pl.when(pl.program_id(0) % tiles_per_seq == 0)
        def _load_initial_state():
            ccarry_ref[...] = cinit_ref[...]
            scarry_ref[...] = sinit_ref[...]

    streams = range(x_ref.shape[0])
    mid = [_mixer_project(b, x_ref, gm_ref, perm_ref, win_ref, cw_ref, cb_ref, cinit_ref, bb_ref, cout_ref, bu_ref,
                          ccarry_ref, carried=carried) for b in streams]
    for b in streams:
        _mixer_recur(b, sinit_ref, ab_ref, hop_ref, pw_ref, sout_ref, bu_ref, scarry_ref, carried=carried)
        _mixer_output(b, *mid[b], unperm_ref, cc_ref, d_ref, gluw_ref, glub_ref, wout_ref, h_ref, bu_ref)


def _mixer_project(b, x_ref, gm_ref, perm_ref, win_ref, cw_ref, cb_ref, cinit_ref, bb_ref, cout_ref, bu_ref,
                   ccarry_ref, *, carried):
    tm = TM_MIX
    x = x_ref[b]
    hn = _rms(x, gm_ref[...]).astype(BF16)
    hn = _dot(perm_ref[...], hn).astype(BF16)
    proj = _dot(hn, win_ref[...])
    g_b = proj[:, :CONV_CH]
    cin = proj[:, CONV_CH:2 * CONV_CH] * proj[:, 2 * CONV_CH:3 * CONV_CH]
    u = proj[:, 3 * CONV_CH:]
    ub = u.astype(BF16)
    for half in range(2):
        bu_ref[b, :, half * S5_STATE:(half + 1) * S5_STATE] = _dot(ub[:, half * HALF_CH:(half + 1) * HALF_CH],
                                                                  bb_ref[half])

    last, before_last = cin[tm - SUBLANES:], cin[tm - 2 * SUBLANES:tm - SUBLANES]
    if carried:
        first_chunk = lax.broadcasted_iota(jnp.int32, (SUBLANES, CONV_CH), 0) == 0
        newer = jnp.where(first_chunk, ccarry_ref[1, b:b + 1, :], pltpu.roll(last, 1, 0))
        older = jnp.where(first_chunk, ccarry_ref[0, b:b + 1, :], pltpu.roll(before_last, 1, 0))
        for i, rows in enumerate((before_last, last)):
            ccarry_ref[i, b:b + 1, :] = rows[SUBLANES - 1:]
            cout_ref[i, b:b + 1, :] = rows[SUBLANES - 1:]
    else:
        older, newer = cinit_ref[0], cinit_ref[1]
        cout_ref[0] = before_last
        cout_ref[1] = last
    ext = jnp.concatenate([older, newer, cin], axis=0)
    cw = cw_ref[...]
    out_a = g_b * (cw[0:1] * ext[:tm] + cw[1:2] * ext[SUBLANES:tm + SUBLANES] + cw[2:3] * cin + cb_ref[...])
    return x, u, out_a


def _mixer_recur(b, sinit_ref, ab_ref, hop_ref, pw_ref, sout_ref, bu_ref, scarry_ref, *, carried):
    for c in range(S5_STATE // SCAN_LANES):
        half, within = divmod(c * SCAN_LANES, HALF_STATE)
        nat = slice(c * SCAN_LANES, (c + 1) * SCAN_LANES)
        re_cols = slice(half * S5_STATE + within, half * S5_STATE + within + SCAN_LANES)
        im_cols = slice(re_cols.start + HALF_STATE, re_cols.stop + HALF_STATE)
        ar, ai = ab_ref[0, :, nat], ab_ref[1, :, nat]
        if carried:
            sr = si = jnp.zeros((SUBLANES, SCAN_LANES), F32)
        else:
            sr, si = sinit_ref[:, re_cols], sinit_ref[:, im_cols]
        for g in range(CHUNK_ROWS):
            rows = slice(SUBLANES * g, SUBLANES * (g + 1))
            sr, si = (ar * sr - ai * si + bu_ref[b, rows, re_cols], ar * si + ai * sr + bu_ref[b, rows, im_cols])
            bu_ref[b, rows, re_cols] = sr
            bu_ref[b, rows, im_cols] = si
        if not carried:
            sout_ref[:, re_cols] = sr
            sout_ref[:, im_cols] = si
            continue
        first_chunk = lax.broadcasted_iota(jnp.int32, (SUBLANES, SCAN_LANES), 0) == 0
        er = jnp.where(first_chunk, scarry_ref[b:b + 1, re_cols], pltpu.roll(sr, 1, 0))
        ei = jnp.where(first_chunk, scarry_ref[b:b + 1, im_cols], pltpu.roll(si, 1, 0))
        for k, sh in enumerate((1, 2, 4)):
            wr, wi = hop_ref[2 * k, :, nat], hop_ref[2 * k + 1, :, nat]
            tr, ti = pltpu.roll(er, sh, 0), pltpu.roll(ei, sh, 0)
            er, ei = er + wr * tr - wi * ti, ei + wr * ti + wi * tr
        wr, wi = hop_ref[6, :, nat], hop_ref[7, :, nat]
        fr, fi = wr * er - wi * ei + sr, wr * ei + wi * er + si
        for ref in (scarry_ref, sout_ref):
            ref[b:b + 1, re_cols] = fr[SUBLANES - 1:]
            ref[b:b + 1, im_cols] = fi[SUBLANES - 1:]
        for g in range(CHUNK_ROWS):
            rows = slice(SUBLANES * g, SUBLANES * (g + 1))
            pr, pi = pw_ref[0, g, :, nat], pw_ref[1, g, :, nat]
            bu_ref[b, rows, re_cols] += pr * er - pi * ei
            bu_ref[b, rows, im_cols] += pr * ei + pi * er


def _mixer_output(b, x, u, out_a, unperm_ref, cc_ref, d_ref, gluw_ref, glub_ref, wout_ref, h_ref, bu_ref):
    y = jnp.concatenate([_dot(bu_ref[b, :, half * S5_STATE:(half + 1) * S5_STATE].astype(BF16), cc_ref[half])
                         for half in range(2)], axis=-1) + d_ref[...] * u
    z = jax.nn.gelu(y)
    out_b = z * jax.nn.sigmoid(_dot(z.astype(BF16), gluw_ref[...]) + glub_ref[...])
    mix = jnp.concatenate([out_a, out_b], axis=-1).astype(BF16)
    mix = _dot(unperm_ref[...], mix).astype(BF16)
    h_ref[b] = x + _dot(mix, wout_ref[...])


def _even_mixer(x, gm, w_in, cw, cb, cinit, sinit, ab, hop, pw, bb, cc, d, glu_w, glu_b, w_out, *, carried):
    nb, rows = x.shape[0], x.shape[1]
    tm = TM_MIX
    chunks = tm // CHUNK_ROWS
    assert chunks == SUBLANES and rows % tm == 0 and (carried or nb == 1)
    per_step = nb if carried else chunks
    tiles_per_seq = rows // tm if carried else 1
    nseq = sinit.shape[0]
    t_of = np.arange(tm)
    perm = np.zeros((tm, tm), np.float32)
    perm[SUBLANES * (t_of % CHUNK_ROWS) + t_of // CHUNK_ROWS, t_of] = 1.0
    kern = functools.partial(_even_mixer_kernel, carried=carried, tiles_per_seq=tiles_per_seq)
    return pl.pallas_call(
        kern,
        grid=(rows // tm,),
        in_specs=[
            pl.BlockSpec((nb, tm, D_MODEL), lambda t: (0, t, 0)),
            _const_spec((1, D_MODEL)),
            _const_spec((tm, tm)),
            _const_spec((tm, tm)),
            _const_spec((D_MODEL, 4 * CONV_CH)),
            _const_spec((3, CONV_CH)),
            _const_spec((1, CONV_CH)),
            pl.BlockSpec((2, per_step, CONV_CH), lambda t: (0, t // tiles_per_seq, 0)),
            pl.BlockSpec((per_step, 2 * S5_STATE), lambda t: (t // tiles_per_seq, 0)),
            _const_spec(ab.shape),
            _const_spec(hop.shape),
            _const_spec(pw.shape),
            _const_spec((2, HALF_CH, S5_STATE)),
            _const_spec((2, S5_STATE, HALF_CH)),
            _const_spec((1, S5_CH)),
            _const_spec((S5_CH, S5_CH)),
            _const_spec((1, S5_CH)),
            _const_spec((D_MODEL, D_MODEL)),
        ],
        out_specs=[
            pl.BlockSpec((nb, tm, D_MODEL), lambda t: (0, t, 0)),
            pl.BlockSpec((2, per_step, CONV_CH), lambda t: (0, t // tiles_per_seq, 0)),
            pl.BlockSpec((per_step, 2 * S5_STATE), lambda t: (t // tiles_per_seq, 0)),
        ],
        out_shape=[
            jax.ShapeDtypeStruct((nb, rows, D_MODEL), F32),
            jax.ShapeDtypeStruct((2, nseq, CONV_CH), F32),
            jax.ShapeDtypeStruct((nseq, 2 * S5_STATE), F32),
        ],
        scratch_shapes=[
            pltpu.VMEM((nb, tm, 2 * S5_STATE), F32),
            pltpu.VMEM((2, nb, CONV_CH), F32),
            pltpu.VMEM((nb, 2 * S5_STATE), F32),
        ],
        compiler_params=_params("arbitrary"),
        name="even_mixer",
    )(x, gm, jnp.asarray(perm, BF16), jnp.asarray(perm.T, BF16), w_in, cw, cb, cinit, sinit, ab, hop, pw, bb, cc, d,
      glu_w, glu_b, w_out)


def _ffn_kernel(x_ref, g_ref, w1_ref, w3_ref, w2_ref, o_ref):
    x = x_ref[...]
    hn = _rms(x, g_ref[...]).astype(BF16)
    a = _dot(hn, w1_ref[...])
    b = _dot(hn, w3_ref[...])
    o_ref[...] = x + _dot((a * jax.nn.sigmoid(a) * b).astype(BF16), w2_ref[...])


def _ffn(x, g, w1, w3, w2, *, tm):
    rows = x.shape[0]
    return pl.pallas_call(
        _ffn_kernel,
        grid=(rows // tm,),
        in_specs=[
            pl.BlockSpec((tm, D_MODEL), lambda t: (t, 0)),
            _const_spec((1, D_MODEL)),
            _const_spec((D_MODEL, D_FF)),
            _const_spec((D_MODEL, D_FF)),
            _const_spec((D_FF, D_MODEL)),
        ],
        out_specs=pl.BlockSpec((tm, D_MODEL), lambda t: (t, 0)),
        out_shape=jax.ShapeDtypeStruct((rows, D_MODEL), F32),
        compiler_params=_params("parallel"),
        name="ffn",
    )(x, g, w1, w3, w2)


def _qkv_kernel(x_ref, g_ref, w_ref, b_ref, q_ref, k_ref, v_ref):
    hn = _rms(x_ref[...], g_ref[...]).astype(BF16)
    qkv = _dot(hn, w_ref[...]) + b_ref[...]
    nq = N_HEADS * HEAD_DIM
    q_ref[...] = (qkv[:, :nq] * (HEAD_DIM ** -0.5)).astype(BF16)
    k_ref[...] = qkv[:, nq:nq + KV_DIM]
    v_ref[...] = qkv[:, nq + KV_DIM:]


def _qkv(x, g, w, b, *, tm):
    rows = x.shape[0]
    ncol = (N_HEADS + 2 * N_KV) * HEAD_DIM
    return pl.pallas_call(
        _qkv_kernel,
        grid=(rows // tm,),
        in_specs=[
            pl.BlockSpec((tm, D_MODEL), lambda t: (t, 0)),
            _const_spec((1, D_MODEL)),
            _const_spec((D_MODEL, ncol)),
            _const_spec((1, ncol)),
        ],
        out_specs=[
            pl.BlockSpec((tm, N_HEADS * HEAD_DIM), lambda t: (t, 0)),
            pl.BlockSpec((tm, KV_DIM), lambda t: (t, 0)),
            pl.BlockSpec((tm, KV_DIM), lambda t: (t, 0)),
        ],
        out_shape=[
            jax.ShapeDtypeStruct((rows, N_HEADS * HEAD_DIM), BF16),
            jax.ShapeDtypeStruct((rows, KV_DIM), F32),
            jax.ShapeDtypeStruct((rows, KV_DIM), F32),
        ],
        compiler_params=_params("parallel"),
        name="qkv",
    )(x, g, w, b)


def _attend_groups(q, keys, vals, bias_of_group, sink_of_group, o_scr, row0):
    nq = q.shape[0]
    for kv in range(N_KV):
        kv_cols = slice(kv * HEAD_DIM, (kv + 1) * HEAD_DIM)
        heads = [slice((kv * GQ + g) * HEAD_DIM, (kv * GQ + g + 1) * HEAD_DIM) for g in range(GQ)]
        qs = jnp.concatenate([q[:, cols] for cols in heads], axis=0)
        st = lax.dot_general(keys[:, kv_cols], qs, (((1,), (1,)), ((), ())),
                             preferred_element_type=F32) + bias_of_group(kv)
        sk = sink_of_group(kv)
        m = jnp.maximum(jnp.max(st, axis=0, keepdims=True), sk)
        p = jnp.exp(st - m)
        den = jnp.sum(p, axis=0, keepdims=True) + jnp.exp(sk - m)
        o = lax.dot_general((p * (1.0 / den)).astype(BF16), vals[:, kv_cols], (((0,), (0,)), ((), ())),
                            preferred_element_type=F32)
        for g, cols in enumerate(heads):
            o_scr[row0:row0 + nq, cols] = o[g * nq:(g + 1) * nq]


def _attn_prompt_kernel(q_ref, k_ref, v_ref, bias_ref, sink_ref, x_ref, wo_ref, bo_ref, o_ref, o_scr):
    t = pl.program_id(1)
    for u in range(TQ_PER_STEP):
        tile = t * TQ_PER_STEP + u
        start = pl.multiple_of(tile * TQ, TQ)
        keys = k_ref[0, pl.ds(start, TQ + WINDOW), :]
        vals = v_ref[0, pl.ds(start, TQ + WINDOW), :]
        variant = jnp.minimum(tile, 1)
        _attend_groups(q_ref[u * TQ:(u + 1) * TQ], keys, vals, lambda kv: bias_ref[variant, kv],
                       lambda kv: sink_ref[kv], o_scr, u * TQ)
    o_ref[...] = x_ref[...] + _dot(o_scr[...].astype(BF16), wo_ref[...]) + bo_ref[...]


def _attn_prompt(q, kpad, vpad, bias, sink, x, w_o, b_o):
    nb, lpad = kpad.shape[0], kpad.shape[1]
    tm = TQ_PER_STEP * TQ
    steps = (lpad - WINDOW) // tm
    return pl.pallas_call(
        _attn_prompt_kernel,
        grid=(nb, steps),
        in_specs=[
            pl.BlockSpec((tm, D_MODEL), lambda b, t: (b * steps + t, 0)),
            pl.BlockSpec((1, lpad, KV_DIM), lambda b, t: (b, 0, 0)),
            pl.BlockSpec((1, lpad, KV_DIM), lambda b, t: (b, 0, 0)),
            _const_spec((2, N_KV, TQ + WINDOW, GQ * TQ)),
            _const_spec((N_KV, 1, GQ * TQ)),
            pl.BlockSpec((tm, D_MODEL), lambda b, t: (b * steps + t, 0)),
            _const_spec((D_MODEL, D_MODEL)),
            _const_spec((1, D_MODEL)),
        ],
        out_specs=pl.BlockSpec((tm, D_MODEL), lambda b, t: (b * steps + t, 0)),
        out_shape=jax.ShapeDtypeStruct(x.shape, F32),
        scratch_shapes=[pltpu.VMEM((tm, D_MODEL), F32)],
        compiler_params=_params("parallel", "arbitrary"),
        name="attn_prompt",
    )(q, kpad, vpad, bias, sink, x, w_o, b_o)


def _attn_small_kernel(q_ref, kc_ref, vc_ref, kn_ref, vn_ref, bias_ref, sink_ref, x_ref, wo_ref, bo_ref,
                       o_ref, o_scr):
    step = pl.program_id(0)
    for j in range(SEG_PER_STEP):
        rows = slice(j * SEG, (j + 1) * SEG)
        keys = jnp.concatenate([kc_ref[j], kn_ref[rows]], axis=0).astype(BF16)
        vals = jnp.concatenate([vc_ref[j], vn_ref[rows]], axis=0).astype(BF16)
        variant = jnp.where(step * SEG_PER_STEP + j == META_SEG, 1, 0)
        _attend_groups(q_ref[rows], keys, vals, lambda kv: bias_ref[variant, kv], lambda kv: sink_ref[kv], o_scr,
                       j * SEG)
    o_ref[...] = x_ref[...] + _dot(o_scr[...].astype(BF16), wo_ref[...]) + bo_ref[...]


def _attn_small(q, kc, vc, kn, vn, bias, sink, x, w_o, b_o):
    tm = SEG_PER_STEP * SEG
    steps = x.shape[0] // tm
    cache_blocks = kc.shape[0] // SEG_PER_STEP
    cache_spec = pl.BlockSpec((SEG_PER_STEP, WINDOW, KV_DIM), lambda s: (jnp.minimum(s, cache_blocks - 1), 0, 0))
    return pl.pallas_call(
        _attn_small_kernel,
        grid=(steps,),
        in_specs=[
            pl.BlockSpec((tm, D_MODEL), lambda s: (s, 0)),
            cache_spec,
            cache_spec,
            pl.BlockSpec((tm, KV_DIM), lambda s: (s, 0)),
            pl.BlockSpec((tm, KV_DIM), lambda s: (s, 0)),
            _const_spec((2, N_KV, WINDOW + SEG, GQ * SEG)),
            _const_spec((N_KV, 1, GQ * SEG)),
            pl.BlockSpec((tm, D_MODEL), lambda s: (s, 0)),
            _const_spec((D_MODEL, D_MODEL)),
            _const_spec((1, D_MODEL)),
        ],
        out_specs=pl.BlockSpec((tm, D_MODEL), lambda s: (s, 0)),
        out_shape=jax.ShapeDtypeStruct(x.shape, F32),
        scratch_shapes=[pltpu.VMEM((tm, D_MODEL), F32)],
        compiler_params=_params("parallel"),
        name="attn_small",
    )(q, kc, vc, kn, vn, bias, sink, x, w_o, b_o)


def _expert_of_step(block, step):
    return jnp.where(block % 2 == 0, step, N_EXPERTS - 1 - step)


def _moe_kernel(x_ref, g_ref, rwt_ref, rb_ref, w1_ref, w3_ref, w2_ref, gf_ref, o_ref,
                hn_scr, slot_scr, gate_scr, cnt_smem, acc_scr, yall_scr, *, subs, tile_rows):
    step = pl.program_id(1)
    e = _expert_of_step(pl.program_id(0), step)
    tb = x_ref.shape[0]
    ts = tb // subs
    cap = tile_rows[-1]
    sub_cols = [slice(s * ts, (s + 1) * ts) for s in range(subs)]

    @pl.when(step == 0)
    def _route():
        hn = _rms(x_ref[...], g_ref[...]).astype(BF16)
        hn_scr[...] = hn
        logits = lax.dot_general(rwt_ref[...], hn, (((1,), (1,)), ((), ())),
                                 preferred_element_type=F32) + rb_ref[...]
        row = lax.broadcasted_iota(jnp.int32, logits.shape, 0).astype(F32)
        v1 = jnp.max(logits, axis=0, keepdims=True)
        i1 = jnp.min(jnp.where(logits == v1, row, float(N_EXPERTS)), axis=0, keepdims=True)
        rest = jnp.where(row == i1, -jnp.inf, logits)
        v2 = jnp.max(rest, axis=0, keepdims=True)
        i2 = jnp.min(jnp.where(rest == v2, row, float(N_EXPERTS)), axis=0, keepdims=True)
        e2 = jnp.exp(v2 - v1)
        den = 1.0 + e2
        gate_scr[...] = jnp.where(row == i1, 1.0 / den, 0.0) + jnp.where(row == i2, e2 / den, 0.0)
        chosen = (row == i1) | (row == i2)
        ind = jnp.where(chosen, 1.0, 0.0)
        before = lax.broadcasted_iota(jnp.int32, (ts, ts), 0) < lax.broadcasted_iota(jnp.int32, (ts, ts), 1)
        before = jnp.where(before, 1.0, 0.0).astype(BF16)
        for s, cols in enumerate(sub_cols):
            rank = _dot(ind[:, cols].astype(BF16), before)
            slot_scr[:, cols] = jnp.where(chosen[:, cols], rank, -1.0).astype(jnp.int32)
            for k in range(N_EXPERTS):
                cnt_smem[k * subs + s] = jnp.sum(ind[k:k + 1, cols]).astype(jnp.int32)
        acc_scr[...] = jnp.zeros_like(acc_scr)

    def _scatter(onehot, rows):
        return lax.dot_general(onehot, rows, (((0,), (0,)), ((), ())), preferred_element_type=F32)

    def _expert_tile(i, rows=cap):
        packed_row = lax.broadcasted_iota(jnp.int32, (rows, ts), 0) + i * cap
        hits = [packed_row == slot_scr[pl.ds(e, 1), cols] for cols in sub_cols]
        onehots = [jnp.where(hit, 1.0, 0.0).astype(BF16) for hit in hits]
        xe = jnp.concatenate([_dot(onehot, hn_scr[cols, :]) for onehot, cols in zip(onehots, sub_cols)],
                             axis=0).astype(BF16)
        a = _dot(xe, w1_ref[0])
        b = _dot(xe, w3_ref[0])
        y = _dot((a * jax.nn.sigmoid(a) * b).astype(BF16), w2_ref[0])
        g_row = jnp.concatenate([jnp.sum(jnp.where(hit, gate_scr[pl.ds(e, 1), cols], 0.0), axis=1, keepdims=True)
                                 for hit, cols in zip(hits, sub_cols)], axis=0)
        ys = (y * g_row).astype(BF16)
        return onehots, [ys[s * rows:(s + 1) * rows] for s in range(subs)]

    base = pl.multiple_of(e * cap, BF16_TILE_ROWS)
    count = cnt_smem[e * subs]
    for s in range(1, subs):
        count = jnp.maximum(count, cnt_smem[e * subs + s])
    for lower, rows in zip((-1,) + tile_rows, tile_rows):
        fits = (count > lower) & (count <= rows) if rows < cap else count > lower

        @pl.when(fits)
        def _first_tile(rows=rows):
            for s, ys in enumerate(_expert_tile(0, rows)[1]):
                yall_scr[s, pl.ds(base, rows), :] = ys
                if rows < cap:
                    yall_scr[s, pl.ds(base + rows, cap - rows), :] = jnp.zeros((cap - rows, D_MODEL), BF16)

    def _overflow_tile(i, carry):
        onehots, yss = _expert_tile(i)
        for onehot, ys, cols in zip(onehots, yss, sub_cols):
            acc_scr[cols, :] += _scatter(onehot, ys)
        return carry

    lax.fori_loop(1, (count + cap - 1) // cap, _overflow_tile, 0)

    @pl.when(step == N_EXPERTS - 1)
    def _finish():
        packed_row = lax.broadcasted_iota(jnp.int32, (cap, ts), 0)
        for s, cols in enumerate(sub_cols):
            onehot = jnp.concatenate([jnp.where(packed_row == slot_scr[k:k + 1, cols], 1.0, 0.0).astype(BF16)
                                      for k in range(N_EXPERTS)], axis=0)
            o_ref[cols, :] = _rms(x_ref[cols, :] + acc_scr[cols, :] + _scatter(onehot, yall_scr[s]), gf_ref[...])


def _moe_final(x, g, rwt, rb, w1, w3, w2, gf, *, tb, subs, tile_rows):
    rows, cap = x.shape[0], tile_rows[-1]
    return pl.pallas_call(
        functools.partial(_moe_kernel, subs=subs, tile_rows=tile_rows),
        grid=(rows // tb, N_EXPERTS),
        in_specs=[
            pl.BlockSpec((tb, D_MODEL), lambda t, e: (t, 0)),
            _const_spec((1, D_MODEL)),
            _const_spec((N_EXPERTS, D_MODEL)),
            _const_spec((N_EXPERTS, 1)),
            pl.BlockSpec((1, D_MODEL, D_FF_E), lambda t, s: (_expert_of_step(t, s), 0, 0)),
            pl.BlockSpec((1, D_MODEL, D_FF_E), lambda t, s: (_expert_of_step(t, s), 0, 0)),
            pl.BlockSpec((1, D_FF_E, D_MODEL), lambda t, s: (_expert_of_step(t, s), 0, 0)),
            _const_spec((1, D_MODEL)),
        ],
        out_specs=pl.BlockSpec((tb, D_MODEL), lambda t, e: (t, 0)),
        out_shape=jax.ShapeDtypeStruct((rows, D_MODEL), F32),
        scratch_shapes=[
            pltpu.VMEM((tb, D_MODEL), BF16),
            pltpu.VMEM((N_EXPERTS, tb), jnp.int32),
            pltpu.VMEM((N_EXPERTS, tb), F32),
            pltpu.SMEM((N_EXPERTS * subs,), jnp.int32),
            pltpu.VMEM((tb, D_MODEL), F32),
            pltpu.VMEM((subs, N_EXPERTS * cap, D_MODEL), BF16),
        ],
        compiler_params=_params("parallel", "arbitrary"),
        name="moe_final",
    )(x, g, rwt, rb, w1, w3, w2, gf)


def _t5_bucket_np(rel):
    half = N_BUCKETS // 2
    max_exact = half // 2
    ret = np.where(rel > 0, half, 0)
    n = np.abs(rel)
    nf = np.maximum(n, 1).astype(np.float32)
    large = max_exact + (np.log(nf / np.float32(max_exact)) / np.float32(math.log(MAX_DIST / max_exact))
                         * np.float32(half - max_exact)).astype(np.int32)
    large = np.minimum(large, half - 1)
    return ret + np.where(n < max_exact, n, large)


def _bias_variants(table, bucket, keeps):
    nq, nk = bucket.shape
    onehot = jnp.asarray(np.eye(N_BUCKETS, dtype=np.float32)[bucket])
    bias = jnp.einsum('rjb,bh->hjr', onehot, table, precision=lax.Precision.HIGHEST)
    bias = jnp.stack([jnp.where(keep.T[None], bias, NEG) for keep in keeps], axis=0)
    bias = bias.reshape(len(keeps), N_KV, GQ, nk, nq)
    return jnp.transpose(bias, (0, 1, 3, 2, 4)).reshape(len(keeps), N_KV, nk, GQ * nq)


def _lane_sink(sink, nq):
    return jnp.broadcast_to(sink.reshape(N_KV, 1, GQ, 1), (N_KV, 1, GQ, nq)).reshape(N_KV, 1, GQ * nq)


def _prompt_bias(table):
    r = np.arange(TQ)[:, None]
    j = np.arange(TQ + WINDOW)[None, :]
    band = (j - CHUNK * (r // CHUNK) >= 0) & (j - CHUNK * (r // CHUNK) < WINDOW + CHUNK)
    first = band & (j >= WINDOW - N_META)
    return _bias_variants(table, _t5_bucket_np(j - WINDOW - r), [first, band])


def _small_bias(table):
    i = np.arange(SEG)[:, None]
    m = np.arange(WINDOW + SEG)[None, :]
    everything = np.ones((SEG, WINDOW + SEG), bool)
    meta_only = everything & (m >= WINDOW + SEG - N_META)
    return _bias_variants(table, _t5_bucket_np(m - WINDOW - i), [everything, meta_only])


def _pack_state(re, im):
    return jnp.concatenate([re[..., :HALF_STATE], im[..., :HALF_STATE], re[..., HALF_STATE:], im[..., HALF_STATE:]],
                           axis=-1)


def _cmul(ar, ai, br, bi):
    return ar * br - ai * bi, ar * bi + ai * br


def _s5_prepare(a_re, a_im, log_dt, b_re, b_im, c_re, c_im):
    dt = jnp.exp(log_dt)[:, None]
    mag = jnp.exp(a_re * dt)
    ab_re, ab_im = mag * jnp.cos(a_im * dt), mag * jnp.sin(a_im * dt)
    den = a_re * a_re + a_im * a_im
    num_re = ab_re - 1.0
    cf_re = (num_re * a_re + ab_im * a_im) / den
    cf_im = (ab_im * a_re - num_re * a_im) / den
    bb_re = cf_re[..., None] * b_re - cf_im[..., None] * b_im
    bb_im = cf_re[..., None] * b_im + cf_im[..., None] * b_re

    def group_diag(v):
        r, c = v.shape[1], v.shape[2]
        gh = S5_GROUPS // 2
        spread = np.tile(np.eye(c, dtype=np.float32), (1, gh))
        wide = jnp.einsum('hrc,cn->hrn', v.reshape(2, gh * r, c), spread, precision=lax.Precision.HIGHEST)
        keep = (np.arange(gh * r)[:, None] // r) == (np.arange(gh * c)[None, :] // c)
        return jnp.where(keep, wide, 0.0)

    def t(v):
        return jnp.swapaxes(v, 1, 2)

    bb = jnp.concatenate([group_diag(t(bb_re)), group_diag(t(bb_im))], axis=2).astype(BF16)
    cc = jnp.concatenate([group_diag(t(c_re)), -group_diag(t(c_im))], axis=1).astype(BF16)

    pr, pi = ab_re.reshape(1, S5_STATE), ab_im.reshape(1, S5_STATE)
    n = 1
    while n < CHUNK_ROWS:
        qr, qi = _cmul(pr, pi, pr[n - 1:n], pi[n - 1:n])
        pr, pi = jnp.concatenate([pr, qr], axis=0), jnp.concatenate([pi, qi], axis=0)
        n *= 2

    def tile8(v):
        return jnp.broadcast_to(v, (SUBLANES,) + v.shape[1:])

    ab = jnp.stack([tile8(pr[0:1]), tile8(pi[0:1])])
    sub = np.arange(SUBLANES)[:, None]
    hop, w = [], (pr[CHUNK_ROWS - 1:], pi[CHUNK_ROWS - 1:])
    for k in (1, 2, 4):
        hop += [jnp.where(sub >= k, w[0], 0.0), jnp.where(sub >= k, w[1], 0.0)]
        if k < 4:
            w = _cmul(*w, *w)
    hop = jnp.stack(hop + [tile8(pr[CHUNK_ROWS - 1:]), tile8(pi[CHUNK_ROWS - 1:])])
    pw = jnp.stack([jnp.broadcast_to(p[:, None, :], (CHUNK_ROWS, SUBLANES, S5_STATE)) for p in (pr, pi)])
    return bb, cc, ab, hop, pw


def kernel(x_prompt, x_sample, cache_conv, state_s5_re, state_s5_im, cache_swa_k, cache_swa_v, meta_tokens, rel_bias_table, norm_mix, norm_ffn, norm_final, w_in0, conv_w, conv_b, s5_a_re, s5_a_im, s5_log_dt, s5_b_re, s5_b_im, s5_c_re, s5_c_im, s5_d, s5_glu_w, s5_glu_b, w_out0, ffn_w1, ffn_w3, ffn_w2, w_qkv, b_qkv, attn_sink, w_o, b_o, router_w, router_b, moe_w1, moe_w3, moe_w2):
    nb, seq = x_prompt.shape[0], x_prompt.shape[1]
    nsb = x_sample.shape[0]
    assert x_sample.shape[1] == SEG and nsb == N_SAMPLE_SEG and seq % TM_MIX == 0

    def row(v):
        return v.reshape(1, -1)

    meta_seg = jnp.concatenate([jnp.zeros((SEG - N_META, D_MODEL), F32), meta_tokens], axis=0)
    filler = jnp.zeros(((N_SMALL_SEG - N_SAMPLE_SEG - 1) * SEG, D_MODEL), F32)
    xs = jnp.concatenate([x_sample.reshape(nsb * SEG, D_MODEL), meta_seg, filler], axis=0)
    n_extra = N_SMALL_SEG - N_SAMPLE_SEG

    bb, cc, ab, hop, pw = _s5_prepare(s5_a_re[0], s5_a_im[0], s5_log_dt[0], s5_b_re[0], s5_b_im[0], s5_c_re[0],
                                      s5_c_im[0])
    mixer_w = (row(norm_mix[0]), w_in0[0].astype(BF16), conv_w[0], row(conv_b[0]))
    mixer_w2 = (ab, hop, pw, bb, cc, row(s5_d[0]), s5_glu_w[0].astype(BF16), row(s5_glu_b[0]), w_out0[0].astype(BF16))
    cinit_s = jnp.concatenate([jnp.swapaxes(cache_conv[0], 0, 1), jnp.zeros((2, n_extra, CONV_CH), F32)], axis=1)
    sinit_s = _pack_state(state_s5_re[0].reshape(nsb, S5_STATE), state_s5_im[0].reshape(nsb, S5_STATE))
    sinit_s = jnp.concatenate([sinit_s, jnp.zeros((n_extra, 2 * S5_STATE), F32)], axis=0)
    hs, conv_s, state_s = _even_mixer(xs[None], *mixer_w, cinit_s, sinit_s, *mixer_w2, carried=False)
    hs = hs[0]
    cinit_p = jnp.broadcast_to(conv_s[:, META_SEG:META_SEG + 1], (2, nb, CONV_CH))
    sinit_p = jnp.broadcast_to(state_s[META_SEG:META_SEG + 1], (nb, 2 * S5_STATE))
    hp, conv_p, state_p = _even_mixer(x_prompt, *mixer_w, cinit_p, sinit_p, *mixer_w2, carried=True)
    hp = hp.reshape(nb * seq, D_MODEL)
    ffn_w = (row(norm_ffn[0]), ffn_w1[0].astype(BF16), ffn_w3[0].astype(BF16), ffn_w2[0].astype(BF16))
    hs = _ffn(hs, *ffn_w, tm=256)
    hp = _ffn(hp, *ffn_w, tm=512)

    qkv_w = (row(norm_mix[1]), w_qkv[0].astype(BF16), row(b_qkv[0]))
    qs, ks, vs = _qkv(hs, *qkv_w, tm=640)
    qp, kp, vp = _qkv(hp, *qkv_w, tm=1024)
    wo = (w_o[0].astype(BF16), row(b_o[0]))
    hs = _attn_small(qs, cache_swa_k[0].reshape(nsb, WINDOW, KV_DIM), cache_swa_v[0].reshape(nsb, WINDOW, KV_DIM),
                     ks, vs, _small_bias(rel_bias_table), _lane_sink(attn_sink[0], SEG), hs, *wo)

    def padded(new, small):
        meta_rows = small[META_SEG * SEG:(META_SEG + 1) * SEG]
        front = jnp.concatenate([jnp.zeros((WINDOW - SEG, KV_DIM), F32), meta_rows], axis=0)
        front = jnp.broadcast_to(front[None], (nb, WINDOW, KV_DIM))
        return jnp.concatenate([front, new.reshape(nb, seq, KV_DIM)], axis=1).astype(BF16)

    hp = _attn_prompt(qp, padded(kp, ks), padded(vp, vs), _prompt_bias(rel_bias_table),
                      _lane_sink(attn_sink[0], TQ), hp, *wo)
    moe_w = (row(norm_ffn[1]), router_w[0].T.astype(BF16), router_b[0].reshape(N_EXPERTS, 1), moe_w1[0].astype(BF16),
             moe_w3[0].astype(BF16), moe_w2[0].astype(BF16), row(norm_final))
    ys = _moe_final(hs, *moe_w, tb=hs.shape[0], subs=1, tile_rows=(352,))
    yp = _moe_final(hp, *moe_w, tb=1024, subs=2, tile_rows=tuple(range(96, 225, BF16_TILE_ROWS)))

    n_real = nsb * SEG
    y_prompt = yp.reshape(nb, seq, D_MODEL)
    y_sample = ys[:n_real].reshape(nsb, SEG, D_MODEL)

    def split_state(st, n):
        st = st[:n].reshape(n, 2, 2, HALF_STATE)
        return tuple(st[:, :, ri].reshape(1, n, S5_GROUPS, S5_P) for ri in range(2))

    s5rp, s5ip = split_state(state_p, nb)
    s5rs, s5is = split_state(state_s, nsb)
    kp4 = kp.reshape(nb, seq, KV_DIM)[:, -WINDOW:].reshape(1, nb, WINDOW, N_KV, HEAD_DIM)
    vp4 = vp.reshape(nb, seq, KV_DIM)[:, -WINDOW:].reshape(1, nb, WINDOW, N_KV, HEAD_DIM)
    ks4 = ks[:n_real].reshape(1, nsb, SEG, N_KV, HEAD_DIM)
    vs4 = vs[:n_real].reshape(1, nsb, SEG, N_KV, HEAD_DIM)
    conv_p4 = jnp.swapaxes(conv_p, 0, 1)[None]
    conv_s4 = jnp.swapaxes(conv_s[:, :nsb], 0, 1)[None]
    return (y_prompt, y_sample, conv_p4, conv_s4, s5rp, s5ip, s5rs, s5is, kp4, vp4, ks4, vs4)
```

```python
import functools
import math

import numpy as np
import jax
import jax.numpy as jnp
from jax import lax
from jax.experimental import pallas as pl
from jax.experimental.pallas import tpu as pltpu

F32 = jnp.float32
BF16 = jnp.bfloat16

D_MODEL = 1024
CONV_CH = 512
S5_CH = 512
S5_GROUP = 16
S5_GROUPS = 32
S5_P = 64
S5_STATE = S5_GROUPS * S5_P
HALF_CH = S5_CH // 2
HALF_STATE = S5_STATE // 2
D_FF = 2816
N_HEADS = 16
N_KV = 2
GQ = N_HEADS // N_KV
HEAD_DIM = 64
KV_DIM = N_KV * HEAD_DIM
WINDOW = 128
CHUNK = 64
N_META = 16
N_BUCKETS = 32
MAX_DIST = 128
N_EXPERTS = 8
D_FF_E = 1024
EPS = 1e-6
NEG = -1e30

SUBLANES = 8
BF16_TILE_ROWS = 16
SEG = 32
N_SAMPLE_SEG = 32
META_SEG = N_SAMPLE_SEG
N_SMALL_SEG = 40
TM_MIX = 256
CHUNK_ROWS = SEG
SCAN_LANES = 512
TQ = 128
TQ_PER_STEP = 8
SEG_PER_STEP = 8
VMEM_LIMIT = 56 * 1024 * 1024


def _const_spec(shape):
    nd = len(shape)
    return pl.BlockSpec(shape, lambda *_: (0,) * nd, pipeline_mode=pl.Buffered(1))


def _params(*sem):
    return pltpu.CompilerParams(dimension_semantics=sem, vmem_limit_bytes=VMEM_LIMIT)


def _rms(x, g):
    return x * lax.rsqrt(jnp.mean(x * x, axis=-1, keepdims=True) + EPS) * g


def _dot(a, b):
    return jnp.dot(a, b, preferred_element_type=F32)


def _even_mixer_kernel(x_ref, gm_ref, perm_ref, unperm_ref, win_ref, cw_ref, cb_ref, cinit_ref, sinit_ref,
                       ab_ref, hop_ref, pw_ref, bb_ref, cc_ref, d_ref, gluw_ref, glub_ref, wout_ref,
                       h_ref, cout_ref, sout_ref, bu_ref, ccarry_ref, scarry_ref, *, carried, tiles_per_seq):
    if carried:
        @pl.when(pl.program_id(0) % tiles_per_seq == 0)
        def _load_initial_state():
            ccarry_ref[...] = cinit_ref[...]
            scarry_ref[...] = sinit_ref[...]

    streams = range(x_ref.shape[0])
    mid = [_mixer_project(b, x_ref, gm_ref, perm_ref, win_ref, cw_ref, cb_ref, cinit_ref, bb_ref, cout_ref, bu_ref,
                          ccarry_ref, carried=carried) for b in streams]
    for b in streams:
        _mixer_recur(b, sinit_ref, ab_ref, hop_ref, pw_ref, sout_ref, bu_ref, scarry_ref, carried=carried)
        _mixer_output(b, *mid[b], unperm_ref, cc_ref, d_ref, gluw_ref, glub_ref, wout_ref, h_ref, bu_ref)


def _mixer_project(b, x_ref, gm_ref, perm_ref, win_ref, cw_ref, cb_ref, cinit_ref, bb_ref, cout_ref, bu_ref,
                   ccarry_ref, *, carried):
    tm = TM_MIX
    x = x_ref[b]
    hn = _rms(x, gm_ref[...]).astype(BF16)
    hn = _dot(perm_ref[...], hn).astype(BF16)
    proj = _dot(hn, win_ref[...])
    g_b = proj[:, :CONV_CH]
    cin = proj[:, CONV_CH:2 * CONV_CH] * proj[:, 2 * CONV_CH:3 * CONV_CH]
    u = proj[:, 3 * CONV_CH:]
    ub = u.astype(BF16)
    for half in range(2):
        bu_ref[b, :, half * S5_STATE:(half + 1) * S5_STATE] = _dot(ub[:, half * HALF_CH:(half + 1) * HALF_CH],
                                                                  bb_ref[half])

    last, before_last = cin[tm - SUBLANES:], cin[tm - 2 * SUBLANES:tm - SUBLANES]
    if carried:
        first_chunk = lax.broadcasted_iota(jnp.int32, (SUBLANES, CONV_CH), 0) == 0
        newer = jnp.where(first_chunk, ccarry_ref[1, b:b + 1, :], pltpu.roll(last, 1, 0))
        older = jnp.where(first_chunk, ccarry_ref[0, b:b + 1, :], pltpu.roll(before_last, 1, 0))
        for i, rows in enumerate((before_last, last)):
            ccarry_ref[i, b:b + 1, :] = rows[SUBLANES - 1:]
            cout_ref[i, b:b + 1, :] = rows[SUBLANES - 1:]
    else:
        older, newer = cinit_ref[0], cinit_ref[1]
        cout_ref[0] = before_last
        cout_ref[1] = last
    ext = jnp.concatenate([older, newer, cin], axis=0)
    cw = cw_ref[...]
    out_a = g_b * (cw[0:1] * ext[:tm] + cw[1:2] * ext[SUBLANES:tm + SUBLANES] + cw[2:3] * cin + cb_ref[...])
    return x, u, out_a


def _mixer_recur(b, sinit_ref, ab_ref, hop_ref, pw_ref, sout_ref, bu_ref, scarry_ref, *, carried):
    for c in range(S5_STATE // SCAN_LANES):
        half, within = divmod(c * SCAN_LANES, HALF_STATE)
        nat = slice(c * SCAN_LANES, (c + 1) * SCAN_LANES)
        re_cols = slice(half * S5_STATE + within, half * S5_STATE + within + SCAN_LANES)
        im_cols = slice(re_cols.start + HALF_STATE, re_cols.stop + HALF_STATE)
        ar, ai = ab_ref[0, :, nat], ab_ref[1, :, nat]
        if carried:
            sr = si = jnp.zeros((SUBLANES, SCAN_LANES), F32)
        else:
            sr, si = sinit_ref[:, re_cols], sinit_ref[:, im_cols]
        for g in range(CHUNK_ROWS):
            rows = slice(SUBLANES * g, SUBLANES * (g + 1))
            sr, si = (ar * sr - ai * si + bu_ref[b, rows, re_cols], ar * si + ai * sr + bu_ref[b, rows, im_cols])
            bu_ref[b, rows, re_cols] = sr
            bu_ref[b, rows, im_cols] = si
        if not carried:
            sout_ref[:, re_cols] = sr
            sout_ref[:, im_cols] = si
            continue
        first_chunk = lax.broadcasted_iota(jnp.int32, (SUBLANES, SCAN_LANES), 0) == 0
        er = jnp.where(first_chunk, scarry_ref[b:b + 1, re_cols], pltpu.roll(sr, 1, 0))
        ei = jnp.where(first_chunk, scarry_ref[b:b + 1, im_cols], pltpu.roll(si, 1, 0))
        for k, sh in enumerate((1, 2, 4)):
            wr, wi = hop_ref[2 * k, :, nat], hop_ref[2 * k + 1, :, nat]
            tr, ti = pltpu.roll(er, sh, 0), pltpu.roll(ei, sh, 0)
            er, ei = er + wr * tr - wi * ti, ei + wr * ti + wi * tr
        wr, wi = hop_ref[6, :, nat], hop_ref[7, :, nat]
        fr, fi = wr * er - wi * ei + sr, wr * ei + wi * er + si
        for ref in (scarry_ref, sout_ref):
            ref[b:b + 1, re_cols] = fr[SUBLANES - 1:]
            ref[b:b + 1, im_cols] = fi[SUBLANES - 1:]
        for g in range(CHUNK_ROWS):
            rows = slice(SUBLANES * g, SUBLANES * (g + 1))
            pr, pi = pw_ref[0, g, :, nat], pw_ref[1, g, :, nat]
            bu_ref[b, rows, re_cols] += pr * er - pi * ei
            bu_ref[b, rows, im_cols] += pr * ei + pi * er


def _mixer_output(b, x, u, out_a, unperm_ref, cc_ref, d_ref, gluw_ref, glub_ref, wout_ref, h_ref, bu_ref):
    y = jnp.concatenate([_dot(bu_ref[b, :, half * S5_STATE:(half + 1) * S5_STATE].astype(BF16), cc_ref[half])
                         for half in range(2)], axis=-1) + d_ref[...] * u
    z = jax.nn.gelu(y)
    out_b = z * jax.nn.sigmoid(_dot(z.astype(BF16), gluw_ref[...]) + glub_ref[...])
    mix = jnp.concatenate([out_a, out_b], axis=-1).astype(BF16)
    mix = _dot(unperm_ref[...], mix).astype(BF16)
    h_ref[b] = x + _dot(mix, wout_ref[...])


def _even_mixer(x, gm, w_in, cw, cb, cinit, sinit, ab, hop, pw, bb, cc, d, glu_w, glu_b, w_out, *, carried):
    nb, rows = x.shape[0], x.shape[1]
    tm = TM_MIX
    chunks = tm // CHUNK_ROWS
    assert chunks == SUBLANES and rows % tm == 0 and (carried or nb == 1)
    per_step = nb if carried else chunks
    tiles_per_seq = rows // tm if carried else 1
    nseq = sinit.shape[0]
    t_of = np.arange(tm)
    perm = np.zeros((tm, tm), np.float32)
    perm[SUBLANES * (t_of % CHUNK_ROWS) + t_of // CHUNK_ROWS, t_of] = 1.0
    kern = functools.partial(_even_mixer_kernel, carried=carried, tiles_per_seq=tiles_per_seq)
    return pl.pallas_call(
        kern,
        grid=(rows // tm,),
        in_specs=[
            pl.BlockSpec((nb, tm, D_MODEL), lambda t: (0, t, 0)),
            _const_spec((1, D_MODEL)),
            _const_spec((tm, tm)),
            _const_spec((tm, tm)),
            _const_spec((D_MODEL, 4 * CONV_CH)),
            _const_spec((3, CONV_CH)),
            _const_spec((1, CONV_CH)),
            pl.BlockSpec((2, per_step, CONV_CH), lambda t: (0, t // tiles_per_seq, 0)),
            pl.BlockSpec((per_step, 2 * S5_STATE), lambda t: (t // tiles_per_seq, 0)),
            _const_spec(ab.shape),
            _const_spec(hop.shape),
            _const_spec(pw.shape),
            _const_spec((2, HALF_CH, S5_STATE)),
            _const_spec((2, S5_STATE, HALF_CH)),
            _const_spec((1, S5_CH)),
            _const_spec((S5_CH, S5_CH)),
            _const_spec((1, S5_CH)),
            _const_spec((D_MODEL, D_MODEL)),
        ],
        out_specs=[
            pl.BlockSpec((nb, tm, D_MODEL), lambda t: (0, t, 0)),
            pl.BlockSpec((2, per_step, CONV_CH), lambda t: (0, t // tiles_per_seq, 0)),
            pl.BlockSpec((per_step, 2 * S5_STATE), lambda t: (t // tiles_per_seq, 0)),
        ],
        out_shape=[
            jax.ShapeDtypeStruct((nb, rows, D_MODEL), F32),
            jax.ShapeDtypeStruct((2, nseq, CONV_CH), F32),
            jax.ShapeDtypeStruct((nseq, 2 * S5_STATE), F32),
        ],
        scratch_shapes=[
            pltpu.VMEM((nb, tm, 2 * S5_STATE), F32),
            pltpu.VMEM((2, nb, CONV_CH), F32),
            pltpu.VMEM((nb, 2 * S5_STATE), F32),
        ],
        compiler_params=_params("arbitrary"),
        name="even_mixer",
    )(x, gm, jnp.asarray(perm, BF16), jnp.asarray(perm.T, BF16), w_in, cw, cb, cinit, sinit, ab, hop, pw, bb, cc, d,
      glu_w, glu_b, w_out)


def _ffn_kernel(x_ref, g_ref, w1_ref, w3_ref, w2_ref, o_ref):
    x = x_ref[...]
    hn = _rms(x, g_ref[...]).astype(BF16)
    a = _dot(hn, w1_ref[...])
    b = _dot(hn, w3_ref[...])
    o_ref[...] = x + _dot((a * jax.nn.sigmoid(a) * b).astype(BF16), w2_ref[...])


def _ffn(x, g, w1, w3, w2, *, tm):
    rows = x.shape[0]
    return pl.pallas_call(
        _ffn_kernel,
        grid=(rows // tm,),
        in_specs=[
            pl.BlockSpec((tm, D_MODEL), lambda t: (t, 0)),
            _const_spec((1, D_MODEL)),
            _const_spec((D_MODEL, D_FF)),
            _const_spec((D_MODEL, D_FF)),
            _const_spec((D_FF, D_MODEL)),
        ],
        out_specs=pl.BlockSpec((tm, D_MODEL), lambda t: (t, 0)),
        out_shape=jax.ShapeDtypeStruct((rows, D_MODEL), F32),
        compiler_params=_params("parallel"),
        name="ffn",
    )(x, g, w1, w3, w2)


def _qkv_kernel(x_ref, g_ref, w_ref, b_ref, q_ref, k_ref, v_ref):
    hn = _rms(x_ref[...], g_ref[...]).astype(BF16)
    qkv = _dot(hn, w_ref[...]) + b_ref[...]
    nq = N_HEADS * HEAD_DIM
    q_ref[...] = (qkv[:, :nq] * (HEAD_DIM ** -0.5)).astype(BF16)
    k_ref[...] = qkv[:, nq:nq + KV_DIM]
    v_ref[...] = qkv[:, nq + KV_DIM:]


def _qkv(x, g, w, b, *, tm):
    rows = x.shape[0]
    ncol = (N_HEADS + 2 * N_KV) * HEAD_DIM
    return pl.pallas_call(
        _qkv_kernel,
        grid=(rows // tm,),
        in_specs=[
            pl.BlockSpec((tm, D_MODEL), lambda t: (t, 0)),
            _const_spec((1, D_MODEL)),
            _const_spec((D_MODEL, ncol)),
            _const_spec((1, ncol)),
        ],
        out_specs=[
            pl.BlockSpec((tm, N_HEADS * HEAD_DIM), lambda t: (t, 0)),
            pl.BlockSpec((tm, KV_DIM), lambda t: (t, 0)),
            pl.BlockSpec((tm, KV_DIM), lambda t: (t, 0)),
        ],
        out_shape=[
            jax.ShapeDtypeStruct((rows, N_HEADS * HEAD_DIM), BF16),
            jax.ShapeDtypeStruct((rows, KV_DIM), F32),
            jax.ShapeDtypeStruct((rows, KV_DIM), F32),
        ],
        compiler_params=_params("parallel"),
        name="qkv",
    )(x, g, w, b)


def _attend_groups(q, keys, vals, bias_of_group, sink_of_group, o_scr, row0):
    nq = q.shape[0]
    for kv in range(N_KV):
        kv_cols = slice(kv * HEAD_DIM, (kv + 1) * HEAD_DIM)
        heads = [slice((kv * GQ + g) * HEAD_DIM, (kv * GQ + g + 1) * HEAD_DIM) for g in range(GQ)]
        qs = jnp.concatenate([q[:, cols] for cols in heads], axis=0)
        st = lax.dot_general(keys[:, kv_cols], qs, (((1,), (1,)), ((), ())),
                             preferred_element_type=F32) + bias_of_group(kv)
        sk = sink_of_group(kv)
        m = jnp.maximum(jnp.max(st, axis=0, keepdims=True), sk)
        p = jnp.exp(st - m)
        den = jnp.sum(p, axis=0, keepdims=True) + jnp.exp(sk - m)
        o = lax.dot_general((p * (1.0 / den)).astype(BF16), vals[:, kv_cols], (((0,), (0,)), ((), ())),
                            preferred_element_type=F32)
        for g, cols in enumerate(heads):
            o_scr[row0:row0 + nq, cols] = o[g * nq:(g + 1) * nq]


def _attn_prompt_kernel(q_ref, k_ref, v_ref, bias_ref, sink_ref, x_ref, wo_ref, bo_ref, o_ref, o_scr):
    t = pl.program_id(1)
    for u in range(TQ_PER_STEP):
        tile = t * TQ_PER_STEP + u
        start = pl.multiple_of(tile * TQ, TQ)
        keys = k_ref[0, pl.ds(start, TQ + WINDOW), :]
        vals = v_ref[0, pl.ds(start, TQ + WINDOW), :]
        variant = jnp.minimum(tile, 1)
        _attend_groups(q_ref[u * TQ:(u + 1) * TQ], keys, vals, lambda kv: bias_ref[variant, kv],
                       lambda kv: sink_ref[kv], o_scr, u * TQ)
    o_ref[...] = x_ref[...] + _dot(o_scr[...].astype(BF16), wo_ref[...]) + bo_ref[...]


def _attn_prompt(q, kpad, vpad, bias, sink, x, w_o, b_o):
    nb, lpad = kpad.shape[0], kpad.shape[1]
    tm = TQ_PER_STEP * TQ
    steps = (lpad - WINDOW) // tm
    return pl.pallas_call(
        _attn_prompt_kernel,
        grid=(nb, steps),
        in_specs=[
            pl.BlockSpec((tm, D_MODEL), lambda b, t: (b * steps + t, 0)),
            pl.BlockSpec((1, lpad, KV_DIM), lambda b, t: (b, 0, 0)),
            pl.BlockSpec((1, lpad, KV_DIM), lambda b, t: (b, 0, 0)),
            _const_spec((2, N_KV, TQ + WINDOW, GQ * TQ)),
            _const_spec((N_KV, 1, GQ * TQ)),
            pl.BlockSpec((tm, D_MODEL), lambda b, t: (b * steps + t, 0)),
            _const_spec((D_MODEL, D_MODEL)),
            _const_spec((1, D_MODEL)),
        ],
        out_specs=pl.BlockSpec((tm, D_MODEL), lambda b, t: (b * steps + t, 0)),
        out_shape=jax.ShapeDtypeStruct(x.shape, F32),
        scratch_shapes=[pltpu.VMEM((tm, D_MODEL), F32)],
        compiler_params=_params("parallel", "arbitrary"),
        name="attn_prompt",
    )(q, kpad, vpad, bias, sink, x, w_o, b_o)


def _attn_small_kernel(q_ref, kc_ref, vc_ref, kn_ref, vn_ref, bias_ref, sink_ref, x_ref, wo_ref, bo_ref,
                       o_ref, o_scr):
    step = pl.program_id(0)
    for j in range(SEG_PER_STEP):
        rows = slice(j * SEG, (j + 1) * SEG)
        keys = jnp.concatenate([kc_ref[j], kn_ref[rows]], axis=0).astype(BF16)
        vals = jnp.concatenate([vc_ref[j], vn_ref[rows]], axis=0).astype(BF16)
        variant = jnp.where(step * SEG_PER_STEP + j == META_SEG, 1, 0)
        _attend_groups(q_ref[rows], keys, vals, lambda kv: bias_ref[variant, kv], lambda kv: sink_ref[kv], o_scr,
                       j * SEG)
    o_ref[...] = x_ref[...] + _dot(o_scr[...].astype(BF16), wo_ref[...]) + bo_ref[...]


def _attn_small(q, kc, vc, kn, vn, bias, sink, x, w_o, b_o):
    tm = SEG_PER_STEP * SEG
    steps = x.shape[0] // tm
    cache_blocks = kc.shape[0] // SEG_PER_STEP
    cache_spec = pl.BlockSpec((SEG_PER_STEP, WINDOW, KV_DIM), lambda s: (jnp.minimum(s, cache_blocks - 1), 0, 0))
    return pl.pallas_call(
        _attn_small_kernel,
        grid=(steps,),
        in_specs=[
            pl.BlockSpec((tm, D_MODEL), lambda s: (s, 0)),
            cache_spec,
            cache_spec,
            pl.BlockSpec((tm, KV_DIM), lambda s: (s, 0)),
            pl.BlockSpec((tm, KV_DIM), lambda s: (s, 0)),
            _const_spec((2, N_KV, WINDOW + SEG, GQ * SEG)),
            _const_spec((N_KV, 1, GQ * SEG)),
            pl.BlockSpec((tm, D_MODEL), lambda s: (s, 0)),
            _const_spec((D_MODEL, D_MODEL)),
            _const_spec((1, D_MODEL)),
        ],
        out_specs=pl.BlockSpec((tm, D_MODEL), lambda s: (s, 0)),
        out_shape=jax.ShapeDtypeStruct(x.shape, F32),
        scratch_shapes=[pltpu.VMEM((tm, D_MODEL), F32)],
        compiler_params=_params("parallel"),
        name="attn_small",
    )(q, kc, vc, kn, vn, bias, sink, x, w_o, b_o)


def _expert_of_step(block, step):
    return jnp.where(block % 2 == 0, step, N_EXPERTS - 1 - step)


def _moe_kernel(x_ref, g_ref, rwt_ref, rb_ref, w1_ref, w3_ref, w2_ref, gf_ref, o_ref,
                hn_scr, slot_scr, gate_scr, cnt_smem, acc_scr, yall_scr, *, subs, tile_rows):
    step = pl.program_id(1)
    e = _expert_of_step(pl.program_id(0), step)
    tb = x_ref.shape[0]
    ts = tb // subs
    cap = tile_rows[-1]
    sub_cols = [slice(s * ts, (s + 1) * ts) for s in range(subs)]

    @pl.when(step == 0)
    def _route():
        hn = _rms(x_ref[...], g_ref[...]).astype(BF16)
        hn_scr[...] = hn
        logits = lax.dot_general(rwt_ref[...], hn, (((1,), (1,)), ((), ())),
                                 preferred_element_type=F32) + rb_ref[...]
        row = lax.broadcasted_iota(jnp.int32, logits.shape, 0).astype(F32)
        v1 = jnp.max(logits, axis=0, keepdims=True)
        i1 = jnp.min(jnp.where(logits == v1, row, float(N_EXPERTS)), axis=0, keepdims=True)
        rest = jnp.where(row == i1, -jnp.inf, logits)
        v2 = jnp.max(rest, axis=0, keepdims=True)
        i2 = jnp.min(jnp.where(rest == v2, row, float(N_EXPERTS)), axis=0, keepdims=True)
        e2 = jnp.exp(v2 - v1)
        den = 1.0 + e2
        gate_scr[...] = jnp.where(row == i1, 1.0 / den, 0.0) + jnp.where(row == i2, e2 / den, 0.0)
        chosen = (row == i1) | (row == i2)
        ind = jnp.where(chosen, 1.0, 0.0)
        before = lax.broadcasted_iota(jnp.int32, (ts, ts), 0) < lax.broadcasted_iota(jnp.int32, (ts, ts), 1)
        before = jnp.where(before, 1.0, 0.0).astype(BF16)
        for s, cols in enumerate(sub_cols):
            rank = _dot(ind[:, cols].astype(BF16), before)
            slot_scr[:, cols] = jnp.where(chosen[:, cols], rank, -1.0).astype(jnp.int32)
            for k in range(N_EXPERTS):
                cnt_smem[k * subs + s] = jnp.sum(ind[k:k + 1, cols]).astype(jnp.int32)
        acc_scr[...] = jnp.zeros_like(acc_scr)

    def _scatter(onehot, rows):
        return lax.dot_general(onehot, rows, (((0,), (0,)), ((), ())), preferred_element_type=F32)

    def _expert_tile(i, rows=cap):
        packed_row = lax.broadcasted_iota(jnp.int32, (rows, ts), 0) + i * cap
        hits = [packed_row == slot_scr[pl.ds(e, 1), cols] for cols in sub_cols]
        onehots = [jnp.where(hit, 1.0, 0.0).astype(BF16) for hit in hits]
        xe = jnp.concatenate([_dot(onehot, hn_scr[cols, :]) for onehot, cols in zip(onehots, sub_cols)],
                             axis=0).astype(BF16)
        a = _dot(xe, w1_ref[0])
        b = _dot(xe, w3_ref[0])
        y = _dot((a * jax.nn.sigmoid(a) * b).astype(BF16), w2_ref[0])
        g_row = jnp.concatenate([jnp.sum(jnp.where(hit, gate_scr[pl.ds(e, 1), cols], 0.0), axis=1, keepdims=True)
                                 for hit, cols in zip(hits, sub_cols)], axis=0)
        ys = (y * g_row).astype(BF16)
        return onehots, [ys[s * rows:(s + 1) * rows] for s in range(subs)]

    base = pl.multiple_of(e * cap, BF16_TILE_ROWS)
    count = cnt_smem[e * subs]
    for s in range(1, subs):
        count = jnp.maximum(count, cnt_smem[e * subs + s])
    for lower, rows in zip((-1,) + tile_rows, tile_rows):
        fits = (count > lower) & (count <= rows) if rows < cap else count > lower

        @pl.when(fits)
        def _first_tile(rows=rows):
            for s, ys in enumerate(_expert_tile(0, rows)[1]):
                yall_scr[s, pl.ds(base, rows), :] = ys
                if rows < cap:
                    yall_scr[s, pl.ds(base + rows, cap - rows), :] = jnp.zeros((cap - rows, D_MODEL), BF16)

    def _overflow_tile(i, carry):
        onehots, yss = _expert_tile(i)
        for onehot, ys, cols in zip(onehots, yss, sub_cols):
            acc_scr[cols, :] += _scatter(onehot, ys)
        return carry

    lax.fori_loop(1, (count + cap - 1) // cap, _overflow_tile, 0)

    @pl.when(step == N_EXPERTS - 1)
    def _finish():
        packed_row = lax.broadcasted_iota(jnp.int32, (cap, ts), 0)
        for s, cols in enumerate(sub_cols):
            onehot = jnp.concatenate([jnp.where(packed_row == slot_scr[k:k + 1, cols], 1.0, 0.0).astype(BF16)
                                      for k in range(N_EXPERTS)], axis=0)
            o_ref[cols, :] = _rms(x_ref[cols, :] + acc_scr[cols, :] + _scatter(onehot, yall_scr[s]), gf_ref[...])


def _moe_final(x, g, rwt, rb, w1, w3, w2, gf, *, tb, subs, tile_rows):
    rows, cap = x.shape[0], tile_rows[-1]
    return pl.pallas_call(
        functools.partial(_moe_kernel, subs=subs, tile_rows=tile_rows),
        grid=(rows // tb, N_EXPERTS),
        in_specs=[
            pl.BlockSpec((tb, D_MODEL), lambda t, e: (t, 0)),
            _const_spec((1, D_MODEL)),
            _const_spec((N_EXPERTS, D_MODEL)),
            _const_spec((N_EXPERTS, 1)),
            pl.BlockSpec((1, D_MODEL, D_FF_E), lambda t, s: (_expert_of_step(t, s), 0, 0)),
            pl.BlockSpec((1, D_MODEL, D_FF_E), lambda t, s: (_expert_of_step(t, s), 0, 0)),
            pl.BlockSpec((1, D_FF_E, D_MODEL), lambda t, s: (_expert_of_step(t, s), 0, 0)),
            _const_spec((1, D_MODEL)),
        ],
        out_specs=pl.BlockSpec((tb, D_MODEL), lambda t, e: (t, 0)),
        out_shape=jax.ShapeDtypeStruct((rows, D_MODEL), F32),
        scratch_shapes=[
            pltpu.VMEM((tb, D_MODEL), BF16),
            pltpu.VMEM((N_EXPERTS, tb), jnp.int32),
            pltpu.VMEM((N_EXPERTS, tb), F32),
            pltpu.SMEM((N_EXPERTS * subs,), jnp.int32),
            pltpu.VMEM((tb, D_MODEL), F32),
            pltpu.VMEM((subs, N_EXPERTS * cap, D_MODEL), BF16),
        ],
        compiler_params=_params("parallel", "arbitrary"),
        name="moe_final",
    )(x, g, rwt, rb, w1, w3, w2, gf)


def _t5_bucket_np(rel):
    half = N_BUCKETS // 2
    max_exact = half // 2
    ret = np.where(rel > 0, half, 0)
    n = np.abs(rel)
    nf = np.maximum(n, 1).astype(np.float32)
    large = max_exact + (np.log(nf / np.float32(max_exact)) / np.float32(math.log(MAX_DIST / max_exact))
                         * np.float32(half - max_exact)).astype(np.int32)
    large = np.minimum(large, half - 1)
    return ret + np.where(n < max_exact, n, large)


def _bias_variants(table, bucket, keeps):
    nq, nk = bucket.shape
    onehot = jnp.asarray(np.eye(N_BUCKETS, dtype=np.float32)[bucket])
    bias = jnp.einsum('rjb,bh->hjr', onehot, table, precision=lax.Precision.HIGHEST)
    bias = jnp.stack([jnp.where(keep.T[None], bias, NEG) for keep in keeps], axis=0)
    bias = bias.reshape(len(keeps), N_KV, GQ, nk, nq)
    return jnp.transpose(bias, (0, 1, 3, 2, 4)).reshape(len(keeps), N_KV, nk, GQ * nq)


def _lane_sink(sink, nq):
    return jnp.broadcast_to(sink.reshape(N_KV, 1, GQ, 1), (N_KV, 1, GQ, nq)).reshape(N_KV, 1, GQ * nq)


def _prompt_bias(table):
    r = np.arange(TQ)[:, None]
    j = np.arange(TQ + WINDOW)[None, :]
    band = (j - CHUNK * (r // CHUNK) >= 0) & (j - CHUNK * (r // CHUNK) < WINDOW + CHUNK)
    first = band & (j >= WINDOW - N_META)
    return _bias_variants(table, _t5_bucket_np(j - WINDOW - r), [first, band])


def _small_bias(table):
    i = np.arange(SEG)[:, None]
    m = np.arange(WINDOW + SEG)[None, :]
    everything = np.ones((SEG, WINDOW + SEG), bool)
    meta_only = everything & (m >= WINDOW + SEG - N_META)
    return _bias_variants(table, _t5_bucket_np(m - WINDOW - i), [everything, meta_only])


def _pack_state(re, im):
    return jnp.concatenate([re[..., :HALF_STATE], im[..., :HALF_STATE], re[..., HALF_STATE:], im[..., HALF_STATE:]],
                           axis=-1)


def _cmul(ar, ai, br, bi):
    return ar * br - ai * bi, ar * bi + ai * br


def _s5_prepare(a_re, a_im, log_dt, b_re, b_im, c_re, c_im):
    dt = jnp.exp(log_dt)[:, None]
    mag = jnp.exp(a_re * dt)
    ab_re, ab_im = mag * jnp.cos(a_im * dt), mag * jnp.sin(a_im * dt)
    den = a_re * a_re + a_im * a_im
    num_re = ab_re - 1.0
    cf_re = (num_re * a_re + ab_im * a_im) / den
    cf_im = (ab_im * a_re - num_re * a_im) / den
    bb_re = cf_re[..., None] * b_re - cf_im[..., None] * b_im
    bb_im = cf_re[..., None] * b_im + cf_im[..., None] * b_re

    def group_diag(v):
        r, c = v.shape[1], v.shape[2]
        gh = S5_GROUPS // 2
        spread = np.tile(np.eye(c, dtype=np.float32), (1, gh))
        wide = jnp.einsum('hrc,cn->hrn', v.reshape(2, gh * r, c), spread, precision=lax.Precision.HIGHEST)
        keep = (np.arange(gh * r)[:, None] // r) == (np.arange(gh * c)[None, :] // c)
        return jnp.where(keep, wide, 0.0)

    def t(v):
        return jnp.swapaxes(v, 1, 2)

    bb = jnp.concatenate([group_diag(t(bb_re)), group_diag(t(bb_im))], axis=2).astype(BF16)
    cc = jnp.concatenate([group_diag(t(c_re)), -group_diag(t(c_im))], axis=1).astype(BF16)

    pr, pi = ab_re.reshape(1, S5_STATE), ab_im.reshape(1, S5_STATE)
    n = 1
    while n < CHUNK_ROWS:
        qr, qi = _cmul(pr, pi, pr[n - 1:n], pi[n - 1:n])
        pr, pi = jnp.concatenate([pr, qr], axis=0), jnp.concatenate([pi, qi], axis=0)
        n *= 2

    def tile8(v):
        return jnp.broadcast_to(v, (SUBLANES,) + v.shape[1:])

    ab = jnp.stack([tile8(pr[0:1]), tile8(pi[0:1])])
    sub = np.arange(SUBLANES)[:, None]
    hop, w = [], (pr[CHUNK_ROWS - 1:], pi[CHUNK_ROWS - 1:])
    for k in (1, 2, 4):
        hop += [jnp.where(sub >= k, w[0], 0.0), jnp.where(sub >= k, w[1], 0.0)]
        if k < 4:
            w = _cmul(*w, *w)
    hop = jnp.stack(hop + [tile8(pr[CHUNK_ROWS - 1:]), tile8(pi[CHUNK_ROWS - 1:])])
    pw = jnp.stack([jnp.broadcast_to(p[:, None, :], (CHUNK_ROWS, SUBLANES, S5_STATE)) for p in (pr, pi)])
    return bb, cc, ab, hop, pw


def kernel(x_prompt, x_sample, cache_conv, state_s5_re, state_s5_im, cache_swa_k, cache_swa_v, meta_tokens, rel_bias_table, norm_mix, norm_ffn, norm_final, w_in0, conv_w, conv_b, s5_a_re, s5_a_im, s5_log_dt, s5_b_re, s5_b_im, s5_c_re, s5_c_im, s5_d, s5_glu_w, s5_glu_b, w_out0, ffn_w1, ffn_w3, ffn_w2, w_qkv, b_qkv, attn_sink, w_o, b_o, router_w, router_b, moe_w1, moe_w3, moe_w2):
    nb, seq = x_prompt.shape[0], x_prompt.shape[1]
    nsb = x_sample.shape[0]
    assert x_sample.shape[1] == SEG and nsb == N_SAMPLE_SEG and seq % TM_MIX == 0

    def row(v):
        return v.reshape(1, -1)

    meta_seg = jnp.concatenate([jnp.zeros((SEG - N_META, D_MODEL), F32), meta_tokens], axis=0)
    filler = jnp.zeros(((N_SMALL_SEG - N_SAMPLE_SEG - 1) * SEG, D_MODEL), F32)
    xs = jnp.concatenate([x_sample.reshape(nsb * SEG, D_MODEL), meta_seg, filler], axis=0)
    n_extra = N_SMALL_SEG - N_SAMPLE_SEG

    bb, cc, ab, hop, pw = _s5_prepare(s5_a_re[0], s5_a_im[0], s5_log_dt[0], s5_b_re[0], s5_b_im[0], s5_c_re[0],
                                      s5_c_im[0])
    mixer_w = (row(norm_mix[0]), w_in0[0].astype(BF16), conv_w[0], row(conv_b[0]))
    mixer_w2 = (ab, hop, pw, bb, cc, row(s5_d[0]), s5_glu_w[0].astype(BF16), row(s5_glu_b[0]), w_out0[0].astype(BF16))
    cinit_s = jnp.concatenate([jnp.swapaxes(cache_conv[0], 0, 1), jnp.zeros((2, n_extra, CONV_CH), F32)], axis=1)
    sinit_s = _pack_state(state_s5_re[0].reshape(nsb, S5_STATE), state_s5_im[0].reshape(nsb, S5_STATE))
    sinit_s = jnp.concatenate([sinit_s, jnp.zeros((n_extra, 2 * S5_STATE), F32)], axis=0)
    hs, conv_s, state_s = _even_mixer(xs[None], *mixer_w, cinit_s, sinit_s, *mixer_w2, carried=False)
    hs = hs[0]
    cinit_p = jnp.broadcast_to(conv_s[:, META_SEG:META_SEG + 1], (2, nb, CONV_CH))
    sinit_p = jnp.broadcast_to(state_s[META_SEG:META_SEG + 1], (nb, 2 * S5_STATE))
    hp, conv_p, state_p = _even_mixer(x_prompt, *mixer_w, cinit_p, sinit_p, *mixer_w2, carried=True)
    hp = hp.reshape(nb * seq, D_MODEL)
    ffn_w = (row(norm_ffn[0]), ffn_w1[0].astype(BF16), ffn_w3[0].astype(BF16), ffn_w2[0].astype(BF16))
    hs = _ffn(hs, *ffn_w, tm=256)
    hp = _ffn(hp, *ffn_w, tm=512)

    qkv_w = (row(norm_mix[1]), w_qkv[0].astype(BF16), row(b_qkv[0]))
    qs, ks, vs = _qkv(hs, *qkv_w, tm=640)
    qp, kp, vp = _qkv(hp, *qkv_w, tm=1024)
    wo = (w_o[0].astype(BF16), row(b_o[0]))
    hs = _attn_small(qs, cache_swa_k[0].reshape(nsb, WINDOW, KV_DIM), cache_swa_v[0].reshape(nsb, WINDOW, KV_DIM),
                     ks, vs, _small_bias(rel_bias_table), _lane_sink(attn_sink[0], SEG), hs, *wo)

    def padded(new, small):
        meta_rows = small[META_SEG * SEG:(META_SEG + 1) * SEG]
        front = jnp.concatenate([jnp.zeros((WINDOW - SEG, KV_DIM), F32), meta_rows], axis=0)
        front = jnp.broadcast_to(front[None], (nb, WINDOW, KV_DIM))
        return jnp.concatenate([front, new.reshape(nb, seq, KV_DIM)], axis=1).astype(BF16)

    hp = _attn_prompt(qp, padded(kp, ks), padded(vp, vs), _prompt_bias(rel_bias_table),
                      _lane_sink(attn_sink[0], TQ), hp, *wo)
    moe_w = (row(norm_ffn[1]), router_w[0].T.astype(BF16), router_b[0].reshape(N_EXPERTS, 1), moe_w1[0].astype(BF16),
             moe_w3[0].astype(BF16), moe_w2[0].astype(BF16), row(norm_final))
    ys = _moe_final(hs, *moe_w, tb=hs.shape[0], subs=1, tile_rows=(352,))
    yp = _moe_final(hp, *moe_w, tb=1024, subs=2, tile_rows=tuple(range(96, 209, BF16_TILE_ROWS)))

    n_real = nsb * SEG
    y_prompt = yp.reshape(nb, seq, D_MODEL)
    y_sample = ys[:n_real].reshape(nsb, SEG, D_MODEL)

    def split_state(st, n):
        st = st[:n].reshape(n, 2, 2, HALF_STATE)
        return tuple(st[:, :, ri].reshape(1, n, S5_GROUPS, S5_P) for ri in range(2))

    s5rp, s5ip = split_state(state_p, nb)
    s5rs, s5is = split_state(state_s, nsb)
    kp4 = kp.reshape(nb, seq, KV_DIM)[:, -WINDOW:].reshape(1, nb, WINDOW, N_KV, HEAD_DIM)
    vp4 = vp.reshape(nb, seq, KV_DIM)[:, -WINDOW:].reshape(1, nb, WINDOW, N_KV, HEAD_DIM)
    ks4 = ks[:n_real].reshape(1, nsb, SEG, N_KV, HEAD_DIM)
    vs4 = vs[:n_real].reshape(1, nsb, SEG, N_KV, HEAD_DIM)
    conv_p4 = jnp.swapaxes(conv_p, 0, 1)[None]
    conv_s4 = jnp.swapaxes(conv_s[:, :nsb], 0, 1)[None]
    return (y_prompt, y_sample, conv_p4, conv_s4, s5rp, s5ip, s5rs, s5is, kp4, vp4, ks4, vs4)
```

```python
import functools
import math

import numpy as np
import jax
import jax.numpy as jnp
from jax import lax
from jax.experimental import pallas as pl
from jax.experimental.pallas import tpu as pltpu

F32 = jnp.float32
BF16 = jnp.bfloat16

D_MODEL = 1024
CONV_CH = 512
S5_CH = 512
S5_GROUP = 16
S5_GROUPS = 32
S5_P = 64
S5_STATE = S5_GROUPS * S5_P
HALF_CH = S5_CH // 2
HALF_STATE = S5_STATE // 2
D_FF = 2816
N_HEADS = 16
N_KV = 2
GQ = N_HEADS // N_KV
HEAD_DIM = 64
KV_DIM = N_KV * HEAD_DIM
WINDOW = 128
CHUNK = 64
N_META = 16
N_BUCKETS = 32
MAX_DIST = 128
N_EXPERTS = 8
D_FF_E = 1024
EPS = 1e-6
NEG = -1e30

SUBLANES = 8
BF16_TILE_ROWS = 16
SEG = 32
N_SAMPLE_SEG = 32
META_SEG = N_SAMPLE_SEG
N_SMALL_SEG = 40
TM_MIX = 256
CHUNK_ROWS = SEG
SCAN_LANES = 512
TQ = 128
TQ_PER_STEP = 8
SEG_PER_STEP = 8
VMEM_LIMIT = 56 * 1024 * 1024


def _const_spec(shape):
    nd = len(shape)
    return pl.BlockSpec(shape, lambda *_: (0,) * nd, pipeline_mode=pl.Buffered(1))


def _params(*sem):
    return pltpu.CompilerParams(dimension_semantics=sem, vmem_limit_bytes=VMEM_LIMIT)


def _rms(x, g):
    return x * lax.rsqrt(jnp.mean(x * x, axis=-1, keepdims=True) + EPS) * g


def _dot(a, b):
    return jnp.dot(a, b, preferred_element_type=F32)


def _even_mixer_kernel(x_ref, gm_ref, perm_ref, unperm_ref, win_ref, cw_ref, cb_ref, cinit_ref, sinit_ref,
                       ab_ref, hop_ref, pw_ref, bb_ref, cc_ref, d_ref, gluw_ref, glub_ref, wout_ref,
                       h_ref, cout_ref, sout_ref, bu_ref, ccarry_ref, scarry_ref, *, carried, tiles_per_seq):
    if carried:
        @pl.when(pl.program_id(0) % tiles_per_seq == 0)
        def _load_initial_state():
            ccarry_ref[...] = cinit_ref[...]
            scarry_ref[...] = sinit_ref[...]

    streams = range(x_ref.shape[0])
    mid = [_mixer_project(b, x_ref, gm_ref, perm_ref, win_ref, cw_ref, cb_ref, cinit_ref, bb_ref, cout_ref, bu_ref,
                          ccarry_ref, carried=carried) for b in streams]
    for b in streams:
        _mixer_recur(b, sinit_ref, ab_ref, hop_ref, pw_ref, sout_ref, bu_ref, scarry_ref, carried=carried)
        _mixer_output(b, *mid[b], unperm_ref, cc_ref, d_ref, gluw_ref, glub_ref, wout_ref, h_ref, bu_ref)


def _mixer_project(b, x_ref, gm_ref, perm_ref, win_ref, cw_ref, cb_ref, cinit_ref, bb_ref, cout_ref, bu_ref,
                   ccarry_ref, *, carried):
    tm = TM_MIX
    x = x_ref[b]
    hn = _rms(x, gm_ref[...]).astype(BF16)
    hn = _dot(perm_ref[...], hn).astype(BF16)
    proj = _dot(hn, win_ref[...])
    g_b = proj[:, :CONV_CH]
    cin = proj[:, CONV_CH:2 * CONV_CH] * proj[:, 2 * CONV_CH:3 * CONV_CH]
    u = proj[:, 3 * CONV_CH:]
    ub = u.astype(BF16)
    for half in range(2):
        bu_ref[b, :, half * S5_STATE:(half + 1) * S5_STATE] = _dot(ub[:, half * HALF_CH:(half + 1) * HALF_CH],
                                                                  bb_ref[half])

    last, before_last = cin[tm - SUBLANES:], cin[tm - 2 * SUBLANES:tm - SUBLANES]
    if carried:
        first_chunk = lax.broadcasted_iota(jnp.int32, (SUBLANES, CONV_CH), 0) == 0
        newer = jnp.where(first_chunk, ccarry_ref[1, b:b + 1, :], pltpu.roll(last, 1, 0))
        older = jnp.where(first_chunk, ccarry_ref[0, b:b + 1, :], pltpu.roll(before_last, 1, 0))
        for i, rows in enumerate((before_last, last)):
            ccarry_ref[i, b:b + 1, :] = rows[SUBLANES - 1:]
            cout_ref[i, b:b + 1, :] = rows[SUBLANES - 1:]
    else:
        older, newer = cinit_ref[0], cinit_ref[1]
        cout_ref[0] = before_last
        cout_ref[1] = last
    ext = jnp.concatenate([older, newer, cin], axis=0)
    cw = cw_ref[...]
    out_a = g_b * (cw[0:1] * ext[:tm] + cw[1:2] * ext[SUBLANES:tm + SUBLANES] + cw[2:3] * cin + cb_ref[...])
    return x, u, out_a


def _mixer_recur(b, sinit_ref, ab_ref, hop_ref, pw_ref, sout_ref, bu_ref, scarry_ref, *, carried):
    for c in range(S5_STATE // SCAN_LANES):
        half, within = divmod(c * SCAN_LANES, HALF_STATE)
        nat = slice(c * SCAN_LANES, (c + 1) * SCAN_LANES)
        re_cols = slice(half * S5_STATE + within, half * S5_STATE + within + SCAN_LANES)
        im_cols = slice(re_cols.start + HALF_STATE, re_cols.stop + HALF_STATE)
        ar, ai = ab_ref[0, :, nat], ab_ref[1, :, nat]
        if carried:
            sr = si = jnp.zeros((SUBLANES, SCAN_LANES), F32)
        else:
            sr, si = sinit_ref[:, re_cols], sinit_ref[:, im_cols]
        for g in range(CHUNK_ROWS):
            rows = slice(SUBLANES * g, SUBLANES * (g + 1))
            sr, si = (ar * sr - ai * si + bu_ref[b, rows, re_cols], ar * si + ai * sr + bu_ref[b, rows, im_cols])
            bu_ref[b, rows, re_cols] = sr
            bu_ref[b, rows, im_cols] = si
        if not carried:
            sout_ref[:, re_cols] = sr
            sout_ref[:, im_cols] = si
            continue
        first_chunk = lax.broadcasted_iota(jnp.int32, (SUBLANES, SCAN_LANES), 0) == 0
        er = jnp.where(first_chunk, scarry_ref[b:b + 1, re_cols], pltpu.roll(sr, 1, 0))
        ei = jnp.where(first_chunk, scarry_ref[b:b + 1, im_cols], pltpu.roll(si, 1, 0))
        for k, sh in enumerate((1, 2, 4)):
            wr, wi = hop_ref[2 * k, :, nat], hop_ref[2 * k + 1, :, nat]
            tr, ti = pltpu.roll(er, sh, 0), pltpu.roll(ei, sh, 0)
            er, ei = er + wr * tr - wi * ti, ei + wr * ti + wi * tr
        wr, wi = hop_ref[6, :, nat], hop_ref[7, :, nat]
        fr, fi = wr * er - wi * ei + sr, wr * ei + wi * er + si
        for ref in (scarry_ref, sout_ref):
            ref[b:b + 1, re_cols] = fr[SUBLANES - 1:]
            ref[b:b + 1, im_cols] = fi[SUBLANES - 1:]
        for g in range(CHUNK_ROWS):
            rows = slice(SUBLANES * g, SUBLANES * (g + 1))
            pr, pi = pw_ref[0, g, :, nat], pw_ref[1, g, :, nat]
            bu_ref[b, rows, re_cols] += pr * er - pi * ei
            bu_ref[b, rows, im_cols] += pr * ei + pi * er


def _mixer_output(b, x, u, out_a, unperm_ref, cc_ref, d_ref, gluw_ref, glub_ref, wout_ref, h_ref, bu_ref):
    y = jnp.concatenate([_dot(bu_ref[b, :, half * S5_STATE:(half + 1) * S5_STATE].astype(BF16), cc_ref[half])
                         for half in range(2)], axis=-1) + d_ref[...] * u
    z = jax.nn.gelu(y)
    out_b = z * jax.nn.sigmoid(_dot(z.astype(BF16), gluw_ref[...]) + glub_ref[...])
    mix = jnp.concatenate([out_a, out_b], axis=-1).astype(BF16)
    mix = _dot(unperm_ref[...], mix).astype(BF16)
    h_ref[b] = x + _dot(mix, wout_ref[...])


def _even_mixer(x, gm, w_in, cw, cb, cinit, sinit, ab, hop, pw, bb, cc, d, glu_w, glu_b, w_out, *, carried):
    nb, rows = x.shape[0], x.shape[1]
    tm = TM_MIX
    chunks = tm // CHUNK_ROWS
    assert chunks == SUBLANES and rows % tm == 0 and (carried or nb == 1)
    per_step = nb if carried else chunks
    tiles_per_seq = rows // tm if carried else 1
    nseq = sinit.shape[0]
    t_of = np.arange(tm)
    perm = np.zeros((tm, tm), np.float32)
    perm[SUBLANES * (t_of % CHUNK_ROWS) + t_of // CHUNK_ROWS, t_of] = 1.0
    kern = functools.partial(_even_mixer_kernel, carried=carried, tiles_per_seq=tiles_per_seq)
    return pl.pallas_call(
        kern,
        grid=(rows // tm,),
        in_specs=[
            pl.BlockSpec((nb, tm, D_MODEL), lambda t: (0, t, 0)),
            _const_spec((1, D_MODEL)),
            _const_spec((tm, tm)),
            _const_spec((tm, tm)),
            _const_spec((D_MODEL, 4 * CONV_CH)),
            _const_spec((3, CONV_CH)),
            _const_spec((1, CONV_CH)),
            pl.BlockSpec((2, per_step, CONV_CH), lambda t: (0, t // tiles_per_seq, 0)),
            pl.BlockSpec((per_step, 2 * S5_STATE), lambda t: (t // tiles_per_seq, 0)),
            _const_spec(ab.shape),
            _const_spec(hop.shape),
            _const_spec(pw.shape),
            _const_spec((2, HALF_CH, S5_STATE)),
            _const_spec((2, S5_STATE, HALF_CH)),
            _const_spec((1, S5_CH)),
            _const_spec((S5_CH, S5_CH)),
            _const_spec((1, S5_CH)),
            _const_spec((D_MODEL, D_MODEL)),
        ],
        out_specs=[
            pl.BlockSpec((nb, tm, D_MODEL), lambda t: (0, t, 0)),
            pl.BlockSpec((2, per_step, CONV_CH), lambda t: (0, t // tiles_per_seq, 0)),
            pl.BlockSpec((per_step, 2 * S5_STATE), lambda t: (t // tiles_per_seq, 0)),
        ],
        out_shape=[
            jax.ShapeDtypeStruct((nb, rows, D_MODEL), F32),
            jax.ShapeDtypeStruct((2, nseq, CONV_CH), F32),
            jax.ShapeDtypeStruct((nseq, 2 * S5_STATE), F32),
        ],
        scratch_shapes=[
            pltpu.VMEM((nb, tm, 2 * S5_STATE), F32),
            pltpu.VMEM((2, nb, CONV_CH), F32),
            pltpu.VMEM((nb, 2 * S5_STATE), F32),
        ],
        compiler_params=_params("arbitrary"),
        name="even_mixer",
    )(x, gm, jnp.asarray(perm, BF16), jnp.asarray(perm.T, BF16), w_in, cw, cb, cinit, sinit, ab, hop, pw, bb, cc, d,
      glu_w, glu_b, w_out)


def _ffn_kernel(x_ref, g_ref, w1_ref, w3_ref, w2_ref, o_ref):
    x = x_ref[...]
    hn = _rms(x, g_ref[...]).astype(BF16)
    a = _dot(hn, w1_ref[...])
    b = _dot(hn, w3_ref[...])
    o_ref[...] = x + _dot((a * jax.nn.sigmoid(a) * b).astype(BF16), w2_ref[...])


def _ffn(x, g, w1, w3, w2, *, tm):
    rows = x.shape[0]
    return pl.pallas_call(
        _ffn_kernel,
        grid=(rows // tm,),
        in_specs=[
            pl.BlockSpec((tm, D_MODEL), lambda t: (t, 0)),
            _const_spec((1, D_MODEL)),
            _const_spec((D_MODEL, D_FF)),
            _const_spec((D_MODEL, D_FF)),
            _const_spec((D_FF, D_MODEL)),
        ],
        out_specs=pl.BlockSpec((tm, D_MODEL), lambda t: (t, 0)),
        out_shape=jax.ShapeDtypeStruct((rows, D_MODEL), F32),
        compiler_params=_params("parallel"),
        name="ffn",
    )(x, g, w1, w3, w2)


def _qkv_kernel(x_ref, g_ref, w_ref, b_ref, q_ref, k_ref, v_ref):
    hn = _rms(x_ref[...], g_ref[...]).astype(BF16)
    qkv = _dot(hn, w_ref[...]) + b_ref[...]
    nq = N_HEADS * HEAD_DIM
    q_ref[...] = (qkv[:, :nq] * (HEAD_DIM ** -0.5)).astype(BF16)
    k_ref[...] = qkv[:, nq:nq + KV_DIM]
    v_ref[...] = qkv[:, nq + KV_DIM:]


def _qkv(x, g, w, b, *, tm):
    rows = x.shape[0]
    ncol = (N_HEADS + 2 * N_KV) * HEAD_DIM
    return pl.pallas_call(
        _qkv_kernel,
        grid=(rows // tm,),
        in_specs=[
            pl.BlockSpec((tm, D_MODEL), lambda t: (t, 0)),
            _const_spec((1, D_MODEL)),
            _const_spec((D_MODEL, ncol)),
            _const_spec((1, ncol)),
        ],
        out_specs=[
            pl.BlockSpec((tm, N_HEADS * HEAD_DIM), lambda t: (t, 0)),
            pl.BlockSpec((tm, KV_DIM), lambda t: (t, 0)),
            pl.BlockSpec((tm, KV_DIM), lambda t: (t, 0)),
        ],
        out_shape=[
            jax.ShapeDtypeStruct((rows, N_HEADS * HEAD_DIM), BF16),
            jax.ShapeDtypeStruct((rows, KV_DIM), F32),
            jax.ShapeDtypeStruct((rows, KV_DIM), F32),
        ],
        compiler_params=_params("parallel"),
        name="qkv",
    )(x, g, w, b)


def _attend_groups(q, keys, vals, bias_of_group, sink_of_group, o_scr, row0):
    nq = q.shape[0]
    for kv in range(N_KV):
        kv_cols = slice(kv * HEAD_DIM, (kv + 1) * HEAD_DIM)
        heads = [slice((kv * GQ + g) * HEAD_DIM, (kv * GQ + g + 1) * HEAD_DIM) for g in range(GQ)]
        qs = jnp.concatenate([q[:, cols] for cols in heads], axis=0)
        st = lax.dot_general(keys[:, kv_cols], qs, (((1,), (1,)), ((), ())),
                             preferred_element_type=F32) + bias_of_group(kv)
        sk = sink_of_group(kv)
        m = jnp.maximum(jnp.max(st, axis=0, keepdims=True), sk)
        p = jnp.exp(st - m)
        den = jnp.sum(p, axis=0, keepdims=True) + jnp.exp(sk - m)
        o = lax.dot_general((p * (1.0 / den)).astype(BF16), vals[:, kv_cols], (((0,), (0,)), ((), ())),
                            preferred_element_type=F32)
        for g, cols in enumerate(heads):
            o_scr[row0:row0 + nq, cols] = o[g * nq:(g + 1) * nq]


def _attn_prompt_kernel(q_ref, k_ref, v_ref, bias_ref, sink_ref, x_ref, wo_ref, bo_ref, o_ref, o_scr):
    t = pl.program_id(1)
    for u in range(TQ_PER_STEP):
        tile = t * TQ_PER_STEP + u
        start = pl.multiple_of(tile * TQ, TQ)
        keys = k_ref[0, pl.ds(start, TQ + WINDOW), :]
        vals = v_ref[0, pl.ds(start, TQ + WINDOW), :]
        variant = jnp.minimum(tile, 1)
        _attend_groups(q_ref[u * TQ:(u + 1) * TQ], keys, vals, lambda kv: bias_ref[variant, kv],
                       lambda kv: sink_ref[kv], o_scr, u * TQ)
    o_ref[...] = x_ref[...] + _dot(o_scr[...].astype(BF16), wo_ref[...]) + bo_ref[...]


def _attn_prompt(q, kpad, vpad, bias, sink, x, w_o, b_o):
    nb, lpad = kpad.shape[0], kpad.shape[1]
    tm = TQ_PER_STEP * TQ
    steps = (lpad - WINDOW) // tm
    return pl.pallas_call(
        _attn_prompt_kernel,
        grid=(nb, steps),
        in_specs=[
            pl.BlockSpec((tm, D_MODEL), lambda b, t: (b * steps + t, 0)),
            pl.BlockSpec((1, lpad, KV_DIM), lambda b, t: (b, 0, 0)),
            pl.BlockSpec((1, lpad, KV_DIM), lambda b, t: (b, 0, 0)),
            _const_spec((2, N_KV, TQ + WINDOW, GQ * TQ)),
            _const_spec((N_KV, 1, GQ * TQ)),
            pl.BlockSpec((tm, D_MODEL), lambda b, t: (b * steps + t, 0)),
            _const_spec((D_MODEL, D_MODEL)),
            _const_spec((1, D_MODEL)),
        ],
        out_specs=pl.BlockSpec((tm, D_MODEL), lambda b, t: (b * steps + t, 0)),
        out_shape=jax.ShapeDtypeStruct(x.shape, F32),
        scratch_shapes=[pltpu.VMEM((tm, D_MODEL), F32)],
        compiler_params=_params("parallel", "arbitrary"),
        name="attn_prompt",
    )(q, kpad, vpad, bias, sink, x, w_o, b_o)


def _attn_small_kernel(q_ref, kc_ref, vc_ref, kn_ref, vn_ref, bias_ref, sink_ref, x_ref, wo_ref, bo_ref,
                       o_ref, o_scr):
    step = pl.program_id(0)
    for j in range(SEG_PER_STEP):
        rows = slice(j * SEG, (j + 1) * SEG)
        keys = jnp.concatenate([kc_ref[j], kn_ref[rows]], axis=0).astype(BF16)
        vals = jnp.concatenate([vc_ref[j], vn_ref[rows]], axis=0).astype(BF16)
        variant = jnp.where(step * SEG_PER_STEP + j == META_SEG, 1, 0)
        _attend_groups(q_ref[rows], keys, vals, lambda kv: bias_ref[variant, kv], lambda kv: sink_ref[kv], o_scr,
                       j * SEG)
    o_ref[...] = x_ref[...] + _dot(o_scr[...].astype(BF16), wo_ref[...]) + bo_ref[...]


def _attn_small(q, kc, vc, kn, vn, bias, sink, x, w_o, b_o):
    tm = SEG_PER_STEP * SEG
    steps = x.shape[0] // tm
    cache_blocks = kc.shape[0] // SEG_PER_STEP
    cache_spec = pl.BlockSpec((SEG_PER_STEP, WINDOW, KV_DIM), lambda s: (jnp.minimum(s, cache_blocks - 1), 0, 0))
    return pl.pallas_call(
        _attn_small_kernel,
        grid=(steps,),
        in_specs=[
            pl.BlockSpec((tm, D_MODEL), lambda s: (s, 0)),
            cache_spec,
            cache_spec,
            pl.BlockSpec((tm, KV_DIM), lambda s: (s, 0)),
            pl.BlockSpec((tm, KV_DIM), lambda s: (s, 0)),
            _const_spec((2, N_KV, WINDOW + SEG, GQ * SEG)),
            _const_spec((N_KV, 1, GQ * SEG)),
            pl.BlockSpec((tm, D_MODEL), lambda s: (s, 0)),
            _const_spec((D_MODEL, D_MODEL)),
            _const_spec((1, D_MODEL)),
        ],
        out_specs=pl.BlockSpec((tm, D_MODEL), lambda s: (s, 0)),
        out_shape=jax.ShapeDtypeStruct(x.shape, F32),
        scratch_shapes=[pltpu.VMEM((tm, D_MODEL), F32)],
        compiler_params=_params("parallel"),
        name="attn_small",
    )(q, kc, vc, kn, vn, bias, sink, x, w_o, b_o)


def _expert_of_step(block, step):
    return jnp.where(block % 2 == 0, step, N_EXPERTS - 1 - step)


def _moe_kernel(x_ref, g_ref, rwt_ref, rb_ref, w1_ref, w3_ref, w2_ref, gf_ref, o_ref,
                hn_scr, slot_scr, gate_scr, cnt_smem, acc_scr, yall_scr, *, subs, tile_rows):
    step = pl.program_id(1)
    e = _expert_of_step(pl.program_id(0), step)
    tb = x_ref.shape[0]
    ts = tb // subs
    cap = tile_rows[-1]
    sub_cols = [slice(s * ts, (s + 1) * ts) for s in range(subs)]

    @pl.when(step == 0)
    def _route():
        hn = _rms(x_ref[...], g_ref[...]).astype(BF16)
        hn_scr[...] = hn
        logits = lax.dot_general(rwt_ref[...], hn, (((1,), (1,)), ((), ())),
                                 preferred_element_type=F32) + rb_ref[...]
        row = lax.broadcasted_iota(jnp.int32, logits.shape, 0).astype(F32)
        v1 = jnp.max(logits, axis=0, keepdims=True)
        i1 = jnp.min(jnp.where(logits == v1, row, float(N_EXPERTS)), axis=0, keepdims=True)
        rest = jnp.where(row == i1, -jnp.inf, logits)
        v2 = jnp.max(rest, axis=0, keepdims=True)
        i2 = jnp.min(jnp.where(rest == v2, row, float(N_EXPERTS)), axis=0, keepdims=True)
        e2 = jnp.exp(v2 - v1)
        den = 1.0 + e2
        gate_scr[...] = jnp.where(row == i1, 1.0 / den, 0.0) + jnp.where(row == i2, e2 / den, 0.0)
        chosen = (row == i1) | (row == i2)
        ind = jnp.where(chosen, 1.0, 0.0)
        before = lax.broadcasted_iota(jnp.int32, (ts, ts), 0) < lax.broadcasted_iota(jnp.int32, (ts, ts), 1)
        before = jnp.where(before, 1.0, 0.0).astype(BF16)
        for s, cols in enumerate(sub_cols):
            rank = _dot(ind[:, cols].astype(BF16), before)
            slot_scr[:, cols] = jnp.where(chosen[:, cols], rank, -1.0).astype(jnp.int32)
            for k in range(N_EXPERTS):
                cnt_smem[k * subs + s] = jnp.sum(ind[k:k + 1, cols]).astype(jnp.int32)
        acc_scr[...] = jnp.zeros_like(acc_scr)

    def _scatter(onehot, rows):
        return lax.dot_general(onehot, rows, (((0,), (0,)), ((), ())), preferred_element_type=F32)

    def _expert_tile(i, rows=cap):
        packed_row = lax.broadcasted_iota(jnp.int32, (rows, ts), 0) + i * cap
        hits = [packed_row == slot_scr[pl.ds(e, 1), cols] for cols in sub_cols]
        onehots = [jnp.where(hit, 1.0, 0.0).astype(BF16) for hit in hits]
        xe = jnp.concatenate([_dot(onehot, hn_scr[cols, :]) for onehot, cols in zip(onehots, sub_cols)],
                             axis=0).astype(BF16)
        a = _dot(xe, w1_ref[0])
        b = _dot(xe, w3_ref[0])
        y = _dot((a * jax.nn.sigmoid(a) * b).astype(BF16), w2_ref[0])
        g_row = jnp.concatenate([jnp.sum(jnp.where(hit, gate_scr[pl.ds(e, 1), cols], 0.0), axis=1, keepdims=True)
                                 for hit, cols in zip(hits, sub_cols)], axis=0)
        ys = (y * g_row).astype(BF16)
        return onehots, [ys[s * rows:(s + 1) * rows] for s in range(subs)]

    base = pl.multiple_of(e * cap, BF16_TILE_ROWS)
    count = cnt_smem[e * subs]
    for s in range(1, subs):
        count = jnp.maximum(count, cnt_smem[e * subs + s])
    for lower, rows in zip((-1,) + tile_rows, tile_rows):
        fits = (count > lower) & (count <= rows) if rows < cap else count > lower

        @pl.when(fits)
        def _first_tile(rows=rows):
            for s, ys in enumerate(_expert_tile(0, rows)[1]):
                yall_scr[s, pl.ds(base, rows), :] = ys
                if rows < cap:
                    yall_scr[s, pl.ds(base + rows, cap - rows), :] = jnp.zeros((cap - rows, D_MODEL), BF16)

    def _overflow_tile(i, carry):
        onehots, yss = _expert_tile(i)
        for onehot, ys, cols in zip(onehots, yss, sub_cols):
            acc_scr[cols, :] += _scatter(onehot, ys)
        return carry

    lax.fori_loop(1, (count + cap - 1) // cap, _overflow_tile, 0)

    @pl.when(step == N_EXPERTS - 1)
    def _finish():
        packed_row = lax.broadcasted_iota(jnp.int32, (cap, ts), 0)
        for s, cols in enumerate(sub_cols):
            onehot = jnp.concatenate([jnp.where(packed_row == slot_scr[k:k + 1, cols], 1.0, 0.0).astype(BF16)
                                      for k in range(N_EXPERTS)], axis=0)
            o_ref[cols, :] = _rms(x_ref[cols, :] + acc_scr[cols, :] + _scatter(onehot, yall_scr[s]), gf_ref[...])


def _moe_final(x, g, rwt, rb, w1, w3, w2, gf, *, tb, subs, tile_rows):
    rows, cap = x.shape[0], tile_rows[-1]
    return pl.pallas_call(
        functools.partial(_moe_kernel, subs=subs, tile_rows=tile_rows),
        grid=(rows // tb, N_EXPERTS),
        in_specs=[
            pl.BlockSpec((tb, D_MODEL), lambda t, e: (t, 0)),
            _const_spec((1, D_MODEL)),
            _const_spec((N_EXPERTS, D_MODEL)),
            _const_spec((N_EXPERTS, 1)),
            pl.BlockSpec((1, D_MODEL, D_FF_E), lambda t, s: (_expert_of_step(t, s), 0, 0)),
            pl.BlockSpec((1, D_MODEL, D_FF_E), lambda t, s: (_expert_of_step(t, s), 0, 0)),
            pl.BlockSpec((1, D_FF_E, D_MODEL), lambda t, s: (_expert_of_step(t, s), 0, 0)),
            _const_spec((1, D_MODEL)),
        ],
        out_specs=pl.BlockSpec((tb, D_MODEL), lambda t, e: (t, 0)),
        out_shape=jax.ShapeDtypeStruct((rows, D_MODEL), F32),
        scratch_shapes=[
            pltpu.VMEM((tb, D_MODEL), BF16),
            pltpu.VMEM((N_EXPERTS, tb), jnp.int32),
            pltpu.VMEM((N_EXPERTS, tb), F32),
            pltpu.SMEM((N_EXPERTS * subs,), jnp.int32),
            pltpu.VMEM((tb, D_MODEL), F32),
            pltpu.VMEM((subs, N_EXPERTS * cap, D_MODEL), BF16),
        ],
        compiler_params=_params("parallel", "arbitrary"),
        name="moe_final",
    )(x, g, rwt, rb, w1, w3, w2, gf)


def _t5_bucket_np(rel):
    half = N_BUCKETS // 2
    max_exact = half // 2
    ret = np.where(rel > 0, half, 0)
    n = np.abs(rel)
    nf = np.maximum(n, 1).astype(np.float32)
    large = max_exact + (np.log(nf / np.float32(max_exact)) / np.float32(math.log(MAX_DIST / max_exact))
                         * np.float32(half - max_exact)).astype(np.int32)
    large = np.minimum(large, half - 1)
    return ret + np.where(n < max_exact, n, large)


def _bias_variants(table, bucket, keeps):
    nq, nk = bucket.shape
    onehot = jnp.asarray(np.eye(N_BUCKETS, dtype=np.float32)[bucket])
    bias = jnp.einsum('rjb,bh->hjr', onehot, table, precision=lax.Precision.HIGHEST)
    bias = jnp.stack([jnp.where(keep.T[None], bias, NEG) for keep in keeps], axis=0)
    bias = bias.reshape(len(keeps), N_KV, GQ, nk, nq)
    return jnp.transpose(bias, (0, 1, 3, 2, 4)).reshape(len(keeps), N_KV, nk, GQ * nq)


def _lane_sink(sink, nq):
    return jnp.broadcast_to(sink.reshape(N_KV, 1, GQ, 1), (N_KV, 1, GQ, nq)).reshape(N_KV, 1, GQ * nq)


def _prompt_bias(table):
    r = np.arange(TQ)[:, None]
    j = np.arange(TQ + WINDOW)[None, :]
    band = (j - CHUNK * (r // CHUNK) >= 0) & (j - CHUNK * (r // CHUNK) < WINDOW + CHUNK)
    first = band & (j >= WINDOW - N_META)
    return _bias_variants(table, _t5_bucket_np(j - WINDOW - r), [first, band])


def _small_bias(table):
    i = np.arange(SEG)[:, None]
    m = np.arange(WINDOW + SEG)[None, :]
    everything = np.ones((SEG, WINDOW + SEG), bool)
    meta_only = everything & (m >= WINDOW + SEG - N_META)
    return _bias_variants(table, _t5_bucket_np(m - WINDOW - i), [everything, meta_only])


def _pack_state(re, im):
    return jnp.concatenate([re[..., :HALF_STATE], im[..., :HALF_STATE], re[..., HALF_STATE:], im[..., HALF_STATE:]],
                           axis=-1)


def _cmul(ar, ai, br, bi):
    return ar * br - ai * bi, ar * bi + ai * br


def _s5_prepare(a_re, a_im, log_dt, b_re, b_im, c_re, c_im):
    dt = jnp.exp(log_dt)[:, None]
    mag = jnp.exp(a_re * dt)
    ab_re, ab_im = mag * jnp.cos(a_im * dt), mag * jnp.sin(a_im * dt)
    den = a_re * a_re + a_im * a_im
    num_re = ab_re - 1.0
    cf_re = (num_re * a_re + ab_im * a_im) / den
    cf_im = (ab_im * a_re - num_re * a_im) / den
    bb_re = cf_re[..., None] * b_re - cf_im[..., None] * b_im
    bb_im = cf_re[..., None] * b_im + cf_im[..., None] * b_re

    def group_diag(v):
        r, c = v.shape[1], v.shape[2]
        gh = S5_GROUPS // 2
        spread = np.tile(np.eye(c, dtype=np.float32), (1, gh))
        wide = jnp.einsum('hrc,cn->hrn', v.reshape(2, gh * r, c), spread, precision=lax.Precision.HIGHEST)
        keep = (np.arange(gh * r)[:, None] // r) == (np.arange(gh * c)[None, :] // c)
        return jnp.where(keep, wide, 0.0)

    def t(v):
        return jnp.swapaxes(v, 1, 2)

    bb = jnp.concatenate([group_diag(t(bb_re)), group_diag(t(bb_im))], axis=2).astype(BF16)
    cc = jnp.concatenate([group_diag(t(c_re)), -group_diag(t(c_im))], axis=1).astype(BF16)

    pr, pi = ab_re.reshape(1, S5_STATE), ab_im.reshape(1, S5_STATE)
    n = 1
    while n < CHUNK_ROWS:
        qr, qi = _cmul(pr, pi, pr[n - 1:n], pi[n - 1:n])
        pr, pi = jnp.concatenate([pr, qr], axis=0), jnp.concatenate([pi, qi], axis=0)
        n *= 2

    def tile8(v):
        return jnp.broadcast_to(v, (SUBLANES,) + v.shape[1:])

    ab = jnp.stack([tile8(pr[0:1]), tile8(pi[0:1])])
    sub = np.arange(SUBLANES)[:, None]
    hop, w = [], (pr[CHUNK_ROWS - 1:], pi[CHUNK_ROWS - 1:])
    for k in (1, 2, 4):
        hop += [jnp.where(sub >= k, w[0], 0.0), jnp.where(sub >= k, w[1], 0.0)]
        if k < 4:
            w = _cmul(*w, *w)
    hop = jnp.stack(hop + [tile8(pr[CHUNK_ROWS - 1:]), tile8(pi[CHUNK_ROWS - 1:])])
    pw = jnp.stack([jnp.broadcast_to(p[:, None, :], (CHUNK_ROWS, SUBLANES, S5_STATE)) for p in (pr, pi)])
    return bb, cc, ab, hop, pw


def kernel(x_prompt, x_sample, cache_conv, state_s5_re, state_s5_im, cache_swa_k, cache_swa_v, meta_tokens, rel_bias_table, norm_mix, norm_ffn, norm_final, w_in0, conv_w, conv_b, s5_a_re, s5_a_im, s5_log_dt, s5_b_re, s5_b_im, s5_c_re, s5_c_im, s5_d, s5_glu_w, s5_glu_b, w_out0, ffn_w1, ffn_w3, ffn_w2, w_qkv, b_qkv, attn_sink, w_o, b_o, router_w, router_b, moe_w1, moe_w3, moe_w2):
    nb, seq = x_prompt.shape[0], x_prompt.shape[1]
    nsb = x_sample.shape[0]
    assert x_sample.shape[1] == SEG and nsb == N_SAMPLE_SEG and seq % TM_MIX == 0

    def row(v):
        return v.reshape(1, -1)

    meta_seg = jnp.concatenate([jnp.zeros((SEG - N_META, D_MODEL), F32), meta_tokens], axis=0)
    filler = jnp.zeros(((N_SMALL_SEG - N_SAMPLE_SEG - 1) * SEG, D_MODEL), F32)
    xs = jnp.concatenate([x_sample.reshape(nsb * SEG, D_MODEL), meta_seg, filler], axis=0)
    n_extra = N_SMALL_SEG - N_SAMPLE_SEG

    bb, cc, ab, hop, pw = _s5_prepare(s5_a_re[0], s5_a_im[0], s5_log_dt[0], s5_b_re[0], s5_b_im[0], s5_c_re[0],
                                      s5_c_im[0])
    mixer_w = (row(norm_mix[0]), w_in0[0].astype(BF16), conv_w[0], row(conv_b[0]))
    mixer_w2 = (ab, hop, pw, bb, cc, row(s5_d[0]), s5_glu_w[0].astype(BF16), row(s5_glu_b[0]), w_out0[0].astype(BF16))
    cinit_s = jnp.concatenate([jnp.swapaxes(cache_conv[0], 0, 1), jnp.zeros((2, n_extra, CONV_CH), F32)], axis=1)
    sinit_s = _pack_state(state_s5_re[0].reshape(nsb, S5_STATE), state_s5_im[0].reshape(nsb, S5_STATE))
    sinit_s = jnp.concatenate([sinit_s, jnp.zeros((n_extra, 2 * S5_STATE), F32)], axis=0)
    hs, conv_s, state_s = _even_mixer(xs[None], *mixer_w, cinit_s, sinit_s, *mixer_w2, carried=False)
    hs = hs[0]
    cinit_p = jnp.broadcast_to(conv_s[:, META_SEG:META_SEG + 1], (2, nb, CONV_CH))
    sinit_p = jnp.broadcast_to(state_s[META_SEG:META_SEG + 1], (nb, 2 * S5_STATE))
    hp, conv_p, state_p = _even_mixer(x_prompt, *mixer_w, cinit_p, sinit_p, *mixer_w2, carried=True)
    hp = hp.reshape(nb * seq, D_MODEL)
    ffn_w = (row(norm_ffn[0]), ffn_w1[0].astype(BF16), ffn_w3[0].astype(BF16), ffn_w2[0].astype(BF16))
    hs = _ffn(hs, *ffn_w, tm=256)
    hp = _ffn(hp, *ffn_w, tm=512)

    qkv_w = (row(norm_mix[1]), w_qkv[0].astype(BF16), row(b_qkv[0]))
    qs, ks, vs = _qkv(hs, *qkv_w, tm=640)
    qp, kp, vp = _qkv(hp, *qkv_w, tm=1024)
    wo = (w_o[0].astype(BF16), row(b_o[0]))
    hs = _attn_small(qs, cache_swa_k[0].reshape(nsb, WINDOW, KV_DIM), cache_swa_v[0].reshape(nsb, WINDOW, KV_DIM),
                     ks, vs, _small_bias(rel_bias_table), _lane_sink(attn_sink[0], SEG), hs, *wo)

    def padded(new, small):
        meta_rows = small[META_SEG * SEG:(META_SEG + 1) * SEG]
        front = jnp.concatenate([jnp.zeros((WINDOW - SEG, KV_DIM), F32), meta_rows], axis=0)
        front = jnp.broadcast_to(front[None], (nb, WINDOW, KV_DIM))
        return jnp.concatenate([front, new.reshape(nb, seq, KV_DIM)], axis=1).astype(BF16)

    hp = _attn_prompt(qp, padded(kp, ks), padded(vp, vs), _prompt_bias(rel_bias_table),
                      _lane_sink(attn_sink[0], TQ), hp, *wo)
    moe_w = (row(norm_ffn[1]), router_w[0].T.astype(BF16), router_b[0].reshape(N_EXPERTS, 1), moe_w1[0].astype(BF16),
             moe_w3[0].astype(BF16), moe_w2[0].astype(BF16), row(norm_final))
    ys = _moe_final(hs, *moe_w, tb=hs.shape[0], subs=1, tile_rows=(352,))
    yp = _moe_final(hp, *moe_w, tb=1024, subs=4, tile_rows=tuple(range(48, 129, BF16_TILE_ROWS)))

    n_real = nsb * SEG
    y_prompt = yp.reshape(nb, seq, D_MODEL)
    y_sample = ys[:n_real].reshape(nsb, SEG, D_MODEL)

    def split_state(st, n):
        st = st[:n].reshape(n, 2, 2, HALF_STATE)
        return tuple(st[:, :, ri].reshape(1, n, S5_GROUPS, S5_P) for ri in range(2))

    s5rp, s5ip = split_state(state_p, nb)
    s5rs, s5is = split_state(state_s, nsb)
    kp4 = kp.reshape(nb, seq, KV_DIM)[:, -WINDOW:].reshape(1, nb, WINDOW, N_KV, HEAD_DIM)
    vp4 = vp.reshape(nb, seq, KV_DIM)[:, -WINDOW:].reshape(1, nb, WINDOW, N_KV, HEAD_DIM)
    ks4 = ks[:n_real].reshape(1, nsb, SEG, N_KV, HEAD_DIM)
    vs4 = vs[:n_real].reshape(1, nsb, SEG, N_KV, HEAD_DIM)
    conv_p4 = jnp.swapaxes(conv_p, 0, 1)[None]
    conv_s4 = jnp.swapaxes(conv_s[:, :nsb], 0, 1)[None]
    return (y_prompt, y_sample, conv_p4, conv_s4, s5rp, s5ip, s5rs, s5is, kp4, vp4, ks4, vs4)
```
